```python
import math
import jax, jax.numpy as jnp
from jax import lax
import numpy as np

D_MODEL = 1024
BATCH = 16
SEQ = 4096
DEPTH = 1
DEC_BATCH = 8
DEC_SEQ = 2048
PAST_LEN = 128

N_META = 16
GRID_W = 64
WIN_H = 8
WIN_W = 16
Q_BLOCK_W = 16
K_BLOCK_W = 32
N_COL_BLOCKS = GRID_W // Q_BLOCK_W
N_HEADS_ATT = 8
HEAD_DIM = 64
D_ATT = N_HEADS_ATT * HEAD_DIM
D_CONV = D_MODEL
CONV_W = 3
N_GROUPS = 4
EXPERTS_PER_GROUP = 8
N_EXPERTS = N_GROUPS * EXPERTS_PER_GROUP
TOP_K_INNER = 2
D_EXPERT = D_MODEL // 2
MOE_BLOCK = 128
ALPHA = (2.0 * DEPTH) ** 0.25
BETA = (8.0 * DEPTH) ** -0.25
LN_EPS = 1e-5
D_IN_PROJ = 3 * D_ATT + 3 * D_CONV + 2 * D_MODEL

kernel_name = "hybrid_na_shortconv_hmoe_encoder"


def _col_tables():
    cb = np.arange(N_COL_BLOCKS)
    k_start = np.clip(cb * Q_BLOCK_W - (K_BLOCK_W - Q_BLOCK_W) // 2, 0, GRID_W - K_BLOCK_W)
    key_cols = k_start[:, None] + np.arange(K_BLOCK_W)[None, :]
    q_cols = cb[:, None] * Q_BLOCK_W + np.arange(Q_BLOCK_W)[None, :]
    w_start = np.clip(q_cols - WIN_W // 2, 0, GRID_W - WIN_W)
    kc = key_cols[:, None, :]
    valid = (kc >= w_start[:, :, None]) & (kc < w_start[:, :, None] + WIN_W)
    dx_idx = np.clip(kc - q_cols[:, :, None] + WIN_W - 1, 0, 2 * WIN_W - 2)
    return key_cols.astype(np.int32), valid, dx_idx.astype(np.int32)


def _layer_norm(x, g, b):
    xf = x.astype(jnp.float32)
    mu = jnp.mean(xf, axis=-1, keepdims=True)
    var = jnp.mean(jnp.square(xf - mu), axis=-1, keepdims=True)
    return ((xf - mu) * lax.rsqrt(var + LN_EPS) * g.astype(jnp.float32) + b.astype(jnp.float32)).astype(x.dtype)


def _neighbourhood_attention(q, k, v, rpb):
    bsz, L = q.shape[0], q.shape[1]
    T = L - N_META
    rows = T // GRID_W
    kh = min(WIN_H, rows)
    q = q * (HEAD_DIM ** -0.5)
    qm, km, vm = q[:, :N_META], k[:, :N_META], v[:, :N_META]
    s_meta = jnp.einsum('bqhd,bkhd->bhqk', qm, km).astype(jnp.float32)
    p_meta = jax.nn.softmax(s_meta, axis=-1).astype(v.dtype)
    o_meta = jnp.einsum('bhqk,bkhd->bqhd', p_meta, vm)
    qg = q[:, N_META:].reshape(bsz, rows, N_COL_BLOCKS, Q_BLOCK_W, N_HEADS_ATT, HEAD_DIM)
    kg = k[:, N_META:].reshape(bsz, rows, GRID_W, N_HEADS_ATT, HEAD_DIM)
    vg = v[:, N_META:].reshape(bsz, rows, GRID_W, N_HEADS_ATT, HEAD_DIM)
    key_cols, valid_np, dx_idx = _col_tables()
    valid = jnp.asarray(valid_np)[None, None, :, :, None, :]

    def row_fn(r):
        rs = jnp.clip(r - kh // 2, 0, rows - kh)
        q_r = lax.dynamic_index_in_dim(qg, r, axis=1, keepdims=False)
        k_r = lax.dynamic_slice_in_dim(kg, rs, kh, axis=1)
        v_r = lax.dynamic_slice_in_dim(vg, rs, kh, axis=1)
        k_blk = k_r[:, :, key_cols]
        v_blk = v_r[:, :, key_cols]
        s_loc = jnp.einsum('bcqhd,bkcmhd->bhcqkm', q_r, k_blk).astype(jnp.float32)
        dy = rs + jnp.arange(kh) - r
        bias = rpb[:, dy + WIN_H - 1][:, :, dx_idx]
        bias = jnp.transpose(bias, (0, 2, 3, 1, 4)).astype(jnp.float32)
        s_loc = jnp.where(valid, s_loc + bias[None], -jnp.inf)
        s_loc = s_loc.reshape(bsz, N_HEADS_ATT, N_COL_BLOCKS, Q_BLOCK_W, kh * K_BLOCK_W)
        s_mk = jnp.einsum('bcqhd,bkhd->bhcqk', q_r, km).astype(jnp.float32)
        p = jax.nn.softmax(jnp.concatenate([s_mk, s_loc], axis=-1), axis=-1).astype(v.dtype)
        p_m = p[..., :N_META]
        p_l = p[..., N_META:].reshape(bsz, N_HEADS_ATT, N_COL_BLOCKS, Q_BLOCK_W, kh, K_BLOCK_W)
        o = jnp.einsum('bhcqk,bkhd->bcqhd', p_m, vm) + jnp.einsum('bhcqkm,bkcmhd->bcqhd', p_l, v_blk)
        return o.reshape(bsz, GRID_W, N_HEADS_ATT, HEAD_DIM)

    o_real = lax.map(row_fn, jnp.arange(rows, dtype=jnp.int32))
    o_real = jnp.transpose(o_real, (1, 0, 2, 3, 4)).reshape(bsz, T, N_HEADS_ATT, HEAD_DIM)
    return jnp.concatenate([o_meta, o_real], axis=1)


def _short_conv(u, w, b):
    up = jnp.pad(u, ((0, 0), (1, 1), (0, 0)))
    return up[:, :-2] * w[0] + up[:, 1:-1] * w[1] + up[:, 2:] * w[2] + b


def _mixer(x, w_in, rpb, conv_w, conv_b, w_att_proj, w_conv_proj, w_out):
    bsz, L, _ = x.shape
    proj = x @ w_in
    splits = [D_ATT, 2 * D_ATT, 3 * D_ATT, 3 * D_ATT + D_CONV, 3 * D_ATT + 2 * D_CONV,
              3 * D_ATT + 3 * D_CONV, 3 * D_ATT + 3 * D_CONV + D_MODEL]
    q, k, v, b_gate, c_gate, x_conv, g_att, g_conv = jnp.split(proj, splits, axis=-1)
    hs = (bsz, L, N_HEADS_ATT, HEAD_DIM)
    att = _neighbourhood_attention(q.reshape(hs), k.reshape(hs), v.reshape(hs), rpb)
    att = att.reshape(bsz, L, D_ATT) @ w_att_proj
    conv = (b_gate * _short_conv(c_gate * x_conv, conv_w, conv_b)) @ w_conv_proj
    merged = jax.nn.sigmoid(g_att) * att + jax.nn.sigmoid(g_conv) * conv
    return merged @ w_out


def _hier_moe(x, w_router_group, w_router_expert, w_e_gate, w_e_up, w_e_down):
    shp = x.shape
    xf = x.reshape(-1, D_MODEL)
    n = xf.shape[0]
    lg = (xf @ w_router_group).astype(jnp.float32)
    pg = jax.nn.softmax(lg, axis=-1)
    grp = jnp.argmax(lg, axis=-1).astype(jnp.int32)
    pg_sel = jnp.take_along_axis(pg, grp[:, None], axis=-1)
    le = (xf @ w_router_expert).astype(jnp.float32).reshape(n, N_GROUPS, EXPERTS_PER_GROUP)
    le_sel = jnp.take_along_axis(le, grp[:, None, None], axis=1)[:, 0]
    top_v, top_i = lax.top_k(le_sel, TOP_K_INNER)
    gate = pg_sel * jax.nn.softmax(top_v, axis=-1)
    expert = grp[:, None] * EXPERTS_PER_GROUP + top_i.astype(jnp.int32)
    nk = n * TOP_K_INNER
    e_flat = expert.reshape(-1)
    tok_flat = jnp.broadcast_to(jnp.arange(n, dtype=jnp.int32)[:, None], (n, TOP_K_INNER)).reshape(-1)
    g_flat = gate.reshape(-1)
    order = jnp.argsort(e_flat)
    e_s, tok_s, g_s = e_flat[order], tok_flat[order], g_flat[order]
    counts = jnp.bincount(e_flat, length=N_EXPERTS).astype(jnp.int32)
    padded = (counts + MOE_BLOCK - 1) // MOE_BLOCK * MOE_BLOCK
    start = jnp.cumsum(counts) - counts
    pend = jnp.cumsum(padded)
    pstart = pend - padded
    dest = pstart[e_s] + (jnp.arange(nk, dtype=jnp.int32) - start[e_s])
    n_blocks = (nk + N_EXPERTS * (MOE_BLOCK - 1) + MOE_BLOCK - 1) // MOE_BLOCK
    P = n_blocks * MOE_BLOCK
    buf_tok = jnp.full((P,), n, dtype=jnp.int32).at[dest].set(tok_s)
    buf_gate = jnp.zeros((P,), dtype=jnp.float32).at[dest].set(g_s)
    blk_expert = jnp.clip(jnp.searchsorted(pend, jnp.arange(n_blocks, dtype=jnp.int32) * MOE_BLOCK, side='right'),
                          0, N_EXPERTS - 1).astype(jnp.int32)
    x_pad = jnp.concatenate([xf, jnp.zeros((1, D_MODEL), xf.dtype)], axis=0)
    xb = x_pad[buf_tok].reshape(n_blocks, MOE_BLOCK, D_MODEL)

    def block_fn(args):
        xblk, e = args
        h = jax.nn.silu(xblk @ w_e_gate[e]) * (xblk @ w_e_up[e])
        return h @ w_e_down[e]

    yb = lax.map(block_fn, (xb, blk_expert)).reshape(P, D_MODEL)
    yb = yb * buf_gate[:, None].astype(yb.dtype)
    y = jax.ops.segment_sum(yb, buf_tok, num_segments=n + 1)[:n]
    return y.reshape(shp)


def _encode(x, meta_tokens, w_in, rpb, conv_w, conv_b, w_att_proj, w_conv_proj, w_out,
            ln1_g, ln1_b, w_router_group, w_router_expert, w_e_gate, w_e_up, w_e_down, ln2_g, ln2_b):
    bsz = x.shape[0]
    meta = jnp.broadcast_to(meta_tokens[None].astype(x.dtype), (bsz, N_META, D_MODEL))
    h = jnp.concatenate([meta, x], axis=1)
    for i in range(DEPTH):
        mix = _mixer(h, w_in[i], rpb[i], conv_w[i], conv_b[i], w_att_proj[i], w_conv_proj[i], w_out[i])
        h = _layer_norm(ALPHA * h + mix, ln1_g[i], ln1_b[i])
        ffn = _hier_moe(h, w_router_group[i], w_router_expert[i], w_e_gate[i], w_e_up[i], w_e_down[i])
        h = _layer_norm(ALPHA * h + ffn, ln2_g[i], ln2_b[i])
    return h[:, N_META:]


def setup_inputs(seed: int = 0) -> dict:
    key = jax.random.key(seed)
    ks = jax.random.split(key, 21)
    f32 = jnp.float32

    def nrm(k, shape, scale):
        return jax.random.normal(k, shape, f32) * scale

    return {
        "x_prompt": nrm(ks[0], (BATCH, SEQ, D_MODEL), 1.0),
        "x_sample": nrm(ks[1], (DEC_BATCH, DEC_SEQ, D_MODEL), 1.0),
        "meta_tokens": nrm(ks[2], (N_META, D_MODEL), 1.0),
        "w_in": nrm(ks[3], (DEPTH, D_MODEL, D_IN_PROJ), D_MODEL ** -0.5),
        "rpb": nrm(ks[4], (DEPTH, N_HEADS_ATT, 2 * WIN_H - 1, 2 * WIN_W - 1), 0.1),
        "conv_w": nrm(ks[5], (DEPTH, CONV_W, D_CONV), CONV_W ** -0.5),
        "conv_b": nrm(ks[6], (DEPTH, D_CONV), 0.02),
        "w_att_proj": nrm(ks[7], (DEPTH, D_ATT, D_MODEL), BETA * D_ATT ** -0.5),
        "w_conv_proj": nrm(ks[8], (DEPTH, D_CONV, D_MODEL), BETA * D_CONV ** -0.5),
        "w_out": nrm(ks[9], (DEPTH, D_MODEL, D_MODEL), BETA * D_MODEL ** -0.5),
        "ln1_g": 1.0 + nrm(ks[10], (DEPTH, D_MODEL), 0.02),
        "ln1_b": nrm(ks[11], (DEPTH, D_MODEL), 0.02),
        "w_router_group": nrm(ks[12], (DEPTH, D_MODEL, N_GROUPS), D_MODEL ** -0.5),
        "w_router_expert": nrm(ks[13], (DEPTH, D_MODEL, N_EXPERTS), D_MODEL ** -0.5),
        "w_e_gate": nrm(ks[14], (DEPTH, N_EXPERTS, D_MODEL, D_EXPERT), D_MODEL ** -0.5),
        "w_e_up": nrm(ks[15], (DEPTH, N_EXPERTS, D_MODEL, D_EXPERT), BETA * D_MODEL ** -0.5),
        "w_e_down": nrm(ks[16], (DEPTH, N_EXPERTS, D_EXPERT, D_MODEL), BETA * D_EXPERT ** -0.5),
        "ln2_g": 1.0 + nrm(ks[17], (DEPTH, D_MODEL), 0.02),
        "ln2_b": nrm(ks[18], (DEPTH, D_MODEL), 0.02),
    }


def reference(x_prompt, x_sample, meta_tokens, w_in, rpb, conv_w, conv_b, w_att_proj, w_conv_proj, w_out,
              ln1_g, ln1_b, w_router_group, w_router_expert, w_e_gate, w_e_up, w_e_down, ln2_g, ln2_b):
    y_prompt = _encode(x_prompt, meta_tokens, w_in, rpb, conv_w, conv_b, w_att_proj, w_conv_proj, w_out,
                       ln1_g, ln1_b, w_router_group, w_router_expert, w_e_gate, w_e_up, w_e_down, ln2_g, ln2_b)
    y_sample = _encode(x_sample, meta_tokens, w_in, rpb, conv_w, conv_b, w_att_proj, w_conv_proj, w_out,
                       ln1_g, ln1_b, w_router_group, w_router_expert, w_e_gate, w_e_up, w_e_down, ln2_g, ln2_b)
    return (y_prompt, y_sample)
```

```python
import functools

import numpy as np
import jax
import jax.numpy as jnp
from jax import lax
from jax.experimental import pallas as pl
from jax.experimental.pallas import tpu as pltpu

D_MODEL = 1024
N_META = 16
GRID_W = 64
WIN_H = 8
WIN_W = 16
N_HEADS = 8
HEAD_DIM = 64
D_ATT = N_HEADS * HEAD_DIM
D_CONV = D_MODEL
N_GROUPS = 4
EXPERTS_PER_GROUP = 8
N_EXPERTS = N_GROUPS * EXPERTS_PER_GROUP
D_EXPERT = D_MODEL // 2
DEPTH = 1
ALPHA = (2.0 * DEPTH) ** 0.25
LN_EPS = 1e-5
D_QKV = 3 * D_ATT
D_REST = 3 * D_CONV + 2 * D_MODEL
D_IN_PROJ = D_QKV + D_REST

LANES = 128
HEADS_PER_VREG = LANES // HEAD_DIM
N_HEAD_PAIRS = N_HEADS // HEADS_PER_VREG
KEY_ROWS = WIN_H * GRID_W
NEG_BIG = -1e30

PROJ_TM = 512
PROJ_TN = 512
ATT_ROWS = 8
TAIL_TM = 256
MOE_TS = 512
MOE_BLK = 256
RANK_TILE = 1024
N_ZERO_FILLS = 2 * N_EXPERTS
VMEM_LIMIT = 56 * 1024 * 1024

bf16 = jnp.bfloat16
f32 = jnp.float32


def _cparams(n_axes):
    return pltpu.CompilerParams(dimension_semantics=("arbitrary",) * n_axes,
                                vmem_limit_bytes=VMEM_LIMIT)


def _in_proj_kernel(x_ref, w_ref, qkv_ref, rest_ref):
    xb = x_ref[...].astype(bf16)
    for c in range(0, D_IN_PROJ, PROJ_TN):
        y = jnp.dot(xb, w_ref[:, c:c + PROJ_TN], preferred_element_type=f32)
        if c < D_ATT:
            y = y * (HEAD_DIM ** -0.5)
        if c < D_QKV:
            qkv_ref[:, c:c + PROJ_TN] = y.astype(bf16)
        else:
            rest_ref[:, c - D_QKV:c - D_QKV + PROJ_TN] = y.astype(bf16)


def _in_proj(x, w_bf, tm):
    n = x.shape[0]
    return pl.pallas_call(
        _in_proj_kernel,
        grid=(n // tm,),
        in_specs=[pl.BlockSpec((tm, D_MODEL), lambda i: (i, 0)),
                  pl.BlockSpec((D_MODEL, D_IN_PROJ), lambda i: (0, 0))],
        out_specs=[pl.BlockSpec((tm, D_QKV), lambda i: (i, 0)),
                   pl.BlockSpec((tm, D_REST), lambda i: (i, 0))],
        out_shape=[jax.ShapeDtypeStruct((n, D_QKV), bf16),
                   jax.ShapeDtypeStruct((n, D_REST), bf16)],
        compiler_params=_cparams(1),
        name="in_proj",
    )(x, w_bf)


def _bias_slabs(rpb):
    qc = np.arange(GRID_W)[:, None]
    kc = np.arange(GRID_W)[None, :]
    w_start = np.clip(qc - WIN_W // 2, 0, GRID_W - WIN_W)
    valid = (kc >= w_start) & (kc < w_start + WIN_W)
    dx_idx = np.clip(kc - qc + WIN_W - 1, 0, 2 * WIN_W - 2)
    t = rpb.astype(f32)[:, :, dx_idx]
    t = jnp.where(jnp.asarray(valid)[None, None], t, NEG_BIG)
    dy_idx = np.arange(WIN_H)[:, None] + np.arange(WIN_H)[None, :]
    s = t[:, dy_idx]
    s = jnp.transpose(s, (0, 1, 3, 2, 4)).reshape(N_HEADS, WIN_H, GRID_W, KEY_ROWS)
    s = s.reshape(N_HEAD_PAIRS, HEADS_PER_VREG, WIN_H, GRID_W, KEY_ROWS)
    s = jnp.transpose(s, (0, 2, 1, 3, 4))
    return s.reshape(N_HEAD_PAIRS, WIN_H, HEADS_PER_VREG * GRID_W, KEY_ROWS)


def _attention_kernel(q_ref, k_ref, v_ref, km_ref, vm_ref, bias_ref, o_ref, *, rows):
    j = pl.program_id(1)
    lane = lax.broadcasted_iota(jnp.int32, (GRID_W, LANES), 1)
    low = lane < HEAD_DIM
    nt = (((1,), (1,)), ((), ()))

    def row_body(i, carry):
        r = j * ATT_ROWS + i
        rs = jnp.clip(r - WIN_H // 2, 0, rows - WIN_H)
        dy0 = rs - r + (WIN_H - 1)
        q0 = pl.multiple_of(i * GRID_W, GRID_W)
        k0 = pl.multiple_of(rs * GRID_W, GRID_W)
        for pair in range(N_HEAD_PAIRS):
            cs = slice(pair * LANES, (pair + 1) * LANES)
            q2 = q_ref[pl.ds(q0, GRID_W), cs]
            zero = jnp.zeros_like(q2)
            qq = jnp.concatenate([jnp.where(low, q2, zero), jnp.where(low, zero, q2)], axis=0)
            k2 = k_ref[pl.ds(k0, KEY_ROWS), cs]
            v2 = v_ref[pl.ds(k0, KEY_ROWS), cs]
            s = lax.dot_general(qq, k2, nt, preferred_element_type=f32)
            s = s + bias_ref[pair, dy0]
            sm = lax.dot_general(qq, km_ref[:, cs], nt, preferred_element_type=f32)
            m = jnp.maximum(jnp.max(s, axis=-1, keepdims=True), jnp.max(sm, axis=-1, keepdims=True))
            e = jnp.exp(s - m)
            em = jnp.exp(sm - m)
            l = jnp.sum(e, axis=-1, keepdims=True) + jnp.sum(em, axis=-1, keepdims=True)
            o2 = jnp.dot(e.astype(bf16), v2, preferred_element_type=f32)
            o2 = o2 + jnp.dot(em.astype(bf16), vm_ref[:, cs], preferred_element_type=f32)
            o2 = o2 / l
            o = jnp.where(low, o2[:GRID_W], o2[GRID_W:])
            o_ref[pl.ds(q0, GRID_W), cs] = o.astype(bf16)
        return carry

    lax.fori_loop(0, ATT_ROWS, row_body, 0)


def _attention(qkv, qkv_meta, slabs, bsz, t):
    rows = t // GRID_W
    assert rows >= 2 * WIN_H and rows % ATT_ROWS == 0
    steps = rows // ATT_ROWS
    tq = ATT_ROWS * GRID_W
    return pl.pallas_call(
        functools.partial(_attention_kernel, rows=rows),
        grid=(bsz, steps),
        in_specs=[pl.BlockSpec((tq, D_ATT), lambda b, j: (b * steps + j, 0)),
                  pl.BlockSpec((t, D_ATT), lambda b, j: (b, 1)),
                  pl.BlockSpec((t, D_ATT), lambda b, j: (b, 2)),
                  pl.BlockSpec((N_META, D_ATT), lambda b, j: (0, 1)),
                  pl.BlockSpec((N_META, D_ATT), lambda b, j: (0, 2)),
                  pl.BlockSpec(slabs.shape, lambda b, j: (0, 0, 0, 0))],
        out_specs=pl.BlockSpec((tq, D_ATT), lambda b, j: (b * steps + j, 0)),
        out_shape=jax.ShapeDtypeStruct((bsz * t, D_ATT), bf16),
        compiler_params=_cparams(2),
        name="attention",
    )(qkv, qkv, qkv, qkv_meta, qkv_meta, slabs)


def _layer_norm(z, g, b):
    mu = jnp.mean(z, axis=-1, keepdims=True)
    d = z - mu
    var = jnp.mean(d * d, axis=-1, keepdims=True)
    return d * lax.rsqrt(var + LN_EPS) * g + b


def _mixer_tail_kernel(x_ref, att_ref, rest_ref, prev_ref, next_ref, meta_ref,
                       convw_ref, convb_ref, wap_ref, wcp_ref, wout_ref, g_ref, b_ref, wr_ref,
                       h_ref, lg_ref, *, tiles_per_seq):
    i = pl.program_id(0)
    pos = i % tiles_per_seq
    tm = x_ref.shape[0]
    c0, c1, c2, c3, c4 = (k * D_MODEL for k in range(5))

    u = rest_ref[:, c0:c1].astype(f32) * rest_ref[:, c1:c2].astype(f32)
    last = prev_ref.shape[0] - 1
    u_prev = prev_ref[last:, c0:c1].astype(f32) * prev_ref[last:, c1:c2].astype(f32)
    u_meta = meta_ref[N_META - 1:, c0:c1].astype(f32) * meta_ref[N_META - 1:, c1:c2].astype(f32)
    u_next = next_ref[:1, c0:c1].astype(f32) * next_ref[:1, c1:c2].astype(f32)
    u_prev = jnp.where(pos == 0, u_meta, u_prev)
    u_next = jnp.where(pos == tiles_per_seq - 1, jnp.zeros_like(u_next), u_next)
    row = lax.broadcasted_iota(jnp.int32, (tm, D_CONV), 0)
    u_m1 = jnp.where(row == 0, u_prev, pltpu.roll(u, 1, axis=0))
    u_p1 = jnp.where(row == tm - 1, u_next, pltpu.roll(u, tm - 1, axis=0))
    s = u_m1 * convw_ref[0:1, :] + u * convw_ref[1:2, :] + u_p1 * convw_ref[2:3, :] + convb_ref[...]
    conv_in = (rest_ref[:, c2:c3].astype(f32) * s).astype(bf16)
    cv = jnp.dot(conv_in, wcp_ref[...], preferred_element_type=f32)
    ap = jnp.dot(att_ref[...], wap_ref[...], preferred_element_type=f32)
    merged = (jax.nn.sigmoid(rest_ref[:, c3:c4].astype(f32)) * ap
              + jax.nn.sigmoid(rest_ref[:, c4:].astype(f32)) * cv)
    mix = jnp.dot(merged.astype(bf16), wout_ref[...], preferred_element_type=f32)
    h = _layer_norm(ALPHA * x_ref[...] + mix, g_ref[...], b_ref[...])
    h_ref[...] = h
    lg_ref[...] = jnp.dot(h.astype(bf16), wr_ref[...], preferred_element_type=f32)


def _mixer_tail(x, att, rest, rest_meta, p, t):
    n = x.shape[0]
    tm = TAIL_TM
    halo = 16
    hb = tm // halo
    n_halo = n // halo
    tiles_per_seq = t // tm
    full = lambda a: pl.BlockSpec(a.shape, lambda i: (0,) * a.ndim)
    return pl.pallas_call(
        functools.partial(_mixer_tail_kernel, tiles_per_seq=tiles_per_seq),
        grid=(n // tm,),
        in_specs=[pl.BlockSpec((tm, D_MODEL), lambda i: (i, 0)),
                  pl.BlockSpec((tm, D_ATT), lambda i: (i, 0)),
                  pl.BlockSpec((tm, D_REST), lambda i: (i, 0)),
                  pl.BlockSpec((halo, 2 * D_CONV), lambda i: (jnp.maximum(i * hb - 1, 0), 0)),
                  pl.BlockSpec((halo, 2 * D_CONV), lambda i: (jnp.minimum((i + 1) * hb, n_halo - 1), 0)),
                  full(rest_meta), full(p["conv_w"]), full(p["conv_b"]), full(p["w_att_proj"]),
                  full(p["w_conv_proj"]), full(p["w_out"]), full(p["ln1_g"]), full(p["ln1_b"]),
                  full(p["w_router"])],
        out_specs=[pl.BlockSpec((tm, D_MODEL), lambda i: (i, 0)),
                   pl.BlockSpec((tm, LANES), lambda i: (i, 0))],
        out_shape=[jax.ShapeDtypeStruct((n, D_MODEL), f32),
                   jax.ShapeDtypeStruct((n, LANES), f32)],
        compiler_params=_cparams(1),
        name="mixer_tail",
    )(x, att, rest, rest, rest, rest_meta, p["conv_w"], p["conv_b"], p["w_att_proj"],
      p["w_conv_proj"], p["w_out"], p["ln1_g"], p["ln1_b"], p["w_router"])


def _route(logits, n):
    lg = logits[:, :N_GROUPS]
    le = logits[:, N_GROUPS:N_GROUPS + N_EXPERTS].reshape(n, N_GROUPS, EXPERTS_PER_GROUP)
    pg = jax.nn.softmax(lg, axis=-1)
    grp = jnp.argmax(lg, axis=-1).astype(jnp.int32)
    pg_sel = jnp.take_along_axis(pg, grp[:, None], axis=-1)
    le_sel = jnp.take_along_axis(le, grp[:, None, None], axis=1)[:, 0]
    top_v, top_i = lax.top_k(le_sel, 2)
    gate = pg_sel * jax.nn.softmax(top_v, axis=-1)
    expert = grp[:, None] * EXPERTS_PER_GROUP + top_i.astype(jnp.int32)

    nk = 2 * n
    e_flat = expert.reshape(nk)
    onehot = (e_flat[:, None] == jnp.arange(N_EXPERTS, dtype=jnp.int32)[None, :])
    oh = onehot.astype(f32).reshape(nk // RANK_TILE, RANK_TILE, N_EXPERTS)
    tril = jnp.asarray(np.tril(np.ones((RANK_TILE, RANK_TILE), np.float32), -1))
    rank_all = jnp.einsum("ij,tjk->tik", tril, oh)
    rank = jnp.sum(rank_all * oh, axis=-1).reshape(nk).astype(jnp.int32)
    tile_cnt = jnp.sum(oh, axis=1).astype(jnp.int32)
    tile_off = jnp.cumsum(tile_cnt, axis=0) - tile_cnt
    counts = jnp.sum(tile_cnt, axis=0)
    padded = (counts + MOE_BLK - 1) // MOE_BLK * MOE_BLK
    pend = jnp.cumsum(padded)
    pstart = pend - padded
    base = (pstart[None, :] + tile_off).astype(f32)
    base_sel = jnp.sum(base[:, None, :] * oh, axis=-1).reshape(nk).astype(jnp.int32)
    dest = base_sel + rank

    n_blocks = (nk + N_EXPERTS * (MOE_BLK - 1) + MOE_BLK - 1) // MOE_BLK
    blk_start = jnp.arange(n_blocks, dtype=jnp.int32) * MOE_BLK
    blk_expert = jnp.clip(jnp.searchsorted(pend, blk_start, side="right"), 0, N_EXPERTS - 1).astype(jnp.int32)
    blk_valid = jnp.clip((pstart + counts)[blk_expert] - blk_start, 0, MOE_BLK)
    blk_valid = jnp.where(blk_start < pend[-1], blk_valid, 0).astype(jnp.int32)
    tail = (pend[-1] // MOE_BLK + jnp.arange(N_ZERO_FILLS - N_EXPERTS, dtype=jnp.int32)) * MOE_BLK
    zstart = jnp.concatenate([jnp.where(padded > counts, pend - MOE_BLK, -1),
                              jnp.where(tail < n_blocks * MOE_BLK, tail, -1)]).astype(jnp.int32)
    return gate, dest, blk_expert, blk_valid, zstart, n_blocks


def _dispatch_kernel(zstart_ref, dest_hbm, h_ref, xs_hbm, idx_smem, zeros, idx_sem, row_sem, zero_sem):
    i = pl.program_id(0)
    ts = h_ref.shape[0]

    @pl.when(i == 0)
    def _():
        zeros[...] = jnp.zeros_like(zeros)

        def fill(z):
            start = pl.multiple_of(jnp.maximum(zstart_ref[z], 0), MOE_BLK)
            return pltpu.make_async_copy(zeros, xs_hbm.at[pl.ds(start, MOE_BLK)], zero_sem)

        for z in range(N_ZERO_FILLS):
            pl.when(zstart_ref[z] >= 0)(lambda z=z: fill(z).start())
        for z in range(N_ZERO_FILLS):
            pl.when(zstart_ref[z] >= 0)(lambda z=z: fill(z).wait())

    cp = pltpu.make_async_copy(dest_hbm.at[i], idx_smem, idx_sem)
    cp.start()
    cp.wait()

    def body(tok, carry):
        for k in range(2):
            d = idx_smem[2 * tok + k]
            pltpu.make_async_copy(h_ref.at[pl.ds(tok, 1)], xs_hbm.at[pl.ds(d, 1)], row_sem).start()
        return carry

    lax.fori_loop(0, ts, body, 0)
    for _ in range(2):
        pltpu.make_async_copy(h_ref, xs_hbm.at[pl.ds(0, ts)], row_sem).wait()


def _dispatch(h, dest, zstart, p_rows):
    n = h.shape[0]
    ts = MOE_TS
    grid_spec = pltpu.PrefetchScalarGridSpec(
        num_scalar_prefetch=1,
        grid=(n // ts,),
        in_specs=[pl.BlockSpec(memory_space=pl.ANY),
                  pl.BlockSpec((ts, D_MODEL), lambda i, z: (i, 0))],
        out_specs=pl.BlockSpec(memory_space=pl.ANY),
        scratch_shapes=[pltpu.SMEM((2 * ts,), jnp.int32),
                        pltpu.VMEM((MOE_BLK, D_MODEL), f32),
                        pltpu.SemaphoreType.DMA(()),
                        pltpu.SemaphoreType.DMA(()),
                        pltpu.SemaphoreType.DMA(())],
    )
    return pl.pallas_call(
        _dispatch_kernel,
        grid_spec=grid_spec,
        out_shape=jax.ShapeDtypeStruct((p_rows, D_MODEL), f32),
        compiler_params=_cparams(1),
        name="dispatch",
    )(zstart, dest.reshape(n // ts, 2 * ts), h)


def _experts_kernel(be_ref, bv_ref, xs_ref, wg_ref, wu_ref, wd_ref, ys_ref):
    i = pl.program_id(0)
    valid = bv_ref[i]

    @pl.when(valid > 0)
    def _():
        row = lax.broadcasted_iota(jnp.int32, xs_ref.shape, 0)
        x = jnp.where(row < valid, xs_ref[...], 0.0).astype(bf16)
        g = jnp.dot(x, wg_ref[0], preferred_element_type=f32)
        u = jnp.dot(x, wu_ref[0], preferred_element_type=f32)
        hmid = (g * jax.nn.sigmoid(g) * u).astype(bf16)
        ys_ref[...] = jnp.dot(hmid, wd_ref[0], preferred_element_type=f32)

    @pl.when(valid == 0)
    def _():
        ys_ref[...] = jnp.zeros_like(ys_ref)


def _experts(xs, blk_expert, blk_valid, wg, wu, wd, n_blocks):
    grid_spec = pltpu.PrefetchScalarGridSpec(
        num_scalar_prefetch=2,
        grid=(n_blocks,),
        in_specs=[pl.BlockSpec((MOE_BLK, D_MODEL), lambda i, be, bv: (i, 0)),
                  pl.BlockSpec((1, D_MODEL, D_EXPERT), lambda i, be, bv: (be[i], 0, 0)),
                  pl.BlockSpec((1, D_MODEL, D_EXPERT), lambda i, be, bv: (be[i], 0, 0)),
                  pl.BlockSpec((1, D_EXPERT, D_MODEL), lambda i, be, bv: (be[i], 0, 0))],
        out_specs=pl.BlockSpec((MOE_BLK, D_MODEL), lambda i, be, bv: (i, 0)),
    )
    return pl.pallas_call(
        _experts_kernel,
        grid_spec=grid_spec,
        out_shape=jax.ShapeDtypeStruct(xs.shape, f32),
        compiler_params=_cparams(1),
        name="experts",
    )(blk_expert, blk_valid, xs, wg, wu, wd)


def _combine_kernel(dest_hbm, ys_hbm, h_ref, gate_ref, g_ref, b_ref, o_ref,
                    idx_smem, ybuf, idx_sem, row_sem):
    i = pl.program_id(0)
    ts = h_ref.shape[0]
    cp = pltpu.make_async_copy(dest_hbm.at[i], idx_smem, idx_sem)
    cp.start()
    cp.wait()

    def body(tok, carry):
        for k in range(2):
            d = idx_smem[2 * tok + k]
            pltpu.make_async_copy(ys_hbm.at[pl.ds(d, 1)], ybuf.at[k, pl.ds(tok, 1)], row_sem).start()
        return carry

    lax.fori_loop(0, ts, body, 0)
    for k in range(2):
        pltpu.make_async_copy(ys_hbm.at[pl.ds(0, ts)], ybuf.at[k], row_sem).wait()

    ffn = ybuf[0] * gate_ref[:, 0:1] + ybuf[1] * gate_ref[:, 1:2]
    o_ref[...] = _layer_norm(ALPHA * h_ref[...] + ffn, g_ref[...], b_ref[...])


def _combine(h, ys, dest, gate, ln_g, ln_b):
    n = h.shape[0]
    ts = MOE_TS
    full = lambda a: pl.BlockSpec(a.shape, lambda i: (0,) * a.ndim)
    return pl.pallas_call(
        _combine_kernel,
        grid=(n // ts,),
        in_specs=[pl.BlockSpec(memory_space=pl.ANY),
                  pl.BlockSpec(memory_space=pl.ANY),
                  pl.BlockSpec((ts, D_MODEL), lambda i: (i, 0)),
                  pl.BlockSpec((ts, 2), lambda i: (i, 0)),
                  full(ln_g), full(ln_b)],
        out_specs=pl.BlockSpec((ts, D_MODEL), lambda i: (i, 0)),
        out_shape=jax.ShapeDtypeStruct((n, D_MODEL), f32),
        scratch_shapes=[pltpu.SMEM((2 * ts,), jnp.int32),
                        pltpu.VMEM((2, ts, D_MODEL), f32),
                        pltpu.SemaphoreType.DMA(()),
                        pltpu.SemaphoreType.DMA(())],
        compiler_params=_cparams(1),
        name="combine",
    )(dest.reshape(n // ts, 2 * ts), ys, h, gate, ln_g, ln_b)


def _encode(x, p, qkv_meta, rest_meta, slabs):
    bsz, t, _ = x.shape
    n = bsz * t
    assert t % MOE_TS == 0 and t % TAIL_TM == 0 and t % PROJ_TM == 0 and (2 * n) % RANK_TILE == 0
    xf = x.reshape(n, D_MODEL)
    qkv, rest = _in_proj(xf, p["w_in"], PROJ_TM)
    att = _attention(qkv, qkv_meta, slabs, bsz, t)
    h1, logits = _mixer_tail(xf, att, rest, rest_meta, p, t)
    gate, dest, blk_expert, blk_valid, zstart, n_blocks = _route(logits, n)
    xs = _dispatch(h1, dest, zstart, n_blocks * MOE_BLK)
    ys = _experts(xs, blk_expert, blk_valid, p["w_e_gate"], p["w_e_up"], p["w_e_down"], n_blocks)
    y = _combine(h1, ys, dest, gate, p["ln2_g"], p["ln2_b"])
    return y.reshape(bsz, t, D_MODEL)


def kernel(x_prompt, x_sample, meta_tokens, w_in, rpb, conv_w, conv_b, w_att_proj, w_conv_proj, w_out,
           ln1_g, ln1_b, w_router_group, w_router_expert, w_e_gate, w_e_up, w_e_down, ln2_g, ln2_b):
    w = w_in[0]
    a, c = D_QKV, D_CONV
    w_perm = jnp.concatenate([w[:, :a], w[:, a + c:a + 3 * c], w[:, a:a + c], w[:, a + 3 * c:]], axis=1)
    w_router = jnp.concatenate([w_router_group[0], w_router_expert[0]], axis=1)
    w_router = jnp.pad(w_router, ((0, 0), (0, LANES - w_router.shape[1])))
    row = lambda v: v[0].reshape(1, -1).astype(f32)
    p = {
        "w_in": w_perm.astype(bf16),
        "conv_w": conv_w[0].astype(f32), "conv_b": row(conv_b),
        "w_att_proj": w_att_proj[0].astype(bf16), "w_conv_proj": w_conv_proj[0].astype(bf16),
        "w_out": w_out[0].astype(bf16), "ln1_g": row(ln1_g), "ln1_b": row(ln1_b),
        "w_router": w_router.astype(bf16),
        "w_e_gate": w_e_gate[0].astype(bf16), "w_e_up": w_e_up[0].astype(bf16),
        "w_e_down": w_e_down[0].astype(bf16), "ln2_g": row(ln2_g), "ln2_b": row(ln2_b),
    }
    qkv_meta, rest_meta = _in_proj(meta_tokens.astype(f32), p["w_in"], N_META)
    slabs = _bias_slabs(rpb[0])
    y_prompt = _encode(x_prompt, p, qkv_meta, rest_meta, slabs)
    y_sample = _encode(x_sample, p, qkv_meta, rest_meta, slabs)
    return (y_prompt, y_sample)
```

```python
import functools

import numpy as np
import jax
import jax.numpy as jnp
from jax import lax
from jax.experimental import pallas as pl
from jax.experimental.pallas import tpu as pltpu

D_MODEL = 1024
N_META = 16
GRID_W = 64
WIN_H = 8
WIN_W = 16
N_HEADS = 8
HEAD_DIM = 64
D_ATT = N_HEADS * HEAD_DIM
D_CONV = D_MODEL
N_GROUPS = 4
EXPERTS_PER_GROUP = 8
N_EXPERTS = N_GROUPS * EXPERTS_PER_GROUP
D_EXPERT = D_MODEL // 2
DEPTH = 1
ALPHA = (2.0 * DEPTH) ** 0.25
LN_EPS = 1e-5
D_QKV = 3 * D_ATT
D_REST = 3 * D_CONV + 2 * D_MODEL
D_IN_PROJ = D_QKV + D_REST

LANES = 128
HEADS_PER_VREG = LANES // HEAD_DIM
N_HEAD_PAIRS = N_HEADS // HEADS_PER_VREG
KEY_ROWS = WIN_H * GRID_W
NEG_BIG = -1e30

PROJ_TM = 512
PROJ_TN = 512
ATT_ROWS = 8
TAIL_TM = 256
MOE_TS = 512
MOE_BLK = 256
ROW_DMA_UNROLL = 8
N_ZERO_FILLS = 2 * N_EXPERTS
VMEM_LIMIT = 56 * 1024 * 1024

bf16 = jnp.bfloat16
f32 = jnp.float32


def _cparams(n_axes):
    return pltpu.CompilerParams(dimension_semantics=("arbitrary",) * n_axes,
                                vmem_limit_bytes=VMEM_LIMIT)


def _in_proj_kernel(x_ref, w_ref, qkv_ref, rest_ref):
    xb = x_ref[...].astype(bf16)
    for c in range(0, D_IN_PROJ, PROJ_TN):
        y = jnp.dot(xb, w_ref[:, c:c + PROJ_TN], preferred_element_type=f32)
        if c < D_ATT:
            y = y * (HEAD_DIM ** -0.5)
        if c < D_QKV:
            qkv_ref[:, c:c + PROJ_TN] = y.astype(bf16)
        else:
            rest_ref[:, c - D_QKV:c - D_QKV + PROJ_TN] = y.astype(bf16)


def _in_proj(x, w_bf, tm):
    n = x.shape[0]
    return pl.pallas_call(
        _in_proj_kernel,
        grid=(n // tm,),
        in_specs=[pl.BlockSpec((tm, D_MODEL), lambda i: (i, 0)),
                  pl.BlockSpec((D_MODEL, D_IN_PROJ), lambda i: (0, 0))],
        out_specs=[pl.BlockSpec((tm, D_QKV), lambda i: (i, 0)),
                   pl.BlockSpec((tm, D_REST), lambda i: (i, 0))],
        out_shape=[jax.ShapeDtypeStruct((n, D_QKV), bf16),
                   jax.ShapeDtypeStruct((n, D_REST), bf16)],
        compiler_params=_cparams(1),
        name="in_proj",
    )(x, w_bf)


def _bias_slabs(rpb):
    qc = np.arange(GRID_W)[:, None]
    kc = np.arange(GRID_W)[None, :]
    w_start = np.clip(qc - WIN_W // 2, 0, GRID_W - WIN_W)
    valid = (kc >= w_start) & (kc < w_start + WIN_W)
    dx_idx = np.clip(kc - qc + WIN_W - 1, 0, 2 * WIN_W - 2)
    t = rpb.astype(f32)[:, :, dx_idx]
    t = jnp.where(jnp.asarray(valid)[None, None], t, NEG_BIG)
    dy_idx = np.arange(WIN_H)[:, None] + np.arange(WIN_H)[None, :]
    s = t[:, dy_idx]
    s = jnp.transpose(s, (0, 1, 3, 2, 4)).reshape(N_HEADS, WIN_H, GRID_W, KEY_ROWS)
    s = s.reshape(N_HEAD_PAIRS, HEADS_PER_VREG, WIN_H, GRID_W, KEY_ROWS)
    s = jnp.transpose(s, (0, 2, 1, 3, 4))
    return s.reshape(N_HEAD_PAIRS, WIN_H, HEADS_PER_VREG * GRID_W, KEY_ROWS)


def _attention_kernel(q_ref, k_ref, v_ref, km_ref, vm_ref, bias_ref, o_ref, *, rows):
    j = pl.program_id(1)
    lane = lax.broadcasted_iota(jnp.int32, (GRID_W, LANES), 1)
    low = lane < HEAD_DIM
    nt = (((1,), (1,)), ((), ()))

    def row_body(i, carry):
        r = j * ATT_ROWS + i
        rs = jnp.clip(r - WIN_H // 2, 0, rows - WIN_H)
        dy0 = rs - r + (WIN_H - 1)
        q0 = pl.multiple_of(i * GRID_W, GRID_W)
        k0 = pl.multiple_of(rs * GRID_W, GRID_W)
        for pair in range(N_HEAD_PAIRS):
            cs = slice(pair * LANES, (pair + 1) * LANES)
            q2 = q_ref[pl.ds(q0, GRID_W), cs]
            zero = jnp.zeros_like(q2)
            qq = jnp.concatenate([jnp.where(low, q2, zero), jnp.where(low, zero, q2)], axis=0)
            k2 = k_ref[pl.ds(k0, KEY_ROWS), cs]
            v2 = v_ref[pl.ds(k0, KEY_ROWS), cs]
            s = lax.dot_general(qq, k2, nt, preferred_element_type=f32)
            s = s + bias_ref[pair, dy0]
            sm = lax.dot_general(qq, km_ref[:, cs], nt, preferred_element_type=f32)
            m = jnp.maximum(jnp.max(s, axis=-1, keepdims=True), jnp.max(sm, axis=-1, keepdims=True))
            e = jnp.exp(s - m)
            em = jnp.exp(sm - m)
            l = jnp.sum(e, axis=-1, keepdims=True) + jnp.sum(em, axis=-1, keepdims=True)
            o2 = jnp.dot(e.astype(bf16), v2, preferred_element_type=f32)
            o2 = o2 + jnp.dot(em.astype(bf16), vm_ref[:, cs], preferred_element_type=f32)
            o2 = o2 / l
            o = jnp.where(low, o2[:GRID_W], o2[GRID_W:])
            o_ref[pl.ds(q0, GRID_W), cs] = o.astype(bf16)
        return carry

    lax.fori_loop(0, ATT_ROWS, row_body, 0)


def _attention(qkv, qkv_meta, slabs, bsz, t):
    rows = t // GRID_W
    assert rows >= 2 * WIN_H and rows % ATT_ROWS == 0
    steps = rows // ATT_ROWS
    tq = ATT_ROWS * GRID_W
    return pl.pallas_call(
        functools.partial(_attention_kernel, rows=rows),
        grid=(bsz, steps),
        in_specs=[pl.BlockSpec((tq, D_ATT), lambda b, j: (b * steps + j, 0)),
                  pl.BlockSpec((t, D_ATT), lambda b, j: (b, 1)),
                  pl.BlockSpec((t, D_ATT), lambda b, j: (b, 2)),
                  pl.BlockSpec((N_META, D_ATT), lambda b, j: (0, 1)),
                  pl.BlockSpec((N_META, D_ATT), lambda b, j: (0, 2)),
                  pl.BlockSpec(slabs.shape, lambda b, j: (0, 0, 0, 0))],
        out_specs=pl.BlockSpec((tq, D_ATT), lambda b, j: (b * steps + j, 0)),
        out_shape=jax.ShapeDtypeStruct((bsz * t, D_ATT), bf16),
        compiler_params=_cparams(2),
        name="attention",
    )(qkv, qkv, qkv, qkv_meta, qkv_meta, slabs)


def _layer_norm(z, g, b):
    mu = jnp.mean(z, axis=-1, keepdims=True)
    d = z - mu
    var = jnp.mean(d * d, axis=-1, keepdims=True)
    return d * lax.rsqrt(var + LN_EPS) * g + b


def _route_tile(lg):
    tm = lg.shape[0]
    lane = lax.broadcasted_iota(jnp.int32, lg.shape, 1).astype(f32)
    neg = -jnp.inf
    first = lambda hit: jnp.min(jnp.where(hit, lane, float(LANES)), axis=-1, keepdims=True)

    gl = jnp.where(lane < N_GROUPS, lg, neg)
    gmax = jnp.max(gl, axis=-1, keepdims=True)
    grp = first(gl == gmax)
    pg_sel = 1.0 / jnp.sum(jnp.exp(gl - gmax), axis=-1, keepdims=True)

    e_lo = N_GROUPS + grp * EXPERTS_PER_GROUP
    el = jnp.where((lane >= e_lo) & (lane < e_lo + EXPERTS_PER_GROUP), lg, neg)
    t1 = jnp.max(el, axis=-1, keepdims=True)
    i1 = first(el == t1)
    el2 = jnp.where(lane == i1, neg, el)
    t2 = jnp.max(el2, axis=-1, keepdims=True)
    i2 = first(el2 == t2)
    r = jnp.exp(t2 - t1)
    g1 = pg_sel / (1.0 + r)
    g2 = pg_sel * r / (1.0 + r)
    e1 = i1 - N_GROUPS
    e2 = i2 - N_GROUPS

    hit1 = lane == e1
    hit2 = lane == e2
    onehot = jnp.where(hit1 | hit2, 1.0, 0.0)
    tri_r = lax.broadcasted_iota(jnp.int32, (tm, tm), 0)
    tri_c = lax.broadcasted_iota(jnp.int32, (tm, tm), 1)
    tri = jnp.where(tri_r > tri_c, 1.0, 0.0).astype(bf16)
    before = jnp.dot(tri, onehot.astype(bf16), preferred_element_type=f32)
    rank1 = jnp.sum(jnp.where(hit1, before, 0.0), axis=-1, keepdims=True)
    rank2 = jnp.sum(jnp.where(hit2, before, 0.0), axis=-1, keepdims=True)
    route = jnp.zeros_like(lg)
    for col, val in enumerate((e1, e2, g1, g2, rank1, rank2)):
        route = jnp.where(lane == col, val, route)
    return route, jnp.sum(onehot, axis=0, keepdims=True)


def _mixer_tail_kernel(x_ref, att_ref, rest_ref, prev_ref, next_ref, meta_ref,
                       convw_ref, convb_ref, wap_ref, wcp_ref, wout_ref, g_ref, b_ref, wr_ref,
                       h_ref, route_ref, cnt_ref, *, tiles_per_seq):
    i = pl.program_id(0)
    pos = i % tiles_per_seq
    tm = x_ref.shape[0]
    c0, c1, c2, c3, c4 = (k * D_MODEL for k in range(5))

    u = rest_ref[:, c0:c1].astype(f32) * rest_ref[:, c1:c2].astype(f32)
    last = prev_ref.shape[0] - 1
    u_prev = prev_ref[last:, c0:c1].astype(f32) * prev_ref[last:, c1:c2].astype(f32)
    u_meta = meta_ref[N_META - 1:, c0:c1].astype(f32) * meta_ref[N_META - 1:, c1:c2].astype(f32)
    u_next = next_ref[:1, c0:c1].astype(f32) * next_ref[:1, c1:c2].astype(f32)
    u_prev = jnp.where(pos == 0, u_meta, u_prev)
    u_next = jnp.where(pos == tiles_per_seq - 1, jnp.zeros_like(u_next), u_next)
    row = lax.broadcasted_iota(jnp.int32, (tm, D_CONV), 0)
    u_m1 = jnp.where(row == 0, u_prev, pltpu.roll(u, 1, axis=0))
    u_p1 = jnp.where(row == tm - 1, u_next, pltpu.roll(u, tm - 1, axis=0))
    s = u_m1 * convw_ref[0:1, :] + u * convw_ref[1:2, :] + u_p1 * convw_ref[2:3, :] + convb_ref[...]
    conv_in = (rest_ref[:, c2:c3].astype(f32) * s).astype(bf16)
    cv = jnp.dot(conv_in, wcp_ref[...], preferred_element_type=f32)
    ap = jnp.dot(att_ref[...], wap_ref[...], preferred_element_type=f32)
    merged = (jax.nn.sigmoid(rest_ref[:, c3:c4].astype(f32)) * ap
              + jax.nn.sigmoid(rest_ref[:, c4:].astype(f32)) * cv)
    mix = jnp.dot(merged.astype(bf16), wout_ref[...], preferred_element_type=f32)
    h = _layer_norm(ALPHA * x_ref[...] + mix, g_ref[...], b_ref[...])
    h_ref[...] = h
    logits = jnp.dot(h.astype(bf16), wr_ref[...], preferred_element_type=f32)
    route, count = _route_tile(logits)
    route_ref[...] = route
    cnt_ref[0] = count


def _mixer_tail(x, att, rest, rest_meta, p, t):
    n = x.shape[0]
    tm = TAIL_TM
    halo = 16
    hb = tm // halo
    n_halo = n // halo
    tiles_per_seq = t // tm
    full = lambda a: pl.BlockSpec(a.shape, lambda i: (0,) * a.ndim)
    return pl.pallas_call(
        functools.partial(_mixer_tail_kernel, tiles_per_seq=tiles_per_seq),
        grid=(n // tm,),
        in_specs=[pl.BlockSpec((tm, D_MODEL), lambda i: (i, 0)),
                  pl.BlockSpec((tm, D_ATT), lambda i: (i, 0)),
                  pl.BlockSpec((tm, D_REST), lambda i: (i, 0)),
                  pl.BlockSpec((halo, 2 * D_CONV), lambda i: (jnp.maximum(i * hb - 1, 0), 0)),
                  pl.BlockSpec((halo, 2 * D_CONV), lambda i: (jnp.minimum((i + 1) * hb, n_halo - 1), 0)),
                  full(rest_meta), full(p["conv_w"]), full(p["conv_b"]), full(p["w_att_proj"]),
                  full(p["w_conv_proj"]), full(p["w_out"]), full(p["ln1_g"]), full(p["ln1_b"]),
                  full(p["w_router"])],
        out_specs=[pl.BlockSpec((tm, D_MODEL), lambda i: (i, 0)),
                   pl.BlockSpec((tm, LANES), lambda i: (i, 0)),
                   pl.BlockSpec((1, 1, LANES), lambda i: (i, 0, 0))],
        out_shape=[jax.ShapeDtypeStruct((n, D_MODEL), f32),
                   jax.ShapeDtypeStruct((n, LANES), f32),
                   jax.ShapeDtypeStruct((n // tm, 1, LANES), f32)],
        compiler_params=_cparams(1),
        name="mixer_tail",
    )(x, att, rest, rest, rest, rest_meta, p["conv_w"], p["conv_b"], p["w_att_proj"],
      p["w_conv_proj"], p["w_out"], p["ln1_g"], p["ln1_b"], p["w_router"])


def _route(route, cnt, n):
    n_tiles = n // TAIL_TM
    expert = route[:, 0:2].astype(jnp.int32)
    gate = route[:, 2:4]
    rank = route[:, 4:6].astype(jnp.int32)
    tile_cnt = cnt[:, 0, :N_EXPERTS].astype(jnp.int32)
    tile_off = jnp.cumsum(tile_cnt, axis=0) - tile_cnt
    counts = jnp.sum(tile_cnt, axis=0)
    padded = (counts + MOE_BLK - 1) // MOE_BLK * MOE_BLK
    pend = jnp.cumsum(padded)
    pstart = pend - padded
    base = pstart[None, :] + tile_off
    hit = expert.reshape(n_tiles, TAIL_TM, 2, 1) == jnp.arange(N_EXPERTS, dtype=jnp.int32)
    base_sel = jnp.sum(jnp.where(hit, base[:, None, None, :], 0), axis=-1)
    dest = (base_sel.reshape(n, 2) + rank).reshape(2 * n)

    n_blocks = (2 * n + N_EXPERTS * (MOE_BLK - 1) + MOE_BLK - 1) // MOE_BLK
    blk_start = jnp.arange(n_blocks, dtype=jnp.int32) * MOE_BLK
    blk_expert = jnp.sum((pend[None, :] <= blk_start[:, None]).astype(jnp.int32), axis=1)
    blk_expert = jnp.minimum(blk_expert, N_EXPERTS - 1)
    blk_valid = jnp.clip((pstart + counts)[blk_expert] - blk_start, 0, MOE_BLK)
    blk_valid = jnp.where(blk_start < pend[-1], blk_valid, 0).astype(jnp.int32)
    tail = (pend[-1] // MOE_BLK + jnp.arange(N_ZERO_FILLS - N_EXPERTS, dtype=jnp.int32)) * MOE_BLK
    zstart = jnp.concatenate([jnp.where(padded > counts, pend - MOE_BLK, -1),
                              jnp.where(tail < n_blocks * MOE_BLK, tail, -1)]).astype(jnp.int32)
    return gate, dest, blk_expert, blk_valid, zstart, n_blocks


def _dispatch_kernel(zstart_ref, dest_hbm, h_ref, xs_hbm, idx_smem, zeros, idx_sem, row_sem, zero_sem):
    i = pl.program_id(0)
    ts = h_ref.shape[0]

    @pl.when(i == 0)
    def _():
        zeros[...] = jnp.zeros_like(zeros)

        def fill(z):
            start = pl.multiple_of(jnp.maximum(zstart_ref[z], 0), MOE_BLK)
            return pltpu.make_async_copy(zeros, xs_hbm.at[pl.ds(start, MOE_BLK)], zero_sem)

        for z in range(N_ZERO_FILLS):
            pl.when(zstart_ref[z] >= 0)(lambda z=z: fill(z).start())
        for z in range(N_ZERO_FILLS):
            pl.when(zstart_ref[z] >= 0)(lambda z=z: fill(z).wait())

    cp = pltpu.make_async_copy(dest_hbm.at[i], idx_smem, idx_sem)
    cp.start()
    cp.wait()

    def body(tok, carry):
        for k in range(2):
            d = idx_smem[2 * tok + k]
            pltpu.make_async_copy(h_ref.at[pl.ds(tok, 1)], xs_hbm.at[pl.ds(d, 1)], row_sem).start()
        return carry

    lax.fori_loop(0, ts, body, 0, unroll=ROW_DMA_UNROLL)
    for _ in range(2):
        pltpu.make_async_copy(h_ref, xs_hbm.at[pl.ds(0, ts)], row_sem).wait()


def _dispatch(h, dest, zstart, p_rows):
    n = h.shape[0]
    ts = MOE_TS
    grid_spec = pltpu.PrefetchScalarGridSpec(
        num_scalar_prefetch=1,
        grid=(n // ts,),
        in_specs=[pl.BlockSpec(memory_space=pl.ANY),
                  pl.BlockSpec((ts, D_MODEL), lambda i, z: (i, 0))],
        out_specs=pl.BlockSpec(memory_space=pl.ANY),
        scratch_shapes=[pltpu.SMEM((2 * ts,), jnp.int32),
                        pltpu.VMEM((MOE_BLK, D_MODEL), f32),
                        pltpu.SemaphoreType.DMA(()),
                        pltpu.SemaphoreType.DMA(()),
                        pltpu.SemaphoreType.DMA(())],
    )
    return pl.pallas_call(
        _dispatch_kernel,
        grid_spec=grid_spec,
        out_shape=jax.ShapeDtypeStruct((p_rows, D_MODEL), f32),
        compiler_params=_cparams(1),
        name="dispatch",
    )(zstart, dest.reshape(n // ts, 2 * ts), h)


def _experts_kernel(be_ref, bv_ref, xs_ref, wg_ref, wu_ref, wd_ref, ys_ref, wg_bf, wu_bf, wd_bf):
    i = pl.program_id(0)
    valid = bv_ref[i]

    @pl.when(jnp.logical_or(i == 0, be_ref[i] != be_ref[jnp.maximum(i - 1, 0)]))
    def _():
        wg_bf[...] = wg_ref[0].astype(bf16)
        wu_bf[...] = wu_ref[0].astype(bf16)
        wd_bf[...] = wd_ref[0].astype(bf16)

    @pl.when(valid > 0)
    def _():
        row = lax.broadcasted_iota(jnp.int32, xs_ref.shape, 0)
        x = jnp.where(row < valid, xs_ref[...], 0.0).astype(bf16)
        g = jnp.dot(x, wg_bf[...], preferred_element_type=f32)
        u = jnp.dot(x, wu_bf[...], preferred_element_type=f32)
        hmid = (g * jax.nn.sigmoid(g) * u).astype(bf16)
        ys_ref[...] = jnp.dot(hmid, wd_bf[...], preferred_element_type=f32)

    @pl.when(valid == 0)
    def _():
        ys_ref[...] = jnp.zeros_like(ys_ref)


def _experts(xs, blk_expert, blk_valid, wg, wu, wd, n_blocks):
    grid_spec = pltpu.PrefetchScalarGridSpec(
        num_scalar_prefetch=2,
        grid=(n_blocks,),
        in_specs=[pl.BlockSpec((MOE_BLK, D_MODEL), lambda i, be, bv: (i, 0)),
                  pl.BlockSpec((1, D_MODEL, D_EXPERT), lambda i, be, bv: (be[i], 0, 0)),
                  pl.BlockSpec((1, D_MODEL, D_EXPERT), lambda i, be, bv: (be[i], 0, 0)),
                  pl.BlockSpec((1, D_EXPERT, D_MODEL), lambda i, be, bv: (be[i], 0, 0))],
        out_specs=pl.BlockSpec((MOE_BLK, D_MODEL), lambda i, be, bv: (i, 0)),
        scratch_shapes=[pltpu.VMEM((D_MODEL, D_EXPERT), bf16),
                        pltpu.VMEM((D_MODEL, D_EXPERT), bf16),
                        pltpu.VMEM((D_EXPERT, D_MODEL), bf16)],
    )
    return pl.pallas_call(
        _experts_kernel,
        grid_spec=grid_spec,
        out_shape=jax.ShapeDtypeStruct(xs.shape, f32),
        compiler_params=_cparams(1),
        name="experts",
    )(blk_expert, blk_valid, xs, wg, wu, wd)


def _combine_kernel(dest_hbm, ys_hbm, h_ref, gate_ref, g_ref, b_ref, o_ref,
                    idx_smem, ybuf, idx_sem, row_sem):
    i = pl.program_id(0)
    ts = h_ref.shape[0]
    cp = pltpu.make_async_copy(dest_hbm.at[i], idx_smem, idx_sem)
    cp.start()
    cp.wait()

    def body(tok, carry):
        for k in range(2):
            d = idx_smem[2 * tok + k]
            pltpu.make_async_copy(ys_hbm.at[pl.ds(d, 1)], ybuf.at[k, pl.ds(tok, 1)], row_sem).start()
        return carry

    lax.fori_loop(0, ts, body, 0, unroll=ROW_DMA_UNROLL)
    for k in range(2):
        pltpu.make_async_copy(ys_hbm.at[pl.ds(0, ts)], ybuf.at[k], row_sem).wait()

    ffn = ybuf[0] * gate_ref[:, 0:1] + ybuf[1] * gate_ref[:, 1:2]
    o_ref[...] = _layer_norm(ALPHA * h_ref[...] + ffn, g_ref[...], b_ref[...])


def _combine(h, ys, dest, gate, ln_g, ln_b):
    n = h.shape[0]
    ts = MOE_TS
    full = lambda a: pl.BlockSpec(a.shape, lambda i: (0,) * a.ndim)
    return pl.pallas_call(
        _combine_kernel,
        grid=(n // ts,),
        in_specs=[pl.BlockSpec(memory_space=pl.ANY),
                  pl.BlockSpec(memory_space=pl.ANY),
                  pl.BlockSpec((ts, D_MODEL), lambda i: (i, 0)),
                  pl.BlockSpec((ts, 2), lambda i: (i, 0)),
                  full(ln_g), full(ln_b)],
        out_specs=pl.BlockSpec((ts, D_MODEL), lambda i: (i, 0)),
        out_shape=jax.ShapeDtypeStruct((n, D_MODEL), f32),
        scratch_shapes=[pltpu.SMEM((2 * ts,), jnp.int32),
                        pltpu.VMEM((2, ts, D_MODEL), f32),
                        pltpu.SemaphoreType.DMA(()),
                        pltpu.SemaphoreType.DMA(())],
        compiler_params=_cparams(1),
        name="combine",
    )(dest.reshape(n // ts, 2 * ts), ys, h, gate, ln_g, ln_b)


def _encode(x, p, qkv_meta, rest_meta, slabs):
    bsz, t, _ = x.shape
    n = bsz * t
    assert t % MOE_TS == 0 and t % TAIL_TM == 0 and t % PROJ_TM == 0
    xf = x.reshape(n, D_MODEL)
    qkv, rest = _in_proj(xf, p["w_in"], PROJ_TM)
    att = _attention(qkv, qkv_meta, slabs, bsz, t)
    h1, route, cnt = _mixer_tail(xf, att, rest, rest_meta, p, t)
    gate, dest, blk_expert, blk_valid, zstart, n_blocks = _route(route, cnt, n)
    xs = _dispatch(h1, dest, zstart, n_blocks * MOE_BLK)
    ys = _experts(xs, blk_expert, blk_valid, p["w_e_gate"], p["w_e_up"], p["w_e_down"], n_blocks)
    y = _combine(h1, ys, dest, gate, p["ln2_g"], p["ln2_b"])
    return y.reshape(bsz, t, D_MODEL)


def kernel(x_prompt, x_sample, meta_tokens, w_in, rpb, conv_w, conv_b, w_att_proj, w_conv_proj, w_out,
           ln1_g, ln1_b, w_router_group, w_router_expert, w_e_gate, w_e_up, w_e_down, ln2_g, ln2_b):
    w = w_in[0]
    a, c = D_QKV, D_CONV
    w_perm = jnp.concatenate([w[:, :a], w[:, a + c:a + 3 * c], w[:, a:a + c], w[:, a + 3 * c:]], axis=1)
    w_router = jnp.concatenate([w_router_group[0], w_router_expert[0]], axis=1)
    w_router = jnp.pad(w_router, ((0, 0), (0, LANES - w_router.shape[1])))
    row = lambda v: v[0].reshape(1, -1).astype(f32)
    p = {
        "w_in": w_perm.astype(bf16),
        "conv_w": conv_w[0].astype(f32), "conv_b": row(conv_b),
        "w_att_proj": w_att_proj[0].astype(bf16), "w_conv_proj": w_conv_proj[0].astype(bf16),
        "w_out": w_out[0].astype(bf16), "ln1_g": row(ln1_g), "ln1_b": row(ln1_b),
        "w_router": w_router.astype(bf16),
        "w_e_gate": w_e_gate[0], "w_e_up": w_e_up[0], "w_e_down": w_e_down[0],
        "ln2_g": row(ln2_g), "ln2_b": row(ln2_b),
    }
    qkv_meta, rest_meta = _in_proj(meta_tokens.astype(f32), p["w_in"], N_META)
    slabs = _bias_slabs(rpb[0])
    y_prompt = _encode(x_prompt, p, qkv_meta, rest_meta, slabs)
    y_sample = _encode(x_sample, p, qkv_meta, rest_meta, slabs)
    return (y_prompt, y_sample)
```

```python
import functools

import numpy as np
import jax
import jax.numpy as jnp
from jax import lax
from jax.experimental import pallas as pl
from jax.experimental.pallas import tpu as pltpu

D_MODEL = 1024
N_META = 16
GRID_W = 64
WIN_H = 8
WIN_W = 16
N_HEADS = 8
HEAD_DIM = 64
D_ATT = N_HEADS * HEAD_DIM
D_CONV = D_MODEL
N_GROUPS = 4
EXPERTS_PER_GROUP = 8
N_EXPERTS = N_GROUPS * EXPERTS_PER_GROUP
D_EXPERT = D_MODEL // 2
DEPTH = 1
ALPHA = (2.0 * DEPTH) ** 0.25
LN_EPS = 1e-5
D_QKV = 3 * D_ATT
D_REST = 3 * D_CONV + 2 * D_MODEL
D_IN_PROJ = D_QKV + D_REST

LANES = 128
HEADS_PER_VREG = LANES // HEAD_DIM
N_HEAD_PAIRS = N_HEADS // HEADS_PER_VREG
KEY_ROWS = WIN_H * GRID_W
NEG_BIG = -1e30

PROJ_TM = 512
PROJ_TN = 512
ATT_ROWS = 8
TAIL_TM = 512
MOE_TS = 512
MOE_BLK = 512
ROW_DMA_UNROLL = 8
N_ZERO_FILLS = 2 * N_EXPERTS
VMEM_LIMIT = 56 * 1024 * 1024

bf16 = jnp.bfloat16
f32 = jnp.float32


def _cparams(n_axes):
    return pltpu.CompilerParams(dimension_semantics=("arbitrary",) * n_axes,
                                vmem_limit_bytes=VMEM_LIMIT)


def _in_proj_kernel(x_ref, w_ref, qkv_ref, rest_ref):
    xb = x_ref[...].astype(bf16)
    for c in range(0, D_IN_PROJ, PROJ_TN):
        y = jnp.dot(xb, w_ref[:, c:c + PROJ_TN], preferred_element_type=f32)
        if c < D_ATT:
            y = y * (HEAD_DIM ** -0.5)
        if c < D_QKV:
            qkv_ref[:, c:c + PROJ_TN] = y.astype(bf16)
        else:
            rest_ref[:, c - D_QKV:c - D_QKV + PROJ_TN] = y.astype(bf16)


def _in_proj(x, w_bf, tm):
    n = x.shape[0]
    return pl.pallas_call(
        _in_proj_kernel,
        grid=(n // tm,),
        in_specs=[pl.BlockSpec((tm, D_MODEL), lambda i: (i, 0)),
                  pl.BlockSpec((D_MODEL, D_IN_PROJ), lambda i: (0, 0))],
        out_specs=[pl.BlockSpec((tm, D_QKV), lambda i: (i, 0)),
                   pl.BlockSpec((tm, D_REST), lambda i: (i, 0))],
        out_shape=[jax.ShapeDtypeStruct((n, D_QKV), bf16),
                   jax.ShapeDtypeStruct((n, D_REST), bf16)],
        compiler_params=_cparams(1),
        name="in_proj",
    )(x, w_bf)


def _bias_slabs(rpb):
    qc = np.arange(GRID_W)[:, None]
    kc = np.arange(GRID_W)[None, :]
    w_start = np.clip(qc - WIN_W // 2, 0, GRID_W - WIN_W)
    valid = (kc >= w_start) & (kc < w_start + WIN_W)
    dx_idx = np.clip(kc - qc + WIN_W - 1, 0, 2 * WIN_W - 2)
    t = rpb.astype(f32)[:, :, dx_idx]
    t = jnp.where(jnp.asarray(valid)[None, None], t, NEG_BIG)
    dy_idx = np.arange(WIN_H)[:, None] + np.arange(WIN_H)[None, :]
    s = t[:, dy_idx]
    s = jnp.transpose(s, (0, 1, 3, 2, 4)).reshape(N_HEADS, WIN_H, GRID_W, KEY_ROWS)
    s = s.reshape(N_HEAD_PAIRS, HEADS_PER_VREG, WIN_H, GRID_W, KEY_ROWS)
    s = jnp.transpose(s, (0, 2, 1, 3, 4))
    return s.reshape(N_HEAD_PAIRS, WIN_H, HEADS_PER_VREG * GRID_W, KEY_ROWS)


def _attention_kernel(q_ref, k_ref, v_ref, km_ref, vm_ref, bias_ref, o_ref, *, rows):
    j = pl.program_id(1)
    lane = lax.broadcasted_iota(jnp.int32, (GRID_W, LANES), 1)
    low = lane < HEAD_DIM
    nt = (((1,), (1,)), ((), ()))

    def row_body(i, carry):
        r = j * ATT_ROWS + i
        rs = jnp.clip(r - WIN_H // 2, 0, rows - WIN_H)
        dy0 = rs - r + (WIN_H - 1)
        q0 = pl.multiple_of(i * GRID_W, GRID_W)
        k0 = pl.multiple_of(rs * GRID_W, GRID_W)
        for pair in range(N_HEAD_PAIRS):
            cs = slice(pair * LANES, (pair + 1) * LANES)
            q2 = q_ref[pl.ds(q0, GRID_W), cs]
            zero = jnp.zeros_like(q2)
            qq = jnp.concatenate([jnp.where(low, q2, zero), jnp.where(low, zero, q2)], axis=0)
            k2 = jnp.concatenate([k_ref[pl.ds(k0, KEY_ROWS), cs], km_ref[:, cs]], axis=0)
            v2 = jnp.concatenate([v_ref[pl.ds(k0, KEY_ROWS), cs], vm_ref[:, cs]], axis=0)
            s = lax.dot_general(qq, k2, nt, preferred_element_type=f32)
            s = jnp.concatenate([s[:, :KEY_ROWS] + bias_ref[pair, dy0], s[:, KEY_ROWS:]], axis=1)
            m = jnp.max(s, axis=-1, keepdims=True)
            e = jnp.exp(s - m)
            l = jnp.sum(e, axis=-1, keepdims=True)
            o2 = jnp.dot(e.astype(bf16), v2, preferred_element_type=f32)
            o2 = o2 / l
            o = jnp.where(low, o2[:GRID_W], o2[GRID_W:])
            o_ref[pl.ds(q0, GRID_W), cs] = o.astype(bf16)
        return carry

    lax.fori_loop(0, ATT_ROWS, row_body, 0, unroll=2)


def _attention(qkv, qkv_meta, slabs, bsz, t):
    rows = t // GRID_W
    assert rows >= 2 * WIN_H and rows % ATT_ROWS == 0
    steps = rows // ATT_ROWS
    tq = ATT_ROWS * GRID_W
    return pl.pallas_call(
        functools.partial(_attention_kernel, rows=rows),
        grid=(bsz, steps),
        in_specs=[pl.BlockSpec((tq, D_ATT), lambda b, j: (b * steps + j, 0)),
                  pl.BlockSpec((t, D_ATT), lambda b, j: (b, 1)),
                  pl.BlockSpec((t, D_ATT), lambda b, j: (b, 2)),
                  pl.BlockSpec((N_META, D_ATT), lambda b, j: (0, 1)),
                  pl.BlockSpec((N_META, D_ATT), lambda b, j: (0, 2)),
                  pl.BlockSpec(slabs.shape, lambda b, j: (0, 0, 0, 0))],
        out_specs=pl.BlockSpec((tq, D_ATT), lambda b, j: (b * steps + j, 0)),
        out_shape=jax.ShapeDtypeStruct((bsz * t, D_ATT), bf16),
        compiler_params=_cparams(2),
        name="attention",
    )(qkv, qkv, qkv, qkv_meta, qkv_meta, slabs)


def _layer_norm(z, g, b):
    mu = jnp.mean(z, axis=-1, keepdims=True)
    d = z - mu
    var = jnp.mean(d * d, axis=-1, keepdims=True)
    return d * lax.rsqrt(var + LN_EPS) * g + b


def _route_tile(lg):
    tm = lg.shape[0]
    lane = lax.broadcasted_iota(jnp.int32, lg.shape, 1).astype(f32)
    neg = -jnp.inf
    first = lambda hit: jnp.min(jnp.where(hit, lane, float(LANES)), axis=-1, keepdims=True)

    gl = jnp.where(lane < N_GROUPS, lg, neg)
    gmax = jnp.max(gl, axis=-1, keepdims=True)
    grp = first(gl == gmax)
    pg_sel = 1.0 / jnp.sum(jnp.exp(gl - gmax), axis=-1, keepdims=True)

    e_lo = N_GROUPS + grp * EXPERTS_PER_GROUP
    el = jnp.where((lane >= e_lo) & (lane < e_lo + EXPERTS_PER_GROUP), lg, neg)
    t1 = jnp.max(el, axis=-1, keepdims=True)
    i1 = first(el == t1)
    el2 = jnp.where(lane == i1, neg, el)
    t2 = jnp.max(el2, axis=-1, keepdims=True)
    i2 = first(el2 == t2)
    r = jnp.exp(t2 - t1)
    g1 = pg_sel / (1.0 + r)
    g2 = pg_sel * r / (1.0 + r)
    e1 = i1 - N_GROUPS
    e2 = i2 - N_GROUPS

    hit1 = lane == e1
    hit2 = lane == e2
    onehot = jnp.where(hit1 | hit2, 1.0, 0.0)
    tri_r = lax.broadcasted_iota(jnp.int32, (tm, tm), 0)
    tri_c = lax.broadcasted_iota(jnp.int32, (tm, tm), 1)
    tri = jnp.where(tri_r > tri_c, 1.0, 0.0).astype(bf16)
    before = jnp.dot(tri, onehot.astype(bf16), preferred_element_type=f32)
    rank1 = jnp.sum(jnp.where(hit1, before, 0.0), axis=-1, keepdims=True)
    rank2 = jnp.sum(jnp.where(hit2, before, 0.0), axis=-1, keepdims=True)
    route = jnp.zeros_like(lg)
    for col, val in enumerate((e1, e2, g1, g2, rank1, rank2)):
        route = jnp.where(lane == col, val, route)
    return route, jnp.sum(onehot, axis=0, keepdims=True)


def _mixer_tail_kernel(x_ref, att_ref, rest_ref, prev_ref, next_ref, meta_ref,
                       convw_ref, convb_ref, wap_ref, wcp_ref, wout_ref, g_ref, b_ref, wr_ref,
                       h_ref, route_ref, cnt_ref, *, tiles_per_seq):
    i = pl.program_id(0)
    pos = i % tiles_per_seq
    tm = x_ref.shape[0]
    c0, c1, c2, c3, c4 = (k * D_MODEL for k in range(5))

    u = rest_ref[:, c0:c1].astype(f32) * rest_ref[:, c1:c2].astype(f32)
    last = prev_ref.shape[0] - 1
    u_prev = prev_ref[last:, c0:c1].astype(f32) * prev_ref[last:, c1:c2].astype(f32)
    u_meta = meta_ref[N_META - 1:, c0:c1].astype(f32) * meta_ref[N_META - 1:, c1:c2].astype(f32)
    u_next = next_ref[:1, c0:c1].astype(f32) * next_ref[:1, c1:c2].astype(f32)
    u_prev = jnp.where(pos == 0, u_meta, u_prev)
    u_next = jnp.where(pos == tiles_per_seq - 1, jnp.zeros_like(u_next), u_next)
    row = lax.broadcasted_iota(jnp.int32, (tm, D_CONV), 0)
    u_m1 = jnp.where(row == 0, u_prev, pltpu.roll(u, 1, axis=0))
    u_p1 = jnp.where(row == tm - 1, u_next, pltpu.roll(u, tm - 1, axis=0))
    s = u_m1 * convw_ref[0:1, :] + u * convw_ref[1:2, :] + u_p1 * convw_ref[2:3, :] + convb_ref[...]
    conv_in = (rest_ref[:, c2:c3].astype(f32) * s).astype(bf16)
    cv = jnp.dot(conv_in, wcp_ref[...], preferred_element_type=f32)
    ap = jnp.dot(att_ref[...], wap_ref[...], preferred_element_type=f32)
    merged = (jax.nn.sigmoid(rest_ref[:, c3:c4].astype(f32)) * ap
              + jax.nn.sigmoid(rest_ref[:, c4:].astype(f32)) * cv)
    mix = jnp.dot(merged.astype(bf16), wout_ref[...], preferred_element_type=f32)
    h = _layer_norm(ALPHA * x_ref[...] + mix, g_ref[...], b_ref[...])
    h_ref[...] = h
    logits = jnp.dot(h.astype(bf16), wr_ref[...], preferred_element_type=f32)
    route, count = _route_tile(logits)
    route_ref[...] = route
    cnt_ref[0] = count


def _mixer_tail(x, att, rest, rest_meta, p, t):
    n = x.shape[0]
    tm = TAIL_TM
    halo = 16
    hb = tm // halo
    n_halo = n // halo
    tiles_per_seq = t // tm
    full = lambda a: pl.BlockSpec(a.shape, lambda i: (0,) * a.ndim)
    return pl.pallas_call(
        functools.partial(_mixer_tail_kernel, tiles_per_seq=tiles_per_seq),
        grid=(n // tm,),
        in_specs=[pl.BlockSpec((tm, D_MODEL), lambda i: (i, 0)),
                  pl.BlockSpec((tm, D_ATT), lambda i: (i, 0)),
                  pl.BlockSpec((tm, D_REST), lambda i: (i, 0)),
                  pl.BlockSpec((halo, 2 * D_CONV), lambda i: (jnp.maximum(i * hb - 1, 0), 0)),
                  pl.BlockSpec((halo, 2 * D_CONV), lambda i: (jnp.minimum((i + 1) * hb, n_halo - 1), 0)),
                  full(rest_meta), full(p["conv_w"]), full(p["conv_b"]), full(p["w_att_proj"]),
                  full(p["w_conv_proj"]), full(p["w_out"]), full(p["ln1_g"]), full(p["ln1_b"]),
                  full(p["w_router"])],
        out_specs=[pl.BlockSpec((tm, D_MODEL), lambda i: (i, 0)),
                   pl.BlockSpec((tm, LANES), lambda i: (i, 0)),
                   pl.BlockSpec((1, 1, LANES), lambda i: (i, 0, 0))],
        out_shape=[jax.ShapeDtypeStruct((n, D_MODEL), f32),
                   jax.ShapeDtypeStruct((n, LANES), f32),
                   jax.ShapeDtypeStruct((n // tm, 1, LANES), f32)],
        compiler_params=_cparams(1),
        name="mixer_tail",
    )(x, att, rest, rest, rest, rest_meta, p["conv_w"], p["conv_b"], p["w_att_proj"],
      p["w_conv_proj"], p["w_out"], p["ln1_g"], p["ln1_b"], p["w_router"])


def _route(route, cnt, n):
    n_tiles = n // TAIL_TM
    expert = route[:, 0:2].astype(jnp.int32)
    gate = route[:, 2:4]
    rank = route[:, 4:6].astype(jnp.int32)
    tile_cnt = cnt[:, 0, :N_EXPERTS].astype(jnp.int32)
    tile_off = jnp.cumsum(tile_cnt, axis=0) - tile_cnt
    counts = jnp.sum(tile_cnt, axis=0)
    padded = (counts + MOE_BLK - 1) // MOE_BLK * MOE_BLK
    pend = jnp.cumsum(padded)
    pstart = pend - padded
    base = pstart[None, :] + tile_off
    hit = expert.reshape(n_tiles, TAIL_TM, 2, 1) == jnp.arange(N_EXPERTS, dtype=jnp.int32)
    base_sel = jnp.sum(jnp.where(hit, base[:, None, None, :], 0), axis=-1)
    dest = (base_sel.reshape(n, 2) + rank).reshape(2 * n)

    n_blocks = (2 * n + N_EXPERTS * (MOE_BLK - 1) + MOE_BLK - 1) // MOE_BLK
    blk_start = jnp.arange(n_blocks, dtype=jnp.int32) * MOE_BLK
    blk_expert = jnp.sum((pend[None, :] <= blk_start[:, None]).astype(jnp.int32), axis=1)
    blk_expert = jnp.minimum(blk_expert, N_EXPERTS - 1)
    blk_valid = jnp.clip((pstart + counts)[blk_expert] - blk_start, 0, MOE_BLK)
    blk_valid = jnp.where(blk_start < pend[-1], blk_valid, 0).astype(jnp.int32)
    tail = (pend[-1] // MOE_BLK + jnp.arange(N_ZERO_FILLS - N_EXPERTS, dtype=jnp.int32)) * MOE_BLK
    zstart = jnp.concatenate([jnp.where(padded > counts, pend - MOE_BLK, -1),
                              jnp.where(tail < n_blocks * MOE_BLK, tail, -1)]).astype(jnp.int32)
    return gate, dest, blk_expert, blk_valid, zstart, n_blocks


def _dispatch_kernel(zstart_ref, dest_hbm, h_ref, xs_hbm, idx_smem, zeros, idx_sem, row_sem, zero_sem):
    i = pl.program_id(0)
    ts = h_ref.shape[0]

    @pl.when(i == 0)
    def _():
        zeros[...] = jnp.zeros_like(zeros)

        def fill(z):
            start = pl.multiple_of(jnp.maximum(zstart_ref[z], 0), MOE_BLK)
            return pltpu.make_async_copy(zeros, xs_hbm.at[pl.ds(start, MOE_BLK)], zero_sem)

        for z in range(N_ZERO_FILLS):
            pl.when(zstart_ref[z] >= 0)(lambda z=z: fill(z).start())
        for z in range(N_ZERO_FILLS):
            pl.when(zstart_ref[z] >= 0)(lambda z=z: fill(z).wait())

    cp = pltpu.make_async_copy(dest_hbm.at[i], idx_smem, idx_sem)
    cp.start()
    cp.wait()

    def body(tok, carry):
        for k in range(2):
            d = idx_smem[2 * tok + k]
            pltpu.make_async_copy(h_ref.at[pl.ds(tok, 1)], xs_hbm.at[pl.ds(d, 1)], row_sem).start()
        return carry

    lax.fori_loop(0, ts, body, 0, unroll=ROW_DMA_UNROLL)
    for _ in range(2):
        pltpu.make_async_copy(h_ref, xs_hbm.at[pl.ds(0, ts)], row_sem).wait()


def _dispatch(h, dest, zstart, p_rows):
    n = h.shape[0]
    ts = MOE_TS
    grid_spec = pltpu.PrefetchScalarGridSpec(
        num_scalar_prefetch=1,
        grid=(n // ts,),
        in_specs=[pl.BlockSpec(memory_space=pl.ANY),
                  pl.BlockSpec((ts, D_MODEL), lambda i, z: (i, 0))],
        out_specs=pl.BlockSpec(memory_space=pl.ANY),
        scratch_shapes=[pltpu.SMEM((2 * ts,), jnp.int32),
                        pltpu.VMEM((MOE_BLK, D_MODEL), f32),
                        pltpu.SemaphoreType.DMA(()),
                        pltpu.SemaphoreType.DMA(()),
                        pltpu.SemaphoreType.DMA(())],
    )
    return pl.pallas_call(
        _dispatch_kernel,
        grid_spec=grid_spec,
        out_shape=jax.ShapeDtypeStruct((p_rows, D_MODEL), f32),
        compiler_params=_cparams(1),
        name="dispatch",
    )(zstart, dest.reshape(n // ts, 2 * ts), h)


def _experts_kernel(be_ref, bv_ref, xs_ref, wg_ref, wu_ref, wd_ref, ys_ref, wg_bf, wu_bf, wd_bf):
    i = pl.program_id(0)
    valid = bv_ref[i]

    @pl.when(jnp.logical_or(i == 0, be_ref[i] != be_ref[jnp.maximum(i - 1, 0)]))
    def _():
        wg_bf[...] = wg_ref[0].astype(bf16)
        wu_bf[...] = wu_ref[0].astype(bf16)
        wd_bf[...] = wd_ref[0].astype(bf16)

    @pl.when(valid > 0)
    def _():
        row = lax.broadcasted_iota(jnp.int32, xs_ref.shape, 0)
        x = jnp.where(row < valid, xs_ref[...], 0.0).astype(bf16)
        g = jnp.dot(x, wg_bf[...], preferred_element_type=f32)
        u = jnp.dot(x, wu_bf[...], preferred_element_type=f32)
        hmid = (g * jax.nn.sigmoid(g) * u).astype(bf16)
        ys_ref[...] = jnp.dot(hmid, wd_bf[...], preferred_element_type=f32)

    @pl.when(valid == 0)
    def _():
        ys_ref[...] = jnp.zeros_like(ys_ref)


def _experts(xs, blk_expert, blk_valid, wg, wu, wd, n_blocks):
    grid_spec = pltpu.PrefetchScalarGridSpec(
        num_scalar_prefetch=2,
        grid=(n_blocks,),
        in_specs=[pl.BlockSpec((MOE_BLK, D_MODEL), lambda i, be, bv: (i, 0)),
                  pl.BlockSpec((1, D_MODEL, D_EXPERT), lambda i, be, bv: (be[i], 0, 0)),
                  pl.BlockSpec((1, D_MODEL, D_EXPERT), lambda i, be, bv: (be[i], 0, 0)),
                  pl.BlockSpec((1, D_EXPERT, D_MODEL), lambda i, be, bv: (be[i], 0, 0))],
        out_specs=pl.BlockSpec((MOE_BLK, D_MODEL), lambda i, be, bv: (i, 0)),
        scratch_shapes=[pltpu.VMEM((D_MODEL, D_EXPERT), bf16),
                        pltpu.VMEM((D_MODEL, D_EXPERT), bf16),
                        pltpu.VMEM((D_EXPERT, D_MODEL), bf16)],
    )
    return pl.pallas_call(
        _experts_kernel,
        grid_spec=grid_spec,
        out_shape=jax.ShapeDtypeStruct(xs.shape, f32),
        compiler_params=_cparams(1),
        name="experts",
    )(blk_expert, blk_valid, xs, wg, wu, wd)


def _combine_kernel(dest_hbm, ys_hbm, h_ref, gate_ref, g_ref, b_ref, o_ref,
                    idx_smem, ybuf, idx_sem, row_sem):
    i = pl.program_id(0)
    ts = h_ref.shape[0]
    cp = pltpu.make_async_copy(dest_hbm.at[i], idx_smem, idx_sem)
    cp.start()
    cp.wait()

    def body(tok, carry):
        for k in range(2):
            d = idx_smem[2 * tok + k]
            pltpu.make_async_copy(ys_hbm.at[pl.ds(d, 1)], ybuf.at[k, pl.ds(tok, 1)], row_sem).start()
        return carry

    lax.fori_loop(0, ts, body, 0, unroll=ROW_DMA_UNROLL)
    for k in range(2):
        pltpu.make_async_copy(ys_hbm.at[pl.ds(0, ts)], ybuf.at[k], row_sem).wait()

    ffn = ybuf[0] * gate_ref[:, 0:1] + ybuf[1] * gate_ref[:, 1:2]
    o_ref[...] = _layer_norm(ALPHA * h_ref[...] + ffn, g_ref[...], b_ref[...])


def _combine(h, ys, dest, gate, ln_g, ln_b):
    n = h.shape[0]
    ts = MOE_TS
    full = lambda a: pl.BlockSpec(a.shape, lambda i: (0,) * a.ndim)
    return pl.pallas_call(
        _combine_kernel,
        grid=(n // ts,),
        in_specs=[pl.BlockSpec(memory_space=pl.ANY),
                  pl.BlockSpec(memory_space=pl.ANY),
                  pl.BlockSpec((ts, D_MODEL), lambda i: (i, 0)),
                  pl.BlockSpec((ts, 2), lambda i: (i, 0)),
                  full(ln_g), full(ln_b)],
        out_specs=pl.BlockSpec((ts, D_MODEL), lambda i: (i, 0)),
        out_shape=jax.ShapeDtypeStruct((n, D_MODEL), f32),
        scratch_shapes=[pltpu.SMEM((2 * ts,), jnp.int32),
                        pltpu.VMEM((2, ts, D_MODEL), f32),
                        pltpu.SemaphoreType.DMA(()),
                        pltpu.SemaphoreType.DMA(())],
        compiler_params=_cparams(1),
        name="combine",
    )(dest.reshape(n // ts, 2 * ts), ys, h, gate, ln_g, ln_b)


def _encode(x, p, qkv_meta, rest_meta, slabs):
    bsz, t, _ = x.shape
    n = bsz * t
    assert t % MOE_TS == 0 and t % TAIL_TM == 0 and t % PROJ_TM == 0
    xf = x.reshape(n, D_MODEL)
    qkv, rest = _in_proj(xf, p["w_in"], PROJ_TM)
    att = _attention(qkv, qkv_meta, slabs, bsz, t)
    h1, route, cnt = _mixer_tail(xf, att, rest, rest_meta, p, t)
    gate, dest, blk_expert, blk_valid, zstart, n_blocks = _route(route, cnt, n)
    xs = _dispatch(h1, dest, zstart, n_blocks * MOE_BLK)
    ys = _experts(xs, blk_expert, blk_valid, p["w_e_gate"], p["w_e_up"], p["w_e_down"], n_blocks)
    y = _combine(h1, ys, dest, gate, p["ln2_g"], p["ln2_b"])
    return y.reshape(bsz, t, D_MODEL)


def kernel(x_prompt, x_sample, meta_tokens, w_in, rpb, conv_w, conv_b, w_att_proj, w_conv_proj, w_out,
           ln1_g, ln1_b, w_router_group, w_router_expert, w_e_gate, w_e_up, w_e_down, ln2_g, ln2_b):
    w = w_in[0]
    a, c = D_QKV, D_CONV
    w_perm = jnp.concatenate([w[:, :a], w[:, a + c:a + 3 * c], w[:, a:a + c], w[:, a + 3 * c:]], axis=1)
    w_router = jnp.concatenate([w_router_group[0], w_router_expert[0]], axis=1)
    w_router = jnp.pad(w_router, ((0, 0), (0, LANES - w_router.shape[1])))
    row = lambda v: v[0].reshape(1, -1).astype(f32)
    p = {
        "w_in": w_perm.astype(bf16),
        "conv_w": conv_w[0].astype(f32), "conv_b": row(conv_b),
        "w_att_proj": w_att_proj[0].astype(bf16), "w_conv_proj": w_conv_proj[0].astype(bf16),
        "w_out": w_out[0].astype(bf16), "ln1_g": row(ln1_g), "ln1_b": row(ln1_b),
        "w_router": w_router.astype(bf16),
        "w_e_gate": w_e_gate[0], "w_e_up": w_e_up[0], "w_e_down": w_e_down[0],
        "ln2_g": row(ln2_g), "ln2_b": row(ln2_b),
    }
    qkv_meta, rest_meta = _in_proj(meta_tokens.astype(f32), p["w_in"], N_META)
    slabs = _bias_slabs(rpb[0])
    y_prompt = _encode(x_prompt, p, qkv_meta, rest_meta, slabs)
    y_sample = _encode(x_sample, p, qkv_meta, rest_meta, slabs)
    return (y_prompt, y_sample)
```

```python
import functools

import numpy as np
import jax
import jax.numpy as jnp
from jax import lax
from jax.experimental import pallas as pl
from jax.experimental.pallas import tpu as pltpu
from jax.experimental.pallas import tpu_sc as plsc

D_MODEL = 1024
N_META = 16
GRID_W = 64
WIN_H = 8
WIN_W = 16
N_HEADS = 8
HEAD_DIM = 64
D_ATT = N_HEADS * HEAD_DIM
D_CONV = D_MODEL
N_GROUPS = 4
EXPERTS_PER_GROUP = 8
N_EXPERTS = N_GROUPS * EXPERTS_PER_GROUP
D_EXPERT = D_MODEL // 2
DEPTH = 1
ALPHA = (2.0 * DEPTH) ** 0.25
LN_EPS = 1e-5
D_QKV = 3 * D_ATT
D_REST = 3 * D_CONV + 2 * D_MODEL
D_IN_PROJ = D_QKV + D_REST

LANES = 128
HEADS_PER_VREG = LANES // HEAD_DIM
N_HEAD_PAIRS = N_HEADS // HEADS_PER_VREG
KEY_ROWS = WIN_H * GRID_W
NEG_BIG = -1e30

PROJ_TM = 512
PROJ_TN = 512
ATT_ROWS = 8
TAIL_TM = 512
MOE_TS = 512
MOE_BLK = 512
ROW_DMA_UNROLL = 8
N_ZERO_FILLS = 2 * N_EXPERTS
VMEM_LIMIT = 56 * 1024 * 1024

bf16 = jnp.bfloat16
f32 = jnp.float32


def _cparams(n_axes):
    return pltpu.CompilerParams(dimension_semantics=("arbitrary",) * n_axes,
                                vmem_limit_bytes=VMEM_LIMIT)


def _in_proj_kernel(x_ref, w_ref, qkv_ref, rest_ref):
    xb = x_ref[...].astype(bf16)
    for c in range(0, D_IN_PROJ, PROJ_TN):
        y = jnp.dot(xb, w_ref[:, c:c + PROJ_TN], preferred_element_type=f32)
        if c < D_ATT:
            y = y * (HEAD_DIM ** -0.5)
        if c < D_QKV:
            qkv_ref[:, c:c + PROJ_TN] = y.astype(bf16)
        else:
            rest_ref[:, c - D_QKV:c - D_QKV + PROJ_TN] = y.astype(bf16)


def _in_proj(x, w_bf, tm):
    n = x.shape[0]
    return pl.pallas_call(
        _in_proj_kernel,
        grid=(n // tm,),
        in_specs=[pl.BlockSpec((tm, D_MODEL), lambda i: (i, 0)),
                  pl.BlockSpec((D_MODEL, D_IN_PROJ), lambda i: (0, 0))],
        out_specs=[pl.BlockSpec((tm, D_QKV), lambda i: (i, 0)),
                   pl.BlockSpec((tm, D_REST), lambda i: (i, 0))],
        out_shape=[jax.ShapeDtypeStruct((n, D_QKV), bf16),
                   jax.ShapeDtypeStruct((n, D_REST), bf16)],
        compiler_params=_cparams(1),
        name="in_proj",
    )(x, w_bf)


def _bias_slabs(rpb):
    qc = np.arange(GRID_W)[:, None]
    kc = np.arange(GRID_W)[None, :]
    w_start = np.clip(qc - WIN_W // 2, 0, GRID_W - WIN_W)
    valid = (kc >= w_start) & (kc < w_start + WIN_W)
    dx_idx = np.clip(kc - qc + WIN_W - 1, 0, 2 * WIN_W - 2)
    t = rpb.astype(f32)[:, :, dx_idx]
    t = jnp.where(jnp.asarray(valid)[None, None], t, NEG_BIG)
    dy_idx = np.arange(WIN_H)[:, None] + np.arange(WIN_H)[None, :]
    s = t[:, dy_idx]
    s = jnp.transpose(s, (0, 1, 3, 2, 4)).reshape(N_HEADS, WIN_H, GRID_W, KEY_ROWS)
    s = s.reshape(N_HEAD_PAIRS, HEADS_PER_VREG, WIN_H, GRID_W, KEY_ROWS)
    s = jnp.transpose(s, (0, 2, 1, 3, 4))
    return s.reshape(N_HEAD_PAIRS, WIN_H, HEADS_PER_VREG * GRID_W, KEY_ROWS)


def _attention_kernel(q_ref, k_ref, v_ref, km_ref, vm_ref, bias_ref, o_ref, *, rows):
    j = pl.program_id(1)
    lane = lax.broadcasted_iota(jnp.int32, (GRID_W, LANES), 1)
    low = lane < HEAD_DIM
    nt = (((1,), (1,)), ((), ()))

    def row_body(i, carry):
        r = j * ATT_ROWS + i
        rs = jnp.clip(r - WIN_H // 2, 0, rows - WIN_H)
        dy0 = rs - r + (WIN_H - 1)
        q0 = pl.multiple_of(i * GRID_W, GRID_W)
        k0 = pl.multiple_of(rs * GRID_W, GRID_W)
        for pair in range(N_HEAD_PAIRS):
            cs = slice(pair * LANES, (pair + 1) * LANES)
            q2 = q_ref[pl.ds(q0, GRID_W), cs]
            zero = jnp.zeros_like(q2)
            qq = jnp.concatenate([jnp.where(low, q2, zero), jnp.where(low, zero, q2)], axis=0)
            k2 = jnp.concatenate([k_ref[pl.ds(k0, KEY_ROWS), cs], km_ref[:, cs]], axis=0)
            v2 = jnp.concatenate([v_ref[pl.ds(k0, KEY_ROWS), cs], vm_ref[:, cs]], axis=0)
            s = lax.dot_general(qq, k2, nt, preferred_element_type=f32)
            s = jnp.concatenate([s[:, :KEY_ROWS] + bias_ref[pair, dy0], s[:, KEY_ROWS:]], axis=1)
            m = jnp.max(s, axis=-1, keepdims=True)
            e = jnp.exp(s - m)
            l = jnp.sum(e, axis=-1, keepdims=True)
            o2 = jnp.dot(e.astype(bf16), v2, preferred_element_type=f32)
            o2 = o2 / l
            o = jnp.where(low, o2[:GRID_W], o2[GRID_W:])
            o_ref[pl.ds(q0, GRID_W), cs] = o.astype(bf16)
        return carry

    lax.fori_loop(0, ATT_ROWS, row_body, 0, unroll=2)


def _attention(qkv, qkv_meta, slabs, bsz, t):
    rows = t // GRID_W
    assert rows >= 2 * WIN_H and rows % ATT_ROWS == 0
    steps = rows // ATT_ROWS
    tq = ATT_ROWS * GRID_W
    return pl.pallas_call(
        functools.partial(_attention_kernel, rows=rows),
        grid=(bsz, steps),
        in_specs=[pl.BlockSpec((tq, D_ATT), lambda b, j: (b * steps + j, 0)),
                  pl.BlockSpec((t, D_ATT), lambda b, j: (b, 1)),
                  pl.BlockSpec((t, D_ATT), lambda b, j: (b, 2)),
                  pl.BlockSpec((N_META, D_ATT), lambda b, j: (0, 1)),
                  pl.BlockSpec((N_META, D_ATT), lambda b, j: (0, 2)),
                  pl.BlockSpec(slabs.shape, lambda b, j: (0, 0, 0, 0))],
        out_specs=pl.BlockSpec((tq, D_ATT), lambda b, j: (b * steps + j, 0)),
        out_shape=jax.ShapeDtypeStruct((bsz * t, D_ATT), bf16),
        compiler_params=_cparams(2),
        name="attention",
    )(qkv, qkv, qkv, qkv_meta, qkv_meta, slabs)


def _layer_norm(z, g, b):
    mu = jnp.mean(z, axis=-1, keepdims=True)
    d = z - mu
    var = jnp.mean(d * d, axis=-1, keepdims=True)
    return d * lax.rsqrt(var + LN_EPS) * g + b


def _route_tile(lg):
    tm = lg.shape[0]
    lane = lax.broadcasted_iota(jnp.int32, lg.shape, 1).astype(f32)
    neg = -jnp.inf
    first = lambda hit: jnp.min(jnp.where(hit, lane, float(LANES)), axis=-1, keepdims=True)

    gl = jnp.where(lane < N_GROUPS, lg, neg)
    gmax = jnp.max(gl, axis=-1, keepdims=True)
    grp = first(gl == gmax)
    pg_sel = 1.0 / jnp.sum(jnp.exp(gl - gmax), axis=-1, keepdims=True)

    e_lo = N_GROUPS + grp * EXPERTS_PER_GROUP
    el = jnp.where((lane >= e_lo) & (lane < e_lo + EXPERTS_PER_GROUP), lg, neg)
    t1 = jnp.max(el, axis=-1, keepdims=True)
    i1 = first(el == t1)
    el2 = jnp.where(lane == i1, neg, el)
    t2 = jnp.max(el2, axis=-1, keepdims=True)
    i2 = first(el2 == t2)
    r = jnp.exp(t2 - t1)
    g1 = pg_sel / (1.0 + r)
    g2 = pg_sel * r / (1.0 + r)
    e1 = i1 - N_GROUPS
    e2 = i2 - N_GROUPS

    hit1 = lane == e1
    hit2 = lane == e2
    onehot = jnp.where(hit1 | hit2, 1.0, 0.0)
    tri_r = lax.broadcasted_iota(jnp.int32, (tm, tm), 0)
    tri_c = lax.broadcasted_iota(jnp.int32, (tm, tm), 1)
    tri = jnp.where(tri_r > tri_c, 1.0, 0.0).astype(bf16)
    before = jnp.dot(tri, onehot.astype(bf16), preferred_element_type=f32)
    rank1 = jnp.sum(jnp.where(hit1, before, 0.0), axis=-1, keepdims=True)
    rank2 = jnp.sum(jnp.where(hit2, before, 0.0), axis=-1, keepdims=True)
    route = jnp.zeros_like(lg)
    for col, val in enumerate((e1, e2, g1, g2, rank1, rank2)):
        route = jnp.where(lane == col, val, route)
    return route, jnp.sum(onehot, axis=0, keepdims=True)


def _mixer_tail_kernel(x_ref, att_ref, rest_ref, prev_ref, next_ref, meta_ref,
                       convw_ref, convb_ref, wap_ref, wcp_ref, wout_ref, g_ref, b_ref, wr_ref,
                       h_ref, route_ref, cnt_ref, *, tiles_per_seq):
    i = pl.program_id(0)
    pos = i % tiles_per_seq
    tm = x_ref.shape[0]
    c0, c1, c2, c3, c4 = (k * D_MODEL for k in range(5))

    u = rest_ref[:, c0:c1].astype(f32) * rest_ref[:, c1:c2].astype(f32)
    last = prev_ref.shape[0] - 1
    u_prev = prev_ref[last:, c0:c1].astype(f32) * prev_ref[last:, c1:c2].astype(f32)
    u_meta = meta_ref[N_META - 1:, c0:c1].astype(f32) * meta_ref[N_META - 1:, c1:c2].astype(f32)
    u_next = next_ref[:1, c0:c1].astype(f32) * next_ref[:1, c1:c2].astype(f32)
    u_prev = jnp.where(pos == 0, u_meta, u_prev)
    u_next = jnp.where(pos == tiles_per_seq - 1, jnp.zeros_like(u_next), u_next)
    row = lax.broadcasted_iota(jnp.int32, (tm, D_CONV), 0)
    u_m1 = jnp.where(row == 0, u_prev, pltpu.roll(u, 1, axis=0))
    u_p1 = jnp.where(row == tm - 1, u_next, pltpu.roll(u, tm - 1, axis=0))
    s = u_m1 * convw_ref[0:1, :] + u * convw_ref[1:2, :] + u_p1 * convw_ref[2:3, :] + convb_ref[...]
    conv_in = (rest_ref[:, c2:c3].astype(f32) * s).astype(bf16)
    cv = jnp.dot(conv_in, wcp_ref[...], preferred_element_type=f32)
    ap = jnp.dot(att_ref[...], wap_ref[...], preferred_element_type=f32)
    merged = (jax.nn.sigmoid(rest_ref[:, c3:c4].astype(f32)) * ap
              + jax.nn.sigmoid(rest_ref[:, c4:].astype(f32)) * cv)
    mix = jnp.dot(merged.astype(bf16), wout_ref[...], preferred_element_type=f32)
    h = _layer_norm(ALPHA * x_ref[...] + mix, g_ref[...], b_ref[...])
    h_ref[...] = h
    logits = jnp.dot(h.astype(bf16), wr_ref[...], preferred_element_type=f32)
    route, count = _route_tile(logits)
    route_ref[...] = route
    cnt_ref[0] = count


def _mixer_tail(x, att, rest, rest_meta, p, t):
    n = x.shape[0]
    tm = TAIL_TM
    halo = 16
    hb = tm // halo
    n_halo = n // halo
    tiles_per_seq = t // tm
    full = lambda a: pl.BlockSpec(a.shape, lambda i: (0,) * a.ndim)
    return pl.pallas_call(
        functools.partial(_mixer_tail_kernel, tiles_per_seq=tiles_per_seq),
        grid=(n // tm,),
        in_specs=[pl.BlockSpec((tm, D_MODEL), lambda i: (i, 0)),
                  pl.BlockSpec((tm, D_ATT), lambda i: (i, 0)),
                  pl.BlockSpec((tm, D_REST), lambda i: (i, 0)),
                  pl.BlockSpec((halo, 2 * D_CONV), lambda i: (jnp.maximum(i * hb - 1, 0), 0)),
                  pl.BlockSpec((halo, 2 * D_CONV), lambda i: (jnp.minimum((i + 1) * hb, n_halo - 1), 0)),
                  full(rest_meta), full(p["conv_w"]), full(p["conv_b"]), full(p["w_att_proj"]),
                  full(p["w_conv_proj"]), full(p["w_out"]), full(p["ln1_g"]), full(p["ln1_b"]),
                  full(p["w_router"])],
        out_specs=[pl.BlockSpec((tm, D_MODEL), lambda i: (i, 0)),
                   pl.BlockSpec((tm, LANES), lambda i: (i, 0)),
                   pl.BlockSpec((1, 1, LANES), lambda i: (i, 0, 0))],
        out_shape=[jax.ShapeDtypeStruct((n, D_MODEL), f32),
                   jax.ShapeDtypeStruct((n, LANES), f32),
                   jax.ShapeDtypeStruct((n // tm, 1, LANES), f32)],
        compiler_params=_cparams(1),
        name="mixer_tail",
    )(x, att, rest, rest, rest, rest_meta, p["conv_w"], p["conv_b"], p["w_att_proj"],
      p["w_conv_proj"], p["w_out"], p["ln1_g"], p["ln1_b"], p["w_router"])


def _route(route, cnt, n):
    n_tiles = n // TAIL_TM
    expert = route[:, 0:2].astype(jnp.int32)
    gate = route[:, 2:4]
    rank = route[:, 4:6].astype(jnp.int32)
    tile_cnt = cnt[:, 0, :N_EXPERTS].astype(jnp.int32)
    tile_off = jnp.cumsum(tile_cnt, axis=0) - tile_cnt
    counts = jnp.sum(tile_cnt, axis=0)
    padded = (counts + MOE_BLK - 1) // MOE_BLK * MOE_BLK
    pend = jnp.cumsum(padded)
    pstart = pend - padded
    base = pstart[None, :] + tile_off
    hit = expert.reshape(n_tiles, TAIL_TM, 2, 1) == jnp.arange(N_EXPERTS, dtype=jnp.int32)
    base_sel = jnp.sum(jnp.where(hit, base[:, None, None, :], 0), axis=-1)
    dest = (base_sel.reshape(n, 2) + rank).reshape(2 * n)

    n_blocks = (2 * n + N_EXPERTS * (MOE_BLK - 1) + MOE_BLK - 1) // MOE_BLK
    blk_start = jnp.arange(n_blocks, dtype=jnp.int32) * MOE_BLK
    blk_expert = jnp.sum((pend[None, :] <= blk_start[:, None]).astype(jnp.int32), axis=1)
    blk_expert = jnp.minimum(blk_expert, N_EXPERTS - 1)
    blk_valid = jnp.clip((pstart + counts)[blk_expert] - blk_start, 0, MOE_BLK)
    blk_valid = jnp.where(blk_start < pend[-1], blk_valid, 0).astype(jnp.int32)
    tail = (pend[-1] // MOE_BLK + jnp.arange(N_ZERO_FILLS - N_EXPERTS, dtype=jnp.int32)) * MOE_BLK
    zstart = jnp.concatenate([jnp.where(padded > counts, pend - MOE_BLK, -1),
                              jnp.where(tail < n_blocks * MOE_BLK, tail, -1)]).astype(jnp.int32)
    return gate, dest, blk_expert, blk_valid, zstart, n_blocks


def _dispatch_kernel(zstart_ref, dest_hbm, h_ref, xs_hbm, idx_smem, zeros, idx_sem, row_sem, zero_sem):
    i = pl.program_id(0)
    ts = h_ref.shape[0]

    @pl.when(i == 0)
    def _():
        zeros[...] = jnp.zeros_like(zeros)

        def fill(z):
            start = pl.multiple_of(jnp.maximum(zstart_ref[z], 0), MOE_BLK)
            return pltpu.make_async_copy(zeros, xs_hbm.at[pl.ds(start, MOE_BLK)], zero_sem)

        for z in range(N_ZERO_FILLS):
            pl.when(zstart_ref[z] >= 0)(lambda z=z: fill(z).start())
        for z in range(N_ZERO_FILLS):
            pl.when(zstart_ref[z] >= 0)(lambda z=z: fill(z).wait())

    cp = pltpu.make_async_copy(dest_hbm.at[i], idx_smem, idx_sem)
    cp.start()
    cp.wait()

    def body(tok, carry):
        for k in range(2):
            d = idx_smem[2 * tok + k]
            pltpu.make_async_copy(h_ref.at[pl.ds(tok, 1)], xs_hbm.at[pl.ds(d, 1)], row_sem).start()
        return carry

    lax.fori_loop(0, ts, body, 0, unroll=ROW_DMA_UNROLL)
    for _ in range(2):
        pltpu.make_async_copy(h_ref, xs_hbm.at[pl.ds(0, ts)], row_sem).wait()


def _dispatch(h, dest, zstart, p_rows):
    n = h.shape[0]
    ts = MOE_TS
    grid_spec = pltpu.PrefetchScalarGridSpec(
        num_scalar_prefetch=1,
        grid=(n // ts,),
        in_specs=[pl.BlockSpec(memory_space=pl.ANY),
                  pl.BlockSpec((ts, D_MODEL), lambda i, z: (i, 0))],
        out_specs=pl.BlockSpec(memory_space=pl.ANY),
        scratch_shapes=[pltpu.SMEM((2 * ts,), jnp.int32),
                        pltpu.VMEM((MOE_BLK, D_MODEL), f32),
                        pltpu.SemaphoreType.DMA(()),
                        pltpu.SemaphoreType.DMA(()),
                        pltpu.SemaphoreType.DMA(())],
    )
    return pl.pallas_call(
        _dispatch_kernel,
        grid_spec=grid_spec,
        out_shape=jax.ShapeDtypeStruct((p_rows, D_MODEL), f32),
        compiler_params=_cparams(1),
        name="dispatch",
    )(zstart, dest.reshape(n // ts, 2 * ts), h)


SC_CORES = 2
SC_SUBCORES = 16
SC_WORKERS = SC_CORES * SC_SUBCORES
SC_IDX_WIN = 128
SC_ROWS = 32


def _sc_mesh():
    return plsc.VectorSubcoreMesh(core_axis_name="c", subcore_axis_name="s")


def _sc_dispatch(h, dest0, dest1, p_rows):
    n, d = h.shape
    per = n // SC_WORKERS
    assert n % (SC_WORKERS * SC_IDX_WIN) == 0

    @pl.kernel(out_type=jax.ShapeDtypeStruct((p_rows, d), h.dtype), mesh=_sc_mesh(),
               scratch_types=[pltpu.VMEM((2, SC_IDX_WIN), jnp.int32), pltpu.VMEM((SC_ROWS, d), h.dtype)])
    def k(h_hbm, d0_hbm, d1_hbm, xs_hbm, idx, buf):
        base = (lax.axis_index("c") * SC_SUBCORES + lax.axis_index("s")) * per

        @pl.loop(0, per // SC_IDX_WIN)
        def _(w):
            off = base + w * SC_IDX_WIN
            pltpu.sync_copy(d0_hbm.at[pl.ds(off, SC_IDX_WIN)], idx.at[0])
            pltpu.sync_copy(d1_hbm.at[pl.ds(off, SC_IDX_WIN)], idx.at[1])
            for r in range(SC_IDX_WIN // SC_ROWS):
                pltpu.sync_copy(h_hbm.at[pl.ds(off + r * SC_ROWS, SC_ROWS)], buf)
                pltpu.sync_copy(buf, xs_hbm.at[idx.at[0, pl.ds(r * SC_ROWS, SC_ROWS)]])
                pltpu.sync_copy(buf, xs_hbm.at[idx.at[1, pl.ds(r * SC_ROWS, SC_ROWS)]])

    return k(h, dest0, dest1)


def _sc_gather2(ys, dest0, dest1):
    n = dest0.shape[0]
    d = ys.shape[1]
    per = n // SC_WORKERS
    out = jax.ShapeDtypeStruct((n, d), ys.dtype)

    @pl.kernel(out_type=[out, out], mesh=_sc_mesh(),
               scratch_types=[pltpu.VMEM((2, SC_IDX_WIN), jnp.int32), pltpu.VMEM((SC_ROWS, d), ys.dtype)])
    def k(ys_hbm, d0_hbm, d1_hbm, y0_hbm, y1_hbm, idx, buf):
        base = (lax.axis_index("c") * SC_SUBCORES + lax.axis_index("s")) * per

        @pl.loop(0, per // SC_IDX_WIN)
        def _(w):
            off = base + w * SC_IDX_WIN
            pltpu.sync_copy(d0_hbm.at[pl.ds(off, SC_IDX_WIN)], idx.at[0])
            pltpu.sync_copy(d1_hbm.at[pl.ds(off, SC_IDX_WIN)], idx.at[1])
            for r in range(SC_IDX_WIN // SC_ROWS):
                for kk, y_hbm in enumerate((y0_hbm, y1_hbm)):
                    pltpu.sync_copy(ys_hbm.at[idx.at[kk, pl.ds(r * SC_ROWS, SC_ROWS)]], buf)
                    pltpu.sync_copy(buf, y_hbm.at[pl.ds(off + r * SC_ROWS, SC_ROWS)])

    return k(ys, dest0, dest1)


def _final_norm_kernel(h_ref, y0_ref, y1_ref, gate_ref, g_ref, b_ref, o_ref):
    ffn = y0_ref[...] * gate_ref[:, 0:1] + y1_ref[...] * gate_ref[:, 1:2]
    o_ref[...] = _layer_norm(ALPHA * h_ref[...] + ffn, g_ref[...], b_ref[...])


def _final_norm(h, y0, y1, gate, ln_g, ln_b):
    n = h.shape[0]
    ts = MOE_TS
    full = lambda a: pl.BlockSpec(a.shape, lambda i: (0,) * a.ndim)
    tile = pl.BlockSpec((ts, D_MODEL), lambda i: (i, 0))
    return pl.pallas_call(
        _final_norm_kernel,
        grid=(n // ts,),
        in_specs=[tile, tile, tile, pl.BlockSpec((ts, 2), lambda i: (i, 0)), full(ln_g), full(ln_b)],
        out_specs=tile,
        out_shape=jax.ShapeDtypeStruct((n, D_MODEL), f32),
        compiler_params=_cparams(1),
        name="final_norm",
    )(h, y0, y1, gate, ln_g, ln_b)


def _experts_kernel(be_ref, bv_ref, xs_ref, wg_ref, wu_ref, wd_ref, ys_ref, wg_bf, wu_bf, wd_bf):
    i = pl.program_id(0)
    valid = bv_ref[i]

    @pl.when(jnp.logical_or(i == 0, be_ref[i] != be_ref[jnp.maximum(i - 1, 0)]))
    def _():
        wg_bf[...] = wg_ref[0].astype(bf16)
        wu_bf[...] = wu_ref[0].astype(bf16)
        wd_bf[...] = wd_ref[0].astype(bf16)

    @pl.when(valid > 0)
    def _():
        row = lax.broadcasted_iota(jnp.int32, xs_ref.shape, 0)
        x = jnp.where(row < valid, xs_ref[...], 0.0).astype(bf16)
        g = jnp.dot(x, wg_bf[...], preferred_element_type=f32)
        u = jnp.dot(x, wu_bf[...], preferred_element_type=f32)
        hmid = (g * jax.nn.sigmoid(g) * u).astype(bf16)
        ys_ref[...] = jnp.dot(hmid, wd_bf[...], preferred_element_type=f32)

    @pl.when(valid == 0)
    def _():
        ys_ref[...] = jnp.zeros_like(ys_ref)


def _experts(xs, blk_expert, blk_valid, wg, wu, wd, n_blocks):
    grid_spec = pltpu.PrefetchScalarGridSpec(
        num_scalar_prefetch=2,
        grid=(n_blocks,),
        in_specs=[pl.BlockSpec((MOE_BLK, D_MODEL), lambda i, be, bv: (i, 0)),
                  pl.BlockSpec((1, D_MODEL, D_EXPERT), lambda i, be, bv: (be[i], 0, 0)),
                  pl.BlockSpec((1, D_MODEL, D_EXPERT), lambda i, be, bv: (be[i], 0, 0)),
                  pl.BlockSpec((1, D_EXPERT, D_MODEL), lambda i, be, bv: (be[i], 0, 0))],
        out_specs=pl.BlockSpec((MOE_BLK, D_MODEL), lambda i, be, bv: (i, 0)),
        scratch_shapes=[pltpu.VMEM((D_MODEL, D_EXPERT), bf16),
                        pltpu.VMEM((D_MODEL, D_EXPERT), bf16),
                        pltpu.VMEM((D_EXPERT, D_MODEL), bf16)],
    )
    return pl.pallas_call(
        _experts_kernel,
        grid_spec=grid_spec,
        out_shape=jax.ShapeDtypeStruct(xs.shape, f32),
        compiler_params=_cparams(1),
        name="experts",
    )(blk_expert, blk_valid, xs, wg, wu, wd)


def _combine_kernel(dest_hbm, ys_hbm, h_ref, gate_ref, g_ref, b_ref, o_ref,
                    idx_smem, ybuf, idx_sem, row_sem):
    i = pl.program_id(0)
    ts = h_ref.shape[0]
    cp = pltpu.make_async_copy(dest_hbm.at[i], idx_smem, idx_sem)
    cp.start()
    cp.wait()

    def body(tok, carry):
        for k in range(2):
            d = idx_smem[2 * tok + k]
            pltpu.make_async_copy(ys_hbm.at[pl.ds(d, 1)], ybuf.at[k, pl.ds(tok, 1)], row_sem).start()
        return carry

    lax.fori_loop(0, ts, body, 0, unroll=ROW_DMA_UNROLL)
    for k in range(2):
        pltpu.make_async_copy(ys_hbm.at[pl.ds(0, ts)], ybuf.at[k], row_sem).wait()

    ffn = ybuf[0] * gate_ref[:, 0:1] + ybuf[1] * gate_ref[:, 1:2]
    o_ref[...] = _layer_norm(ALPHA * h_ref[...] + ffn, g_ref[...], b_ref[...])


def _combine(h, ys, dest, gate, ln_g, ln_b):
    n = h.shape[0]
    ts = MOE_TS
    full = lambda a: pl.BlockSpec(a.shape, lambda i: (0,) * a.ndim)
    return pl.pallas_call(
        _combine_kernel,
        grid=(n // ts,),
        in_specs=[pl.BlockSpec(memory_space=pl.ANY),
                  pl.BlockSpec(memory_space=pl.ANY),
                  pl.BlockSpec((ts, D_MODEL), lambda i: (i, 0)),
                  pl.BlockSpec((ts, 2), lambda i: (i, 0)),
                  full(ln_g), full(ln_b)],
        out_specs=pl.BlockSpec((ts, D_MODEL), lambda i: (i, 0)),
        out_shape=jax.ShapeDtypeStruct((n, D_MODEL), f32),
        scratch_shapes=[pltpu.SMEM((2 * ts,), jnp.int32),
                        pltpu.VMEM((2, ts, D_MODEL), f32),
                        pltpu.SemaphoreType.DMA(()),
                        pltpu.SemaphoreType.DMA(())],
        compiler_params=_cparams(1),
        name="combine",
    )(dest.reshape(n // ts, 2 * ts), ys, h, gate, ln_g, ln_b)


def _encode(x, p, qkv_meta, rest_meta, slabs):
    bsz, t, _ = x.shape
    n = bsz * t
    assert t % MOE_TS == 0 and t % TAIL_TM == 0 and t % PROJ_TM == 0
    xf = x.reshape(n, D_MODEL)
    qkv, rest = _in_proj(xf, p["w_in"], PROJ_TM)
    att = _attention(qkv, qkv_meta, slabs, bsz, t)
    h1, route, cnt = _mixer_tail(xf, att, rest, rest_meta, p, t)
    gate, dest, blk_expert, blk_valid, zstart, n_blocks = _route(route, cnt, n)
    dest2 = dest.reshape(n, 2)
    dest0, dest1 = dest2[:, 0], dest2[:, 1]
    xs = _sc_dispatch(h1, dest0, dest1, n_blocks * MOE_BLK)
    ys = _experts(xs, blk_expert, blk_valid, p["w_e_gate"], p["w_e_up"], p["w_e_down"], n_blocks)
    y0, y1 = _sc_gather2(ys, dest0, dest1)
    y = _final_norm(h1, y0, y1, gate, p["ln2_g"], p["ln2_b"])
    return y.reshape(bsz, t, D_MODEL)


def kernel(x_prompt, x_sample, meta_tokens, w_in, rpb, conv_w, conv_b, w_att_proj, w_conv_proj, w_out,
           ln1_g, ln1_b, w_router_group, w_router_expert, w_e_gate, w_e_up, w_e_down, ln2_g, ln2_b):
    w = w_in[0]
    a, c = D_QKV, D_CONV
    w_perm = jnp.concatenate([w[:, :a], w[:, a + c:a + 3 * c], w[:, a:a + c], w[:, a + 3 * c:]], axis=1)
    w_router = jnp.concatenate([w_router_group[0], w_router_expert[0]], axis=1)
    w_router = jnp.pad(w_router, ((0, 0), (0, LANES - w_router.shape[1])))
    row = lambda v: v[0].reshape(1, -1).astype(f32)
    p = {
        "w_in": w_perm.astype(bf16),
        "conv_w": conv_w[0].astype(f32), "conv_b": row(conv_b),
        "w_att_proj": w_att_proj[0].astype(bf16), "w_conv_proj": w_conv_proj[0].astype(bf16),
        "w_out": w_out[0].astype(bf16), "ln1_g": row(ln1_g), "ln1_b": row(ln1_b),
        "w_router": w_router.astype(bf16),
        "w_e_gate": w_e_gate[0], "w_e_up": w_e_up[0], "w_e_down": w_e_down[0],
        "ln2_g": row(ln2_g), "ln2_b": row(ln2_b),
    }
    qkv_meta, rest_meta = _in_proj(meta_tokens.astype(f32), p["w_in"], N_META)
    slabs = _bias_slabs(rpb[0])
    y_prompt = _encode(x_prompt, p, qkv_meta, rest_meta, slabs)
    y_sample = _encode(x_sample, p, qkv_meta, rest_meta, slabs)
    return (y_prompt, y_sample)
```

```python
import functools

import numpy as np
import jax
import jax.numpy as jnp
from jax import lax
from jax.experimental import pallas as pl
from jax.experimental.pallas import tpu as pltpu
from jax.experimental.pallas import tpu_sc as plsc

D_MODEL = 1024
N_META = 16
GRID_W = 64
WIN_H = 8
WIN_W = 16
N_HEADS = 8
HEAD_DIM = 64
D_ATT = N_HEADS * HEAD_DIM
D_CONV = D_MODEL
N_GROUPS = 4
EXPERTS_PER_GROUP = 8
N_EXPERTS = N_GROUPS * EXPERTS_PER_GROUP
D_EXPERT = D_MODEL // 2
DEPTH = 1
ALPHA = (2.0 * DEPTH) ** 0.25
LN_EPS = 1e-5
D_QKV = 4 * D_ATT
LOG2E = 1.4426950408889634
D_REST = 3 * D_CONV + 2 * D_MODEL
D_IN_PROJ = 3 * D_ATT + D_REST

LANES = 128
HEADS_PER_VREG = LANES // HEAD_DIM
N_HEAD_PAIRS = N_HEADS // HEADS_PER_VREG
KEY_ROWS = WIN_H * GRID_W
NEG_BIG = -1e30

PROJ_TM = 512
PROJ_TN = 512
ATT_ROWS = 8
ATT_GROUP = 4
TAIL_TM = 512
MOE_TS = 512
MOE_BLK = 512
ROW_DMA_UNROLL = 8
N_ZERO_FILLS = 2 * N_EXPERTS
VMEM_LIMIT = 56 * 1024 * 1024

bf16 = jnp.bfloat16
f32 = jnp.float32


def _cparams(n_axes):
    return pltpu.CompilerParams(dimension_semantics=("arbitrary",) * n_axes,
                                vmem_limit_bytes=VMEM_LIMIT)


def _in_proj_kernel(x_ref, w_ref, qkv_ref, rest_ref):
    assert PROJ_TN == D_ATT
    xb = x_ref[...].astype(bf16)
    for c in range(0, D_IN_PROJ, PROJ_TN):
        y = jnp.dot(xb, w_ref[:, c:c + PROJ_TN], preferred_element_type=f32)
        if c == 0:
            y = (y * (HEAD_DIM ** -0.5 * LOG2E)).astype(bf16)
            lane = lax.broadcasted_iota(jnp.int32, y.shape, 1)
            even = (lane & HEAD_DIM) == 0
            zero = jnp.zeros_like(y)
            qkv_ref[:, :D_ATT] = jnp.where(even, y, zero)
            qkv_ref[:, D_ATT:2 * D_ATT] = jnp.where(even, zero, y)
        elif c < 3 * D_ATT:
            qkv_ref[:, c + D_ATT:c + D_ATT + PROJ_TN] = y.astype(bf16)
        else:
            rest_ref[:, c - 3 * D_ATT:c - 3 * D_ATT + PROJ_TN] = y.astype(bf16)


def _in_proj(x, w_bf, tm):
    n = x.shape[0]
    return pl.pallas_call(
        _in_proj_kernel,
        grid=(n // tm,),
        in_specs=[pl.BlockSpec((tm, D_MODEL), lambda i: (i, 0)),
                  pl.BlockSpec((D_MODEL, D_IN_PROJ), lambda i: (0, 0))],
        out_specs=[pl.BlockSpec((tm, D_QKV), lambda i: (i, 0)),
                   pl.BlockSpec((tm, D_REST), lambda i: (i, 0))],
        out_shape=[jax.ShapeDtypeStruct((n, D_QKV), bf16),
                   jax.ShapeDtypeStruct((n, D_REST), bf16)],
        compiler_params=_cparams(1),
        name="in_proj",
    )(x, w_bf)


def _bias_slabs(rpb):
    qc = np.arange(GRID_W)[:, None]
    kc = np.arange(GRID_W)[None, :]
    w_start = np.clip(qc - WIN_W // 2, 0, GRID_W - WIN_W)
    valid = (kc >= w_start) & (kc < w_start + WIN_W)
    dx_idx = np.clip(kc - qc + WIN_W - 1, 0, 2 * WIN_W - 2)
    t = rpb.astype(f32)[:, :, dx_idx] * LOG2E
    t = jnp.where(jnp.asarray(valid)[None, None], t, NEG_BIG)
    dy_idx = np.arange(WIN_H)[:, None] + np.arange(WIN_H)[None, :]
    s = t[:, dy_idx]
    s = jnp.transpose(s, (0, 1, 3, 2, 4)).reshape(N_HEADS, WIN_H, GRID_W, KEY_ROWS)
    s = s.reshape(N_HEAD_PAIRS, HEADS_PER_VREG, WIN_H, GRID_W, KEY_ROWS)
    s = jnp.transpose(s, (0, 2, 1, 3, 4))
    return s.reshape(N_HEAD_PAIRS, WIN_H, HEADS_PER_VREG * GRID_W, KEY_ROWS)


def _attention_kernel(qe_ref, qo_ref, k_ref, v_ref, km_ref, vm_ref, bias_ref, o_ref, *, rows):
    j = pl.program_id(1)
    lane = lax.broadcasted_iota(jnp.int32, (GRID_W, LANES), 1)
    low = lane < HEAD_DIM
    nt = (((1,), (1,)), ((), ()))

    def scores(i, pair):
        r = j * ATT_ROWS + i
        rs = jnp.clip(r - WIN_H // 2, 0, rows - WIN_H)
        dy0 = rs - r + (WIN_H - 1)
        q0 = pl.multiple_of(i * GRID_W, GRID_W)
        k0 = pl.multiple_of(rs * GRID_W, GRID_W)
        cs = slice(pair * LANES, (pair + 1) * LANES)
        qq = jnp.concatenate([qe_ref[pl.ds(q0, GRID_W), cs], qo_ref[pl.ds(q0, GRID_W), cs]], axis=0)
        k2 = jnp.concatenate([k_ref[pl.ds(k0, KEY_ROWS), cs], km_ref[:, cs]], axis=0)
        s = lax.dot_general(qq, k2, nt, preferred_element_type=f32)
        s = jnp.concatenate([s[:, :KEY_ROWS] + bias_ref[pair, dy0], s[:, KEY_ROWS:]], axis=1)
        return s, jnp.max(s, axis=-1, keepdims=True), k0, q0, cs

    def weights(state):
        s, m, k0, q0, cs = state
        v2 = jnp.concatenate([v_ref[pl.ds(k0, KEY_ROWS), cs], vm_ref[:, cs]], axis=0)
        e = jnp.exp2(s - m)
        l = jnp.sum(e, axis=-1, keepdims=True)
        return jnp.dot(e.astype(bf16), v2, preferred_element_type=f32), l, q0, cs

    def finish(state):
        o2, l, q0, cs = state
        o2 = o2 / l
        o = jnp.where(low, o2[:GRID_W], o2[GRID_W:])
        o_ref[pl.ds(q0, GRID_W), cs] = o.astype(bf16)

    def group_body(g, carry):
        items = [(g * ATT_GROUP + i, pair) for i in range(ATT_GROUP) for pair in range(N_HEAD_PAIRS)]
        a, b = {}, {}
        for step in range(len(items) + 2):
            if step < len(items):
                a[step] = scores(*items[step])
            if 0 <= step - 1 < len(items):
                b[step - 1] = weights(a.pop(step - 1))
            if 0 <= step - 2 < len(items):
                finish(b.pop(step - 2))
        return carry

    lax.fori_loop(0, ATT_ROWS // ATT_GROUP, group_body, 0)


def _attention(qkv, qkv_meta, slabs, bsz, t):
    rows = t // GRID_W
    assert rows >= 2 * WIN_H and rows % ATT_ROWS == 0
    steps = rows // ATT_ROWS
    tq = ATT_ROWS * GRID_W
    return pl.pallas_call(
        functools.partial(_attention_kernel, rows=rows),
        grid=(bsz, steps),
        in_specs=[pl.BlockSpec((tq, D_ATT), lambda b, j: (b * steps + j, 0)),
                  pl.BlockSpec((tq, D_ATT), lambda b, j: (b * steps + j, 1)),
                  pl.BlockSpec((t, D_ATT), lambda b, j: (b, 2)),
                  pl.BlockSpec((t, D_ATT), lambda b, j: (b, 3)),
                  pl.BlockSpec((N_META, D_ATT), lambda b, j: (0, 2)),
                  pl.BlockSpec((N_META, D_ATT), lambda b, j: (0, 3)),
                  pl.BlockSpec(slabs.shape, lambda b, j: (0, 0, 0, 0))],
        out_specs=pl.BlockSpec((tq, D_ATT), lambda b, j: (b * steps + j, 0)),
        out_shape=jax.ShapeDtypeStruct((bsz * t, D_ATT), bf16),
        compiler_params=_cparams(2),
        name="attention",
    )(qkv, qkv, qkv, qkv, qkv_meta, qkv_meta, slabs)


def _layer_norm(z, g, b):
    mu = jnp.mean(z, axis=-1, keepdims=True)
    d = z - mu
    var = jnp.mean(d * d, axis=-1, keepdims=True)
    return d * lax.rsqrt(var + LN_EPS) * g + b


def _route_tile(lg):
    tm = lg.shape[0]
    lane = lax.broadcasted_iota(jnp.int32, lg.shape, 1).astype(f32)
    neg = -jnp.inf
    first = lambda hit: jnp.min(jnp.where(hit, lane, float(LANES)), axis=-1, keepdims=True)

    gl = jnp.where(lane < N_GROUPS, lg, neg)
    gmax = jnp.max(gl, axis=-1, keepdims=True)
    grp = first(gl == gmax)
    pg_sel = 1.0 / jnp.sum(jnp.exp(gl - gmax), axis=-1, keepdims=True)

    e_lo = N_GROUPS + grp * EXPERTS_PER_GROUP
    el = jnp.where((lane >= e_lo) & (lane < e_lo + EXPERTS_PER_GROUP), lg, neg)
    t1 = jnp.max(el, axis=-1, keepdims=True)
    i1 = first(el == t1)
    el2 = jnp.where(lane == i1, neg, el)
    t2 = jnp.max(el2, axis=-1, keepdims=True)
    i2 = first(el2 == t2)
    r = jnp.exp(t2 - t1)
    g1 = pg_sel / (1.0 + r)
    g2 = pg_sel * r / (1.0 + r)
    e1 = i1 - N_GROUPS
    e2 = i2 - N_GROUPS

    hit1 = lane == e1
    hit2 = lane == e2
    onehot = jnp.where(hit1 | hit2, 1.0, 0.0)
    tri_r = lax.broadcasted_iota(jnp.int32, (tm, tm), 0)
    tri_c = lax.broadcasted_iota(jnp.int32, (tm, tm), 1)
    tri = jnp.where(tri_r > tri_c, 1.0, 0.0).astype(bf16)
    before = jnp.dot(tri, onehot.astype(bf16), preferred_element_type=f32)
    rank1 = jnp.sum(jnp.where(hit1, before, 0.0), axis=-1, keepdims=True)
    rank2 = jnp.sum(jnp.where(hit2, before, 0.0), axis=-1, keepdims=True)
    route = jnp.zeros_like(lg)
    for col, val in enumerate((e1, e2, g1, g2, rank1, rank2)):
        route = jnp.where(lane == col, val, route)
    return route, jnp.sum(onehot, axis=0, keepdims=True)


def _mixer_tail_kernel(x_ref, att_ref, rest_ref, prev_ref, next_ref, meta_ref,
                       convw_ref, convb_ref, wap_ref, wcp_ref, wout_ref, g_ref, b_ref, wr_ref,
                       h_ref, route_ref, cnt_ref, *, tiles_per_seq):
    i = pl.program_id(0)
    pos = i % tiles_per_seq
    tm = x_ref.shape[0]
    c0, c1, c2, c3, c4 = (k * D_MODEL for k in range(5))

    u = rest_ref[:, c0:c1].astype(f32) * rest_ref[:, c1:c2].astype(f32)
    last = prev_ref.shape[0] - 1
    u_prev = prev_ref[last:, c0:c1].astype(f32) * prev_ref[last:, c1:c2].astype(f32)
    u_meta = meta_ref[N_META - 1:, c0:c1].astype(f32) * meta_ref[N_META - 1:, c1:c2].astype(f32)
    u_next = next_ref[:1, c0:c1].astype(f32) * next_ref[:1, c1:c2].astype(f32)
    u_prev = jnp.where(pos == 0, u_meta, u_prev)
    u_next = jnp.where(pos == tiles_per_seq - 1, jnp.zeros_like(u_next), u_next)
    row = lax.broadcasted_iota(jnp.int32, (tm, D_CONV), 0)
    u_m1 = jnp.where(row == 0, u_prev, pltpu.roll(u, 1, axis=0))
    u_p1 = jnp.where(row == tm - 1, u_next, pltpu.roll(u, tm - 1, axis=0))
    s = u_m1 * convw_ref[0:1, :] + u * convw_ref[1:2, :] + u_p1 * convw_ref[2:3, :] + convb_ref[...]
    conv_in = (rest_ref[:, c2:c3].astype(f32) * s).astype(bf16)
    cv = jnp.dot(conv_in, wcp_ref[...], preferred_element_type=f32)
    ap = jnp.dot(att_ref[...], wap_ref[...], preferred_element_type=f32)
    merged = (jax.nn.sigmoid(rest_ref[:, c3:c4].astype(f32)) * ap
              + jax.nn.sigmoid(rest_ref[:, c4:].astype(f32)) * cv)
    mix = jnp.dot(merged.astype(bf16), wout_ref[...], preferred_element_type=f32)
    h = _layer_norm(ALPHA * x_ref[...] + mix, g_ref[...], b_ref[...])
    h_ref[...] = h
    logits = jnp.dot(h.astype(bf16), wr_ref[...], preferred_element_type=f32)
    route, count = _route_tile(logits)
    route_ref[...] = route
    cnt_ref[0] = count


def _mixer_tail(x, att, rest, rest_meta, p, t):
    n = x.shape[0]
    tm = TAIL_TM
    halo = 16
    hb = tm // halo
    n_halo = n // halo
    tiles_per_seq = t // tm
    full = lambda a: pl.BlockSpec(a.shape, lambda i: (0,) * a.ndim)
    return pl.pallas_call(
        functools.partial(_mixer_tail_kernel, tiles_per_seq=tiles_per_seq),
        grid=(n // tm,),
        in_specs=[pl.BlockSpec((tm, D_MODEL), lambda i: (i, 0)),
                  pl.BlockSpec((tm, D_ATT), lambda i: (i, 0)),
                  pl.BlockSpec((tm, D_REST), lambda i: (i, 0)),
                  pl.BlockSpec((halo, 2 * D_CONV), lambda i: (jnp.maximum(i * hb - 1, 0), 0)),
                  pl.BlockSpec((halo, 2 * D_CONV), lambda i: (jnp.minimum((i + 1) * hb, n_halo - 1), 0)),
                  full(rest_meta), full(p["conv_w"]), full(p["conv_b"]), full(p["w_att_proj"]),
                  full(p["w_conv_proj"]), full(p["w_out"]), full(p["ln1_g"]), full(p["ln1_b"]),
                  full(p["w_router"])],
        out_specs=[pl.BlockSpec((tm, D_MODEL), lambda i: (i, 0)),
                   pl.BlockSpec((tm, LANES), lambda i: (i, 0)),
                   pl.BlockSpec((1, 1, LANES), lambda i: (i, 0, 0))],
        out_shape=[jax.ShapeDtypeStruct((n, D_MODEL), f32),
                   jax.ShapeDtypeStruct((n, LANES), f32),
                   jax.ShapeDtypeStruct((n // tm, 1, LANES), f32)],
        compiler_params=_cparams(1),
        name="mixer_tail",
    )(x, att, rest, rest, rest, rest_meta, p["conv_w"], p["conv_b"], p["w_att_proj"],
      p["w_conv_proj"], p["w_out"], p["ln1_g"], p["ln1_b"], p["w_router"])


def _route(route, cnt, n):
    n_tiles = n // TAIL_TM
    expert = route[:, 0:2].astype(jnp.int32)
    gate = route[:, 2:4]
    rank = route[:, 4:6].astype(jnp.int32)
    tile_cnt = cnt[:, 0, :N_EXPERTS].astype(jnp.int32)
    tile_off = jnp.cumsum(tile_cnt, axis=0) - tile_cnt
    counts = jnp.sum(tile_cnt, axis=0)
    padded = (counts + MOE_BLK - 1) // MOE_BLK * MOE_BLK
    pend = jnp.cumsum(padded)
    pstart = pend - padded
    base = pstart[None, :] + tile_off
    hit = expert.reshape(n_tiles, TAIL_TM, 2, 1) == jnp.arange(N_EXPERTS, dtype=jnp.int32)
    base_sel = jnp.sum(jnp.where(hit, base[:, None, None, :], 0), axis=-1)
    dest = (base_sel.reshape(n, 2) + rank).reshape(2 * n)

    n_blocks = (2 * n + N_EXPERTS * (MOE_BLK - 1) + MOE_BLK - 1) // MOE_BLK
    blk_start = jnp.arange(n_blocks, dtype=jnp.int32) * MOE_BLK
    blk_expert = jnp.sum((pend[None, :] <= blk_start[:, None]).astype(jnp.int32), axis=1)
    blk_expert = jnp.minimum(blk_expert, N_EXPERTS - 1)
    blk_valid = jnp.clip((pstart + counts)[blk_expert] - blk_start, 0, MOE_BLK)
    blk_valid = jnp.where(blk_start < pend[-1], blk_valid, 0).astype(jnp.int32)
    tail = (pend[-1] // MOE_BLK + jnp.arange(N_ZERO_FILLS - N_EXPERTS, dtype=jnp.int32)) * MOE_BLK
    zstart = jnp.concatenate([jnp.where(padded > counts, pend - MOE_BLK, -1),
                              jnp.where(tail < n_blocks * MOE_BLK, tail, -1)]).astype(jnp.int32)
    return gate, dest, blk_expert, blk_valid, zstart, n_blocks


def _dispatch_kernel(zstart_ref, dest_hbm, h_ref, xs_hbm, idx_smem, zeros, idx_sem, row_sem, zero_sem):
    i = pl.program_id(0)
    ts = h_ref.shape[0]

    @pl.when(i == 0)
    def _():
        zeros[...] = jnp.zeros_like(zeros)

        def fill(z):
            start = pl.multiple_of(jnp.maximum(zstart_ref[z], 0), MOE_BLK)
            return pltpu.make_async_copy(zeros, xs_hbm.at[pl.ds(start, MOE_BLK)], zero_sem)

        for z in range(N_ZERO_FILLS):
            pl.when(zstart_ref[z] >= 0)(lambda z=z: fill(z).start())
        for z in range(N_ZERO_FILLS):
            pl.when(zstart_ref[z] >= 0)(lambda z=z: fill(z).wait())

    cp = pltpu.make_async_copy(dest_hbm.at[i], idx_smem, idx_sem)
    cp.start()
    cp.wait()

    def body(tok, carry):
        for k in range(2):
            d = idx_smem[2 * tok + k]
            pltpu.make_async_copy(h_ref.at[pl.ds(tok, 1)], xs_hbm.at[pl.ds(d, 1)], row_sem).start()
        return carry

    lax.fori_loop(0, ts, body, 0, unroll=ROW_DMA_UNROLL)
    for _ in range(2):
        pltpu.make_async_copy(h_ref, xs_hbm.at[pl.ds(0, ts)], row_sem).wait()


def _dispatch(h, dest, zstart, p_rows):
    n = h.shape[0]
    ts = MOE_TS
    grid_spec = pltpu.PrefetchScalarGridSpec(
        num_scalar_prefetch=1,
        grid=(n // ts,),
        in_specs=[pl.BlockSpec(memory_space=pl.ANY),
                  pl.BlockSpec((ts, D_MODEL), lambda i, z: (i, 0))],
        out_specs=pl.BlockSpec(memory_space=pl.ANY),
        scratch_shapes=[pltpu.SMEM((2 * ts,), jnp.int32),
                        pltpu.VMEM((MOE_BLK, D_MODEL), f32),
                        pltpu.SemaphoreType.DMA(()),
                        pltpu.SemaphoreType.DMA(()),
                        pltpu.SemaphoreType.DMA(())],
    )
    return pl.pallas_call(
        _dispatch_kernel,
        grid_spec=grid_spec,
        out_shape=jax.ShapeDtypeStruct((p_rows, D_MODEL), f32),
        compiler_params=_cparams(1),
        name="dispatch",
    )(zstart, dest.reshape(n // ts, 2 * ts), h)


SC_CORES = 2
SC_SUBCORES = 16
SC_WORKERS = SC_CORES * SC_SUBCORES
SC_IDX_WIN = 128
SC_ROWS = 32


def _sc_mesh():
    return plsc.VectorSubcoreMesh(core_axis_name="c", subcore_axis_name="s")


def _sc_dispatch(h, dest0, dest1, p_rows):
    n, d = h.shape
    per = n // SC_WORKERS
    assert n % (SC_WORKERS * SC_IDX_WIN) == 0

    @pl.kernel(out_type=jax.ShapeDtypeStruct((p_rows, d), h.dtype), mesh=_sc_mesh(),
               scratch_types=[pltpu.VMEM((2, SC_IDX_WIN), jnp.int32), pltpu.VMEM((SC_ROWS, d), h.dtype)])
    def k(h_hbm, d0_hbm, d1_hbm, xs_hbm, idx, buf):
        base = (lax.axis_index("c") * SC_SUBCORES + lax.axis_index("s")) * per

        @pl.loop(0, per // SC_IDX_WIN)
        def _(w):
            off = base + w * SC_IDX_WIN
            pltpu.sync_copy(d0_hbm.at[pl.ds(off, SC_IDX_WIN)], idx.at[0])
            pltpu.sync_copy(d1_hbm.at[pl.ds(off, SC_IDX_WIN)], idx.at[1])
            for r in range(SC_IDX_WIN // SC_ROWS):
                pltpu.sync_copy(h_hbm.at[pl.ds(off + r * SC_ROWS, SC_ROWS)], buf)
                pltpu.sync_copy(buf, xs_hbm.at[idx.at[0, pl.ds(r * SC_ROWS, SC_ROWS)]])
                pltpu.sync_copy(buf, xs_hbm.at[idx.at[1, pl.ds(r * SC_ROWS, SC_ROWS)]])

    return k(h, dest0, dest1)


def _sc_gather2(ys, dest0, dest1):
    n = dest0.shape[0]
    d = ys.shape[1]
    per = n // SC_WORKERS
    out = jax.ShapeDtypeStruct((n, d), ys.dtype)

    @pl.kernel(out_type=[out, out], mesh=_sc_mesh(),
               scratch_types=[pltpu.VMEM((2, SC_IDX_WIN), jnp.int32), pltpu.VMEM((SC_ROWS, d), ys.dtype)])
    def k(ys_hbm, d0_hbm, d1_hbm, y0_hbm, y1_hbm, idx, buf):
        base = (lax.axis_index("c") * SC_SUBCORES + lax.axis_index("s")) * per

        @pl.loop(0, per // SC_IDX_WIN)
        def _(w):
            off = base + w * SC_IDX_WIN
            pltpu.sync_copy(d0_hbm.at[pl.ds(off, SC_IDX_WIN)], idx.at[0])
            pltpu.sync_copy(d1_hbm.at[pl.ds(off, SC_IDX_WIN)], idx.at[1])
            for r in range(SC_IDX_WIN // SC_ROWS):
                for kk, y_hbm in enumerate((y0_hbm, y1_hbm)):
                    pltpu.sync_copy(ys_hbm.at[idx.at[kk, pl.ds(r * SC_ROWS, SC_ROWS)]], buf)
                    pltpu.sync_copy(buf, y_hbm.at[pl.ds(off + r * SC_ROWS, SC_ROWS)])

    return k(ys, dest0, dest1)


def _final_norm_kernel(h_ref, y0_ref, y1_ref, gate_ref, g_ref, b_ref, o_ref):
    ffn = y0_ref[...] * gate_ref[:, 0:1] + y1_ref[...] * gate_ref[:, 1:2]
    o_ref[...] = _layer_norm(ALPHA * h_ref[...] + ffn, g_ref[...], b_ref[...])


def _final_norm(h, y0, y1, gate, ln_g, ln_b):
    n = h.shape[0]
    ts = MOE_TS
    full = lambda a: pl.BlockSpec(a.shape, lambda i: (0,) * a.ndim)
    tile = pl.BlockSpec((ts, D_MODEL), lambda i: (i, 0))
    return pl.pallas_call(
        _final_norm_kernel,
        grid=(n // ts,),
        in_specs=[tile, tile, tile, pl.BlockSpec((ts, 2), lambda i: (i, 0)), full(ln_g), full(ln_b)],
        out_specs=tile,
        out_shape=jax.ShapeDtypeStruct((n, D_MODEL), f32),
        compiler_params=_cparams(1),
        name="final_norm",
    )(h, y0, y1, gate, ln_g, ln_b)


def _experts_kernel(be_ref, bv_ref, xs_ref, wg_ref, wu_ref, wd_ref, ys_ref, wg_bf, wu_bf, wd_bf):
    i = pl.program_id(0)
    valid = bv_ref[i]

    @pl.when(jnp.logical_or(i == 0, be_ref[i] != be_ref[jnp.maximum(i - 1, 0)]))
    def _():
        wg_bf[...] = wg_ref[0].astype(bf16)
        wu_bf[...] = wu_ref[0].astype(bf16)
        wd_bf[...] = wd_ref[0].astype(bf16)

    @pl.when(valid > 0)
    def _():
        row = lax.broadcasted_iota(jnp.int32, xs_ref.shape, 0)
        x = jnp.where(row < valid, xs_ref[...], 0.0).astype(bf16)
        g = jnp.dot(x, wg_bf[...], preferred_element_type=f32)
        u = jnp.dot(x, wu_bf[...], preferred_element_type=f32)
        hmid = (g * jax.nn.sigmoid(g) * u).astype(bf16)
        ys_ref[...] = jnp.dot(hmid, wd_bf[...], preferred_element_type=f32)

    @pl.when(valid == 0)
    def _():
        ys_ref[...] = jnp.zeros_like(ys_ref)


def _experts(xs, blk_expert, blk_valid, wg, wu, wd, n_blocks):
    grid_spec = pltpu.PrefetchScalarGridSpec(
        num_scalar_prefetch=2,
        grid=(n_blocks,),
        in_specs=[pl.BlockSpec((MOE_BLK, D_MODEL), lambda i, be, bv: (i, 0)),
                  pl.BlockSpec((1, D_MODEL, D_EXPERT), lambda i, be, bv: (be[i], 0, 0)),
                  pl.BlockSpec((1, D_MODEL, D_EXPERT), lambda i, be, bv: (be[i], 0, 0)),
                  pl.BlockSpec((1, D_EXPERT, D_MODEL), lambda i, be, bv: (be[i], 0, 0))],
        out_specs=pl.BlockSpec((MOE_BLK, D_MODEL), lambda i, be, bv: (i, 0)),
        scratch_shapes=[pltpu.VMEM((D_MODEL, D_EXPERT), bf16),
                        pltpu.VMEM((D_MODEL, D_EXPERT), bf16),
                        pltpu.VMEM((D_EXPERT, D_MODEL), bf16)],
    )
    return pl.pallas_call(
        _experts_kernel,
        grid_spec=grid_spec,
        out_shape=jax.ShapeDtypeStruct(xs.shape, f32),
        compiler_params=_cparams(1),
        name="experts",
    )(blk_expert, blk_valid, xs, wg, wu, wd)


def _combine_kernel(dest_hbm, ys_hbm, h_ref, gate_ref, g_ref, b_ref, o_ref,
                    idx_smem, ybuf, idx_sem, row_sem):
    i = pl.program_id(0)
    ts = h_ref.shape[0]
    cp = pltpu.make_async_copy(dest_hbm.at[i], idx_smem, idx_sem)
    cp.start()
    cp.wait()

    def body(tok, carry):
        for k in range(2):
            d = idx_smem[2 * tok + k]
            pltpu.make_async_copy(ys_hbm.at[pl.ds(d, 1)], ybuf.at[k, pl.ds(tok, 1)], row_sem).start()
        return carry

    lax.fori_loop(0, ts, body, 0, unroll=ROW_DMA_UNROLL)
    for k in range(2):
        pltpu.make_async_copy(ys_hbm.at[pl.ds(0, ts)], ybuf.at[k], row_sem).wait()

    ffn = ybuf[0] * gate_ref[:, 0:1] + ybuf[1] * gate_ref[:, 1:2]
    o_ref[...] = _layer_norm(ALPHA * h_ref[...] + ffn, g_ref[...], b_ref[...])


def _combine(h, ys, dest, gate, ln_g, ln_b):
    n = h.shape[0]
    ts = MOE_TS
    full = lambda a: pl.BlockSpec(a.shape, lambda i: (0,) * a.ndim)
    return pl.pallas_call(
        _combine_kernel,
        grid=(n // ts,),
        in_specs=[pl.BlockSpec(memory_space=pl.ANY),
                  pl.BlockSpec(memory_space=pl.ANY),
                  pl.BlockSpec((ts, D_MODEL), lambda i: (i, 0)),
                  pl.BlockSpec((ts, 2), lambda i: (i, 0)),
                  full(ln_g), full(ln_b)],
        out_specs=pl.BlockSpec((ts, D_MODEL), lambda i: (i, 0)),
        out_shape=jax.ShapeDtypeStruct((n, D_MODEL), f32),
        scratch_shapes=[pltpu.SMEM((2 * ts,), jnp.int32),
                        pltpu.VMEM((2, ts, D_MODEL), f32),
                        pltpu.SemaphoreType.DMA(()),
                        pltpu.SemaphoreType.DMA(())],
        compiler_params=_cparams(1),
        name="combine",
    )(dest.reshape(n // ts, 2 * ts), ys, h, gate, ln_g, ln_b)


def _encode(x, p, qkv_meta, rest_meta, slabs):
    bsz, t, _ = x.shape
    n = bsz * t
    assert t % MOE_TS == 0 and t % TAIL_TM == 0 and t % PROJ_TM == 0
    xf = x.reshape(n, D_MODEL)
    qkv, rest = _in_proj(xf, p["w_in"], PROJ_TM)
    att = _attention(qkv, qkv_meta, slabs, bsz, t)
    h1, route, cnt = _mixer_tail(xf, att, rest, rest_meta, p, t)
    gate, dest, blk_expert, blk_valid, zstart, n_blocks = _route(route, cnt, n)
    dest2 = dest.reshape(n, 2)
    dest0, dest1 = dest2[:, 0], dest2[:, 1]
    xs = _sc_dispatch(h1, dest0, dest1, n_blocks * MOE_BLK)
    ys = _experts(xs, blk_expert, blk_valid, p["w_e_gate"], p["w_e_up"], p["w_e_down"], n_blocks)
    y0, y1 = _sc_gather2(ys, dest0, dest1)
    y = _final_norm(h1, y0, y1, gate, p["ln2_g"], p["ln2_b"])
    return y.reshape(bsz, t, D_MODEL)


def kernel(x_prompt, x_sample, meta_tokens, w_in, rpb, conv_w, conv_b, w_att_proj, w_conv_proj, w_out,
           ln1_g, ln1_b, w_router_group, w_router_expert, w_e_gate, w_e_up, w_e_down, ln2_g, ln2_b):
    w = w_in[0]
    a, c = 3 * D_ATT, D_CONV
    w_perm = jnp.concatenate([w[:, :a], w[:, a + c:a + 3 * c], w[:, a:a + c], w[:, a + 3 * c:]], axis=1)
    w_router = jnp.concatenate([w_router_group[0], w_router_expert[0]], axis=1)
    w_router = jnp.pad(w_router, ((0, 0), (0, LANES - w_router.shape[1])))
    row = lambda v: v[0].reshape(1, -1).astype(f32)
    p = {
        "w_in": w_perm.astype(bf16),
        "conv_w": conv_w[0].astype(f32), "conv_b": row(conv_b),
        "w_att_proj": w_att_proj[0].astype(bf16), "w_conv_proj": w_conv_proj[0].astype(bf16),
        "w_out": w_out[0].astype(bf16), "ln1_g": row(ln1_g), "ln1_b": row(ln1_b),
        "w_router": w_router.astype(bf16),
        "w_e_gate": w_e_gate[0], "w_e_up": w_e_up[0], "w_e_down": w_e_down[0],
        "ln2_g": row(ln2_g), "ln2_b": row(ln2_b),
    }
    qkv_meta, rest_meta = _in_proj(meta_tokens.astype(f32), p["w_in"], N_META)
    slabs = _bias_slabs(rpb[0])
    y_prompt = _encode(x_prompt, p, qkv_meta, rest_meta, slabs)
    y_sample = _encode(x_sample, p, qkv_meta, rest_meta, slabs)
    return (y_prompt, y_sample)
```

```python
import functools

import numpy as np
import jax
import jax.numpy as jnp
from jax import lax
from jax.experimental import pallas as pl
from jax.experimental.pallas import tpu as pltpu
from jax.experimental.pallas import tpu_sc as plsc

D_MODEL = 1024
N_META = 16
GRID_W = 64
WIN_H = 8
WIN_W = 16
N_HEADS = 8
HEAD_DIM = 64
D_ATT = N_HEADS * HEAD_DIM
D_CONV = D_MODEL
N_GROUPS = 4
EXPERTS_PER_GROUP = 8
N_EXPERTS = N_GROUPS * EXPERTS_PER_GROUP
D_EXPERT = D_MODEL // 2
DEPTH = 1
ALPHA = (2.0 * DEPTH) ** 0.25
LN_EPS = 1e-5
D_QKV = 4 * D_ATT
LOG2E = 1.4426950408889634
D_REST = 3 * D_CONV + 2 * D_MODEL
D_IN_PROJ = 3 * D_ATT + D_REST

LANES = 128
HEADS_PER_VREG = LANES // HEAD_DIM
N_HEAD_PAIRS = N_HEADS // HEADS_PER_VREG
KEY_ROWS = WIN_H * GRID_W
NEG_BIG = -1e30

PROJ_TM = 512
PROJ_TN = 512
ATT_ROWS = 8
ATT_GROUP = 4
TAIL_TM = 512
NORM_TS = 512
MOE_BLK = 512
ROUTE_ROWS = 8
VMEM_LIMIT = 56 * 1024 * 1024

SC_CORES = 2
SC_SUBCORES = 16
SC_WORKERS = SC_CORES * SC_SUBCORES
SC_IDX_WIN = 128
SC_ROWS = 32

bf16 = jnp.bfloat16
f32 = jnp.float32


def _cparams(n_axes):
    return pltpu.CompilerParams(dimension_semantics=("arbitrary",) * n_axes,
                                vmem_limit_bytes=VMEM_LIMIT)


def _full(a):
    return pl.BlockSpec(a.shape, lambda *_: (0,) * a.ndim)


def _in_proj_kernel(x_ref, w_ref, qkv_ref, rest_ref):
    assert PROJ_TN == D_ATT
    xb = x_ref[...].astype(bf16)
    for c in range(0, D_IN_PROJ, PROJ_TN):
        y = jnp.dot(xb, w_ref[:, c:c + PROJ_TN], preferred_element_type=f32)
        if c == 0:
            y = (y * (HEAD_DIM ** -0.5 * LOG2E)).astype(bf16)
            lane = lax.broadcasted_iota(jnp.int32, y.shape, 1)
            even = (lane & HEAD_DIM) == 0
            zero = jnp.zeros_like(y)
            qkv_ref[:, :D_ATT] = jnp.where(even, y, zero)
            qkv_ref[:, D_ATT:2 * D_ATT] = jnp.where(even, zero, y)
        elif c < 3 * D_ATT:
            qkv_ref[:, c + D_ATT:c + D_ATT + PROJ_TN] = y.astype(bf16)
        else:
            rest_ref[:, c - 3 * D_ATT:c - 3 * D_ATT + PROJ_TN] = y.astype(bf16)


def _in_proj(x, w_bf, tm):
    n = x.shape[0]
    return pl.pallas_call(
        _in_proj_kernel,
        grid=(n // tm,),
        in_specs=[pl.BlockSpec((tm, D_MODEL), lambda i: (i, 0)),
                  pl.BlockSpec((D_MODEL, D_IN_PROJ), lambda i: (0, 0))],
        out_specs=[pl.BlockSpec((tm, D_QKV), lambda i: (i, 0)),
                   pl.BlockSpec((tm, D_REST), lambda i: (i, 0))],
        out_shape=[jax.ShapeDtypeStruct((n, D_QKV), bf16),
                   jax.ShapeDtypeStruct((n, D_REST), bf16)],
        compiler_params=_cparams(1),
        name="in_proj",
    )(x, w_bf)


def _bias_slabs(rpb):
    qc = np.arange(GRID_W)[:, None]
    kc = np.arange(GRID_W)[None, :]
    w_start = np.clip(qc - WIN_W // 2, 0, GRID_W - WIN_W)
    valid = (kc >= w_start) & (kc < w_start + WIN_W)
    dx_idx = np.clip(kc - qc + WIN_W - 1, 0, 2 * WIN_W - 2)
    t = rpb.astype(f32)[:, :, dx_idx] * LOG2E
    t = jnp.where(jnp.asarray(valid)[None, None], t, NEG_BIG)
    dy_idx = np.arange(WIN_H)[:, None] + np.arange(WIN_H)[None, :]
    s = t[:, dy_idx]
    s = jnp.transpose(s, (0, 1, 3, 2, 4)).reshape(N_HEADS, WIN_H, GRID_W, KEY_ROWS)
    s = s.reshape(N_HEAD_PAIRS, HEADS_PER_VREG, WIN_H, GRID_W, KEY_ROWS)
    s = jnp.transpose(s, (0, 2, 1, 3, 4))
    return s.reshape(N_HEAD_PAIRS, WIN_H, HEADS_PER_VREG * GRID_W, KEY_ROWS)


def _attention_kernel(qe_ref, qo_ref, k_ref, v_ref, km_ref, vm_ref, bias_ref, o_ref, *, rows):
    j = pl.program_id(1)
    lane = lax.broadcasted_iota(jnp.int32, (GRID_W, LANES), 1)
    low = lane < HEAD_DIM
    nt = (((1,), (1,)), ((), ()))

    def scores(i, pair):
        r = j * ATT_ROWS + i
        rs = jnp.clip(r - WIN_H // 2, 0, rows - WIN_H)
        dy0 = rs - r + (WIN_H - 1)
        q0 = pl.multiple_of(i * GRID_W, GRID_W)
        k0 = pl.multiple_of(rs * GRID_W, GRID_W)
        cs = slice(pair * LANES, (pair + 1) * LANES)
        qq = jnp.concatenate([qe_ref[pl.ds(q0, GRID_W), cs], qo_ref[pl.ds(q0, GRID_W), cs]], axis=0)
        k2 = jnp.concatenate([k_ref[pl.ds(k0, KEY_ROWS), cs], km_ref[:, cs]], axis=0)
        s = lax.dot_general(qq, k2, nt, preferred_element_type=f32)
        s = jnp.concatenate([s[:, :KEY_ROWS] + bias_ref[pair, dy0], s[:, KEY_ROWS:]], axis=1)
        return s, jnp.max(s, axis=-1, keepdims=True), k0, q0, cs

    def weights(state):
        s, m, k0, q0, cs = state
        v2 = jnp.concatenate([v_ref[pl.ds(k0, KEY_ROWS), cs], vm_ref[:, cs]], axis=0)
        e = jnp.exp2(s - m)
        l = jnp.sum(e, axis=-1, keepdims=True)
        return jnp.dot(e.astype(bf16), v2, preferred_element_type=f32), l, q0, cs

    def finish(state):
        o2, l, q0, cs = state
        o2 = o2 / l
        o = jnp.where(low, o2[:GRID_W], o2[GRID_W:])
        o_ref[pl.ds(q0, GRID_W), cs] = o.astype(bf16)

    def group_body(g, carry):
        items = [(g * ATT_GROUP + i, pair) for i in range(ATT_GROUP) for pair in range(N_HEAD_PAIRS)]
        a, b = {}, {}
        for step in range(len(items) + 2):
            if step < len(items):
                a[step] = scores(*items[step])
            if 0 <= step - 1 < len(items):
                b[step - 1] = weights(a.pop(step - 1))
            if 0 <= step - 2 < len(items):
                finish(b.pop(step - 2))
        return carry

    lax.fori_loop(0, ATT_ROWS // ATT_GROUP, group_body, 0)


def _attention(qkv, qkv_meta, slabs, bsz, t):
    rows = t // GRID_W
    assert rows >= 2 * WIN_H and rows % ATT_ROWS == 0
    steps = rows // ATT_ROWS
    tq = ATT_ROWS * GRID_W
    return pl.pallas_call(
        functools.partial(_attention_kernel, rows=rows),
        grid=(bsz, steps),
        in_specs=[pl.BlockSpec((tq, D_ATT), lambda b, j: (b * steps + j, 0)),
                  pl.BlockSpec((tq, D_ATT), lambda b, j: (b * steps + j, 1)),
                  pl.BlockSpec((t, D_ATT), lambda b, j: (b, 2)),
                  pl.BlockSpec((t, D_ATT), lambda b, j: (b, 3)),
                  pl.BlockSpec((N_META, D_ATT), lambda b, j: (0, 2)),
                  pl.BlockSpec((N_META, D_ATT), lambda b, j: (0, 3)),
                  _full(slabs)],
        out_specs=pl.BlockSpec((tq, D_ATT), lambda b, j: (b * steps + j, 0)),
        out_shape=jax.ShapeDtypeStruct((bsz * t, D_ATT), bf16),
        compiler_params=_cparams(2),
        name="attention",
    )(qkv, qkv, qkv, qkv, qkv_meta, qkv_meta, slabs)


def _layer_norm(z, g, b):
    mu = jnp.mean(z, axis=-1, keepdims=True)
    d = z - mu
    var = jnp.mean(d * d, axis=-1, keepdims=True)
    return d * lax.rsqrt(var + LN_EPS) * g + b


def _sigmoid(x):
    return 0.5 * jnp.tanh(0.5 * x) + 0.5


def _route_tile(logits, before_ref):
    tm = logits.shape[0]
    lt = logits.T
    el = lt[:N_EXPERTS]
    gl = lt[N_EXPERTS:N_EXPERTS + N_GROUPS]
    neg = -jnp.inf
    erow = lax.broadcasted_iota(jnp.int32, el.shape, 0).astype(f32)
    grow = lax.broadcasted_iota(jnp.int32, gl.shape, 0).astype(f32)
    first = lambda hit, idx, n: jnp.min(jnp.where(hit, idx, float(n)), axis=0, keepdims=True)

    gmax = jnp.max(gl, axis=0, keepdims=True)
    grp = first(gl == gmax, grow, N_GROUPS)
    pg_sel = 1.0 / jnp.sum(jnp.exp(gl - gmax), axis=0, keepdims=True)

    e_lo = grp * EXPERTS_PER_GROUP
    elm = jnp.where((erow >= e_lo) & (erow < e_lo + EXPERTS_PER_GROUP), el, neg)
    t1 = jnp.max(elm, axis=0, keepdims=True)
    e1 = first(elm == t1, erow, N_EXPERTS)
    el2 = jnp.where(erow == e1, neg, elm)
    t2 = jnp.max(el2, axis=0, keepdims=True)
    e2 = first(el2 == t2, erow, N_EXPERTS)
    r = jnp.exp(t2 - t1)
    g1 = pg_sel / (1.0 + r)
    g2 = pg_sel * r / (1.0 + r)

    hit1 = erow == e1
    hit2 = erow == e2
    onehot = jnp.where(hit1 | hit2, 1.0, 0.0)
    before = jnp.dot(onehot.astype(bf16), before_ref[...], preferred_element_type=f32)
    rank1 = jnp.sum(jnp.where(hit1, before, 0.0), axis=0, keepdims=True)
    rank2 = jnp.sum(jnp.where(hit2, before, 0.0), axis=0, keepdims=True)
    rrow = lax.broadcasted_iota(jnp.int32, (ROUTE_ROWS, tm), 0)
    route = jnp.zeros((ROUTE_ROWS, tm), f32)
    for k, val in enumerate((e1, e2, g1, g2, rank1, rank2)):
        route = jnp.where(rrow == k, val, route)
    return route, jnp.sum(onehot, axis=1, keepdims=True)


def _mixer_tail_kernel(x_ref, att_ref, rest_ref, prev_ref, next_ref, meta_ref,
                       convw_ref, convb_ref, wap_ref, wcp_ref, wout_ref, g_ref, b_ref, wr_ref, before_ref,
                       h_ref, route_ref, cnt_ref, *, tiles_per_seq):
    i = pl.program_id(0)
    pos = i % tiles_per_seq
    tm = x_ref.shape[0]
    c0, c1, c2, c3, c4 = (k * D_MODEL for k in range(5))

    ap = jnp.dot(att_ref[...], wap_ref[...], preferred_element_type=f32)
    u = rest_ref[:, c0:c1].astype(f32) * rest_ref[:, c1:c2].astype(f32)
    last = prev_ref.shape[0] - 1
    u_prev = prev_ref[last:, c0:c1].astype(f32) * prev_ref[last:, c1:c2].astype(f32)
    u_meta = meta_ref[N_META - 1:, c0:c1].astype(f32) * meta_ref[N_META - 1:, c1:c2].astype(f32)
    u_next = next_ref[:1, c0:c1].astype(f32) * next_ref[:1, c1:c2].astype(f32)
    u_prev = jnp.where(pos == 0, u_meta, u_prev)
    u_next = jnp.where(pos == tiles_per_seq - 1, jnp.zeros_like(u_next), u_next)
    row = lax.broadcasted_iota(jnp.int32, (tm, D_CONV), 0)
    u_m1 = jnp.where(row == 0, u_prev, pltpu.roll(u, 1, axis=0))
    u_p1 = jnp.where(row == tm - 1, u_next, pltpu.roll(u, tm - 1, axis=0))
    s = u_m1 * convw_ref[0:1, :] + u * convw_ref[1:2, :] + u_p1 * convw_ref[2:3, :] + convb_ref[...]
    conv_in = (rest_ref[:, c2:c3].astype(f32) * s).astype(bf16)
    cv = jnp.dot(conv_in, wcp_ref[...], preferred_element_type=f32)
    merged = (_sigmoid(rest_ref[:, c3:c4].astype(f32)) * ap
              + _sigmoid(rest_ref[:, c4:].astype(f32)) * cv)
    mix = jnp.dot(merged.astype(bf16), wout_ref[...], preferred_element_type=f32)
    h = _layer_norm(ALPHA * x_ref[...] + mix, g_ref[...], b_ref[...])
    h_ref[...] = h
    logits = jnp.dot(h.astype(bf16), wr_ref[...], preferred_element_type=f32)
    route, count = _route_tile(logits, before_ref)
    route_ref[...] = route
    cnt_ref[0] = jnp.broadcast_to(count, (N_EXPERTS, LANES))


def _mixer_tail(x, att, rest, rest_meta, p, t):
    n = x.shape[0]
    tm = TAIL_TM
    halo = 16
    hb = tm // halo
    n_halo = n // halo
    tiles_per_seq = t // tm
    before = jnp.asarray(np.triu(np.ones((tm, tm), np.float32), 1), dtype=bf16)
    consts = (rest_meta, p["conv_w"], p["conv_b"], p["w_att_proj"], p["w_conv_proj"], p["w_out"],
              p["ln1_g"], p["ln1_b"], p["w_router"], before)
    return pl.pallas_call(
        functools.partial(_mixer_tail_kernel, tiles_per_seq=tiles_per_seq),
        grid=(n // tm,),
        in_specs=[pl.BlockSpec((tm, D_MODEL), lambda i: (i, 0)),
                  pl.BlockSpec((tm, D_ATT), lambda i: (i, 0)),
                  pl.BlockSpec((tm, D_REST), lambda i: (i, 0)),
                  pl.BlockSpec((halo, 2 * D_CONV), lambda i: (jnp.maximum(i * hb - 1, 0), 0)),
                  pl.BlockSpec((halo, 2 * D_CONV), lambda i: (jnp.minimum((i + 1) * hb, n_halo - 1), 0))]
                 + [_full(c) for c in consts],
        out_specs=[pl.BlockSpec((tm, D_MODEL), lambda i: (i, 0)),
                   pl.BlockSpec((ROUTE_ROWS, tm), lambda i: (0, i)),
                   pl.BlockSpec((1, N_EXPERTS, LANES), lambda i: (i, 0, 0))],
        out_shape=[jax.ShapeDtypeStruct((n, D_MODEL), f32),
                   jax.ShapeDtypeStruct((ROUTE_ROWS, n), f32),
                   jax.ShapeDtypeStruct((n // tm, N_EXPERTS, LANES), f32)],
        compiler_params=_cparams(1),
        name="mixer_tail",
    )(x, att, rest, rest, rest, *consts)


def _route(route, cnt, n, blk):
    n_tiles = n // TAIL_TM
    tile_cnt = cnt[:, :, 0].astype(jnp.int32)
    tile_off = jnp.cumsum(tile_cnt, axis=0) - tile_cnt
    counts = jnp.sum(tile_cnt, axis=0)
    padded = (counts + blk - 1) // blk * blk
    pend = jnp.cumsum(padded)
    pstart = pend - padded
    base = pstart[None, :] + tile_off
    lanes = jnp.arange(N_EXPERTS, dtype=jnp.int32)

    def rows_of(expert_row, rank_row):
        e = expert_row.astype(jnp.int32).reshape(n_tiles, TAIL_TM, 1)
        sel = jnp.sum(jnp.where(e == lanes, base[:, None, :], 0), axis=-1)
        return sel.reshape(n) + rank_row.astype(jnp.int32)

    dest0 = rows_of(route[0], route[4])
    dest1 = rows_of(route[1], route[5])
    gate = route[2:4].T

    n_blocks = (2 * n + N_EXPERTS * (blk - 1) + blk - 1) // blk
    blk_start = jnp.arange(n_blocks, dtype=jnp.int32) * blk
    blk_expert = jnp.sum((pend[None, :] <= blk_start[:, None]).astype(jnp.int32), axis=1)
    blk_expert = jnp.minimum(blk_expert, N_EXPERTS - 1)
    blk_valid = jnp.clip((pstart + counts)[blk_expert] - blk_start, 0, blk)
    blk_valid = jnp.where(blk_start < pend[-1], blk_valid, 0).astype(jnp.int32)
    return gate, dest0, dest1, blk_expert, blk_valid, n_blocks


def _sc_mesh():
    return plsc.VectorSubcoreMesh(core_axis_name="c", subcore_axis_name="s")


def _sc_dispatch(h, dest0, dest1, p_rows):
    n, d = h.shape
    per = n // SC_WORKERS
    assert n % (SC_WORKERS * SC_IDX_WIN) == 0

    @pl.kernel(out_type=jax.ShapeDtypeStruct((p_rows, d), h.dtype), mesh=_sc_mesh(),
               scratch_types=[pltpu.VMEM((2, SC_IDX_WIN), jnp.int32), pltpu.VMEM((SC_ROWS, d), h.dtype)])
    def k(h_hbm, d0_hbm, d1_hbm, xs_hbm, idx, buf):
        base = (lax.axis_index("c") * SC_SUBCORES + lax.axis_index("s")) * per

        @pl.loop(0, per // SC_IDX_WIN)
        def _(w):
            off = base + w * SC_IDX_WIN
            pltpu.sync_copy(d0_hbm.at[pl.ds(off, SC_IDX_WIN)], idx.at[0])
            pltpu.sync_copy(d1_hbm.at[pl.ds(off, SC_IDX_WIN)], idx.at[1])
            for r in range(SC_IDX_WIN // SC_ROWS):
                pltpu.sync_copy(h_hbm.at[pl.ds(off + r * SC_ROWS, SC_ROWS)], buf)
                pltpu.sync_copy(buf, xs_hbm.at[idx.at[0, pl.ds(r * SC_ROWS, SC_ROWS)]])
                pltpu.sync_copy(buf, xs_hbm.at[idx.at[1, pl.ds(r * SC_ROWS, SC_ROWS)]])

    return k(h, dest0, dest1)


def _sc_gather2(ys, dest0, dest1):
    n = dest0.shape[0]
    d = ys.shape[1]
    per = n // SC_WORKERS
    out = jax.ShapeDtypeStruct((n, d), ys.dtype)

    @pl.kernel(out_type=[out, out], mesh=_sc_mesh(),
               scratch_types=[pltpu.VMEM((2, SC_IDX_WIN), jnp.int32), pltpu.VMEM((SC_ROWS, d), ys.dtype)])
    def k(ys_hbm, d0_hbm, d1_hbm, y0_hbm, y1_hbm, idx, buf):
        base = (lax.axis_index("c") * SC_SUBCORES + lax.axis_index("s")) * per

        @pl.loop(0, per // SC_IDX_WIN)
        def _(w):
            off = base + w * SC_IDX_WIN
            pltpu.sync_copy(d0_hbm.at[pl.ds(off, SC_IDX_WIN)], idx.at[0])
            pltpu.sync_copy(d1_hbm.at[pl.ds(off, SC_IDX_WIN)], idx.at[1])
            for r in range(SC_IDX_WIN // SC_ROWS):
                for kk, y_hbm in enumerate((y0_hbm, y1_hbm)):
                    pltpu.sync_copy(ys_hbm.at[idx.at[kk, pl.ds(r * SC_ROWS, SC_ROWS)]], buf)
                    pltpu.sync_copy(buf, y_hbm.at[pl.ds(off + r * SC_ROWS, SC_ROWS)])

    return k(ys, dest0, dest1)


def _experts_kernel(be_ref, bv_ref, xs_ref, wg_ref, wu_ref, wd_ref, ys_ref, wg_bf, wu_bf, wd_bf):
    i = pl.program_id(0)
    valid = bv_ref[i]

    @pl.when(jnp.logical_or(i == 0, be_ref[i] != be_ref[jnp.maximum(i - 1, 0)]))
    def _():
        wg_bf[...] = wg_ref[0].astype(bf16)
        wu_bf[...] = wu_ref[0].astype(bf16)
        wd_bf[...] = wd_ref[0].astype(bf16)

    @pl.when(valid > 0)
    def _():
        row = lax.broadcasted_iota(jnp.int32, xs_ref.shape, 0)
        x = jnp.where(row < valid, xs_ref[...], 0.0).astype(bf16)
        g = jnp.dot(x, wg_bf[...], preferred_element_type=f32)
        u = jnp.dot(x, wu_bf[...], preferred_element_type=f32)
        hmid = (g * _sigmoid(g) * u).astype(bf16)
        ys_ref[...] = jnp.dot(hmid, wd_bf[...], preferred_element_type=f32)

    @pl.when(valid == 0)
    def _():
        ys_ref[...] = jnp.zeros_like(ys_ref)


def _experts(xs, blk_expert, blk_valid, wg, wu, wd, n_blocks, blk):
    grid_spec = pltpu.PrefetchScalarGridSpec(
        num_scalar_prefetch=2,
        grid=(n_blocks,),
        in_specs=[pl.BlockSpec((blk, D_MODEL), lambda i, be, bv: (i, 0)),
                  pl.BlockSpec((1, D_MODEL, D_EXPERT), lambda i, be, bv: (be[i], 0, 0)),
                  pl.BlockSpec((1, D_MODEL, D_EXPERT), lambda i, be, bv: (be[i], 0, 0)),
                  pl.BlockSpec((1, D_EXPERT, D_MODEL), lambda i, be, bv: (be[i], 0, 0))],
        out_specs=pl.BlockSpec((blk, D_MODEL), lambda i, be, bv: (i, 0)),
        scratch_shapes=[pltpu.VMEM((D_MODEL, D_EXPERT), bf16),
                        pltpu.VMEM((D_MODEL, D_EXPERT), bf16),
                        pltpu.VMEM((D_EXPERT, D_MODEL), bf16)],
    )
    return pl.pallas_call(
        _experts_kernel,
        grid_spec=grid_spec,
        out_shape=jax.ShapeDtypeStruct(xs.shape, f32),
        compiler_params=_cparams(1),
        name="experts",
    )(blk_expert, blk_valid, xs, wg, wu, wd)


def _final_norm_kernel(h_ref, y0_ref, y1_ref, gate_ref, g_ref, b_ref, o_ref):
    ffn = y0_ref[...] * gate_ref[:, 0:1] + y1_ref[...] * gate_ref[:, 1:2]
    o_ref[...] = _layer_norm(ALPHA * h_ref[...] + ffn, g_ref[...], b_ref[...])


def _final_norm(h, y0, y1, gate, ln_g, ln_b):
    n = h.shape[0]
    ts = NORM_TS
    tile = pl.BlockSpec((ts, D_MODEL), lambda i: (i, 0))
    return pl.pallas_call(
        _final_norm_kernel,
        grid=(n // ts,),
        in_specs=[tile, tile, tile, pl.BlockSpec((ts, 2), lambda i: (i, 0)), _full(ln_g), _full(ln_b)],
        out_specs=tile,
        out_shape=jax.ShapeDtypeStruct((n, D_MODEL), f32),
        compiler_params=_cparams(1),
        name="final_norm",
    )(h, y0, y1, gate, ln_g, ln_b)


def _encode(x, p, qkv_meta, rest_meta, slabs):
    bsz, t, _ = x.shape
    n = bsz * t
    assert t % NORM_TS == 0 and t % TAIL_TM == 0 and t % PROJ_TM == 0
    xf = x.reshape(n, D_MODEL)
    qkv, rest = _in_proj(xf, p["w_in"], PROJ_TM)
    att = _attention(qkv, qkv_meta, slabs, bsz, t)
    h1, route, cnt = _mixer_tail(xf, att, rest, rest_meta, p, t)
    blk = MOE_BLK
    gate, dest0, dest1, blk_expert, blk_valid, n_blocks = _route(route, cnt, n, blk)
    xs = _sc_dispatch(h1, dest0, dest1, n_blocks * blk)
    ys = _experts(xs, blk_expert, blk_valid, p["w_e_gate"], p["w_e_up"], p["w_e_down"], n_blocks, blk)
    y0, y1 = _sc_gather2(ys, dest0, dest1)
    y = _final_norm(h1, y0, y1, gate, p["ln2_g"], p["ln2_b"])
    return y.reshape(bsz, t, D_MODEL)


def kernel(x_prompt, x_sample, meta_tokens, w_in, rpb, conv_w, conv_b, w_att_proj, w_conv_proj, w_out,
           ln1_g, ln1_b, w_router_group, w_router_expert, w_e_gate, w_e_up, w_e_down, ln2_g, ln2_b):
    w = w_in[0]
    a, c = 3 * D_ATT, D_CONV
    w_perm = jnp.concatenate([w[:, :a], w[:, a + c:a + 3 * c], w[:, a:a + c], w[:, a + 3 * c:]], axis=1)
    w_router = jnp.concatenate([w_router_expert[0], w_router_group[0]], axis=1)
    w_router = jnp.pad(w_router, ((0, 0), (0, LANES - w_router.shape[1])))
    row = lambda v: v[0].reshape(1, -1).astype(f32)
    p = {
        "w_in": w_perm.astype(bf16),
        "conv_w": conv_w[0].astype(f32), "conv_b": row(conv_b),
        "w_att_proj": w_att_proj[0].astype(bf16), "w_conv_proj": w_conv_proj[0].astype(bf16),
        "w_out": w_out[0].astype(bf16), "ln1_g": row(ln1_g), "ln1_b": row(ln1_b),
        "w_router": w_router.astype(bf16),
        "w_e_gate": w_e_gate[0], "w_e_up": w_e_up[0], "w_e_down": w_e_down[0],
        "ln2_g": row(ln2_g), "ln2_b": row(ln2_b),
    }
    qkv_meta, rest_meta = _in_proj(meta_tokens.astype(f32), p["w_in"], N_META)
    slabs = _bias_slabs(rpb[0])
    y_prompt = _encode(x_prompt, p, qkv_meta, rest_meta, slabs)
    y_sample = _encode(x_sample, p, qkv_meta, rest_meta, slabs)
    return (y_prompt, y_sample)
```

```python
import functools

import numpy as np
import jax
import jax.numpy as jnp
from jax import lax
from jax.experimental import pallas as pl
from jax.experimental.pallas import tpu as pltpu
from jax.experimental.pallas import tpu_sc as plsc

D_MODEL = 1024
N_META = 16
GRID_W = 64
WIN_H = 8
WIN_W = 16
N_HEADS = 8
HEAD_DIM = 64
D_ATT = N_HEADS * HEAD_DIM
D_CONV = D_MODEL
N_GROUPS = 4
EXPERTS_PER_GROUP = 8
N_EXPERTS = N_GROUPS * EXPERTS_PER_GROUP
D_EXPERT = D_MODEL // 2
DEPTH = 1
ALPHA = (2.0 * DEPTH) ** 0.25
LN_EPS = 1e-5
D_QKV = 4 * D_ATT
LOG2E = 1.4426950408889634
D_REST = 3 * D_CONV + 2 * D_MODEL
D_IN_PROJ = 3 * D_ATT + D_REST

LANES = 128
HEADS_PER_VREG = LANES // HEAD_DIM
N_HEAD_PAIRS = N_HEADS // HEADS_PER_VREG
KEY_ROWS = WIN_H * GRID_W
NEG_BIG = -1e30

PROJ_TM = 512
PROJ_TN = 512
ATT_ROWS = 8
ATT_GROUP = 4
TAIL_TM = 512
NORM_TS = 512
MOE_BLK = 512
ROUTE_ROWS = 8
VMEM_LIMIT = 56 * 1024 * 1024

SC_CORES = 2
SC_SUBCORES = 16
SC_WORKERS = SC_CORES * SC_SUBCORES
SC_IDX_WIN = 128
SC_ROWS = 64
D_PACKED = D_MODEL // 2

bf16 = jnp.bfloat16
f32 = jnp.float32


def _cparams(n_axes):
    return pltpu.CompilerParams(dimension_semantics=("arbitrary",) * n_axes,
                                vmem_limit_bytes=VMEM_LIMIT)


def _full(a):
    return pl.BlockSpec(a.shape, lambda *_: (0,) * a.ndim)


def _pack_halves(x):
    half = x.shape[1] // 2
    bits = lambda v: lax.bitcast_convert_type(v.astype(bf16).astype(f32), jnp.uint32)
    return (bits(x[:, :half]) >> 16) | bits(x[:, half:])


def _unpack_halves(p):
    lo = lax.bitcast_convert_type(p << 16, f32)
    hi = lax.bitcast_convert_type(p & jnp.uint32(0xFFFF0000), f32)
    return lo, hi


def _in_proj_kernel(x_ref, w_ref, qkv_ref, rest_ref):
    assert PROJ_TN == D_ATT
    xb = x_ref[...].astype(bf16)
    for c in range(0, D_IN_PROJ, PROJ_TN):
        y = jnp.dot(xb, w_ref[:, c:c + PROJ_TN], preferred_element_type=f32)
        if c == 0:
            y = (y * (HEAD_DIM ** -0.5 * LOG2E)).astype(bf16)
            lane = lax.broadcasted_iota(jnp.int32, y.shape, 1)
            even = (lane & HEAD_DIM) == 0
            zero = jnp.zeros_like(y)
            qkv_ref[:, :D_ATT] = jnp.where(even, y, zero)
            qkv_ref[:, D_ATT:2 * D_ATT] = jnp.where(even, zero, y)
        elif c < 3 * D_ATT:
            qkv_ref[:, c + D_ATT:c + D_ATT + PROJ_TN] = y.astype(bf16)
        else:
            rest_ref[:, c - 3 * D_ATT:c - 3 * D_ATT + PROJ_TN] = y.astype(bf16)


def _in_proj(x, w_bf, tm):
    n = x.shape[0]
    return pl.pallas_call(
        _in_proj_kernel,
        grid=(n // tm,),
        in_specs=[pl.BlockSpec((tm, D_MODEL), lambda i: (i, 0)),
                  pl.BlockSpec((D_MODEL, D_IN_PROJ), lambda i: (0, 0))],
        out_specs=[pl.BlockSpec((tm, D_QKV), lambda i: (i, 0)),
                   pl.BlockSpec((tm, D_REST), lambda i: (i, 0))],
        out_shape=[jax.ShapeDtypeStruct((n, D_QKV), bf16),
                   jax.ShapeDtypeStruct((n, D_REST), bf16)],
        compiler_params=_cparams(1),
        name="in_proj",
    )(x, w_bf)


def _bias_slabs(rpb):
    qc = np.arange(GRID_W)[:, None]
    kc = np.arange(GRID_W)[None, :]
    w_start = np.clip(qc - WIN_W // 2, 0, GRID_W - WIN_W)
    valid = (kc >= w_start) & (kc < w_start + WIN_W)
    dx_idx = np.clip(kc - qc + WIN_W - 1, 0, 2 * WIN_W - 2)
    t = rpb.astype(f32)[:, :, dx_idx] * LOG2E
    t = jnp.where(jnp.asarray(valid)[None, None], t, NEG_BIG)
    dy_idx = np.arange(WIN_H)[:, None] + np.arange(WIN_H)[None, :]
    s = t[:, dy_idx]
    s = jnp.transpose(s, (0, 1, 3, 2, 4)).reshape(N_HEADS, WIN_H, GRID_W, KEY_ROWS)
    s = s.reshape(N_HEAD_PAIRS, HEADS_PER_VREG, WIN_H, GRID_W, KEY_ROWS)
    s = jnp.transpose(s, (0, 2, 1, 3, 4))
    return s.reshape(N_HEAD_PAIRS, WIN_H, HEADS_PER_VREG * GRID_W, KEY_ROWS)


def _attention_kernel(qe_ref, qo_ref, k_ref, v_ref, km_ref, vm_ref, bias_ref, o_ref, *, rows):
    j = pl.program_id(1)
    lane = lax.broadcasted_iota(jnp.int32, (GRID_W, LANES), 1)
    low = lane < HEAD_DIM
    nt = (((1,), (1,)), ((), ()))

    def scores(i, pair):
        r = j * ATT_ROWS + i
        rs = jnp.clip(r - WIN_H // 2, 0, rows - WIN_H)
        dy0 = rs - r + (WIN_H - 1)
        q0 = pl.multiple_of(i * GRID_W, GRID_W)
        k0 = pl.multiple_of(rs * GRID_W, GRID_W)
        cs = slice(pair * LANES, (pair + 1) * LANES)
        qq = jnp.concatenate([qe_ref[pl.ds(q0, GRID_W), cs], qo_ref[pl.ds(q0, GRID_W), cs]], axis=0)
        k2 = jnp.concatenate([k_ref[pl.ds(k0, KEY_ROWS), cs], km_ref[:, cs]], axis=0)
        s = lax.dot_general(qq, k2, nt, preferred_element_type=f32)
        s = jnp.concatenate([s[:, :KEY_ROWS] + bias_ref[pair, dy0], s[:, KEY_ROWS:]], axis=1)
        return s, jnp.max(s, axis=-1, keepdims=True), k0, q0, cs

    def weights(state):
        s, m, k0, q0, cs = state
        v2 = jnp.concatenate([v_ref[pl.ds(k0, KEY_ROWS), cs], vm_ref[:, cs]], axis=0)
        e = jnp.exp2(s - m)
        l = jnp.sum(e, axis=-1, keepdims=True)
        return jnp.dot(e.astype(bf16), v2, preferred_element_type=f32), l, q0, cs

    def finish(state):
        o2, l, q0, cs = state
        o2 = o2 / l
        o = jnp.where(low, o2[:GRID_W], o2[GRID_W:])
        o_ref[pl.ds(q0, GRID_W), cs] = o.astype(bf16)

    def group_body(g, carry):
        items = [(g * ATT_GROUP + i, pair) for i in range(ATT_GROUP) for pair in range(N_HEAD_PAIRS)]
        a, b = {}, {}
        for step in range(len(items) + 2):
            if step < len(items):
                a[step] = scores(*items[step])
            if 0 <= step - 1 < len(items):
                b[step - 1] = weights(a.pop(step - 1))
            if 0 <= step - 2 < len(items):
                finish(b.pop(step - 2))
        return carry

    lax.fori_loop(0, ATT_ROWS // ATT_GROUP, group_body, 0)


def _attention(qkv, qkv_meta, slabs, bsz, t):
    rows = t // GRID_W
    assert rows >= 2 * WIN_H and rows % ATT_ROWS == 0
    steps = rows // ATT_ROWS
    tq = ATT_ROWS * GRID_W
    return pl.pallas_call(
        functools.partial(_attention_kernel, rows=rows),
        grid=(bsz, steps),
        in_specs=[pl.BlockSpec((tq, D_ATT), lambda b, j: (b * steps + j, 0)),
                  pl.BlockSpec((tq, D_ATT), lambda b, j: (b * steps + j, 1)),
                  pl.BlockSpec((t, D_ATT), lambda b, j: (b, 2)),
                  pl.BlockSpec((t, D_ATT), lambda b, j: (b, 3)),
                  pl.BlockSpec((N_META, D_ATT), lambda b, j: (0, 2)),
                  pl.BlockSpec((N_META, D_ATT), lambda b, j: (0, 3)),
                  _full(slabs)],
        out_specs=pl.BlockSpec((tq, D_ATT), lambda b, j: (b * steps + j, 0)),
        out_shape=jax.ShapeDtypeStruct((bsz * t, D_ATT), bf16),
        compiler_params=_cparams(2),
        name="attention",
    )(qkv, qkv, qkv, qkv, qkv_meta, qkv_meta, slabs)


def _layer_norm(z, g, b):
    mu = jnp.mean(z, axis=-1, keepdims=True)
    d = z - mu
    var = jnp.mean(d * d, axis=-1, keepdims=True)
    return d * lax.rsqrt(var + LN_EPS) * g + b


def _sigmoid(x):
    return 0.5 * jnp.tanh(0.5 * x) + 0.5


def _route_tile(logits, before_ref):
    tm = logits.shape[0]
    lt = logits.T
    el = lt[:N_EXPERTS]
    gl = lt[N_EXPERTS:N_EXPERTS + N_GROUPS]
    neg = -jnp.inf
    erow = lax.broadcasted_iota(jnp.int32, el.shape, 0).astype(f32)
    grow = lax.broadcasted_iota(jnp.int32, gl.shape, 0).astype(f32)
    first = lambda hit, idx, n: jnp.min(jnp.where(hit, idx, float(n)), axis=0, keepdims=True)

    gmax = jnp.max(gl, axis=0, keepdims=True)
    grp = first(gl == gmax, grow, N_GROUPS)
    pg_sel = 1.0 / jnp.sum(jnp.exp(gl - gmax), axis=0, keepdims=True)

    e_lo = grp * EXPERTS_PER_GROUP
    elm = jnp.where((erow >= e_lo) & (erow < e_lo + EXPERTS_PER_GROUP), el, neg)
    t1 = jnp.max(elm, axis=0, keepdims=True)
    e1 = first(elm == t1, erow, N_EXPERTS)
    el2 = jnp.where(erow == e1, neg, elm)
    t2 = jnp.max(el2, axis=0, keepdims=True)
    e2 = first(el2 == t2, erow, N_EXPERTS)
    r = jnp.exp(t2 - t1)
    g1 = pg_sel / (1.0 + r)
    g2 = pg_sel * r / (1.0 + r)

    hit1 = erow == e1
    hit2 = erow == e2
    onehot = jnp.where(hit1 | hit2, 1.0, 0.0)
    before = jnp.dot(onehot.astype(bf16), before_ref[...], preferred_element_type=f32)
    rank1 = jnp.sum(jnp.where(hit1, before, 0.0), axis=0, keepdims=True)
    rank2 = jnp.sum(jnp.where(hit2, before, 0.0), axis=0, keepdims=True)
    rrow = lax.broadcasted_iota(jnp.int32, (ROUTE_ROWS, tm), 0)
    route = jnp.zeros((ROUTE_ROWS, tm), f32)
    for k, val in enumerate((e1, e2, g1, g2, rank1, rank2)):
        route = jnp.where(rrow == k, val, route)
    return route, jnp.sum(onehot, axis=1, keepdims=True)


def _mixer_tail_kernel(x_ref, att_ref, rest_ref, prev_ref, next_ref, meta_ref,
                       convw_ref, convb_ref, wap_ref, wcp_ref, wout_ref, g_ref, b_ref, wr_ref, before_ref,
                       h_ref, hp_ref, route_ref, cnt_ref, *, tiles_per_seq):
    i = pl.program_id(0)
    pos = i % tiles_per_seq
    tm = x_ref.shape[0]
    c0, c1, c2, c3, c4 = (k * D_MODEL for k in range(5))

    ap = jnp.dot(att_ref[...], wap_ref[...], preferred_element_type=f32)
    u = rest_ref[:, c0:c1].astype(f32) * rest_ref[:, c1:c2].astype(f32)
    last = prev_ref.shape[0] - 1
    u_prev = prev_ref[last:, c0:c1].astype(f32) * prev_ref[last:, c1:c2].astype(f32)
    u_meta = meta_ref[N_META - 1:, c0:c1].astype(f32) * meta_ref[N_META - 1:, c1:c2].astype(f32)
    u_next = next_ref[:1, c0:c1].astype(f32) * next_ref[:1, c1:c2].astype(f32)
    u_prev = jnp.where(pos == 0, u_meta, u_prev)
    u_next = jnp.where(pos == tiles_per_seq - 1, jnp.zeros_like(u_next), u_next)
    row = lax.broadcasted_iota(jnp.int32, (tm, D_CONV), 0)
    u_m1 = jnp.where(row == 0, u_prev, pltpu.roll(u, 1, axis=0))
    u_p1 = jnp.where(row == tm - 1, u_next, pltpu.roll(u, tm - 1, axis=0))
    s = u_m1 * convw_ref[0:1, :] + u * convw_ref[1:2, :] + u_p1 * convw_ref[2:3, :] + convb_ref[...]
    conv_in = (rest_ref[:, c2:c3].astype(f32) * s).astype(bf16)
    cv = jnp.dot(conv_in, wcp_ref[...], preferred_element_type=f32)
    merged = (_sigmoid(rest_ref[:, c3:c4].astype(f32)) * ap
              + _sigmoid(rest_ref[:, c4:].astype(f32)) * cv)
    mix = jnp.dot(merged.astype(bf16), wout_ref[...], preferred_element_type=f32)
    h = _layer_norm(ALPHA * x_ref[...] + mix, g_ref[...], b_ref[...])
    h_ref[...] = h
    hp_ref[...] = _pack_halves(h)
    logits = jnp.dot(h.astype(bf16), wr_ref[...], preferred_element_type=f32)
    route, count = _route_tile(logits, before_ref)
    route_ref[...] = route
    cnt_ref[0] = jnp.broadcast_to(count, (N_EXPERTS, LANES))


def _mixer_tail(x, att, rest, rest_meta, p, t):
    n = x.shape[0]
    tm = TAIL_TM
    halo = 16
    hb = tm // halo
    n_halo = n // halo
    tiles_per_seq = t // tm
    before = jnp.asarray(np.triu(np.ones((tm, tm), np.float32), 1), dtype=bf16)
    consts = (rest_meta, p["conv_w"], p["conv_b"], p["w_att_proj"], p["w_conv_proj"], p["w_out"],
              p["ln1_g"], p["ln1_b"], p["w_router"], before)
    return pl.pallas_call(
        functools.partial(_mixer_tail_kernel, tiles_per_seq=tiles_per_seq),
        grid=(n // tm,),
        in_specs=[pl.BlockSpec((tm, D_MODEL), lambda i: (i, 0)),
                  pl.BlockSpec((tm, D_ATT), lambda i: (i, 0)),
                  pl.BlockSpec((tm, D_REST), lambda i: (i, 0)),
                  pl.BlockSpec((halo, 2 * D_CONV), lambda i: (jnp.maximum(i * hb - 1, 0), 0)),
                  pl.BlockSpec((halo, 2 * D_CONV), lambda i: (jnp.minimum((i + 1) * hb, n_halo - 1), 0))]
                 + [_full(c) for c in consts],
        out_specs=[pl.BlockSpec((tm, D_MODEL), lambda i: (i, 0)),
                   pl.BlockSpec((tm, D_PACKED), lambda i: (i, 0)),
                   pl.BlockSpec((ROUTE_ROWS, tm), lambda i: (0, i)),
                   pl.BlockSpec((1, N_EXPERTS, LANES), lambda i: (i, 0, 0))],
        out_shape=[jax.ShapeDtypeStruct((n, D_MODEL), f32),
                   jax.ShapeDtypeStruct((n, D_PACKED), jnp.uint32),
                   jax.ShapeDtypeStruct((ROUTE_ROWS, n), f32),
                   jax.ShapeDtypeStruct((n // tm, N_EXPERTS, LANES), f32)],
        compiler_params=_cparams(1),
        name="mixer_tail",
    )(x, att, rest, rest, rest, *consts)


def _route(route, cnt, n, blk):
    n_tiles = n // TAIL_TM
    tile_cnt = cnt[:, :, 0].astype(jnp.int32)
    tile_off = jnp.cumsum(tile_cnt, axis=0) - tile_cnt
    counts = jnp.sum(tile_cnt, axis=0)
    padded = (counts + blk - 1) // blk * blk
    pend = jnp.cumsum(padded)
    pstart = pend - padded
    base = pstart[None, :] + tile_off
    lanes = jnp.arange(N_EXPERTS, dtype=jnp.int32)

    def rows_of(expert_row, rank_row):
        e = expert_row.astype(jnp.int32).reshape(n_tiles, TAIL_TM, 1)
        sel = jnp.sum(jnp.where(e == lanes, base[:, None, :], 0), axis=-1)
        return sel.reshape(n) + rank_row.astype(jnp.int32)

    dest0 = rows_of(route[0], route[4])
    dest1 = rows_of(route[1], route[5])
    gate = route[2:4].T

    n_blocks = (2 * n + N_EXPERTS * (blk - 1) + blk - 1) // blk
    blk_start = jnp.arange(n_blocks, dtype=jnp.int32) * blk
    blk_expert = jnp.sum((pend[None, :] <= blk_start[:, None]).astype(jnp.int32), axis=1)
    blk_expert = jnp.minimum(blk_expert, N_EXPERTS - 1)
    blk_valid = jnp.clip((pstart + counts)[blk_expert] - blk_start, 0, blk)
    blk_valid = jnp.where(blk_start < pend[-1], blk_valid, 0).astype(jnp.int32)
    return gate, dest0, dest1, blk_expert, blk_valid, n_blocks


def _sc_mesh():
    return plsc.VectorSubcoreMesh(core_axis_name="c", subcore_axis_name="s")


def _sc_dispatch(h, dest0, dest1, p_rows):
    n, d = h.shape
    per = n // SC_WORKERS
    assert n % (SC_WORKERS * SC_IDX_WIN) == 0

    @pl.kernel(out_type=jax.ShapeDtypeStruct((p_rows, d), h.dtype), mesh=_sc_mesh(),
               scratch_types=[pltpu.VMEM((2, SC_IDX_WIN), jnp.int32), pltpu.VMEM((SC_ROWS, d), h.dtype)])
    def k(h_hbm, d0_hbm, d1_hbm, xs_hbm, idx, buf):
        base = (lax.axis_index("c") * SC_SUBCORES + lax.axis_index("s")) * per

        @pl.loop(0, per // SC_IDX_WIN)
        def _(w):
            off = base + w * SC_IDX_WIN
            pltpu.sync_copy(d0_hbm.at[pl.ds(off, SC_IDX_WIN)], idx.at[0])
            pltpu.sync_copy(d1_hbm.at[pl.ds(off, SC_IDX_WIN)], idx.at[1])
            for r in range(SC_IDX_WIN // SC_ROWS):
                pltpu.sync_copy(h_hbm.at[pl.ds(off + r * SC_ROWS, SC_ROWS)], buf)
                pltpu.sync_copy(buf, xs_hbm.at[idx.at[0, pl.ds(r * SC_ROWS, SC_ROWS)]])
                pltpu.sync_copy(buf, xs_hbm.at[idx.at[1, pl.ds(r * SC_ROWS, SC_ROWS)]])

    return k(h, dest0, dest1)


def _sc_gather2(ys, dest0, dest1):
    n = dest0.shape[0]
    d = ys.shape[1]
    per = n // SC_WORKERS
    out = jax.ShapeDtypeStruct((n, d), ys.dtype)

    @pl.kernel(out_type=[out, out], mesh=_sc_mesh(),
               scratch_types=[pltpu.VMEM((2, SC_IDX_WIN), jnp.int32), pltpu.VMEM((SC_ROWS, d), ys.dtype)])
    def k(ys_hbm, d0_hbm, d1_hbm, y0_hbm, y1_hbm, idx, buf):
        base = (lax.axis_index("c") * SC_SUBCORES + lax.axis_index("s")) * per

        @pl.loop(0, per // SC_IDX_WIN)
        def _(w):
            off = base + w * SC_IDX_WIN
            pltpu.sync_copy(d0_hbm.at[pl.ds(off, SC_IDX_WIN)], idx.at[0])
            pltpu.sync_copy(d1_hbm.at[pl.ds(off, SC_IDX_WIN)], idx.at[1])
            for r in range(SC_IDX_WIN // SC_ROWS):
                for kk, y_hbm in enumerate((y0_hbm, y1_hbm)):
                    pltpu.sync_copy(ys_hbm.at[idx.at[kk, pl.ds(r * SC_ROWS, SC_ROWS)]], buf)
                    pltpu.sync_copy(buf, y_hbm.at[pl.ds(off + r * SC_ROWS, SC_ROWS)])

    return k(ys, dest0, dest1)


def _experts_kernel(be_ref, bv_ref, xs_ref, wg_ref, wu_ref, wd_ref, ys_ref, wg_bf, wu_bf, wd_bf):
    i = pl.program_id(0)
    valid = bv_ref[i]

    @pl.when(jnp.logical_or(i == 0, be_ref[i] != be_ref[jnp.maximum(i - 1, 0)]))
    def _():
        wg_bf[...] = wg_ref[0].astype(bf16)
        wu_bf[...] = wu_ref[0].astype(bf16)
        wd_bf[...] = wd_ref[0].astype(bf16)

    @pl.when(valid > 0)
    def _():
        row = lax.broadcasted_iota(jnp.int32, xs_ref.shape, 0)
        keep = row < valid
        lo, hi = _unpack_halves(xs_ref[...])
        x = jnp.concatenate([jnp.where(keep, lo, 0.0), jnp.where(keep, hi, 0.0)], axis=1).astype(bf16)
        g = jnp.dot(x, wg_bf[...], preferred_element_type=f32)
        u = jnp.dot(x, wu_bf[...], preferred_element_type=f32)
        hmid = (g * _sigmoid(g) * u).astype(bf16)
        ys_ref[...] = _pack_halves(jnp.dot(hmid, wd_bf[...], preferred_element_type=f32))

    @pl.when(valid == 0)
    def _():
        ys_ref[...] = jnp.zeros_like(ys_ref)


def _experts(xs, blk_expert, blk_valid, wg, wu, wd, n_blocks, blk):
    grid_spec = pltpu.PrefetchScalarGridSpec(
        num_scalar_prefetch=2,
        grid=(n_blocks,),
        in_specs=[pl.BlockSpec((blk, D_PACKED), lambda i, be, bv: (i, 0)),
                  pl.BlockSpec((1, D_MODEL, D_EXPERT), lambda i, be, bv: (be[i], 0, 0)),
                  pl.BlockSpec((1, D_MODEL, D_EXPERT), lambda i, be, bv: (be[i], 0, 0)),
                  pl.BlockSpec((1, D_EXPERT, D_MODEL), lambda i, be, bv: (be[i], 0, 0))],
        out_specs=pl.BlockSpec((blk, D_PACKED), lambda i, be, bv: (i, 0)),
        scratch_shapes=[pltpu.VMEM((D_MODEL, D_EXPERT), bf16),
                        pltpu.VMEM((D_MODEL, D_EXPERT), bf16),
                        pltpu.VMEM((D_EXPERT, D_MODEL), bf16)],
    )
    return pl.pallas_call(
        _experts_kernel,
        grid_spec=grid_spec,
        out_shape=jax.ShapeDtypeStruct(xs.shape, jnp.uint32),
        compiler_params=_cparams(1),
        name="experts",
    )(blk_expert, blk_valid, xs, wg, wu, wd)


def _final_norm_kernel(h_ref, y0_ref, y1_ref, gate_ref, g_ref, b_ref, o_ref):
    g0, g1 = gate_ref[:, 0:1], gate_ref[:, 1:2]
    lo0, hi0 = _unpack_halves(y0_ref[...])
    lo1, hi1 = _unpack_halves(y1_ref[...])
    ffn = jnp.concatenate([lo0 * g0 + lo1 * g1, hi0 * g0 + hi1 * g1], axis=1)
    o_ref[...] = _layer_norm(ALPHA * h_ref[...] + ffn, g_ref[...], b_ref[...])


def _final_norm(h, y0, y1, gate, ln_g, ln_b):
    n = h.shape[0]
    ts = NORM_TS
    tile = pl.BlockSpec((ts, D_MODEL), lambda i: (i, 0))
    packed = pl.BlockSpec((ts, D_PACKED), lambda i: (i, 0))
    return pl.pallas_call(
        _final_norm_kernel,
        grid=(n // ts,),
        in_specs=[tile, packed, packed, pl.BlockSpec((ts, 2), lambda i: (i, 0)), _full(ln_g), _full(ln_b)],
        out_specs=tile,
        out_shape=jax.ShapeDtypeStruct((n, D_MODEL), f32),
        compiler_params=_cparams(1),
        name="final_norm",
    )(h, y0, y1, gate, ln_g, ln_b)


def _encode(x, p, qkv_meta, rest_meta, slabs):
    bsz, t, _ = x.shape
    n = bsz * t
    assert t % NORM_TS == 0 and t % TAIL_TM == 0 and t % PROJ_TM == 0
    xf = x.reshape(n, D_MODEL)
    qkv, rest = _in_proj(xf, p["w_in"], PROJ_TM)
    att = _attention(qkv, qkv_meta, slabs, bsz, t)
    h1, h1_packed, route, cnt = _mixer_tail(xf, att, rest, rest_meta, p, t)
    blk = MOE_BLK
    gate, dest0, dest1, blk_expert, blk_valid, n_blocks = _route(route, cnt, n, blk)
    xs = _sc_dispatch(h1_packed, dest0, dest1, n_blocks * blk)
    ys = _experts(xs, blk_expert, blk_valid, p["w_e_gate"], p["w_e_up"], p["w_e_down"], n_blocks, blk)
    y0, y1 = _sc_gather2(ys, dest0, dest1)
    y = _final_norm(h1, y0, y1, gate, p["ln2_g"], p["ln2_b"])
    return y.reshape(bsz, t, D_MODEL)


def kernel(x_prompt, x_sample, meta_tokens, w_in, rpb, conv_w, conv_b, w_att_proj, w_conv_proj, w_out,
           ln1_g, ln1_b, w_router_group, w_router_expert, w_e_gate, w_e_up, w_e_down, ln2_g, ln2_b):
    w = w_in[0]
    a, c = 3 * D_ATT, D_CONV
    w_perm = jnp.concatenate([w[:, :a], w[:, a + c:a + 3 * c], w[:, a:a + c], w[:, a + 3 * c:]], axis=1)
    w_router = jnp.concatenate([w_router_expert[0], w_router_group[0]], axis=1)
    w_router = jnp.pad(w_router, ((0, 0), (0, LANES - w_router.shape[1])))
    row = lambda v: v[0].reshape(1, -1).astype(f32)
    p = {
        "w_in": w_perm.astype(bf16),
        "conv_w": conv_w[0].astype(f32), "conv_b": row(conv_b),
        "w_att_proj": w_att_proj[0].astype(bf16), "w_conv_proj": w_conv_proj[0].astype(bf16),
        "w_out": w_out[0].astype(bf16), "ln1_g": row(ln1_g), "ln1_b": row(ln1_b),
        "w_router": w_router.astype(bf16),
        "w_e_gate": w_e_gate[0], "w_e_up": w_e_up[0], "w_e_down": w_e_down[0],
        "ln2_g": row(ln2_g), "ln2_b": row(ln2_b),
    }
    qkv_meta, rest_meta = _in_proj(meta_tokens.astype(f32), p["w_in"], N_META)
    slabs = _bias_slabs(rpb[0])
    y_prompt = _encode(x_prompt, p, qkv_meta, rest_meta, slabs)
    y_sample = _encode(x_sample, p, qkv_meta, rest_meta, slabs)
    return (y_prompt, y_sample)
```

```python
import functools

import numpy as np
import jax
import jax.numpy as jnp
from jax import lax
from jax.experimental import pallas as pl
from jax.experimental.pallas import tpu as pltpu
from jax.experimental.pallas import tpu_sc as plsc

D_MODEL = 1024
N_META = 16
GRID_W = 64
WIN_H = 8
WIN_W = 16
N_HEADS = 8
HEAD_DIM = 64
D_ATT = N_HEADS * HEAD_DIM
D_CONV = D_MODEL
N_GROUPS = 4
EXPERTS_PER_GROUP = 8
N_EXPERTS = N_GROUPS * EXPERTS_PER_GROUP
D_EXPERT = D_MODEL // 2
DEPTH = 1
ALPHA = (2.0 * DEPTH) ** 0.25
LN_EPS = 1e-5
D_QKV = 4 * D_ATT
LOG2E = 1.4426950408889634
D_REST = 3 * D_CONV + 2 * D_MODEL
D_IN_PROJ = 3 * D_ATT + D_REST

LANES = 128
HEADS_PER_VREG = LANES // HEAD_DIM
N_HEAD_PAIRS = N_HEADS // HEADS_PER_VREG
KEY_ROWS = WIN_H * GRID_W
NEG_BIG = -1e30

PROJ_TM = 512
PROJ_TN = 512
ATT_ROWS = 8
ATT_GROUP = 4
TAIL_TM = 512
TAIL_CW = 256
NORM_TS = 512
MOE_BLK = 512
ROUTE_ROWS = 8
VMEM_LIMIT = 56 * 1024 * 1024

SC_CORES = 2
SC_SUBCORES = 16
SC_WORKERS = SC_CORES * SC_SUBCORES
SC_IDX_WIN = 128
SC_ROWS = 64
D_PACKED = D_MODEL // 2

bf16 = jnp.bfloat16
f32 = jnp.float32


def _cparams(n_axes):
    return pltpu.CompilerParams(dimension_semantics=("arbitrary",) * n_axes,
                                vmem_limit_bytes=VMEM_LIMIT)


def _full(a):
    return pl.BlockSpec(a.shape, lambda *_: (0,) * a.ndim)


def _pack_halves(x):
    half = x.shape[1] // 2
    bits = lambda v: lax.bitcast_convert_type(v.astype(bf16).astype(f32), jnp.uint32)
    return (bits(x[:, :half]) >> 16) | bits(x[:, half:])


def _unpack_halves(p):
    lo = lax.bitcast_convert_type(p << 16, f32)
    hi = lax.bitcast_convert_type(p & jnp.uint32(0xFFFF0000), f32)
    return lo, hi


def _in_proj_kernel(x_ref, w_ref, qkv_ref, rest_ref):
    assert PROJ_TN == D_ATT
    xb = x_ref[...].astype(bf16)
    for c in range(0, D_IN_PROJ, PROJ_TN):
        y = jnp.dot(xb, w_ref[:, c:c + PROJ_TN], preferred_element_type=f32)
        if c == 0:
            y = (y * (HEAD_DIM ** -0.5 * LOG2E)).astype(bf16)
            lane = lax.broadcasted_iota(jnp.int32, y.shape, 1)
            even = (lane & HEAD_DIM) == 0
            zero = jnp.zeros_like(y)
            qkv_ref[:, :D_ATT] = jnp.where(even, y, zero)
            qkv_ref[:, D_ATT:2 * D_ATT] = jnp.where(even, zero, y)
        elif c < 3 * D_ATT:
            qkv_ref[:, c + D_ATT:c + D_ATT + PROJ_TN] = y.astype(bf16)
        else:
            rest_ref[:, c - 3 * D_ATT:c - 3 * D_ATT + PROJ_TN] = y.astype(bf16)


def _in_proj(x, w_bf, tm):
    n = x.shape[0]
    return pl.pallas_call(
        _in_proj_kernel,
        grid=(n // tm,),
        in_specs=[pl.BlockSpec((tm, D_MODEL), lambda i: (i, 0)),
                  pl.BlockSpec((D_MODEL, D_IN_PROJ), lambda i: (0, 0))],
        out_specs=[pl.BlockSpec((tm, D_QKV), lambda i: (i, 0)),
                   pl.BlockSpec((tm, D_REST), lambda i: (i, 0))],
        out_shape=[jax.ShapeDtypeStruct((n, D_QKV), bf16),
                   jax.ShapeDtypeStruct((n, D_REST), bf16)],
        compiler_params=_cparams(1),
        name="in_proj",
    )(x, w_bf)


def _bias_slabs(rpb):
    qc = np.arange(GRID_W)[:, None]
    kc = np.arange(GRID_W)[None, :]
    w_start = np.clip(qc - WIN_W // 2, 0, GRID_W - WIN_W)
    valid = (kc >= w_start) & (kc < w_start + WIN_W)
    dx_idx = np.clip(kc - qc + WIN_W - 1, 0, 2 * WIN_W - 2)
    t = rpb.astype(f32)[:, :, dx_idx] * LOG2E
    t = jnp.where(jnp.asarray(valid)[None, None], t, NEG_BIG)
    dy_idx = np.arange(WIN_H)[:, None] + np.arange(WIN_H)[None, :]
    s = t[:, dy_idx]
    s = jnp.transpose(s, (0, 1, 3, 2, 4)).reshape(N_HEADS, WIN_H, GRID_W, KEY_ROWS)
    s = s.reshape(N_HEAD_PAIRS, HEADS_PER_VREG, WIN_H, GRID_W, KEY_ROWS)
    s = jnp.transpose(s, (0, 2, 1, 3, 4))
    return s.reshape(N_HEAD_PAIRS, WIN_H, HEADS_PER_VREG * GRID_W, KEY_ROWS)


def _attention_kernel(qe_ref, qo_ref, k_ref, v_ref, km_ref, vm_ref, bias_ref, o_ref, *, rows):
    j = pl.program_id(1)
    lane = lax.broadcasted_iota(jnp.int32, (GRID_W, LANES), 1)
    low = lane < HEAD_DIM
    nt = (((1,), (1,)), ((), ()))

    def scores(i, pair):
        r = j * ATT_ROWS + i
        rs = jnp.clip(r - WIN_H // 2, 0, rows - WIN_H)
        dy0 = rs - r + (WIN_H - 1)
        q0 = pl.multiple_of(i * GRID_W, GRID_W)
        k0 = pl.multiple_of(rs * GRID_W, GRID_W)
        cs = slice(pair * LANES, (pair + 1) * LANES)
        qq = jnp.concatenate([qe_ref[pl.ds(q0, GRID_W), cs], qo_ref[pl.ds(q0, GRID_W), cs]], axis=0)
        k2 = jnp.concatenate([k_ref[pl.ds(k0, KEY_ROWS), cs], km_ref[:, cs]], axis=0)
        s = lax.dot_general(qq, k2, nt, preferred_element_type=f32)
        s = jnp.concatenate([s[:, :KEY_ROWS] + bias_ref[pair, dy0], s[:, KEY_ROWS:]], axis=1)
        return s, jnp.max(s, axis=-1, keepdims=True), k0, q0, cs

    def weights(state):
        s, m, k0, q0, cs = state
        v2 = jnp.concatenate([v_ref[pl.ds(k0, KEY_ROWS), cs], vm_ref[:, cs]], axis=0)
        e = jnp.exp2(s - m)
        l = jnp.sum(e, axis=-1, keepdims=True)
        return jnp.dot(e.astype(bf16), v2, preferred_element_type=f32), l, q0, cs

    def finish(state):
        o2, l, q0, cs = state
        o2 = o2 / l
        o = jnp.where(low, o2[:GRID_W], o2[GRID_W:])
        o_ref[pl.ds(q0, GRID_W), cs] = o.astype(bf16)

    def group_body(g, carry):
        items = [(g * ATT_GROUP + i, pair) for i in range(ATT_GROUP) for pair in range(N_HEAD_PAIRS)]
        a, b = {}, {}
        for step in range(len(items) + 2):
            if step < len(items):
                a[step] = scores(*items[step])
            if 0 <= step - 1 < len(items):
                b[step - 1] = weights(a.pop(step - 1))
            if 0 <= step - 2 < len(items):
                finish(b.pop(step - 2))
        return carry

    lax.fori_loop(0, ATT_ROWS // ATT_GROUP, group_body, 0)


def _attention(qkv, qkv_meta, slabs, bsz, t):
    rows = t // GRID_W
    assert rows >= 2 * WIN_H and rows % ATT_ROWS == 0
    steps = rows // ATT_ROWS
    tq = ATT_ROWS * GRID_W
    return pl.pallas_call(
        functools.partial(_attention_kernel, rows=rows),
        grid=(bsz, steps),
        in_specs=[pl.BlockSpec((tq, D_ATT), lambda b, j: (b * steps + j, 0)),
                  pl.BlockSpec((tq, D_ATT), lambda b, j: (b * steps + j, 1)),
                  pl.BlockSpec((t, D_ATT), lambda b, j: (b, 2)),
                  pl.BlockSpec((t, D_ATT), lambda b, j: (b, 3)),
                  pl.BlockSpec((N_META, D_ATT), lambda b, j: (0, 2)),
                  pl.BlockSpec((N_META, D_ATT), lambda b, j: (0, 3)),
                  _full(slabs)],
        out_specs=pl.BlockSpec((tq, D_ATT), lambda b, j: (b * steps + j, 0)),
        out_shape=jax.ShapeDtypeStruct((bsz * t, D_ATT), bf16),
        compiler_params=_cparams(2),
        name="attention",
    )(qkv, qkv, qkv, qkv, qkv_meta, qkv_meta, slabs)


def _layer_norm(z, g, b):
    mu = jnp.mean(z, axis=-1, keepdims=True)
    d = z - mu
    var = jnp.mean(d * d, axis=-1, keepdims=True)
    return d * lax.rsqrt(var + LN_EPS) * g + b


def _sigmoid(x):
    return 0.5 * jnp.tanh(0.5 * x) + 0.5


def _route_tile(logits, before_ref):
    tm = logits.shape[0]
    lt = logits.T
    el = lt[:N_EXPERTS]
    gl = lt[N_EXPERTS:N_EXPERTS + N_GROUPS]
    neg = -jnp.inf
    erow = lax.broadcasted_iota(jnp.int32, el.shape, 0).astype(f32)
    grow = lax.broadcasted_iota(jnp.int32, gl.shape, 0).astype(f32)
    first = lambda hit, idx, n: jnp.min(jnp.where(hit, idx, float(n)), axis=0, keepdims=True)

    gmax = jnp.max(gl, axis=0, keepdims=True)
    grp = first(gl == gmax, grow, N_GROUPS)
    pg_sel = 1.0 / jnp.sum(jnp.exp(gl - gmax), axis=0, keepdims=True)

    e_lo = grp * EXPERTS_PER_GROUP
    elm = jnp.where((erow >= e_lo) & (erow < e_lo + EXPERTS_PER_GROUP), el, neg)
    t1 = jnp.max(elm, axis=0, keepdims=True)
    e1 = first(elm == t1, erow, N_EXPERTS)
    el2 = jnp.where(erow == e1, neg, elm)
    t2 = jnp.max(el2, axis=0, keepdims=True)
    e2 = first(el2 == t2, erow, N_EXPERTS)
    r = jnp.exp(t2 - t1)
    g1 = pg_sel / (1.0 + r)
    g2 = pg_sel * r / (1.0 + r)

    hit1 = erow == e1
    hit2 = erow == e2
    onehot = jnp.where(hit1 | hit2, 1.0, 0.0)
    before = jnp.dot(onehot.astype(bf16), before_ref[...], preferred_element_type=f32)
    rank1 = jnp.sum(jnp.where(hit1, before, 0.0), axis=0, keepdims=True)
    rank2 = jnp.sum(jnp.where(hit2, before, 0.0), axis=0, keepdims=True)
    rrow = lax.broadcasted_iota(jnp.int32, (ROUTE_ROWS, tm), 0)
    route = jnp.zeros((ROUTE_ROWS, tm), f32)
    for k, val in enumerate((e1, e2, g1, g2, rank1, rank2)):
        route = jnp.where(rrow == k, val, route)
    return route, jnp.sum(onehot, axis=1, keepdims=True)


def _mixer_tail_kernel(x_ref, att_ref, rest_ref, prev_ref, next_ref, meta_ref,
                       convw_ref, convb_ref, wap_ref, wcp_ref, wout_ref, g_ref, b_ref, wr_ref, before_ref,
                       h_ref, hp_ref, route_ref, cnt_ref, merged_scr, *, tiles_per_seq, n_tiles):
    i = pl.program_id(0)
    pos = jnp.minimum(i, n_tiles - 1) % tiles_per_seq
    tm = x_ref.shape[0]
    c0, c1, c2, c3, c4 = (k * D_MODEL for k in range(5))
    cols = lambda j, base=0: slice(base + j * TAIL_CW, base + (j + 1) * TAIL_CW)
    chunks = range(D_MODEL // TAIL_CW)

    @pl.when(i == 0)
    def _():
        merged_scr[...] = jnp.zeros_like(merged_scr)

    def u_of(ref, rws, j):
        return ref[rws, cols(j, c0)].astype(f32) * ref[rws, cols(j, c1)].astype(f32)

    last = prev_ref.shape[0] - 1
    row = lax.broadcasted_iota(jnp.int32, (tm, TAIL_CW), 0)

    def conv_chunk(j):
        u = u_of(rest_ref, slice(0, tm), j)
        u_prev = jnp.where(pos == 0, u_of(meta_ref, slice(N_META - 1, N_META), j),
                           u_of(prev_ref, slice(last, last + 1), j))
        u_next = u_of(next_ref, slice(0, 1), j)
        u_next = jnp.where(pos == tiles_per_seq - 1, jnp.zeros_like(u_next), u_next)
        u_m1 = jnp.where(row == 0, u_prev, pltpu.roll(u, 1, axis=0))
        u_p1 = jnp.where(row == tm - 1, u_next, pltpu.roll(u, tm - 1, axis=0))
        cw = convw_ref[:, cols(j)]
        s = u_m1 * cw[0:1] + u * cw[1:2] + u_p1 * cw[2:3] + convb_ref[:, cols(j)]
        return (rest_ref[:, cols(j, c2)].astype(f32) * s).astype(bf16)

    mix = [jnp.dot(merged_scr[...], wout_ref[:, cols(j)], preferred_element_type=f32) for j in chunks]

    cv = None
    for j in chunks:
        part = jnp.dot(conv_chunk(j), wcp_ref[cols(j), :], preferred_element_type=f32)
        cv = part if cv is None else cv + part
    ap = jnp.dot(att_ref[...], wap_ref[...], preferred_element_type=f32)

    z = [ALPHA * x_ref[:, cols(j)] + mix[j] for j in chunks]
    mu = sum(jnp.sum(zj, axis=-1, keepdims=True) for zj in z) * (1.0 / D_MODEL)
    d = [zj - mu for zj in z]
    var = sum(jnp.sum(dj * dj, axis=-1, keepdims=True) for dj in d) * (1.0 / D_MODEL)
    rstd = lax.rsqrt(var + LN_EPS)
    logits = None
    hs = []
    for j in chunks:
        hj = d[j] * rstd * g_ref[:, cols(j)] + b_ref[:, cols(j)]
        h_ref[:, cols(j)] = hj
        hs.append(hj)
        part = jnp.dot(hj.astype(bf16), wr_ref[cols(j), :], preferred_element_type=f32)
        logits = part if logits is None else logits + part
    hp_ref[...] = _pack_halves(jnp.concatenate(hs, axis=1))

    for j in chunks:
        gate_a = jnp.tanh(rest_ref[:, cols(j, c3)]) + 1.0
        gate_c = jnp.tanh(rest_ref[:, cols(j, c4)]) + 1.0
        merged_scr[:, cols(j)] = gate_a * ap[:, cols(j)].astype(bf16) + gate_c * cv[:, cols(j)].astype(bf16)

    route, count = _route_tile(logits, before_ref)
    route_ref[...] = route
    cnt_ref[0] = jnp.broadcast_to(count, (N_EXPERTS, LANES))


def _mixer_tail(x, att, rest, rest_meta, p, t):
    n = x.shape[0]
    tm = TAIL_TM
    halo = 16
    hb = tm // halo
    n_halo = n // halo
    n_tiles = n // tm
    tiles_per_seq = t // tm
    before = jnp.asarray(np.triu(np.ones((tm, tm), np.float32), 1), dtype=bf16)
    consts = (rest_meta, p["conv_w"], p["conv_b"], p["w_att_proj"], p["w_conv_proj"], p["w_out"],
              p["ln1_g"], p["ln1_b"], p["w_router"], before)
    front = lambda i: jnp.minimum(i, n_tiles - 1)
    back = lambda i: jnp.maximum(i - 1, 0)
    return pl.pallas_call(
        functools.partial(_mixer_tail_kernel, tiles_per_seq=tiles_per_seq, n_tiles=n_tiles),
        grid=(n_tiles + 1,),
        in_specs=[pl.BlockSpec((tm, D_MODEL), lambda i: (back(i), 0)),
                  pl.BlockSpec((tm, D_ATT), lambda i: (front(i), 0)),
                  pl.BlockSpec((tm, D_REST), lambda i: (front(i), 0)),
                  pl.BlockSpec((halo, 2 * D_CONV), lambda i: (jnp.maximum(front(i) * hb - 1, 0), 0)),
                  pl.BlockSpec((halo, 2 * D_CONV), lambda i: (jnp.minimum((front(i) + 1) * hb, n_halo - 1), 0))]
                 + [_full(c) for c in consts],
        out_specs=[pl.BlockSpec((tm, D_MODEL), lambda i: (back(i), 0)),
                   pl.BlockSpec((tm, D_PACKED), lambda i: (back(i), 0)),
                   pl.BlockSpec((ROUTE_ROWS, tm), lambda i: (0, back(i))),
                   pl.BlockSpec((1, N_EXPERTS, LANES), lambda i: (back(i), 0, 0))],
        out_shape=[jax.ShapeDtypeStruct((n, D_MODEL), f32),
                   jax.ShapeDtypeStruct((n, D_PACKED), jnp.uint32),
                   jax.ShapeDtypeStruct((ROUTE_ROWS, n), f32),
                   jax.ShapeDtypeStruct((n_tiles, N_EXPERTS, LANES), f32)],
        scratch_shapes=[pltpu.VMEM((tm, D_MODEL), bf16)],
        compiler_params=_cparams(1),
        name="mixer_tail",
    )(x, att, rest, rest, rest, *consts)


def _route(route, cnt, n, blk):
    n_tiles = n // TAIL_TM
    tile_cnt = cnt[:, :, 0].astype(jnp.int32)
    tile_off = jnp.cumsum(tile_cnt, axis=0) - tile_cnt
    counts = jnp.sum(tile_cnt, axis=0)
    padded = (counts + blk - 1) // blk * blk
    pend = jnp.cumsum(padded)
    pstart = pend - padded
    base = pstart[None, :] + tile_off
    lanes = jnp.arange(N_EXPERTS, dtype=jnp.int32)

    def rows_of(expert_row, rank_row):
        e = expert_row.astype(jnp.int32).reshape(n_tiles, TAIL_TM, 1)
        sel = jnp.sum(jnp.where(e == lanes, base[:, None, :], 0), axis=-1)
        return sel.reshape(n) + rank_row.astype(jnp.int32)

    dest0 = rows_of(route[0], route[4])
    dest1 = rows_of(route[1], route[5])
    gate = route[2:4].T

    n_blocks = (2 * n + N_EXPERTS * (blk - 1) + blk - 1) // blk
    blk_start = jnp.arange(n_blocks, dtype=jnp.int32) * blk
    blk_expert = jnp.sum((pend[None, :] <= blk_start[:, None]).astype(jnp.int32), axis=1)
    blk_expert = jnp.minimum(blk_expert, N_EXPERTS - 1)
    blk_valid = jnp.clip((pstart + counts)[blk_expert] - blk_start, 0, blk)
    blk_valid = jnp.where(blk_start < pend[-1], blk_valid, 0).astype(jnp.int32)
    return gate, dest0, dest1, blk_expert, blk_valid, n_blocks


def _sc_mesh():
    return plsc.VectorSubcoreMesh(core_axis_name="c", subcore_axis_name="s")


def _sc_dispatch(h, dest0, dest1, p_rows):
    n, d = h.shape
    per = n // SC_WORKERS
    assert n % (SC_WORKERS * SC_IDX_WIN) == 0

    @pl.kernel(out_type=jax.ShapeDtypeStruct((p_rows, d), h.dtype), mesh=_sc_mesh(),
               scratch_types=[pltpu.VMEM((2, SC_IDX_WIN), jnp.int32), pltpu.VMEM((SC_ROWS, d), h.dtype)])
    def k(h_hbm, d0_hbm, d1_hbm, xs_hbm, idx, buf):
        base = (lax.axis_index("c") * SC_SUBCORES + lax.axis_index("s")) * per

        @pl.loop(0, per // SC_IDX_WIN)
        def _(w):
            off = base + w * SC_IDX_WIN
            pltpu.sync_copy(d0_hbm.at[pl.ds(off, SC_IDX_WIN)], idx.at[0])
            pltpu.sync_copy(d1_hbm.at[pl.ds(off, SC_IDX_WIN)], idx.at[1])
            for r in range(SC_IDX_WIN // SC_ROWS):
                pltpu.sync_copy(h_hbm.at[pl.ds(off + r * SC_ROWS, SC_ROWS)], buf)
                pltpu.sync_copy(buf, xs_hbm.at[idx.at[0, pl.ds(r * SC_ROWS, SC_ROWS)]])
                pltpu.sync_copy(buf, xs_hbm.at[idx.at[1, pl.ds(r * SC_ROWS, SC_ROWS)]])

    return k(h, dest0, dest1)


def _sc_gather2(ys, dest0, dest1):
    n = dest0.shape[0]
    d = ys.shape[1]
    per = n // SC_WORKERS
    out = jax.ShapeDtypeStruct((n, d), ys.dtype)

    @pl.kernel(out_type=[out, out], mesh=_sc_mesh(),
               scratch_types=[pltpu.VMEM((2, SC_IDX_WIN), jnp.int32), pltpu.VMEM((SC_ROWS, d), ys.dtype)])
    def k(ys_hbm, d0_hbm, d1_hbm, y0_hbm, y1_hbm, idx, buf):
        base = (lax.axis_index("c") * SC_SUBCORES + lax.axis_index("s")) * per

        @pl.loop(0, per // SC_IDX_WIN)
        def _(w):
            off = base + w * SC_IDX_WIN
            pltpu.sync_copy(d0_hbm.at[pl.ds(off, SC_IDX_WIN)], idx.at[0])
            pltpu.sync_copy(d1_hbm.at[pl.ds(off, SC_IDX_WIN)], idx.at[1])
            for r in range(SC_IDX_WIN // SC_ROWS):
                for kk, y_hbm in enumerate((y0_hbm, y1_hbm)):
                    pltpu.sync_copy(ys_hbm.at[idx.at[kk, pl.ds(r * SC_ROWS, SC_ROWS)]], buf)
                    pltpu.sync_copy(buf, y_hbm.at[pl.ds(off + r * SC_ROWS, SC_ROWS)])

    return k(ys, dest0, dest1)


def _experts_kernel(be_ref, bv_ref, xs_ref, wg_ref, wu_ref, wd_ref, ys_ref, wg_bf, wu_bf, wd_bf):
    i = pl.program_id(0)
    valid = bv_ref[i]

    @pl.when(jnp.logical_or(i == 0, be_ref[i] != be_ref[jnp.maximum(i - 1, 0)]))
    def _():
        wg_bf[...] = wg_ref[0].astype(bf16)
        wu_bf[...] = wu_ref[0].astype(bf16)
        wd_bf[...] = wd_ref[0].astype(bf16)

    @pl.when(valid > 0)
    def _():
        row = lax.broadcasted_iota(jnp.int32, xs_ref.shape, 0)
        keep = row < valid
        lo, hi = _unpack_halves(xs_ref[...])
        x = jnp.concatenate([jnp.where(keep, lo, 0.0), jnp.where(keep, hi, 0.0)], axis=1).astype(bf16)
        g = jnp.dot(x, wg_bf[...], preferred_element_type=f32)
        u = jnp.dot(x, wu_bf[...], preferred_element_type=f32)
        hmid = (g * _sigmoid(g) * u).astype(bf16)
        ys_ref[...] = _pack_halves(jnp.dot(hmid, wd_bf[...], preferred_element_type=f32))

    @pl.when(valid == 0)
    def _():
        ys_ref[...] = jnp.zeros_like(ys_ref)


def _experts(xs, blk_expert, blk_valid, wg, wu, wd, n_blocks, blk):
    grid_spec = pltpu.PrefetchScalarGridSpec(
        num_scalar_prefetch=2,
        grid=(n_blocks,),
        in_specs=[pl.BlockSpec((blk, D_PACKED), lambda i, be, bv: (i, 0)),
                  pl.BlockSpec((1, D_MODEL, D_EXPERT), lambda i, be, bv: (be[i], 0, 0)),
                  pl.BlockSpec((1, D_MODEL, D_EXPERT), lambda i, be, bv: (be[i], 0, 0)),
                  pl.BlockSpec((1, D_EXPERT, D_MODEL), lambda i, be, bv: (be[i], 0, 0))],
        out_specs=pl.BlockSpec((blk, D_PACKED), lambda i, be, bv: (i, 0)),
        scratch_shapes=[pltpu.VMEM((D_MODEL, D_EXPERT), bf16),
                        pltpu.VMEM((D_MODEL, D_EXPERT), bf16),
                        pltpu.VMEM((D_EXPERT, D_MODEL), bf16)],
    )
    return pl.pallas_call(
        _experts_kernel,
        grid_spec=grid_spec,
        out_shape=jax.ShapeDtypeStruct(xs.shape, jnp.uint32),
        compiler_params=_cparams(1),
        name="experts",
    )(blk_expert, blk_valid, xs, wg, wu, wd)


def _final_norm_kernel(h_ref, y0_ref, y1_ref, gate_ref, g_ref, b_ref, o_ref):
    g0, g1 = gate_ref[:, 0:1], gate_ref[:, 1:2]
    lo0, hi0 = _unpack_halves(y0_ref[...])
    lo1, hi1 = _unpack_halves(y1_ref[...])
    ffn = jnp.concatenate([lo0 * g0 + lo1 * g1, hi0 * g0 + hi1 * g1], axis=1)
    o_ref[...] = _layer_norm(ALPHA * h_ref[...] + ffn, g_ref[...], b_ref[...])


def _final_norm(h, y0, y1, gate, ln_g, ln_b):
    n = h.shape[0]
    ts = NORM_TS
    tile = pl.BlockSpec((ts, D_MODEL), lambda i: (i, 0))
    packed = pl.BlockSpec((ts, D_PACKED), lambda i: (i, 0))
    return pl.pallas_call(
        _final_norm_kernel,
        grid=(n // ts,),
        in_specs=[tile, packed, packed, pl.BlockSpec((ts, 2), lambda i: (i, 0)), _full(ln_g), _full(ln_b)],
        out_specs=tile,
        out_shape=jax.ShapeDtypeStruct((n, D_MODEL), f32),
        compiler_params=_cparams(1),
        name="final_norm",
    )(h, y0, y1, gate, ln_g, ln_b)


def _encode(x, p, qkv_meta, rest_meta, slabs):
    bsz, t, _ = x.shape
    n = bsz * t
    assert t % NORM_TS == 0 and t % TAIL_TM == 0 and t % PROJ_TM == 0
    xf = x.reshape(n, D_MODEL)
    qkv, rest = _in_proj(xf, p["w_in"], PROJ_TM)
    att = _attention(qkv, qkv_meta, slabs, bsz, t)
    h1, h1_packed, route, cnt = _mixer_tail(xf, att, rest, rest_meta, p, t)
    blk = MOE_BLK
    gate, dest0, dest1, blk_expert, blk_valid, n_blocks = _route(route, cnt, n, blk)
    xs = _sc_dispatch(h1_packed, dest0, dest1, n_blocks * blk)
    ys = _experts(xs, blk_expert, blk_valid, p["w_e_gate"], p["w_e_up"], p["w_e_down"], n_blocks, blk)
    y0, y1 = _sc_gather2(ys, dest0, dest1)
    y = _final_norm(h1, y0, y1, gate, p["ln2_g"], p["ln2_b"])
    return y.reshape(bsz, t, D_MODEL)


def kernel(x_prompt, x_sample, meta_tokens, w_in, rpb, conv_w, conv_b, w_att_proj, w_conv_proj, w_out,
           ln1_g, ln1_b, w_router_group, w_router_expert, w_e_gate, w_e_up, w_e_down, ln2_g, ln2_b):
    w = w_in[0]
    a, c = 3 * D_ATT, D_CONV
    w_perm = jnp.concatenate([w[:, :a], w[:, a + c:a + 3 * c], w[:, a:a + c], 0.5 * w[:, a + 3 * c:]], axis=1)
    w_router = jnp.concatenate([w_router_expert[0], w_router_group[0]], axis=1)
    w_router = jnp.pad(w_router, ((0, 0), (0, LANES - w_router.shape[1])))
    row = lambda v: v[0].reshape(1, -1).astype(f32)
    p = {
        "w_in": w_perm.astype(bf16),
        "conv_w": conv_w[0].astype(f32), "conv_b": row(conv_b),
        "w_att_proj": w_att_proj[0].astype(bf16), "w_conv_proj": w_conv_proj[0].astype(bf16),
        "w_out": (0.5 * w_out[0]).astype(bf16), "ln1_g": row(ln1_g), "ln1_b": row(ln1_b),
        "w_router": w_router.astype(bf16),
        "w_e_gate": w_e_gate[0], "w_e_up": w_e_up[0], "w_e_down": w_e_down[0],
        "ln2_g": row(ln2_g), "ln2_b": row(ln2_b),
    }
    qkv_meta, rest_meta = _in_proj(meta_tokens.astype(f32), p["w_in"], N_META)
    slabs = _bias_slabs(rpb[0])
    y_prompt = _encode(x_prompt, p, qkv_meta, rest_meta, slabs)
    y_sample = _encode(x_sample, p, qkv_meta, rest_meta, slabs)
    return (y_prompt, y_sample)
```

```python
import functools

import numpy as np
import jax
import jax.numpy as jnp
from jax import lax
from jax.experimental import pallas as pl
from jax.experimental.pallas import tpu as pltpu
from jax.experimental.pallas import tpu_sc as plsc

D_MODEL = 1024
N_META = 16
GRID_W = 64
WIN_H = 8
WIN_W = 16
N_HEADS = 8
HEAD_DIM = 64
D_ATT = N_HEADS * HEAD_DIM
D_CONV = D_MODEL
N_GROUPS = 4
EXPERTS_PER_GROUP = 8
N_EXPERTS = N_GROUPS * EXPERTS_PER_GROUP
D_EXPERT = D_MODEL // 2
DEPTH = 1
ALPHA = (2.0 * DEPTH) ** 0.25
LN_EPS = 1e-5
D_QKV = 4 * D_ATT
LOG2E = 1.4426950408889634
D_REST = 3 * D_CONV + 2 * D_MODEL
D_IN_PROJ = 3 * D_ATT + D_REST

LANES = 128
HEADS_PER_VREG = LANES // HEAD_DIM
N_HEAD_PAIRS = N_HEADS // HEADS_PER_VREG
NEG_BIG = -1e30

PROJ_TM = 512
PROJ_TN = 512
ATT_ROWS = 8
ATT_GROUP = 2
SLAB_ROWS = WIN_H + 1
ATT_KEYS = SLAB_ROWS * GRID_W + N_META
TAIL_TM = 512
TAIL_CW = 256
NORM_TS = 512
MOE_BLK = 512
ROUTE_ROWS = 8
VMEM_LIMIT = 56 * 1024 * 1024

SC_CORES = 2
SC_SUBCORES = 16
SC_WORKERS = SC_CORES * SC_SUBCORES
SC_IDX_WIN = 128
SC_ROWS = 64
D_PACKED = D_MODEL // 2

bf16 = jnp.bfloat16
f32 = jnp.float32


def _cparams(n_axes):
    return pltpu.CompilerParams(dimension_semantics=("arbitrary",) * n_axes,
                                vmem_limit_bytes=VMEM_LIMIT)


def _full(a):
    return pl.BlockSpec(a.shape, lambda *_: (0,) * a.ndim)


def _pack_halves(x):
    half = x.shape[1] // 2
    bits = lambda v: lax.bitcast_convert_type(v.astype(bf16).astype(f32), jnp.uint32)
    return (bits(x[:, :half]) >> 16) | bits(x[:, half:])


def _unpack_halves(p):
    lo = lax.bitcast_convert_type(p << 16, f32)
    hi = lax.bitcast_convert_type(p & jnp.uint32(0xFFFF0000), f32)
    return lo, hi


def _in_proj_kernel(x_ref, w_ref, qkv_ref, rest_ref):
    assert PROJ_TN == D_ATT
    xb = x_ref[...].astype(bf16)
    for c in range(0, D_IN_PROJ, PROJ_TN):
        y = jnp.dot(xb, w_ref[:, c:c + PROJ_TN], preferred_element_type=f32)
        if c == 0:
            y = (y * (HEAD_DIM ** -0.5 * LOG2E)).astype(bf16)
            lane = lax.broadcasted_iota(jnp.int32, y.shape, 1)
            even = (lane & HEAD_DIM) == 0
            zero = jnp.zeros_like(y)
            qkv_ref[:, :D_ATT] = jnp.where(even, y, zero)
            qkv_ref[:, D_ATT:2 * D_ATT] = jnp.where(even, zero, y)
        elif c < 3 * D_ATT:
            qkv_ref[:, c + D_ATT:c + D_ATT + PROJ_TN] = y.astype(bf16)
        else:
            rest_ref[:, c - 3 * D_ATT:c - 3 * D_ATT + PROJ_TN] = y.astype(bf16)


def _in_proj(x, w_bf, tm):
    n = x.shape[0]
    return pl.pallas_call(
        _in_proj_kernel,
        grid=(n // tm,),
        in_specs=[pl.BlockSpec((tm, D_MODEL), lambda i: (i, 0)),
                  pl.BlockSpec((D_MODEL, D_IN_PROJ), lambda i: (0, 0))],
        out_specs=[pl.BlockSpec((tm, D_QKV), lambda i: (i, 0)),
                   pl.BlockSpec((tm, D_REST), lambda i: (i, 0))],
        out_shape=[jax.ShapeDtypeStruct((n, D_QKV), bf16),
                   jax.ShapeDtypeStruct((n, D_REST), bf16)],
        compiler_params=_cparams(1),
        name="in_proj",
    )(x, w_bf)


def _bias_slabs(rpb):
    qc = np.arange(GRID_W)[:, None]
    kc = np.arange(GRID_W)[None, :]
    w_start = np.clip(qc - WIN_W // 2, 0, GRID_W - WIN_W)
    valid = (kc >= w_start) & (kc < w_start + WIN_W)
    dx_idx = np.clip(kc - qc + WIN_W - 1, 0, 2 * WIN_W - 2)
    t = rpb.astype(f32)[:, :, dx_idx] * LOG2E
    t = jnp.where(jnp.asarray(valid)[None, None], t, NEG_BIG)
    n_dy = 2 * WIN_H - 1
    t = jnp.concatenate([t, jnp.full((N_HEADS, 1, GRID_W, GRID_W), NEG_BIG, f32)], axis=1)
    dy0 = np.arange(WIN_H)[:, None, None]
    off = np.arange(2)[None, :, None]
    jrow = np.arange(SLAB_ROWS)[None, None, :]
    inside = (jrow - off >= 0) & (jrow - off < WIN_H)
    idx = np.where(inside, dy0 + jrow - off, n_dy)
    s = t[:, idx]
    s = jnp.transpose(s, (0, 1, 2, 4, 3, 5)).reshape(N_HEADS, WIN_H, 2, GRID_W, SLAB_ROWS * GRID_W)
    s = jnp.pad(s, ((0, 0),) * 4 + ((0, N_META),))
    s = s.reshape(N_HEAD_PAIRS, HEADS_PER_VREG, WIN_H, 2, GRID_W, ATT_KEYS)
    s = jnp.transpose(s, (0, 2, 3, 1, 4, 5))
    return s.reshape(N_HEAD_PAIRS, WIN_H, 2, HEADS_PER_VREG * GRID_W, ATT_KEYS)


def _attention_kernel(qe_ref, qo_ref, k_ref, v_ref, km_ref, vm_ref, bias_ref, o_ref, *, rows):
    j = pl.program_id(1)
    tq2 = 2 * GRID_W
    lane = lax.broadcasted_iota(jnp.int32, (tq2, LANES), 1)
    low = lane < HEAD_DIM
    nt = (((1,), (1,)), ((), ()))
    slab_keys = SLAB_ROWS * GRID_W

    def scores(ip, pair):
        r0 = j * ATT_ROWS + 2 * ip
        rs = [jnp.clip(r0 + a - WIN_H // 2, 0, rows - WIN_H) for a in range(2)]
        us = jnp.minimum(rs[0], rows - SLAB_ROWS)
        q0 = pl.multiple_of(ip * tq2, tq2)
        k0 = pl.multiple_of(us * GRID_W, GRID_W)
        cs = slice(pair * LANES, (pair + 1) * LANES)
        qq = jnp.concatenate([qe_ref[pl.ds(q0, tq2), cs], qo_ref[pl.ds(q0, tq2), cs]], axis=0)
        k2 = jnp.concatenate([k_ref[pl.ds(k0, slab_keys), cs], km_ref[:, cs]], axis=0)
        s = lax.dot_general(qq, k2, nt, preferred_element_type=f32)
        b = [bias_ref[pair, rs[a] - (r0 + a) + (WIN_H - 1), rs[a] - us] for a in range(2)]
        s = s + jnp.concatenate([b[0][:GRID_W], b[1][:GRID_W], b[0][GRID_W:], b[1][GRID_W:]], axis=0)
        return s, jnp.max(s, axis=-1, keepdims=True), k0, q0, cs

    def weights(state):
        s, m, k0, q0, cs = state
        v2 = jnp.concatenate([v_ref[pl.ds(k0, slab_keys), cs], vm_ref[:, cs]], axis=0)
        e = jnp.exp2(s - m)
        l = jnp.sum(e, axis=-1, keepdims=True)
        return jnp.dot(e.astype(bf16), v2, preferred_element_type=f32), l, q0, cs

    def finish(state):
        o2, l, q0, cs = state
        o2 = o2 / l
        o = jnp.where(low, o2[:tq2], o2[tq2:])
        o_ref[pl.ds(q0, tq2), cs] = o.astype(bf16)

    def group_body(g, carry):
        items = [(g * ATT_GROUP + ip, pair) for ip in range(ATT_GROUP) for pair in range(N_HEAD_PAIRS)]
        a, b = {}, {}
        for step in range(len(items) + 2):
            if step < len(items):
                a[step] = scores(*items[step])
            if 0 <= step - 1 < len(items):
                b[step - 1] = weights(a.pop(step - 1))
            if 0 <= step - 2 < len(items):
                finish(b.pop(step - 2))
        return carry

    lax.fori_loop(0, ATT_ROWS // (2 * ATT_GROUP), group_body, 0)


def _attention(qkv, qkv_meta, slabs, bsz, t):
    rows = t // GRID_W
    assert rows >= 2 * WIN_H and rows % ATT_ROWS == 0 and ATT_ROWS % (2 * ATT_GROUP) == 0
    steps = rows // ATT_ROWS
    tq = ATT_ROWS * GRID_W
    return pl.pallas_call(
        functools.partial(_attention_kernel, rows=rows),
        grid=(bsz, steps),
        in_specs=[pl.BlockSpec((tq, D_ATT), lambda b, j: (b * steps + j, 0)),
                  pl.BlockSpec((tq, D_ATT), lambda b, j: (b * steps + j, 1)),
                  pl.BlockSpec((t, D_ATT), lambda b, j: (b, 2)),
                  pl.BlockSpec((t, D_ATT), lambda b, j: (b, 3)),
                  pl.BlockSpec((N_META, D_ATT), lambda b, j: (0, 2)),
                  pl.BlockSpec((N_META, D_ATT), lambda b, j: (0, 3)),
                  _full(slabs)],
        out_specs=pl.BlockSpec((tq, D_ATT), lambda b, j: (b * steps + j, 0)),
        out_shape=jax.ShapeDtypeStruct((bsz * t, D_ATT), bf16),
        compiler_params=_cparams(2),
        name="attention",
    )(qkv, qkv, qkv, qkv, qkv_meta, qkv_meta, slabs)


def _layer_norm(z, g, b):
    mu = jnp.mean(z, axis=-1, keepdims=True)
    d = z - mu
    var = jnp.mean(d * d, axis=-1, keepdims=True)
    return d * lax.rsqrt(var + LN_EPS) * g + b


def _sigmoid(x):
    return 0.5 * jnp.tanh(0.5 * x) + 0.5


def _route_tile(logits, before_ref):
    tm = logits.shape[0]
    lt = logits.T
    el = lt[:N_EXPERTS]
    gl = lt[N_EXPERTS:N_EXPERTS + N_GROUPS]
    neg = -jnp.inf
    erow = lax.broadcasted_iota(jnp.int32, el.shape, 0).astype(f32)
    grow = lax.broadcasted_iota(jnp.int32, gl.shape, 0).astype(f32)
    first = lambda hit, idx, n: jnp.min(jnp.where(hit, idx, float(n)), axis=0, keepdims=True)

    gmax = jnp.max(gl, axis=0, keepdims=True)
    grp = first(gl == gmax, grow, N_GROUPS)
    pg_sel = 1.0 / jnp.sum(jnp.exp(gl - gmax), axis=0, keepdims=True)

    e_lo = grp * EXPERTS_PER_GROUP
    elm = jnp.where((erow >= e_lo) & (erow < e_lo + EXPERTS_PER_GROUP), el, neg)
    t1 = jnp.max(elm, axis=0, keepdims=True)
    e1 = first(elm == t1, erow, N_EXPERTS)
    el2 = jnp.where(erow == e1, neg, elm)
    t2 = jnp.max(el2, axis=0, keepdims=True)
    e2 = first(el2 == t2, erow, N_EXPERTS)
    r = jnp.exp(t2 - t1)
    g1 = pg_sel / (1.0 + r)
    g2 = pg_sel * r / (1.0 + r)

    hit1 = erow == e1
    hit2 = erow == e2
    onehot = jnp.where(hit1 | hit2, 1.0, 0.0)
    before = jnp.dot(onehot.astype(bf16), before_ref[...], preferred_element_type=f32)
    rank1 = jnp.sum(jnp.where(hit1, before, 0.0), axis=0, keepdims=True)
    rank2 = jnp.sum(jnp.where(hit2, before, 0.0), axis=0, keepdims=True)
    rrow = lax.broadcasted_iota(jnp.int32, (ROUTE_ROWS, tm), 0)
    route = jnp.zeros((ROUTE_ROWS, tm), f32)
    for k, val in enumerate((e1, e2, g1, g2, rank1, rank2)):
        route = jnp.where(rrow == k, val, route)
    return route, jnp.sum(onehot, axis=1, keepdims=True)


def _mixer_tail_kernel(x_ref, att_ref, rest_ref, prev_ref, next_ref, meta_ref,
                       convw_ref, convb_ref, wap_ref, wcp_ref, wout_ref, g_ref, b_ref, wr_ref, before_ref,
                       h_ref, hp_ref, route_ref, cnt_ref, merged_scr, *, tiles_per_seq, n_tiles):
    i = pl.program_id(0)
    pos = jnp.minimum(i, n_tiles - 1) % tiles_per_seq
    tm = x_ref.shape[0]
    c0, c1, c2, c3, c4 = (k * D_MODEL for k in range(5))
    cols = lambda j, base=0: slice(base + j * TAIL_CW, base + (j + 1) * TAIL_CW)
    chunks = range(D_MODEL // TAIL_CW)

    @pl.when(i == 0)
    def _():
        merged_scr[...] = jnp.zeros_like(merged_scr)

    def u_of(ref, rws, j):
        return ref[rws, cols(j, c0)].astype(f32) * ref[rws, cols(j, c1)].astype(f32)

    last = prev_ref.shape[0] - 1
    row = lax.broadcasted_iota(jnp.int32, (tm, TAIL_CW), 0)

    def conv_chunk(j):
        u = u_of(rest_ref, slice(0, tm), j)
        u_prev = jnp.where(pos == 0, u_of(meta_ref, slice(N_META - 1, N_META), j),
                           u_of(prev_ref, slice(last, last + 1), j))
        u_next = u_of(next_ref, slice(0, 1), j)
        u_next = jnp.where(pos == tiles_per_seq - 1, jnp.zeros_like(u_next), u_next)
        u_m1 = jnp.where(row == 0, u_prev, pltpu.roll(u, 1, axis=0))
        u_p1 = jnp.where(row == tm - 1, u_next, pltpu.roll(u, tm - 1, axis=0))
        cw = convw_ref[:, cols(j)]
        s = u_m1 * cw[0:1] + u * cw[1:2] + u_p1 * cw[2:3] + convb_ref[:, cols(j)]
        return (rest_ref[:, cols(j, c2)].astype(f32) * s).astype(bf16)

    mix = [jnp.dot(merged_scr[...], wout_ref[:, cols(j)], preferred_element_type=f32) for j in chunks]

    cv = None
    for j in chunks:
        part = jnp.dot(conv_chunk(j), wcp_ref[cols(j), :], preferred_element_type=f32)
        cv = part if cv is None else cv + part
    ap = jnp.dot(att_ref[...], wap_ref[...], preferred_element_type=f32)

    z = [ALPHA * x_ref[:, cols(j)] + mix[j] for j in chunks]
    mu = sum(jnp.sum(zj, axis=-1, keepdims=True) for zj in z) * (1.0 / D_MODEL)
    d = [zj - mu for zj in z]
    var = sum(jnp.sum(dj * dj, axis=-1, keepdims=True) for dj in d) * (1.0 / D_MODEL)
    rstd = lax.rsqrt(var + LN_EPS)
    logits = None
    hs = []
    for j in chunks:
        hj = d[j] * rstd * g_ref[:, cols(j)] + b_ref[:, cols(j)]
        h_ref[:, cols(j)] = hj
        hs.append(hj)
        part = jnp.dot(hj.astype(bf16), wr_ref[cols(j), :], preferred_element_type=f32)
        logits = part if logits is None else logits + part
    hp_ref[...] = _pack_halves(jnp.concatenate(hs, axis=1))

    for j in chunks:
        gate_a = jnp.tanh(rest_ref[:, cols(j, c3)]) + 1.0
        gate_c = jnp.tanh(rest_ref[:, cols(j, c4)]) + 1.0
        merged_scr[:, cols(j)] = gate_a * ap[:, cols(j)].astype(bf16) + gate_c * cv[:, cols(j)].astype(bf16)

    route, count = _route_tile(logits, before_ref)
    route_ref[...] = route
    cnt_ref[0] = jnp.broadcast_to(count, (N_EXPERTS, LANES))


def _mixer_tail(x, att, rest, rest_meta, p, t):
    n = x.shape[0]
    tm = TAIL_TM
    halo = 16
    hb = tm // halo
    n_halo = n // halo
    n_tiles = n // tm
    tiles_per_seq = t // tm
    before = jnp.asarray(np.triu(np.ones((tm, tm), np.float32), 1), dtype=bf16)
    consts = (rest_meta, p["conv_w"], p["conv_b"], p["w_att_proj"], p["w_conv_proj"], p["w_out"],
              p["ln1_g"], p["ln1_b"], p["w_router"], before)
    front = lambda i: jnp.minimum(i, n_tiles - 1)
    back = lambda i: jnp.maximum(i - 1, 0)
    return pl.pallas_call(
        functools.partial(_mixer_tail_kernel, tiles_per_seq=tiles_per_seq, n_tiles=n_tiles),
        grid=(n_tiles + 1,),
        in_specs=[pl.BlockSpec((tm, D_MODEL), lambda i: (back(i), 0)),
                  pl.BlockSpec((tm, D_ATT), lambda i: (front(i), 0)),
                  pl.BlockSpec((tm, D_REST), lambda i: (front(i), 0)),
                  pl.BlockSpec((halo, 2 * D_CONV), lambda i: (jnp.maximum(front(i) * hb - 1, 0), 0)),
                  pl.BlockSpec((halo, 2 * D_CONV), lambda i: (jnp.minimum((front(i) + 1) * hb, n_halo - 1), 0))]
                 + [_full(c) for c in consts],
        out_specs=[pl.BlockSpec((tm, D_MODEL), lambda i: (back(i), 0)),
                   pl.BlockSpec((tm, D_PACKED), lambda i: (back(i), 0)),
                   pl.BlockSpec((ROUTE_ROWS, tm), lambda i: (0, back(i))),
                   pl.BlockSpec((1, N_EXPERTS, LANES), lambda i: (back(i), 0, 0))],
        out_shape=[jax.ShapeDtypeStruct((n, D_MODEL), f32),
                   jax.ShapeDtypeStruct((n, D_PACKED), jnp.uint32),
                   jax.ShapeDtypeStruct((ROUTE_ROWS, n), f32),
                   jax.ShapeDtypeStruct((n_tiles, N_EXPERTS, LANES), f32)],
        scratch_shapes=[pltpu.VMEM((tm, D_MODEL), bf16)],
        compiler_params=_cparams(1),
        name="mixer_tail",
    )(x, att, rest, rest, rest, *consts)


def _route(route, cnt, n, blk):
    n_tiles = n // TAIL_TM
    tile_cnt = cnt[:, :, 0].astype(jnp.int32)
    tile_off = jnp.cumsum(tile_cnt, axis=0) - tile_cnt
    counts = jnp.sum(tile_cnt, axis=0)
    padded = (counts + blk - 1) // blk * blk
    pend = jnp.cumsum(padded)
    pstart = pend - padded
    base = pstart[None, :] + tile_off
    lanes = jnp.arange(N_EXPERTS, dtype=jnp.int32)

    def rows_of(expert_row, rank_row):
        e = expert_row.astype(jnp.int32).reshape(n_tiles, TAIL_TM, 1)
        sel = jnp.sum(jnp.where(e == lanes, base[:, None, :], 0), axis=-1)
        return sel.reshape(n) + rank_row.astype(jnp.int32)

    dest0 = rows_of(route[0], route[4])
    dest1 = rows_of(route[1], route[5])
    gate = route[2:4].T

    n_blocks = (2 * n + N_EXPERTS * (blk - 1) + blk - 1) // blk
    blk_start = jnp.arange(n_blocks, dtype=jnp.int32) * blk
    blk_expert = jnp.sum((pend[None, :] <= blk_start[:, None]).astype(jnp.int32), axis=1)
    blk_expert = jnp.minimum(blk_expert, N_EXPERTS - 1)
    blk_valid = jnp.clip((pstart + counts)[blk_expert] - blk_start, 0, blk)
    blk_valid = jnp.where(blk_start < pend[-1], blk_valid, 0).astype(jnp.int32)
    return gate, dest0, dest1, blk_expert, blk_valid, n_blocks


def _sc_mesh():
    return plsc.VectorSubcoreMesh(core_axis_name="c", subcore_axis_name="s")


def _sc_dispatch(h, dest0, dest1, p_rows):
    n, d = h.shape
    per = n // SC_WORKERS
    assert n % (SC_WORKERS * SC_IDX_WIN) == 0

    @pl.kernel(out_type=jax.ShapeDtypeStruct((p_rows, d), h.dtype), mesh=_sc_mesh(),
               scratch_types=[pltpu.VMEM((2, SC_IDX_WIN), jnp.int32), pltpu.VMEM((SC_ROWS, d), h.dtype)])
    def k(h_hbm, d0_hbm, d1_hbm, xs_hbm, idx, buf):
        base = (lax.axis_index("c") * SC_SUBCORES + lax.axis_index("s")) * per

        @pl.loop(0, per // SC_IDX_WIN)
        def _(w):
            off = base + w * SC_IDX_WIN
            pltpu.sync_copy(d0_hbm.at[pl.ds(off, SC_IDX_WIN)], idx.at[0])
            pltpu.sync_copy(d1_hbm.at[pl.ds(off, SC_IDX_WIN)], idx.at[1])
            for r in range(SC_IDX_WIN // SC_ROWS):
                pltpu.sync_copy(h_hbm.at[pl.ds(off + r * SC_ROWS, SC_ROWS)], buf)
                pltpu.sync_copy(buf, xs_hbm.at[idx.at[0, pl.ds(r * SC_ROWS, SC_ROWS)]])
                pltpu.sync_copy(buf, xs_hbm.at[idx.at[1, pl.ds(r * SC_ROWS, SC_ROWS)]])

    return k(h, dest0, dest1)


def _sc_gather2(ys, dest0, dest1):
    n = dest0.shape[0]
    d = ys.shape[1]
    per = n // SC_WORKERS
    out = jax.ShapeDtypeStruct((n, d), ys.dtype)

    @pl.kernel(out_type=[out, out], mesh=_sc_mesh(),
               scratch_types=[pltpu.VMEM((2, SC_IDX_WIN), jnp.int32), pltpu.VMEM((SC_ROWS, d), ys.dtype)])
    def k(ys_hbm, d0_hbm, d1_hbm, y0_hbm, y1_hbm, idx, buf):
        base = (lax.axis_index("c") * SC_SUBCORES + lax.axis_index("s")) * per

        @pl.loop(0, per // SC_IDX_WIN)
        def _(w):
            off = base + w * SC_IDX_WIN
            pltpu.sync_copy(d0_hbm.at[pl.ds(off, SC_IDX_WIN)], idx.at[0])
            pltpu.sync_copy(d1_hbm.at[pl.ds(off, SC_IDX_WIN)], idx.at[1])
            for r in range(SC_IDX_WIN // SC_ROWS):
                for kk, y_hbm in enumerate((y0_hbm, y1_hbm)):
                    pltpu.sync_copy(ys_hbm.at[idx.at[kk, pl.ds(r * SC_ROWS, SC_ROWS)]], buf)
                    pltpu.sync_copy(buf, y_hbm.at[pl.ds(off + r * SC_ROWS, SC_ROWS)])

    return k(ys, dest0, dest1)


def _experts_kernel(be_ref, bv_ref, xs_ref, wg_ref, wu_ref, wd_ref, ys_ref, wg_bf, wu_bf, wd_bf):
    i = pl.program_id(0)
    valid = bv_ref[i]

    @pl.when(jnp.logical_or(i == 0, be_ref[i] != be_ref[jnp.maximum(i - 1, 0)]))
    def _():
        wg_bf[...] = wg_ref[0].astype(bf16)
        wu_bf[...] = wu_ref[0].astype(bf16)
        wd_bf[...] = wd_ref[0].astype(bf16)

    @pl.when(valid > 0)
    def _():
        row = lax.broadcasted_iota(jnp.int32, xs_ref.shape, 0)
        keep = row < valid
        lo, hi = _unpack_halves(xs_ref[...])
        x = jnp.concatenate([jnp.where(keep, lo, 0.0), jnp.where(keep, hi, 0.0)], axis=1).astype(bf16)
        g = jnp.dot(x, wg_bf[...], preferred_element_type=f32)
        u = jnp.dot(x, wu_bf[...], preferred_element_type=f32)
        hmid = (g * _sigmoid(g) * u).astype(bf16)
        ys_ref[...] = _pack_halves(jnp.dot(hmid, wd_bf[...], preferred_element_type=f32))

    @pl.when(valid == 0)
    def _():
        ys_ref[...] = jnp.zeros_like(ys_ref)


def _experts(xs, blk_expert, blk_valid, wg, wu, wd, n_blocks, blk):
    grid_spec = pltpu.PrefetchScalarGridSpec(
        num_scalar_prefetch=2,
        grid=(n_blocks,),
        in_specs=[pl.BlockSpec((blk, D_PACKED), lambda i, be, bv: (i, 0)),
                  pl.BlockSpec((1, D_MODEL, D_EXPERT), lambda i, be, bv: (be[i], 0, 0)),
                  pl.BlockSpec((1, D_MODEL, D_EXPERT), lambda i, be, bv: (be[i], 0, 0)),
                  pl.BlockSpec((1, D_EXPERT, D_MODEL), lambda i, be, bv: (be[i], 0, 0))],
        out_specs=pl.BlockSpec((blk, D_PACKED), lambda i, be, bv: (i, 0)),
        scratch_shapes=[pltpu.VMEM((D_MODEL, D_EXPERT), bf16),
                        pltpu.VMEM((D_MODEL, D_EXPERT), bf16),
                        pltpu.VMEM((D_EXPERT, D_MODEL), bf16)],
    )
    return pl.pallas_call(
        _experts_kernel,
        grid_spec=grid_spec,
        out_shape=jax.ShapeDtypeStruct(xs.shape, jnp.uint32),
        compiler_params=_cparams(1),
        name="experts",
    )(blk_expert, blk_valid, xs, wg, wu, wd)


def _final_norm_kernel(h_ref, y0_ref, y1_ref, gate_ref, g_ref, b_ref, o_ref):
    g0, g1 = gate_ref[:, 0:1], gate_ref[:, 1:2]
    lo0, hi0 = _unpack_halves(y0_ref[...])
    lo1, hi1 = _unpack_halves(y1_ref[...])
    ffn = jnp.concatenate([lo0 * g0 + lo1 * g1, hi0 * g0 + hi1 * g1], axis=1)
    o_ref[...] = _layer_norm(ALPHA * h_ref[...] + ffn, g_ref[...], b_ref[...])


def _final_norm(h, y0, y1, gate, ln_g, ln_b):
    n = h.shape[0]
    ts = NORM_TS
    tile = pl.BlockSpec((ts, D_MODEL), lambda i: (i, 0))
    packed = pl.BlockSpec((ts, D_PACKED), lambda i: (i, 0))
    return pl.pallas_call(
        _final_norm_kernel,
        grid=(n // ts,),
        in_specs=[tile, packed, packed, pl.BlockSpec((ts, 2), lambda i: (i, 0)), _full(ln_g), _full(ln_b)],
        out_specs=tile,
        out_shape=jax.ShapeDtypeStruct((n, D_MODEL), f32),
        compiler_params=_cparams(1),
        name="final_norm",
    )(h, y0, y1, gate, ln_g, ln_b)


def _encode(x, p, qkv_meta, rest_meta, slabs):
    bsz, t, _ = x.shape
    n = bsz * t
    assert t % NORM_TS == 0 and t % TAIL_TM == 0 and t % PROJ_TM == 0
    xf = x.reshape(n, D_MODEL)
    qkv, rest = _in_proj(xf, p["w_in"], PROJ_TM)
    att = _attention(qkv, qkv_meta, slabs, bsz, t)
    h1, h1_packed, route, cnt = _mixer_tail(xf, att, rest, rest_meta, p, t)
    blk = MOE_BLK
    gate, dest0, dest1, blk_expert, blk_valid, n_blocks = _route(route, cnt, n, blk)
    xs = _sc_dispatch(h1_packed, dest0, dest1, n_blocks * blk)
    ys = _experts(xs, blk_expert, blk_valid, p["w_e_gate"], p["w_e_up"], p["w_e_down"], n_blocks, blk)
    y0, y1 = _sc_gather2(ys, dest0, dest1)
    y = _final_norm(h1, y0, y1, gate, p["ln2_g"], p["ln2_b"])
    return y.reshape(bsz, t, D_MODEL)


def kernel(x_prompt, x_sample, meta_tokens, w_in, rpb, conv_w, conv_b, w_att_proj, w_conv_proj, w_out,
           ln1_g, ln1_b, w_router_group, w_router_expert, w_e_gate, w_e_up, w_e_down, ln2_g, ln2_b):
    w = w_in[0]
    a, c = 3 * D_ATT, D_CONV
    w_perm = jnp.concatenate([w[:, :a], w[:, a + c:a + 3 * c], w[:, a:a + c], 0.5 * w[:, a + 3 * c:]], axis=1)
    w_router = jnp.concatenate([w_router_expert[0], w_router_group[0]], axis=1)
    w_router = jnp.pad(w_router, ((0, 0), (0, LANES - w_router.shape[1])))
    row = lambda v: v[0].reshape(1, -1).astype(f32)
    p = {
        "w_in": w_perm.astype(bf16),
        "conv_w": conv_w[0].astype(f32), "conv_b": row(conv_b),
        "w_att_proj": w_att_proj[0].astype(bf16), "w_conv_proj": w_conv_proj[0].astype(bf16),
        "w_out": (0.5 * w_out[0]).astype(bf16), "ln1_g": row(ln1_g), "ln1_b": row(ln1_b),
        "w_router": w_router.astype(bf16),
        "w_e_gate": w_e_gate[0], "w_e_up": w_e_up[0], "w_e_down": w_e_down[0],
        "ln2_g": row(ln2_g), "ln2_b": row(ln2_b),
    }
    qkv_meta, rest_meta = _in_proj(meta_tokens.astype(f32), p["w_in"], N_META)
    slabs = _bias_slabs(rpb[0])
    y_prompt = _encode(x_prompt, p, qkv_meta, rest_meta, slabs)
    y_sample = _encode(x_sample, p, qkv_meta, rest_meta, slabs)
    return (y_prompt, y_sample)
```

```python
import functools

import numpy as np
import jax
import jax.numpy as jnp
from jax import lax
from jax.experimental import pallas as pl
from jax.experimental.pallas import tpu as pltpu
from jax.experimental.pallas import tpu_sc as plsc

D_MODEL = 1024
N_META = 16
GRID_W = 64
WIN_H = 8
WIN_W = 16
N_HEADS = 8
HEAD_DIM = 64
D_ATT = N_HEADS * HEAD_DIM
D_CONV = D_MODEL
N_GROUPS = 4
EXPERTS_PER_GROUP = 8
N_EXPERTS = N_GROUPS * EXPERTS_PER_GROUP
D_EXPERT = D_MODEL // 2
DEPTH = 1
ALPHA = (2.0 * DEPTH) ** 0.25
LN_EPS = 1e-5
D_QKV = 4 * D_ATT
LOG2E = 1.4426950408889634
D_REST = 3 * D_CONV + 2 * D_MODEL
D_IN_PROJ = 3 * D_ATT + D_REST

LANES = 128
HEADS_PER_VREG = LANES // HEAD_DIM
N_HEAD_PAIRS = N_HEADS // HEADS_PER_VREG
NEG_BIG = -1e30

PROJ_TM = 512
PROJ_TN = 512
ATT_ROWS = 8
ATT_GROUP = 2
SLAB_ROWS = WIN_H + 1
ATT_KEYS = SLAB_ROWS * GRID_W + N_META
TAIL_TM = 512
TAIL_CW = 256
NORM_TS = 512
MOE_BLK = 512
ROUTE_ROWS = 8
VMEM_LIMIT = 56 * 1024 * 1024

SC_CORES = 2
SC_SUBCORES = 16
SC_WORKERS = SC_CORES * SC_SUBCORES
SC_IDX_WIN = 128
SC_ROWS = 64
D_PACKED = D_MODEL // 2

bf16 = jnp.bfloat16
f32 = jnp.float32


def _cparams(n_axes):
    return pltpu.CompilerParams(dimension_semantics=("arbitrary",) * n_axes,
                                vmem_limit_bytes=VMEM_LIMIT)


def _full(a):
    return pl.BlockSpec(a.shape, lambda *_: (0,) * a.ndim)


def _pack_halves(x):
    half = x.shape[1] // 2
    bits = lambda v: lax.bitcast_convert_type(v.astype(bf16).astype(f32), jnp.uint32)
    return (bits(x[:, :half]) >> 16) | bits(x[:, half:])


def _unpack_halves(p):
    lo = lax.bitcast_convert_type(p << 16, f32)
    hi = lax.bitcast_convert_type(p & jnp.uint32(0xFFFF0000), f32)
    return lo, hi


def _in_proj_kernel(x_ref, w_ref, qkv_ref, rest_ref):
    assert PROJ_TN == D_ATT
    xb = x_ref[...].astype(bf16)
    for c in range(0, D_IN_PROJ, PROJ_TN):
        y = jnp.dot(xb, w_ref[:, c:c + PROJ_TN], preferred_element_type=f32)
        if c == 0:
            y = (y * (HEAD_DIM ** -0.5 * LOG2E)).astype(bf16)
            lane = lax.broadcasted_iota(jnp.int32, y.shape, 1)
            even = (lane & HEAD_DIM) == 0
            zero = jnp.zeros_like(y)
            qkv_ref[:, :D_ATT] = jnp.where(even, y, zero)
            qkv_ref[:, D_ATT:2 * D_ATT] = jnp.where(even, zero, y)
        elif c < 3 * D_ATT:
            qkv_ref[:, c + D_ATT:c + D_ATT + PROJ_TN] = y.astype(bf16)
        else:
            rest_ref[:, c - 3 * D_ATT:c - 3 * D_ATT + PROJ_TN] = y.astype(bf16)


def _in_proj(x, w_bf, tm):
    n = x.shape[0]
    return pl.pallas_call(
        _in_proj_kernel,
        grid=(n // tm,),
        in_specs=[pl.BlockSpec((tm, D_MODEL), lambda i: (i, 0)),
                  pl.BlockSpec((D_MODEL, D_IN_PROJ), lambda i: (0, 0))],
        out_specs=[pl.BlockSpec((tm, D_QKV), lambda i: (i, 0)),
                   pl.BlockSpec((tm, D_REST), lambda i: (i, 0))],
        out_shape=[jax.ShapeDtypeStruct((n, D_QKV), bf16),
                   jax.ShapeDtypeStruct((n, D_REST), bf16)],
        compiler_params=_cparams(1),
        name="in_proj",
    )(x, w_bf)


def _bias_slabs(rpb):
    qc = np.arange(GRID_W)[:, None]
    kc = np.arange(GRID_W)[None, :]
    w_start = np.clip(qc - WIN_W // 2, 0, GRID_W - WIN_W)
    valid = (kc >= w_start) & (kc < w_start + WIN_W)
    n_dx = 2 * WIN_W - 1
    onehot = (kc - qc + WIN_W - 1)[None] == np.arange(n_dx)[:, None, None]
    t = jnp.einsum("hyx,xqk->hyqk", rpb.astype(f32) * LOG2E, jnp.asarray(onehot & valid[None], f32),
                   precision=lax.Precision.HIGHEST)
    t = jnp.where(jnp.asarray(valid)[None, None], t, NEG_BIG)
    masked = jnp.full((N_HEADS, GRID_W, GRID_W), NEG_BIG, f32)
    zeros = jnp.zeros((N_HEADS, GRID_W, N_META), f32)
    slabs = []
    for dy0 in range(WIN_H):
        for off in range(2):
            tiles = [t[:, dy0 + jr - off] if 0 <= jr - off < WIN_H else masked for jr in range(SLAB_ROWS)]
            slabs.append(jnp.concatenate(tiles + [zeros], axis=-1))
    s = jnp.stack(slabs, axis=1).reshape(N_HEAD_PAIRS, HEADS_PER_VREG, WIN_H, 2, GRID_W, ATT_KEYS)
    s = jnp.transpose(s, (0, 2, 3, 1, 4, 5))
    return s.reshape(N_HEAD_PAIRS, WIN_H, 2, HEADS_PER_VREG * GRID_W, ATT_KEYS)


def _attention_kernel(qe_ref, qo_ref, k_ref, v_ref, km_ref, vm_ref, bias_ref, o_ref, *, rows):
    j = pl.program_id(1)
    tq2 = 2 * GRID_W
    lane = lax.broadcasted_iota(jnp.int32, (tq2, LANES), 1)
    low = lane < HEAD_DIM
    nt = (((1,), (1,)), ((), ()))
    slab_keys = SLAB_ROWS * GRID_W

    def scores(ip, pair):
        r0 = j * ATT_ROWS + 2 * ip
        rs = [jnp.clip(r0 + a - WIN_H // 2, 0, rows - WIN_H) for a in range(2)]
        us = jnp.minimum(rs[0], rows - SLAB_ROWS)
        q0 = pl.multiple_of(ip * tq2, tq2)
        k0 = pl.multiple_of(us * GRID_W, GRID_W)
        cs = slice(pair * LANES, (pair + 1) * LANES)
        qq = jnp.concatenate([qe_ref[pl.ds(q0, tq2), cs], qo_ref[pl.ds(q0, tq2), cs]], axis=0)
        k2 = jnp.concatenate([k_ref[pl.ds(k0, slab_keys), cs], km_ref[:, cs]], axis=0)
        s = lax.dot_general(qq, k2, nt, preferred_element_type=f32)
        b = [bias_ref[pair, rs[a] - (r0 + a) + (WIN_H - 1), rs[a] - us] for a in range(2)]
        s = s + jnp.concatenate([b[0][:GRID_W], b[1][:GRID_W], b[0][GRID_W:], b[1][GRID_W:]], axis=0)
        return s, jnp.max(s, axis=-1, keepdims=True), k0, q0, cs

    def weights(state):
        s, m, k0, q0, cs = state
        v2 = jnp.concatenate([v_ref[pl.ds(k0, slab_keys), cs], vm_ref[:, cs]], axis=0)
        e = jnp.exp2(s - m)
        l = jnp.sum(e, axis=-1, keepdims=True)
        return jnp.dot(e.astype(bf16), v2, preferred_element_type=f32), l, q0, cs

    def finish(state):
        o2, l, q0, cs = state
        o2 = o2 / l
        o = jnp.where(low, o2[:tq2], o2[tq2:])
        o_ref[pl.ds(q0, tq2), cs] = o.astype(bf16)

    def group_body(g, carry):
        items = [(g * ATT_GROUP + ip, pair) for ip in range(ATT_GROUP) for pair in range(N_HEAD_PAIRS)]
        a, b = {}, {}
        for step in range(len(items) + 2):
            if step < len(items):
                a[step] = scores(*items[step])
            if 0 <= step - 1 < len(items):
                b[step - 1] = weights(a.pop(step - 1))
            if 0 <= step - 2 < len(items):
                finish(b.pop(step - 2))
        return carry

    lax.fori_loop(0, ATT_ROWS // (2 * ATT_GROUP), group_body, 0)


def _attention(qkv, qkv_meta, slabs, bsz, t):
    rows = t // GRID_W
    assert rows >= 2 * WIN_H and rows % ATT_ROWS == 0 and ATT_ROWS % (2 * ATT_GROUP) == 0
    steps = rows // ATT_ROWS
    tq = ATT_ROWS * GRID_W
    return pl.pallas_call(
        functools.partial(_attention_kernel, rows=rows),
        grid=(bsz, steps),
        in_specs=[pl.BlockSpec((tq, D_ATT), lambda b, j: (b * steps + j, 0)),
                  pl.BlockSpec((tq, D_ATT), lambda b, j: (b * steps + j, 1)),
                  pl.BlockSpec((t, D_ATT), lambda b, j: (b, 2)),
                  pl.BlockSpec((t, D_ATT), lambda b, j: (b, 3)),
                  pl.BlockSpec((N_META, D_ATT), lambda b, j: (0, 2)),
                  pl.BlockSpec((N_META, D_ATT), lambda b, j: (0, 3)),
                  _full(slabs)],
        out_specs=pl.BlockSpec((tq, D_ATT), lambda b, j: (b * steps + j, 0)),
        out_shape=jax.ShapeDtypeStruct((bsz * t, D_ATT), bf16),
        compiler_params=_cparams(2),
        name="attention",
    )(qkv, qkv, qkv, qkv, qkv_meta, qkv_meta, slabs)


def _layer_norm(z, g, b):
    mu = jnp.mean(z, axis=-1, keepdims=True)
    d = z - mu
    var = jnp.mean(d * d, axis=-1, keepdims=True)
    return d * lax.rsqrt(var + LN_EPS) * g + b


def _sigmoid(x):
    return 0.5 * jnp.tanh(0.5 * x) + 0.5


def _route_tile(logits, before_ref):
    tm = logits.shape[0]
    lt = logits.T
    el = lt[:N_EXPERTS]
    gl = lt[N_EXPERTS:N_EXPERTS + N_GROUPS]
    neg = -jnp.inf
    erow = lax.broadcasted_iota(jnp.int32, el.shape, 0).astype(f32)
    grow = lax.broadcasted_iota(jnp.int32, gl.shape, 0).astype(f32)
    first = lambda hit, idx, n: jnp.min(jnp.where(hit, idx, float(n)), axis=0, keepdims=True)

    gmax = jnp.max(gl, axis=0, keepdims=True)
    grp = first(gl == gmax, grow, N_GROUPS)
    pg_sel = 1.0 / jnp.sum(jnp.exp(gl - gmax), axis=0, keepdims=True)

    e_lo = grp * EXPERTS_PER_GROUP
    elm = jnp.where((erow >= e_lo) & (erow < e_lo + EXPERTS_PER_GROUP), el, neg)
    t1 = jnp.max(elm, axis=0, keepdims=True)
    e1 = first(elm == t1, erow, N_EXPERTS)
    el2 = jnp.where(erow == e1, neg, elm)
    t2 = jnp.max(el2, axis=0, keepdims=True)
    e2 = first(el2 == t2, erow, N_EXPERTS)
    r = jnp.exp(t2 - t1)
    g1 = pg_sel / (1.0 + r)
    g2 = pg_sel * r / (1.0 + r)

    hit1 = erow == e1
    hit2 = erow == e2
    onehot = jnp.where(hit1 | hit2, 1.0, 0.0)
    before = jnp.dot(onehot.astype(bf16), before_ref[...], preferred_element_type=f32)
    rank1 = jnp.sum(jnp.where(hit1, before, 0.0), axis=0, keepdims=True)
    rank2 = jnp.sum(jnp.where(hit2, before, 0.0), axis=0, keepdims=True)
    rrow = lax.broadcasted_iota(jnp.int32, (ROUTE_ROWS, tm), 0)
    route = jnp.zeros((ROUTE_ROWS, tm), f32)
    for k, val in enumerate((e1, e2, g1, g2, rank1, rank2)):
        route = jnp.where(rrow == k, val, route)
    return route, jnp.sum(onehot, axis=1, keepdims=True)


def _mixer_tail_kernel(x_ref, att_ref, rest_ref, prevc_ref, prevx_ref, nextc_ref, nextx_ref, meta_ref,
                       convw_ref, convb_ref, wap_ref, wcp_ref, wout_ref, g_ref, b_ref, wr_ref, before_ref,
                       h_ref, hp_ref, route_ref, cnt_ref, merged_scr, *, tiles_per_seq, n_tiles):
    i = pl.program_id(0)
    pos = jnp.minimum(i, n_tiles - 1) % tiles_per_seq
    tm = x_ref.shape[0]
    c2, c0, c1, c3, c4 = (k * D_MODEL for k in range(5))
    cols = lambda j, base=0: slice(base + j * TAIL_CW, base + (j + 1) * TAIL_CW)
    chunks = range(D_MODEL // TAIL_CW)

    @pl.when(i == 0)
    def _():
        merged_scr[...] = jnp.zeros_like(merged_scr)

    def u_of(ref, rws, j):
        return ref[rws, cols(j, c0)].astype(f32) * ref[rws, cols(j, c1)].astype(f32)

    def halo_u(cref, xref, r, j):
        return cref[r:r + 1, cols(j)].astype(f32) * xref[r:r + 1, cols(j)].astype(f32)

    last = prevc_ref.shape[0] - 1
    row = lax.broadcasted_iota(jnp.int32, (tm, TAIL_CW), 0)

    def conv_chunk(j):
        u = u_of(rest_ref, slice(0, tm), j)
        u_prev = jnp.where(pos == 0, u_of(meta_ref, slice(N_META - 1, N_META), j),
                           halo_u(prevc_ref, prevx_ref, last, j))
        u_next = halo_u(nextc_ref, nextx_ref, 0, j)
        u_next = jnp.where(pos == tiles_per_seq - 1, jnp.zeros_like(u_next), u_next)
        u_m1 = jnp.where(row == 0, u_prev, pltpu.roll(u, 1, axis=0))
        u_p1 = jnp.where(row == tm - 1, u_next, pltpu.roll(u, tm - 1, axis=0))
        cw = convw_ref[:, cols(j)]
        s = u_m1 * cw[0:1] + u * cw[1:2] + u_p1 * cw[2:3] + convb_ref[:, cols(j)]
        return (rest_ref[:, cols(j, c2)].astype(f32) * s).astype(bf16)

    mix = [jnp.dot(merged_scr[...], wout_ref[:, cols(j)], preferred_element_type=f32) for j in chunks]

    cv = None
    for j in chunks:
        part = jnp.dot(conv_chunk(j), wcp_ref[cols(j), :], preferred_element_type=f32)
        cv = part if cv is None else cv + part
    ap = jnp.dot(att_ref[...], wap_ref[...], preferred_element_type=f32)

    z = [ALPHA * x_ref[:, cols(j)] + mix[j] for j in chunks]
    mu = sum(jnp.sum(zj, axis=-1, keepdims=True) for zj in z) * (1.0 / D_MODEL)
    d = [zj - mu for zj in z]
    var = sum(jnp.sum(dj * dj, axis=-1, keepdims=True) for dj in d) * (1.0 / D_MODEL)
    rstd = lax.rsqrt(var + LN_EPS)
    logits = None
    hs = []
    for j in chunks:
        hj = d[j] * rstd * g_ref[:, cols(j)] + b_ref[:, cols(j)]
        h_ref[:, cols(j)] = hj
        hs.append(hj)
        part = jnp.dot(hj.astype(bf16), wr_ref[cols(j), :], preferred_element_type=f32)
        logits = part if logits is None else logits + part
    hp_ref[...] = _pack_halves(jnp.concatenate(hs, axis=1))

    for j in chunks:
        gate_a = jnp.tanh(rest_ref[:, cols(j, c3)]) + 1.0
        gate_c = jnp.tanh(rest_ref[:, cols(j, c4)]) + 1.0
        merged_scr[:, cols(j)] = gate_a * ap[:, cols(j)].astype(bf16) + gate_c * cv[:, cols(j)].astype(bf16)

    route, count = _route_tile(logits, before_ref)
    route_ref[...] = route
    cnt_ref[0] = jnp.broadcast_to(count, (N_EXPERTS, LANES))


def _mixer_tail(x, att, rest, rest_meta, p, t):
    n = x.shape[0]
    tm = TAIL_TM
    halo = 16
    hb = tm // halo
    n_halo = n // halo
    n_tiles = n // tm
    tiles_per_seq = t // tm
    before = jnp.asarray(np.triu(np.ones((tm, tm), np.float32), 1), dtype=bf16)
    consts = (rest_meta, p["conv_w"], p["conv_b"], p["w_att_proj"], p["w_conv_proj"], p["w_out"],
              p["ln1_g"], p["ln1_b"], p["w_router"], before)
    front = lambda i: jnp.minimum(i, n_tiles - 1)
    back = lambda i: jnp.maximum(i - 1, 0)
    prev_blk = lambda i: jnp.maximum(front(i) * hb - 1, 0)
    next_blk = lambda i: jnp.minimum((front(i) + 1) * hb, n_halo - 1)
    return pl.pallas_call(
        functools.partial(_mixer_tail_kernel, tiles_per_seq=tiles_per_seq, n_tiles=n_tiles),
        grid=(n_tiles + 1,),
        in_specs=[pl.BlockSpec((tm, D_MODEL), lambda i: (back(i), 0)),
                  pl.BlockSpec((tm, D_ATT), lambda i: (front(i), 0)),
                  pl.BlockSpec((tm, D_REST), lambda i: (front(i), 0)),
                  pl.BlockSpec((halo, D_CONV), lambda i: (prev_blk(i), 1)),
                  pl.BlockSpec((halo, D_CONV), lambda i: (prev_blk(i), 2)),
                  pl.BlockSpec((halo, D_CONV), lambda i: (next_blk(i), 1)),
                  pl.BlockSpec((halo, D_CONV), lambda i: (next_blk(i), 2))]
                 + [_full(c) for c in consts],
        out_specs=[pl.BlockSpec((tm, D_MODEL), lambda i: (back(i), 0)),
                   pl.BlockSpec((tm, D_PACKED), lambda i: (back(i), 0)),
                   pl.BlockSpec((ROUTE_ROWS, tm), lambda i: (0, back(i))),
                   pl.BlockSpec((1, N_EXPERTS, LANES), lambda i: (back(i), 0, 0))],
        out_shape=[jax.ShapeDtypeStruct((n, D_MODEL), f32),
                   jax.ShapeDtypeStruct((n, D_PACKED), jnp.uint32),
                   jax.ShapeDtypeStruct((ROUTE_ROWS, n), f32),
                   jax.ShapeDtypeStruct((n_tiles, N_EXPERTS, LANES), f32)],
        scratch_shapes=[pltpu.VMEM((tm, D_MODEL), bf16)],
        compiler_params=_cparams(1),
        name="mixer_tail",
    )(x, att, rest, rest, rest, rest, rest, *consts)


def _route(route, cnt, n, blk):
    n_tiles = n // TAIL_TM
    tile_cnt = cnt[:, :, 0].astype(jnp.int32)
    tile_off = jnp.cumsum(tile_cnt, axis=0) - tile_cnt
    counts = jnp.sum(tile_cnt, axis=0)
    padded = (counts + blk - 1) // blk * blk
    pend = jnp.cumsum(padded)
    pstart = pend - padded
    base = pstart[None, :] + tile_off
    lanes = jnp.arange(N_EXPERTS, dtype=jnp.int32)

    def rows_of(expert_row, rank_row):
        e = expert_row.astype(jnp.int32).reshape(n_tiles, TAIL_TM, 1)
        sel = jnp.sum(jnp.where(e == lanes, base[:, None, :], 0), axis=-1)
        return sel.reshape(n) + rank_row.astype(jnp.int32)

    dest0 = rows_of(route[0], route[4])
    dest1 = rows_of(route[1], route[5])

    n_blocks = (2 * n + N_EXPERTS * (blk - 1) + blk - 1) // blk
    blk_start = jnp.arange(n_blocks, dtype=jnp.int32) * blk
    blk_expert = jnp.sum((pend[None, :] <= blk_start[:, None]).astype(jnp.int32), axis=1)
    blk_expert = jnp.minimum(blk_expert, N_EXPERTS - 1)
    blk_valid = jnp.clip((pstart + counts)[blk_expert] - blk_start, 0, blk)
    blk_valid = jnp.where(blk_start < pend[-1], blk_valid, 0).astype(jnp.int32)
    return dest0, dest1, blk_expert, blk_valid, n_blocks


def _sc_mesh():
    return plsc.VectorSubcoreMesh(core_axis_name="c", subcore_axis_name="s")


def _sc_dispatch(h, dest0, dest1, p_rows):
    n, d = h.shape
    per = n // SC_WORKERS
    assert n % (SC_WORKERS * SC_IDX_WIN) == 0

    @pl.kernel(out_type=jax.ShapeDtypeStruct((p_rows, d), h.dtype), mesh=_sc_mesh(),
               scratch_types=[pltpu.VMEM((2, SC_IDX_WIN), jnp.int32), pltpu.VMEM((SC_ROWS, d), h.dtype)])
    def k(h_hbm, d0_hbm, d1_hbm, xs_hbm, idx, buf):
        base = (lax.axis_index("c") * SC_SUBCORES + lax.axis_index("s")) * per

        @pl.loop(0, per // SC_IDX_WIN)
        def _(w):
            off = base + w * SC_IDX_WIN
            pltpu.sync_copy(d0_hbm.at[pl.ds(off, SC_IDX_WIN)], idx.at[0])
            pltpu.sync_copy(d1_hbm.at[pl.ds(off, SC_IDX_WIN)], idx.at[1])
            for r in range(SC_IDX_WIN // SC_ROWS):
                pltpu.sync_copy(h_hbm.at[pl.ds(off + r * SC_ROWS, SC_ROWS)], buf)
                pltpu.sync_copy(buf, xs_hbm.at[idx.at[0, pl.ds(r * SC_ROWS, SC_ROWS)]])
                pltpu.sync_copy(buf, xs_hbm.at[idx.at[1, pl.ds(r * SC_ROWS, SC_ROWS)]])

    return k(h, dest0, dest1)


def _sc_gather2(ys, dest0, dest1):
    n = dest0.shape[0]
    d = ys.shape[1]
    per = n // SC_WORKERS
    out = jax.ShapeDtypeStruct((n, d), ys.dtype)

    @pl.kernel(out_type=[out, out], mesh=_sc_mesh(),
               scratch_types=[pltpu.VMEM((2, SC_IDX_WIN), jnp.int32), pltpu.VMEM((SC_ROWS, d), ys.dtype)])
    def k(ys_hbm, d0_hbm, d1_hbm, y0_hbm, y1_hbm, idx, buf):
        base = (lax.axis_index("c") * SC_SUBCORES + lax.axis_index("s")) * per

        @pl.loop(0, per // SC_IDX_WIN)
        def _(w):
            off = base + w * SC_IDX_WIN
            pltpu.sync_copy(d0_hbm.at[pl.ds(off, SC_IDX_WIN)], idx.at[0])
            pltpu.sync_copy(d1_hbm.at[pl.ds(off, SC_IDX_WIN)], idx.at[1])
            for r in range(SC_IDX_WIN // SC_ROWS):
                for kk, y_hbm in enumerate((y0_hbm, y1_hbm)):
                    pltpu.sync_copy(ys_hbm.at[idx.at[kk, pl.ds(r * SC_ROWS, SC_ROWS)]], buf)
                    pltpu.sync_copy(buf, y_hbm.at[pl.ds(off + r * SC_ROWS, SC_ROWS)])

    return k(ys, dest0, dest1)


def _experts_kernel(be_ref, bv_ref, xs_ref, wg_ref, wu_ref, wd_ref, ys_ref, wg_bf, wu_bf, wd_bf):
    i = pl.program_id(0)
    valid = bv_ref[i]

    @pl.when(jnp.logical_or(i == 0, be_ref[i] != be_ref[jnp.maximum(i - 1, 0)]))
    def _():
        wg_bf[...] = wg_ref[0].astype(bf16)
        wu_bf[...] = wu_ref[0].astype(bf16)
        wd_bf[...] = wd_ref[0].astype(bf16)

    @pl.when(valid > 0)
    def _():
        row = lax.broadcasted_iota(jnp.int32, xs_ref.shape, 0)
        keep = row < valid
        lo, hi = _unpack_halves(xs_ref[...])
        x = jnp.concatenate([jnp.where(keep, lo, 0.0), jnp.where(keep, hi, 0.0)], axis=1).astype(bf16)
        g = jnp.dot(x, wg_bf[...], preferred_element_type=f32)
        u = jnp.dot(x, wu_bf[...], preferred_element_type=f32)
        hmid = (g * _sigmoid(g) * u).astype(bf16)
        ys_ref[...] = _pack_halves(jnp.dot(hmid, wd_bf[...], preferred_element_type=f32))

    @pl.when(valid == 0)
    def _():
        ys_ref[...] = jnp.zeros_like(ys_ref)


def _experts(xs, blk_expert, blk_valid, wg, wu, wd, n_blocks, blk):
    grid_spec = pltpu.PrefetchScalarGridSpec(
        num_scalar_prefetch=2,
        grid=(n_blocks,),
        in_specs=[pl.BlockSpec((blk, D_PACKED), lambda i, be, bv: (i, 0)),
                  pl.BlockSpec((1, D_MODEL, D_EXPERT), lambda i, be, bv: (be[i], 0, 0)),
                  pl.BlockSpec((1, D_MODEL, D_EXPERT), lambda i, be, bv: (be[i], 0, 0)),
                  pl.BlockSpec((1, D_EXPERT, D_MODEL), lambda i, be, bv: (be[i], 0, 0))],
        out_specs=pl.BlockSpec((blk, D_PACKED), lambda i, be, bv: (i, 0)),
        scratch_shapes=[pltpu.VMEM((D_MODEL, D_EXPERT), bf16),
                        pltpu.VMEM((D_MODEL, D_EXPERT), bf16),
                        pltpu.VMEM((D_EXPERT, D_MODEL), bf16)],
    )
    return pl.pallas_call(
        _experts_kernel,
        grid_spec=grid_spec,
        out_shape=jax.ShapeDtypeStruct(xs.shape, jnp.uint32),
        compiler_params=_cparams(1),
        name="experts",
    )(blk_expert, blk_valid, xs, wg, wu, wd)


def _final_norm_kernel(h_ref, y0_ref, y1_ref, route_ref, g_ref, b_ref, o_ref):
    route_t = route_ref[...].T
    g0, g1 = route_t[:, 2:3], route_t[:, 3:4]
    lo0, hi0 = _unpack_halves(y0_ref[...])
    lo1, hi1 = _unpack_halves(y1_ref[...])
    ffn = jnp.concatenate([lo0 * g0 + lo1 * g1, hi0 * g0 + hi1 * g1], axis=1)
    o_ref[...] = _layer_norm(ALPHA * h_ref[...] + ffn, g_ref[...], b_ref[...])


def _final_norm(h, y0, y1, route, ln_g, ln_b):
    n = h.shape[0]
    ts = NORM_TS
    tile = pl.BlockSpec((ts, D_MODEL), lambda i: (i, 0))
    packed = pl.BlockSpec((ts, D_PACKED), lambda i: (i, 0))
    return pl.pallas_call(
        _final_norm_kernel,
        grid=(n // ts,),
        in_specs=[tile, packed, packed, pl.BlockSpec((ROUTE_ROWS, ts), lambda i: (0, i)),
                  _full(ln_g), _full(ln_b)],
        out_specs=tile,
        out_shape=jax.ShapeDtypeStruct((n, D_MODEL), f32),
        compiler_params=_cparams(1),
        name="final_norm",
    )(h, y0, y1, route, ln_g, ln_b)


def _encode(x, p, qkv_meta, rest_meta, slabs):
    bsz, t, _ = x.shape
    n = bsz * t
    assert t % NORM_TS == 0 and t % TAIL_TM == 0 and t % PROJ_TM == 0
    xf = x.reshape(n, D_MODEL)
    qkv, rest = _in_proj(xf, p["w_in"], PROJ_TM)
    att = _attention(qkv, qkv_meta, slabs, bsz, t)
    h1, h1_packed, route, cnt = _mixer_tail(xf, att, rest, rest_meta, p, t)
    blk = MOE_BLK
    dest0, dest1, blk_expert, blk_valid, n_blocks = _route(route, cnt, n, blk)
    xs = _sc_dispatch(h1_packed, dest0, dest1, n_blocks * blk)
    ys = _experts(xs, blk_expert, blk_valid, p["w_e_gate"], p["w_e_up"], p["w_e_down"], n_blocks, blk)
    y0, y1 = _sc_gather2(ys, dest0, dest1)
    y = _final_norm(h1, y0, y1, route, p["ln2_g"], p["ln2_b"])
    return y.reshape(bsz, t, D_MODEL)


def kernel(x_prompt, x_sample, meta_tokens, w_in, rpb, conv_w, conv_b, w_att_proj, w_conv_proj, w_out,
           ln1_g, ln1_b, w_router_group, w_router_expert, w_e_gate, w_e_up, w_e_down, ln2_g, ln2_b):
    col_scale = np.ones((1, D_IN_PROJ), np.float32)
    col_scale[:, D_IN_PROJ - 2 * D_MODEL:] = 0.5
    w_in_bf = (w_in[0] * col_scale).astype(bf16)
    w_router = jnp.concatenate([w_router_expert[0], w_router_group[0]], axis=1)
    w_router = jnp.pad(w_router, ((0, 0), (0, LANES - w_router.shape[1])))
    row = lambda v: v[0].reshape(1, -1).astype(f32)
    p = {
        "w_in": w_in_bf,
        "conv_w": conv_w[0].astype(f32), "conv_b": row(conv_b),
        "w_att_proj": w_att_proj[0].astype(bf16), "w_conv_proj": w_conv_proj[0].astype(bf16),
        "w_out": (0.5 * w_out[0]).astype(bf16), "ln1_g": row(ln1_g), "ln1_b": row(ln1_b),
        "w_router": w_router.astype(bf16),
        "w_e_gate": w_e_gate[0], "w_e_up": w_e_up[0], "w_e_down": w_e_down[0],
        "ln2_g": row(ln2_g), "ln2_b": row(ln2_b),
    }
    qkv_meta, rest_meta = _in_proj(meta_tokens.astype(f32), p["w_in"], N_META)
    slabs = _bias_slabs(rpb[0])
    y_prompt = _encode(x_prompt, p, qkv_meta, rest_meta, slabs)
    y_sample = _encode(x_sample, p, qkv_meta, rest_meta, slabs)
    return (y_prompt, y_sample)
```

```python
import functools

import numpy as np
import jax
import jax.numpy as jnp
from jax import lax
from jax.experimental import pallas as pl
from jax.experimental.pallas import tpu as pltpu
from jax.experimental.pallas import tpu_sc as plsc

D_MODEL = 1024
N_META = 16
GRID_W = 64
WIN_H = 8
WIN_W = 16
N_HEADS = 8
HEAD_DIM = 64
D_ATT = N_HEADS * HEAD_DIM
D_CONV = D_MODEL
N_GROUPS = 4
EXPERTS_PER_GROUP = 8
N_EXPERTS = N_GROUPS * EXPERTS_PER_GROUP
D_EXPERT = D_MODEL // 2
DEPTH = 1
ALPHA = (2.0 * DEPTH) ** 0.25
LN_EPS = 1e-5
D_QKV = 4 * D_ATT
LOG2E = 1.4426950408889634
D_REST = 3 * D_CONV + 2 * D_MODEL
D_IN_PROJ = 3 * D_ATT + D_REST

LANES = 128
HEADS_PER_VREG = LANES // HEAD_DIM
N_HEAD_PAIRS = N_HEADS // HEADS_PER_VREG
NEG_BIG = -1e30

PROJ_TM = 512
PROJ_TN = 512
ATT_ROWS = 8
ATT_GROUP = 4
SLAB_ROWS = WIN_H + 1
ATT_KEYS = SLAB_ROWS * GRID_W + N_META
TAIL_TM = 512
TAIL_CW = 256
NORM_TS = 512
MOE_BLK = 512
ROUTE_ROWS = 8
VMEM_LIMIT = 56 * 1024 * 1024

SC_CORES = 2
SC_SUBCORES = 16
SC_WORKERS = SC_CORES * SC_SUBCORES
SC_IDX_WIN = 128
SC_ROWS = 64
D_PACKED = D_MODEL // 2

bf16 = jnp.bfloat16
f32 = jnp.float32


def _cparams(n_axes):
    return pltpu.CompilerParams(dimension_semantics=("arbitrary",) * n_axes,
                                vmem_limit_bytes=VMEM_LIMIT)


def _full(a):
    return pl.BlockSpec(a.shape, lambda *_: (0,) * a.ndim)


def _pack_halves(x):
    half = x.shape[1] // 2
    bits = lambda v: lax.bitcast_convert_type(v.astype(bf16).astype(f32), jnp.uint32)
    return (bits(x[:, :half]) >> 16) | bits(x[:, half:])


def _unpack_halves(p):
    lo = lax.bitcast_convert_type(p << 16, f32)
    hi = lax.bitcast_convert_type(p & jnp.uint32(0xFFFF0000), f32)
    return lo, hi


def _in_proj_kernel(x_ref, w_ref, qkv_ref, rest_ref):
    assert PROJ_TN == D_ATT
    xb = x_ref[...].astype(bf16)
    for c in range(0, D_IN_PROJ, PROJ_TN):
        y = jnp.dot(xb, w_ref[:, c:c + PROJ_TN], preferred_element_type=f32)
        if c == 0:
            y = (y * (HEAD_DIM ** -0.5 * LOG2E)).astype(bf16)
            lane = lax.broadcasted_iota(jnp.int32, y.shape, 1)
            even = (lane & HEAD_DIM) == 0
            zero = jnp.zeros_like(y)
            qkv_ref[:, :D_ATT] = jnp.where(even, y, zero)
            qkv_ref[:, D_ATT:2 * D_ATT] = jnp.where(even, zero, y)
        elif c < 3 * D_ATT:
            qkv_ref[:, c + D_ATT:c + D_ATT + PROJ_TN] = y.astype(bf16)
        else:
            rest_ref[:, c - 3 * D_ATT:c - 3 * D_ATT + PROJ_TN] = y.astype(bf16)


def _in_proj(x, w_bf, tm):
    n = x.shape[0]
    return pl.pallas_call(
        _in_proj_kernel,
        grid=(n // tm,),
        in_specs=[pl.BlockSpec((tm, D_MODEL), lambda i: (i, 0)),
                  pl.BlockSpec((D_MODEL, D_IN_PROJ), lambda i: (0, 0))],
        out_specs=[pl.BlockSpec((tm, D_QKV), lambda i: (i, 0)),
                   pl.BlockSpec((tm, D_REST), lambda i: (i, 0))],
        out_shape=[jax.ShapeDtypeStruct((n, D_QKV), bf16),
                   jax.ShapeDtypeStruct((n, D_REST), bf16)],
        compiler_params=_cparams(1),
        name="in_proj",
    )(x, w_bf)


def _bias_slabs(rpb):
    qc = np.arange(GRID_W)[:, None]
    kc = np.arange(GRID_W)[None, :]
    w_start = np.clip(qc - WIN_W // 2, 0, GRID_W - WIN_W)
    valid = (kc >= w_start) & (kc < w_start + WIN_W)
    n_dx = 2 * WIN_W - 1
    onehot = (kc - qc + WIN_W - 1)[None] == np.arange(n_dx)[:, None, None]
    t = jnp.einsum("hyx,xqk->hyqk", rpb.astype(f32) * LOG2E, jnp.asarray(onehot & valid[None], f32),
                   precision=lax.Precision.HIGHEST)
    t = jnp.where(jnp.asarray(valid)[None, None], t, NEG_BIG)
    masked = jnp.full((N_HEADS, GRID_W, GRID_W), NEG_BIG, f32)
    zeros = jnp.zeros((N_HEADS, GRID_W, N_META), f32)
    slabs = []
    for dy0 in range(WIN_H):
        for off in range(2):
            tiles = [t[:, dy0 + jr - off] if 0 <= jr - off < WIN_H else masked for jr in range(SLAB_ROWS)]
            slabs.append(jnp.concatenate(tiles + [zeros], axis=-1))
    s = jnp.stack(slabs, axis=1).reshape(N_HEAD_PAIRS, HEADS_PER_VREG, WIN_H, 2, GRID_W, ATT_KEYS)
    s = jnp.transpose(s, (0, 2, 3, 1, 4, 5))
    return s.reshape(N_HEAD_PAIRS, WIN_H, 2, HEADS_PER_VREG * GRID_W, ATT_KEYS)


def _attention_kernel(qe_ref, qo_ref, k_ref, v_ref, km_ref, vm_ref, bias_ref, o_ref, *, rows):
    j = pl.program_id(1)
    tq2 = 2 * GRID_W
    lane = lax.broadcasted_iota(jnp.int32, (tq2, LANES), 1)
    low = lane < HEAD_DIM
    nt = (((1,), (1,)), ((), ()))
    slab_keys = SLAB_ROWS * GRID_W

    def scores(ip, pair):
        r0 = j * ATT_ROWS + 2 * ip
        rs = [jnp.clip(r0 + a - WIN_H // 2, 0, rows - WIN_H) for a in range(2)]
        us = jnp.minimum(rs[0], rows - SLAB_ROWS)
        q0 = pl.multiple_of(ip * tq2, tq2)
        k0 = pl.multiple_of(us * GRID_W, GRID_W)
        cs = slice(pair * LANES, (pair + 1) * LANES)
        qq = jnp.concatenate([qe_ref[pl.ds(q0, tq2), cs], qo_ref[pl.ds(q0, tq2), cs]], axis=0)
        k2 = jnp.concatenate([k_ref[pl.ds(k0, slab_keys), cs], km_ref[:, cs]], axis=0)
        s = lax.dot_general(qq, k2, nt, preferred_element_type=f32)
        b = [bias_ref[pair, rs[a] - (r0 + a) + (WIN_H - 1), rs[a] - us] for a in range(2)]
        s = s + jnp.concatenate([b[0][:GRID_W], b[1][:GRID_W], b[0][GRID_W:], b[1][GRID_W:]], axis=0)
        return s, jnp.max(s, axis=-1, keepdims=True), k0, q0, cs

    def weights(state):
        s, m, k0, q0, cs = state
        v2 = jnp.concatenate([v_ref[pl.ds(k0, slab_keys), cs], vm_ref[:, cs]], axis=0)
        e = jnp.exp2(s - m)
        l = jnp.sum(e, axis=-1, keepdims=True)
        return jnp.dot(e.astype(bf16), v2, preferred_element_type=f32), l, q0, cs

    def finish(state):
        o2, l, q0, cs = state
        o2 = o2 / l
        o = jnp.where(low, o2[:tq2], o2[tq2:])
        o_ref[pl.ds(q0, tq2), cs] = o.astype(bf16)

    def group_body(g, carry):
        items = [(g * ATT_GROUP + ip, pair) for ip in range(ATT_GROUP) for pair in range(N_HEAD_PAIRS)]
        a, b = {}, {}
        for step in range(len(items) + 2):
            if step < len(items):
                a[step] = scores(*items[step])
            if 0 <= step - 1 < len(items):
                b[step - 1] = weights(a.pop(step - 1))
            if 0 <= step - 2 < len(items):
                finish(b.pop(step - 2))
        return carry

    lax.fori_loop(0, ATT_ROWS // (2 * ATT_GROUP), group_body, 0)


def _attention(qkv, qkv_meta, slabs, bsz, t):
    rows = t // GRID_W
    assert rows >= 2 * WIN_H and rows % ATT_ROWS == 0 and ATT_ROWS % (2 * ATT_GROUP) == 0
    steps = rows // ATT_ROWS
    tq = ATT_ROWS * GRID_W
    return pl.pallas_call(
        functools.partial(_attention_kernel, rows=rows),
        grid=(bsz, steps),
        in_specs=[pl.BlockSpec((tq, D_ATT), lambda b, j: (b * steps + j, 0)),
                  pl.BlockSpec((tq, D_ATT), lambda b, j: (b * steps + j, 1)),
                  pl.BlockSpec((t, D_ATT), lambda b, j: (b, 2)),
                  pl.BlockSpec((t, D_ATT), lambda b, j: (b, 3)),
                  pl.BlockSpec((N_META, D_ATT), lambda b, j: (0, 2)),
                  pl.BlockSpec((N_META, D_ATT), lambda b, j: (0, 3)),
                  _full(slabs)],
        out_specs=pl.BlockSpec((tq, D_ATT), lambda b, j: (b * steps + j, 0)),
        out_shape=jax.ShapeDtypeStruct((bsz * t, D_ATT), bf16),
        compiler_params=_cparams(2),
        name="attention",
    )(qkv, qkv, qkv, qkv, qkv_meta, qkv_meta, slabs)


def _layer_norm(z, g, b):
    mu = jnp.mean(z, axis=-1, keepdims=True)
    d = z - mu
    var = jnp.mean(d * d, axis=-1, keepdims=True)
    return d * lax.rsqrt(var + LN_EPS) * g + b


def _sigmoid(x):
    return 0.5 * jnp.tanh(0.5 * x) + 0.5


def _route_tile(logits, before_ref):
    tm = logits.shape[0]
    lt = logits.T
    el = lt[:N_EXPERTS]
    gl = lt[N_EXPERTS:N_EXPERTS + N_GROUPS]
    neg = -jnp.inf
    erow = lax.broadcasted_iota(jnp.int32, el.shape, 0).astype(f32)
    grow = lax.broadcasted_iota(jnp.int32, gl.shape, 0).astype(f32)
    first = lambda hit, idx, n: jnp.min(jnp.where(hit, idx, float(n)), axis=0, keepdims=True)

    gmax = jnp.max(gl, axis=0, keepdims=True)
    grp = first(gl == gmax, grow, N_GROUPS)
    pg_sel = 1.0 / jnp.sum(jnp.exp(gl - gmax), axis=0, keepdims=True)

    e_lo = grp * EXPERTS_PER_GROUP
    elm = jnp.where((erow >= e_lo) & (erow < e_lo + EXPERTS_PER_GROUP), el, neg)
    t1 = jnp.max(elm, axis=0, keepdims=True)
    e1 = first(elm == t1, erow, N_EXPERTS)
    el2 = jnp.where(erow == e1, neg, elm)
    t2 = jnp.max(el2, axis=0, keepdims=True)
    e2 = first(el2 == t2, erow, N_EXPERTS)
    r = jnp.exp(t2 - t1)
    g1 = pg_sel / (1.0 + r)
    g2 = pg_sel * r / (1.0 + r)

    hit1 = erow == e1
    hit2 = erow == e2
    onehot = jnp.where(hit1 | hit2, 1.0, 0.0)
    before = jnp.dot(onehot.astype(bf16), before_ref[...], preferred_element_type=f32)
    rank1 = jnp.sum(jnp.where(hit1, before, 0.0), axis=0, keepdims=True)
    rank2 = jnp.sum(jnp.where(hit2, before, 0.0), axis=0, keepdims=True)
    rrow = lax.broadcasted_iota(jnp.int32, (ROUTE_ROWS, tm), 0)
    route = jnp.zeros((ROUTE_ROWS, tm), f32)
    for k, val in enumerate((e1, e2, g1, g2, rank1, rank2)):
        route = jnp.where(rrow == k, val, route)
    return route, jnp.sum(onehot, axis=1, keepdims=True)


def _mixer_tail_kernel(x_ref, att_ref, rest_ref, prevc_ref, prevx_ref, nextc_ref, nextx_ref, meta_ref,
                       convw_ref, convb_ref, wap_ref, wcp_ref, wout_ref, g_ref, b_ref, wr_ref, before_ref,
                       h_ref, hp_ref, route_ref, cnt_ref, merged_scr, *, tiles_per_seq, n_tiles):
    i = pl.program_id(0)
    pos = jnp.minimum(i, n_tiles - 1) % tiles_per_seq
    tm = x_ref.shape[0]
    c2, c0, c1, c3, c4 = (k * D_MODEL for k in range(5))
    cols = lambda j, base=0: slice(base + j * TAIL_CW, base + (j + 1) * TAIL_CW)
    chunks = range(D_MODEL // TAIL_CW)

    @pl.when(i == 0)
    def _():
        merged_scr[...] = jnp.zeros_like(merged_scr)

    def u_of(ref, rws, j):
        return ref[rws, cols(j, c0)].astype(f32) * ref[rws, cols(j, c1)].astype(f32)

    def halo_u(cref, xref, r, j):
        return cref[r:r + 1, cols(j)].astype(f32) * xref[r:r + 1, cols(j)].astype(f32)

    last = prevc_ref.shape[0] - 1
    row = lax.broadcasted_iota(jnp.int32, (tm, TAIL_CW), 0)

    def conv_chunk(j):
        u = u_of(rest_ref, slice(0, tm), j)
        u_prev = jnp.where(pos == 0, u_of(meta_ref, slice(N_META - 1, N_META), j),
                           halo_u(prevc_ref, prevx_ref, last, j))
        u_next = halo_u(nextc_ref, nextx_ref, 0, j)
        u_next = jnp.where(pos == tiles_per_seq - 1, jnp.zeros_like(u_next), u_next)
        u_m1 = jnp.where(row == 0, u_prev, pltpu.roll(u, 1, axis=0))
        u_p1 = jnp.where(row == tm - 1, u_next, pltpu.roll(u, tm - 1, axis=0))
        cw = convw_ref[:, cols(j)]
        s = u_m1 * cw[0:1] + u * cw[1:2] + u_p1 * cw[2:3] + convb_ref[:, cols(j)]
        return (rest_ref[:, cols(j, c2)].astype(f32) * s).astype(bf16)

    mix = [jnp.dot(merged_scr[...], wout_ref[:, cols(j)], preferred_element_type=f32) for j in chunks]

    cv = None
    for j in chunks:
        part = jnp.dot(conv_chunk(j), wcp_ref[cols(j), :], preferred_element_type=f32)
        cv = part if cv is None else cv + part
    ap = jnp.dot(att_ref[...], wap_ref[...], preferred_element_type=f32)

    z = [ALPHA * x_ref[:, cols(j)] + mix[j] for j in chunks]
    mu = sum(jnp.sum(zj, axis=-1, keepdims=True) for zj in z) * (1.0 / D_MODEL)
    d = [zj - mu for zj in z]
    var = sum(jnp.sum(dj * dj, axis=-1, keepdims=True) for dj in d) * (1.0 / D_MODEL)
    rstd = lax.rsqrt(var + LN_EPS)
    logits = None
    hs = []
    for j in chunks:
        hj = d[j] * rstd * g_ref[:, cols(j)] + b_ref[:, cols(j)]
        h_ref[:, cols(j)] = hj
        hs.append(hj)
        part = jnp.dot(hj.astype(bf16), wr_ref[cols(j), :], preferred_element_type=f32)
        logits = part if logits is None else logits + part
    hp_ref[...] = _pack_halves(jnp.concatenate(hs, axis=1))

    for j in chunks:
        gate_a = jnp.tanh(rest_ref[:, cols(j, c3)]) + 1.0
        gate_c = jnp.tanh(rest_ref[:, cols(j, c4)]) + 1.0
        merged_scr[:, cols(j)] = gate_a * ap[:, cols(j)].astype(bf16) + gate_c * cv[:, cols(j)].astype(bf16)

    route, count = _route_tile(logits, before_ref)
    route_ref[...] = route
    cnt_ref[0] = jnp.broadcast_to(count, (N_EXPERTS, LANES))


def _mixer_tail(x, att, rest, rest_meta, p, t):
    n = x.shape[0]
    tm = TAIL_TM
    halo = 16
    hb = tm // halo
    n_halo = n // halo
    n_tiles = n // tm
    tiles_per_seq = t // tm
    before = jnp.asarray(np.triu(np.ones((tm, tm), np.float32), 1), dtype=bf16)
    consts = (rest_meta, p["conv_w"], p["conv_b"], p["w_att_proj"], p["w_conv_proj"], p["w_out"],
              p["ln1_g"], p["ln1_b"], p["w_router"], before)
    front = lambda i: jnp.minimum(i, n_tiles - 1)
    back = lambda i: jnp.maximum(i - 1, 0)
    prev_blk = lambda i: jnp.maximum(front(i) * hb - 1, 0)
    next_blk = lambda i: jnp.minimum((front(i) + 1) * hb, n_halo - 1)
    return pl.pallas_call(
        functools.partial(_mixer_tail_kernel, tiles_per_seq=tiles_per_seq, n_tiles=n_tiles),
        grid=(n_tiles + 1,),
        in_specs=[pl.BlockSpec((tm, D_MODEL), lambda i: (back(i), 0)),
                  pl.BlockSpec((tm, D_ATT), lambda i: (front(i), 0)),
                  pl.BlockSpec((tm, D_REST), lambda i: (front(i), 0)),
                  pl.BlockSpec((halo, D_CONV), lambda i: (prev_blk(i), 1)),
                  pl.BlockSpec((halo, D_CONV), lambda i: (prev_blk(i), 2)),
                  pl.BlockSpec((halo, D_CONV), lambda i: (next_blk(i), 1)),
                  pl.BlockSpec((halo, D_CONV), lambda i: (next_blk(i), 2))]
                 + [_full(c) for c in consts],
        out_specs=[pl.BlockSpec((tm, D_MODEL), lambda i: (back(i), 0)),
                   pl.BlockSpec((tm, D_PACKED), lambda i: (back(i), 0)),
                   pl.BlockSpec((ROUTE_ROWS, tm), lambda i: (0, back(i))),
                   pl.BlockSpec((1, N_EXPERTS, LANES), lambda i: (back(i), 0, 0))],
        out_shape=[jax.ShapeDtypeStruct((n, D_MODEL), f32),
                   jax.ShapeDtypeStruct((n, D_PACKED), jnp.uint32),
                   jax.ShapeDtypeStruct((ROUTE_ROWS, n), f32),
                   jax.ShapeDtypeStruct((n_tiles, N_EXPERTS, LANES), f32)],
        scratch_shapes=[pltpu.VMEM((tm, D_MODEL), bf16)],
        compiler_params=_cparams(1),
        name="mixer_tail",
    )(x, att, rest, rest, rest, rest, rest, *consts)


def _route(route, cnt, n, blk):
    n_tiles = n // TAIL_TM
    tile_cnt = cnt[:, :, 0].astype(jnp.int32)
    tile_off = jnp.cumsum(tile_cnt, axis=0) - tile_cnt
    counts = jnp.sum(tile_cnt, axis=0)
    padded = (counts + blk - 1) // blk * blk
    pend = jnp.cumsum(padded)
    pstart = pend - padded
    base = pstart[None, :] + tile_off
    lanes = jnp.arange(N_EXPERTS, dtype=jnp.int32)

    def rows_of(expert_row, rank_row):
        e = expert_row.astype(jnp.int32).reshape(n_tiles, TAIL_TM, 1)
        sel = jnp.sum(jnp.where(e == lanes, base[:, None, :], 0), axis=-1)
        return sel.reshape(n) + rank_row.astype(jnp.int32)

    dest0 = rows_of(route[0], route[4])
    dest1 = rows_of(route[1], route[5])

    n_blocks = (2 * n + N_EXPERTS * (blk - 1) + blk - 1) // blk
    blk_start = jnp.arange(n_blocks, dtype=jnp.int32) * blk
    blk_expert = jnp.sum((pend[None, :] <= blk_start[:, None]).astype(jnp.int32), axis=1)
    blk_expert = jnp.minimum(blk_expert, N_EXPERTS - 1)
    blk_valid = jnp.clip((pstart + counts)[blk_expert] - blk_start, 0, blk)
    blk_valid = jnp.where(blk_start < pend[-1], blk_valid, 0).astype(jnp.int32)
    return dest0, dest1, blk_expert, blk_valid, n_blocks


def _sc_mesh():
    return plsc.VectorSubcoreMesh(core_axis_name="c", subcore_axis_name="s")


def _sc_dispatch(h, dest0, dest1, p_rows):
    n, d = h.shape
    per = n // SC_WORKERS
    assert n % (SC_WORKERS * SC_IDX_WIN) == 0

    @pl.kernel(out_type=jax.ShapeDtypeStruct((p_rows, d), h.dtype), mesh=_sc_mesh(),
               scratch_types=[pltpu.VMEM((2, SC_IDX_WIN), jnp.int32), pltpu.VMEM((SC_ROWS, d), h.dtype)])
    def k(h_hbm, d0_hbm, d1_hbm, xs_hbm, idx, buf):
        base = (lax.axis_index("c") * SC_SUBCORES + lax.axis_index("s")) * per

        @pl.loop(0, per // SC_IDX_WIN)
        def _(w):
            off = base + w * SC_IDX_WIN
            pltpu.sync_copy(d0_hbm.at[pl.ds(off, SC_IDX_WIN)], idx.at[0])
            pltpu.sync_copy(d1_hbm.at[pl.ds(off, SC_IDX_WIN)], idx.at[1])
            for r in range(SC_IDX_WIN // SC_ROWS):
                pltpu.sync_copy(h_hbm.at[pl.ds(off + r * SC_ROWS, SC_ROWS)], buf)
                pltpu.sync_copy(buf, xs_hbm.at[idx.at[0, pl.ds(r * SC_ROWS, SC_ROWS)]])
                pltpu.sync_copy(buf, xs_hbm.at[idx.at[1, pl.ds(r * SC_ROWS, SC_ROWS)]])

    return k(h, dest0, dest1)


def _sc_gather2(ys, dest0, dest1):
    n = dest0.shape[0]
    d = ys.shape[1]
    per = n // SC_WORKERS
    out = jax.ShapeDtypeStruct((n, d), ys.dtype)

    @pl.kernel(out_type=[out, out], mesh=_sc_mesh(),
               scratch_types=[pltpu.VMEM((2, SC_IDX_WIN), jnp.int32), pltpu.VMEM((SC_ROWS, d), ys.dtype)])
    def k(ys_hbm, d0_hbm, d1_hbm, y0_hbm, y1_hbm, idx, buf):
        base = (lax.axis_index("c") * SC_SUBCORES + lax.axis_index("s")) * per

        @pl.loop(0, per // SC_IDX_WIN)
        def _(w):
            off = base + w * SC_IDX_WIN
            pltpu.sync_copy(d0_hbm.at[pl.ds(off, SC_IDX_WIN)], idx.at[0])
            pltpu.sync_copy(d1_hbm.at[pl.ds(off, SC_IDX_WIN)], idx.at[1])
            for r in range(SC_IDX_WIN // SC_ROWS):
                for kk, y_hbm in enumerate((y0_hbm, y1_hbm)):
                    pltpu.sync_copy(ys_hbm.at[idx.at[kk, pl.ds(r * SC_ROWS, SC_ROWS)]], buf)
                    pltpu.sync_copy(buf, y_hbm.at[pl.ds(off + r * SC_ROWS, SC_ROWS)])

    return k(ys, dest0, dest1)


def _experts_kernel(be_ref, bv_ref, xs_ref, wg_ref, wu_ref, wd_ref, ys_ref, wg_bf, wu_bf, wd_bf):
    i = pl.program_id(0)
    valid = bv_ref[i]

    @pl.when(jnp.logical_or(i == 0, be_ref[i] != be_ref[jnp.maximum(i - 1, 0)]))
    def _():
        wg_bf[...] = wg_ref[0].astype(bf16)
        wu_bf[...] = wu_ref[0].astype(bf16)
        wd_bf[...] = wd_ref[0].astype(bf16)

    @pl.when(valid > 0)
    def _():
        row = lax.broadcasted_iota(jnp.int32, xs_ref.shape, 0)
        keep = row < valid
        lo, hi = _unpack_halves(xs_ref[...])
        x = jnp.concatenate([jnp.where(keep, lo, 0.0), jnp.where(keep, hi, 0.0)], axis=1).astype(bf16)
        g = jnp.dot(x, wg_bf[...], preferred_element_type=f32)
        u = jnp.dot(x, wu_bf[...], preferred_element_type=f32)
        hmid = (g * _sigmoid(g) * u).astype(bf16)
        ys_ref[...] = _pack_halves(jnp.dot(hmid, wd_bf[...], preferred_element_type=f32))

    @pl.when(valid == 0)
    def _():
        ys_ref[...] = jnp.zeros_like(ys_ref)


def _experts(xs, blk_expert, blk_valid, wg, wu, wd, n_blocks, blk):
    grid_spec = pltpu.PrefetchScalarGridSpec(
        num_scalar_prefetch=2,
        grid=(n_blocks,),
        in_specs=[pl.BlockSpec((blk, D_PACKED), lambda i, be, bv: (i, 0)),
                  pl.BlockSpec((1, D_MODEL, D_EXPERT), lambda i, be, bv: (be[i], 0, 0)),
                  pl.BlockSpec((1, D_MODEL, D_EXPERT), lambda i, be, bv: (be[i], 0, 0)),
                  pl.BlockSpec((1, D_EXPERT, D_MODEL), lambda i, be, bv: (be[i], 0, 0))],
        out_specs=pl.BlockSpec((blk, D_PACKED), lambda i, be, bv: (i, 0)),
        scratch_shapes=[pltpu.VMEM((D_MODEL, D_EXPERT), bf16),
                        pltpu.VMEM((D_MODEL, D_EXPERT), bf16),
                        pltpu.VMEM((D_EXPERT, D_MODEL), bf16)],
    )
    return pl.pallas_call(
        _experts_kernel,
        grid_spec=grid_spec,
        out_shape=jax.ShapeDtypeStruct(xs.shape, jnp.uint32),
        compiler_params=_cparams(1),
        name="experts",
    )(blk_expert, blk_valid, xs, wg, wu, wd)


def _final_norm_kernel(h_ref, y0_ref, y1_ref, route_ref, g_ref, b_ref, o_ref):
    route_t = route_ref[...].T
    g0, g1 = route_t[:, 2:3], route_t[:, 3:4]
    lo0, hi0 = _unpack_halves(y0_ref[...])
    lo1, hi1 = _unpack_halves(y1_ref[...])
    ffn = jnp.concatenate([lo0 * g0 + lo1 * g1, hi0 * g0 + hi1 * g1], axis=1)
    o_ref[...] = _layer_norm(ALPHA * h_ref[...] + ffn, g_ref[...], b_ref[...])


def _final_norm(h, y0, y1, route, ln_g, ln_b):
    n = h.shape[0]
    ts = NORM_TS
    tile = pl.BlockSpec((ts, D_MODEL), lambda i: (i, 0))
    packed = pl.BlockSpec((ts, D_PACKED), lambda i: (i, 0))
    return pl.pallas_call(
        _final_norm_kernel,
        grid=(n // ts,),
        in_specs=[tile, packed, packed, pl.BlockSpec((ROUTE_ROWS, ts), lambda i: (0, i)),
                  _full(ln_g), _full(ln_b)],
        out_specs=tile,
        out_shape=jax.ShapeDtypeStruct((n, D_MODEL), f32),
        compiler_params=_cparams(1),
        name="final_norm",
    )(h, y0, y1, route, ln_g, ln_b)


def _encode(x, p, qkv_meta, rest_meta, slabs):
    bsz, t, _ = x.shape
    n = bsz * t
    assert t % NORM_TS == 0 and t % TAIL_TM == 0 and t % PROJ_TM == 0
    xf = x.reshape(n, D_MODEL)
    qkv, rest = _in_proj(xf, p["w_in"], PROJ_TM)
    att = _attention(qkv, qkv_meta, slabs, bsz, t)
    h1, h1_packed, route, cnt = _mixer_tail(xf, att, rest, rest_meta, p, t)
    blk = MOE_BLK
    dest0, dest1, blk_expert, blk_valid, n_blocks = _route(route, cnt, n, blk)
    xs = _sc_dispatch(h1_packed, dest0, dest1, n_blocks * blk)
    ys = _experts(xs, blk_expert, blk_valid, p["w_e_gate"], p["w_e_up"], p["w_e_down"], n_blocks, blk)
    y0, y1 = _sc_gather2(ys, dest0, dest1)
    y = _final_norm(h1, y0, y1, route, p["ln2_g"], p["ln2_b"])
    return y.reshape(bsz, t, D_MODEL)


def kernel(x_prompt, x_sample, meta_tokens, w_in, rpb, conv_w, conv_b, w_att_proj, w_conv_proj, w_out,
           ln1_g, ln1_b, w_router_group, w_router_expert, w_e_gate, w_e_up, w_e_down, ln2_g, ln2_b):
    col_scale = np.ones((1, D_IN_PROJ), np.float32)
    col_scale[:, D_IN_PROJ - 2 * D_MODEL:] = 0.5
    w_in_bf = (w_in[0] * col_scale).astype(bf16)
    w_router = jnp.concatenate([w_router_expert[0], w_router_group[0]], axis=1)
    w_router = jnp.pad(w_router, ((0, 0), (0, LANES - w_router.shape[1])))
    row = lambda v: v[0].reshape(1, -1).astype(f32)
    p = {
        "w_in": w_in_bf,
        "conv_w": conv_w[0].astype(f32), "conv_b": row(conv_b),
        "w_att_proj": w_att_proj[0].astype(bf16), "w_conv_proj": w_conv_proj[0].astype(bf16),
        "w_out": (0.5 * w_out[0]).astype(bf16), "ln1_g": row(ln1_g), "ln1_b": row(ln1_b),
        "w_router": w_router.astype(bf16),
        "w_e_gate": w_e_gate[0], "w_e_up": w_e_up[0], "w_e_down": w_e_down[0],
        "ln2_g": row(ln2_g), "ln2_b": row(ln2_b),
    }
    qkv_meta, rest_meta = _in_proj(meta_tokens.astype(f32), p["w_in"], N_META)
    slabs = _bias_slabs(rpb[0])
    y_prompt = _encode(x_prompt, p, qkv_meta, rest_meta, slabs)
    p_small = dict(p, **{k: p[k].astype(bf16) for k in ("w_e_gate", "w_e_up", "w_e_down")})
    y_sample = _encode(x_sample, p_small, qkv_meta, rest_meta, slabs)
    return (y_prompt, y_sample)
```

```python
import functools

import numpy as np
import jax
import jax.numpy as jnp
from jax import lax
from jax.experimental import pallas as pl
from jax.experimental.pallas import tpu as pltpu
from jax.experimental.pallas import tpu_sc as plsc

D_MODEL = 1024
N_META = 16
GRID_W = 64
WIN_H = 8
WIN_W = 16
N_HEADS = 8
HEAD_DIM = 64
D_ATT = N_HEADS * HEAD_DIM
D_CONV = D_MODEL
N_GROUPS = 4
EXPERTS_PER_GROUP = 8
N_EXPERTS = N_GROUPS * EXPERTS_PER_GROUP
D_EXPERT = D_MODEL // 2
DEPTH = 1
ALPHA = (2.0 * DEPTH) ** 0.25
LN_EPS = 1e-5
D_QKV = 4 * D_ATT
LOG2E = 1.4426950408889634
D_REST = 3 * D_CONV + 2 * D_MODEL
D_IN_PROJ = 3 * D_ATT + D_REST

LANES = 128
HEADS_PER_VREG = LANES // HEAD_DIM
N_HEAD_PAIRS = N_HEADS // HEADS_PER_VREG
NEG_BIG = -1e30

PROJ_TM = 512
PROJ_TN = 512
ATT_ROWS = 8
ATT_GROUP = 4
SLAB_ROWS = WIN_H + 1
ATT_KEYS = SLAB_ROWS * GRID_W + N_META
TAIL_TM = 512
TAIL_CW = 256
NORM_TS = 512
MOE_BLK = 512
ROUTE_ROWS = 8
VMEM_LIMIT = 56 * 1024 * 1024

SC_CORES = 2
SC_SUBCORES = 16
SC_WORKERS = SC_CORES * SC_SUBCORES
SC_IDX_WIN = 128
SC_ROWS = 64
D_PACKED = D_MODEL // 2

bf16 = jnp.bfloat16
f32 = jnp.float32


def _cparams(n_axes):
    return pltpu.CompilerParams(dimension_semantics=("arbitrary",) * n_axes,
                                vmem_limit_bytes=VMEM_LIMIT)


def _full(a):
    return pl.BlockSpec(a.shape, lambda *_: (0,) * a.ndim)


def _pack_halves(x):
    half = x.shape[1] // 2
    bits = lambda v: lax.bitcast_convert_type(v.astype(bf16).astype(f32), jnp.uint32)
    return (bits(x[:, :half]) >> 16) | bits(x[:, half:])


def _unpack_halves(p):
    lo = lax.bitcast_convert_type(p << 16, f32)
    hi = lax.bitcast_convert_type(p & jnp.uint32(0xFFFF0000), f32)
    return lo, hi


def _in_proj_kernel(x_ref, w_ref, qkv_ref, rest_ref):
    assert PROJ_TN == D_ATT
    xb = x_ref[...].astype(bf16)
    for c in range(0, D_IN_PROJ, PROJ_TN):
        y = jnp.dot(xb, w_ref[:, c:c + PROJ_TN], preferred_element_type=f32)
        if c == 0:
            y = (y * (HEAD_DIM ** -0.5 * LOG2E)).astype(bf16)
            lane = lax.broadcasted_iota(jnp.int32, y.shape, 1)
            even = (lane & HEAD_DIM) == 0
            zero = jnp.zeros_like(y)
            qkv_ref[:, :D_ATT] = jnp.where(even, y, zero)
            qkv_ref[:, D_ATT:2 * D_ATT] = jnp.where(even, zero, y)
        elif c < 3 * D_ATT:
            qkv_ref[:, c + D_ATT:c + D_ATT + PROJ_TN] = y.astype(bf16)
        else:
            rest_ref[:, c - 3 * D_ATT:c - 3 * D_ATT + PROJ_TN] = y.astype(bf16)


def _in_proj(x, w_bf, tm):
    n = x.shape[0]
    return pl.pallas_call(
        _in_proj_kernel,
        grid=(n // tm,),
        in_specs=[pl.BlockSpec((tm, D_MODEL), lambda i: (i, 0)),
                  pl.BlockSpec((D_MODEL, D_IN_PROJ), lambda i: (0, 0))],
        out_specs=[pl.BlockSpec((tm, D_QKV), lambda i: (i, 0)),
                   pl.BlockSpec((tm, D_REST), lambda i: (i, 0))],
        out_shape=[jax.ShapeDtypeStruct((n, D_QKV), bf16),
                   jax.ShapeDtypeStruct((n, D_REST), bf16)],
        compiler_params=_cparams(1),
        name="in_proj",
    )(x, w_bf)


def _bias_slabs(rpb):
    qc = np.arange(GRID_W)[:, None]
    kc = np.arange(GRID_W)[None, :]
    w_start = np.clip(qc - WIN_W // 2, 0, GRID_W - WIN_W)
    valid = (kc >= w_start) & (kc < w_start + WIN_W)
    n_dx = 2 * WIN_W - 1
    onehot = (kc - qc + WIN_W - 1)[None] == np.arange(n_dx)[:, None, None]
    t = jnp.einsum("hyx,xqk->hyqk", rpb.astype(f32) * LOG2E, jnp.asarray(onehot & valid[None], f32),
                   precision=lax.Precision.HIGHEST)
    t = jnp.where(jnp.asarray(valid)[None, None], t, NEG_BIG)
    masked = jnp.full((N_HEADS, GRID_W, GRID_W), NEG_BIG, f32)
    zeros = jnp.zeros((N_HEADS, GRID_W, N_META), f32)
    slabs = []
    for dy0 in range(WIN_H):
        for off in range(2):
            tiles = [t[:, dy0 + jr - off] if 0 <= jr - off < WIN_H else masked for jr in range(SLAB_ROWS)]
            slabs.append(jnp.concatenate(tiles + [zeros], axis=-1))
    s = jnp.stack(slabs, axis=1).reshape(N_HEAD_PAIRS, HEADS_PER_VREG, WIN_H, 2, GRID_W, ATT_KEYS)
    s = jnp.transpose(s, (0, 2, 3, 1, 4, 5))
    return s.reshape(N_HEAD_PAIRS, WIN_H, 2, HEADS_PER_VREG * GRID_W, ATT_KEYS)


def _attention_kernel(qe_ref, qo_ref, k_ref, v_ref, km_ref, vm_ref, bias_ref, o_ref, *, rows):
    j = pl.program_id(1)
    tq2 = 2 * GRID_W
    lane = lax.broadcasted_iota(jnp.int32, (tq2, LANES), 1)
    low = lane < HEAD_DIM
    nt = (((1,), (1,)), ((), ()))
    slab_keys = SLAB_ROWS * GRID_W

    def scores(ip, pair):
        r0 = j * ATT_ROWS + 2 * ip
        rs = [jnp.clip(r0 + a - WIN_H // 2, 0, rows - WIN_H) for a in range(2)]
        us = jnp.minimum(rs[0], rows - SLAB_ROWS)
        q0 = pl.multiple_of(ip * tq2, tq2)
        k0 = pl.multiple_of(us * GRID_W, GRID_W)
        cs = slice(pair * LANES, (pair + 1) * LANES)
        qq = jnp.concatenate([qe_ref[pl.ds(q0, tq2), cs], qo_ref[pl.ds(q0, tq2), cs]], axis=0)
        k2 = jnp.concatenate([k_ref[pl.ds(k0, slab_keys), cs], km_ref[:, cs]], axis=0)
        s = lax.dot_general(qq, k2, nt, preferred_element_type=f32)
        b = [bias_ref[pair, rs[a] - (r0 + a) + (WIN_H - 1), rs[a] - us] for a in range(2)]
        s = s + jnp.concatenate([b[0][:GRID_W], b[1][:GRID_W], b[0][GRID_W:], b[1][GRID_W:]], axis=0)
        return s, jnp.max(s, axis=-1, keepdims=True), k0, q0, cs

    def weights(state):
        s, m, k0, q0, cs = state
        v2 = jnp.concatenate([v_ref[pl.ds(k0, slab_keys), cs], vm_ref[:, cs]], axis=0)
        e = jnp.exp2(s - m)
        l = jnp.sum(e, axis=-1, keepdims=True)
        return jnp.dot(e.astype(bf16), v2, preferred_element_type=f32), l, q0, cs

    def finish(state):
        o2, l, q0, cs = state
        o2 = o2 / l
        o = jnp.where(low, o2[:tq2], o2[tq2:])
        o_ref[pl.ds(q0, tq2), cs] = o.astype(bf16)

    def group_body(g, carry):
        items = [(g * ATT_GROUP + ip, pair) for ip in range(ATT_GROUP) for pair in range(N_HEAD_PAIRS)]
        a, b = {}, {}
        for step in range(len(items) + 2):
            if step < len(items):
                a[step] = scores(*items[step])
            if 0 <= step - 1 < len(items):
                b[step - 1] = weights(a.pop(step - 1))
            if 0 <= step - 2 < len(items):
                finish(b.pop(step - 2))
        return carry

    lax.fori_loop(0, ATT_ROWS // (2 * ATT_GROUP), group_body, 0)


def _attention(qkv, qkv_meta, slabs, bsz, t):
    rows = t // GRID_W
    assert rows >= 2 * WIN_H and rows % ATT_ROWS == 0 and ATT_ROWS % (2 * ATT_GROUP) == 0
    steps = rows // ATT_ROWS
    tq = ATT_ROWS * GRID_W
    return pl.pallas_call(
        functools.partial(_attention_kernel, rows=rows),
        grid=(bsz, steps),
        in_specs=[pl.BlockSpec((tq, D_ATT), lambda b, j: (b * steps + j, 0)),
                  pl.BlockSpec((tq, D_ATT), lambda b, j: (b * steps + j, 1)),
                  pl.BlockSpec((t, D_ATT), lambda b, j: (b, 2)),
                  pl.BlockSpec((t, D_ATT), lambda b, j: (b, 3)),
                  pl.BlockSpec((N_META, D_ATT), lambda b, j: (0, 2)),
                  pl.BlockSpec((N_META, D_ATT), lambda b, j: (0, 3)),
                  _full(slabs)],
        out_specs=pl.BlockSpec((tq, D_ATT), lambda b, j: (b * steps + j, 0)),
        out_shape=jax.ShapeDtypeStruct((bsz * t, D_ATT), bf16),
        compiler_params=_cparams(2),
        name="attention",
    )(qkv, qkv, qkv, qkv, qkv_meta, qkv_meta, slabs)


def _layer_norm(z, g, b):
    mu = jnp.mean(z, axis=-1, keepdims=True)
    d = z - mu
    var = jnp.mean(d * d, axis=-1, keepdims=True)
    return d * lax.rsqrt(var + LN_EPS) * g + b


def _sigmoid(x):
    return 0.5 * jnp.tanh(0.5 * x) + 0.5


def _route_tile(logits, before_ref):
    tm = logits.shape[0]
    lt = logits.T
    el = lt[:N_EXPERTS]
    gl = lt[N_EXPERTS:N_EXPERTS + N_GROUPS]
    neg = -jnp.inf
    erow = lax.broadcasted_iota(jnp.int32, el.shape, 0).astype(f32)
    grow = lax.broadcasted_iota(jnp.int32, gl.shape, 0).astype(f32)
    first = lambda hit, idx, n: jnp.min(jnp.where(hit, idx, float(n)), axis=0, keepdims=True)

    gmax = jnp.max(gl, axis=0, keepdims=True)
    grp = first(gl == gmax, grow, N_GROUPS)
    pg_sel = 1.0 / jnp.sum(jnp.exp(gl - gmax), axis=0, keepdims=True)

    e_lo = grp * EXPERTS_PER_GROUP
    elm = jnp.where((erow >= e_lo) & (erow < e_lo + EXPERTS_PER_GROUP), el, neg)
    t1 = jnp.max(elm, axis=0, keepdims=True)
    e1 = first(elm == t1, erow, N_EXPERTS)
    el2 = jnp.where(erow == e1, neg, elm)
    t2 = jnp.max(el2, axis=0, keepdims=True)
    e2 = first(el2 == t2, erow, N_EXPERTS)
    r = jnp.exp(t2 - t1)
    g1 = pg_sel / (1.0 + r)
    g2 = pg_sel * r / (1.0 + r)

    hit1 = erow == e1
    hit2 = erow == e2
    onehot = jnp.where(hit1 | hit2, 1.0, 0.0)
    before = jnp.dot(onehot.astype(bf16), before_ref[...], preferred_element_type=f32)
    rank1 = jnp.sum(jnp.where(hit1, before, 0.0), axis=0, keepdims=True)
    rank2 = jnp.sum(jnp.where(hit2, before, 0.0), axis=0, keepdims=True)
    rrow = lax.broadcasted_iota(jnp.int32, (ROUTE_ROWS, tm), 0)
    route = jnp.zeros((ROUTE_ROWS, tm), f32)
    for k, val in enumerate((e1, e2, g1, g2, rank1, rank2)):
        route = jnp.where(rrow == k, val, route)
    return route, jnp.sum(onehot, axis=1, keepdims=True)


def _mixer_tail_kernel(x_ref, att_ref, rest_ref, prevc_ref, prevx_ref, nextc_ref, nextx_ref, meta_ref,
                       convw_ref, convb_ref, wap_ref, wcp_ref, wout_ref, g_ref, b_ref, wr_ref, before_ref,
                       h_ref, hp_ref, route_ref, cnt_ref, merged_scr, *, tiles_per_seq, n_tiles):
    i = pl.program_id(0)
    pos = jnp.minimum(i, n_tiles - 1) % tiles_per_seq
    tm = x_ref.shape[0]
    c2, c0, c1, c3, c4 = (k * D_MODEL for k in range(5))
    cols = lambda j, base=0: slice(base + j * TAIL_CW, base + (j + 1) * TAIL_CW)
    chunks = range(D_MODEL // TAIL_CW)

    @pl.when(i == 0)
    def _():
        merged_scr[...] = jnp.zeros_like(merged_scr)

    def u_of(ref, rws, j):
        return ref[rws, cols(j, c0)].astype(f32) * ref[rws, cols(j, c1)].astype(f32)

    def halo_u(cref, xref, r, j):
        return cref[r:r + 1, cols(j)].astype(f32) * xref[r:r + 1, cols(j)].astype(f32)

    last = prevc_ref.shape[0] - 1
    row = lax.broadcasted_iota(jnp.int32, (tm, TAIL_CW), 0)

    def conv_chunk(j):
        u = u_of(rest_ref, slice(0, tm), j)
        u_prev = jnp.where(pos == 0, u_of(meta_ref, slice(N_META - 1, N_META), j),
                           halo_u(prevc_ref, prevx_ref, last, j))
        u_next = halo_u(nextc_ref, nextx_ref, 0, j)
        u_next = jnp.where(pos == tiles_per_seq - 1, jnp.zeros_like(u_next), u_next)
        u_m1 = jnp.where(row == 0, u_prev, pltpu.roll(u, 1, axis=0))
        u_p1 = jnp.where(row == tm - 1, u_next, pltpu.roll(u, tm - 1, axis=0))
        cw = convw_ref[:, cols(j)]
        s = u_m1 * cw[0:1] + u * cw[1:2] + u_p1 * cw[2:3] + convb_ref[:, cols(j)]
        return (rest_ref[:, cols(j, c2)].astype(f32) * s).astype(bf16)

    mix = [jnp.dot(merged_scr[...], wout_ref[:, cols(j)], preferred_element_type=f32) for j in chunks]

    cv = None
    for j in chunks:
        part = jnp.dot(conv_chunk(j), wcp_ref[cols(j), :], preferred_element_type=f32)
        cv = part if cv is None else cv + part
    ap = jnp.dot(att_ref[...], wap_ref[...], preferred_element_type=f32)

    z = [ALPHA * x_ref[:, cols(j)] + mix[j] for j in chunks]
    mu = sum(jnp.sum(zj, axis=-1, keepdims=True) for zj in z) * (1.0 / D_MODEL)
    d = [zj - mu for zj in z]
    var = sum(jnp.sum(dj * dj, axis=-1, keepdims=True) for dj in d) * (1.0 / D_MODEL)
    rstd = lax.rsqrt(var + LN_EPS)
    logits = None
    hs = []
    for j in chunks:
        hj = d[j] * rstd * g_ref[:, cols(j)] + b_ref[:, cols(j)]
        h_ref[:, cols(j)] = hj
        hs.append(hj)
        part = jnp.dot(hj.astype(bf16), wr_ref[cols(j), :], preferred_element_type=f32)
        logits = part if logits is None else logits + part
    hp_ref[...] = _pack_halves(jnp.concatenate(hs, axis=1))

    for j in chunks:
        gate_a = jnp.tanh(rest_ref[:, cols(j, c3)]) + 1.0
        gate_c = jnp.tanh(rest_ref[:, cols(j, c4)]) + 1.0
        merged_scr[:, cols(j)] = gate_a * ap[:, cols(j)].astype(bf16) + gate_c * cv[:, cols(j)].astype(bf16)

    route, count = _route_tile(logits, before_ref)
    route_ref[...] = route
    cnt_ref[0] = jnp.broadcast_to(count, (N_EXPERTS, LANES))


def _mixer_tail(x, att, rest, rest_meta, p, t):
    n = x.shape[0]
    tm = TAIL_TM
    halo = 16
    hb = tm // halo
    n_halo = n // halo
    n_tiles = n // tm
    tiles_per_seq = t // tm
    before = jnp.asarray(np.triu(np.ones((tm, tm), np.float32), 1), dtype=bf16)
    consts = (rest_meta, p["conv_w"], p["conv_b"], p["w_att_proj"], p["w_conv_proj"], p["w_out"],
              p["ln1_g"], p["ln1_b"], p["w_router"], before)
    front = lambda i: jnp.minimum(i, n_tiles - 1)
    back = lambda i: jnp.maximum(i - 1, 0)
    prev_blk = lambda i: jnp.maximum(front(i) * hb - 1, 0)
    next_blk = lambda i: jnp.minimum((front(i) + 1) * hb, n_halo - 1)
    return pl.pallas_call(
        functools.partial(_mixer_tail_kernel, tiles_per_seq=tiles_per_seq, n_tiles=n_tiles),
        grid=(n_tiles + 1,),
        in_specs=[pl.BlockSpec((tm, D_MODEL), lambda i: (back(i), 0)),
                  pl.BlockSpec((tm, D_ATT), lambda i: (front(i), 0)),
                  pl.BlockSpec((tm, D_REST), lambda i: (front(i), 0)),
                  pl.BlockSpec((halo, D_CONV), lambda i: (prev_blk(i), 1)),
                  pl.BlockSpec((halo, D_CONV), lambda i: (prev_blk(i), 2)),
                  pl.BlockSpec((halo, D_CONV), lambda i: (next_blk(i), 1)),
                  pl.BlockSpec((halo, D_CONV), lambda i: (next_blk(i), 2))]
                 + [_full(c) for c in consts],
        out_specs=[pl.BlockSpec((tm, D_MODEL), lambda i: (back(i), 0)),
                   pl.BlockSpec((tm, D_PACKED), lambda i: (back(i), 0)),
                   pl.BlockSpec((ROUTE_ROWS, tm), lambda i: (0, back(i))),
                   pl.BlockSpec((1, N_EXPERTS, LANES), lambda i: (back(i), 0, 0))],
        out_shape=[jax.ShapeDtypeStruct((n, D_MODEL), f32),
                   jax.ShapeDtypeStruct((n, D_PACKED), jnp.uint32),
                   jax.ShapeDtypeStruct((ROUTE_ROWS, n), f32),
                   jax.ShapeDtypeStruct((n_tiles, N_EXPERTS, LANES), f32)],
        scratch_shapes=[pltpu.VMEM((tm, D_MODEL), bf16)],
        compiler_params=_cparams(1),
        name="mixer_tail",
    )(x, att, rest, rest, rest, rest, rest, *consts)


def _route(route, cnt, n, blk, min_blocks):
    n_tiles = n // TAIL_TM
    tile_cnt = cnt[:, :, 0].astype(jnp.int32)
    tile_off = jnp.cumsum(tile_cnt, axis=0) - tile_cnt
    counts = jnp.sum(tile_cnt, axis=0)
    padded = jnp.maximum((counts + blk - 1) // blk, min_blocks) * blk
    pend = jnp.cumsum(padded)
    pstart = pend - padded
    base = pstart[None, :] + tile_off
    lanes = jnp.arange(N_EXPERTS, dtype=jnp.int32)

    def rows_of(expert_row, rank_row):
        e = expert_row.astype(jnp.int32).reshape(n_tiles, TAIL_TM, 1)
        sel = jnp.sum(jnp.where(e == lanes, base[:, None, :], 0), axis=-1)
        return sel.reshape(n) + rank_row.astype(jnp.int32)

    dest0 = rows_of(route[0], route[4])
    dest1 = rows_of(route[1], route[5])

    n_blocks = (2 * n + N_EXPERTS * (blk - 1) + blk - 1) // blk + N_EXPERTS * min_blocks
    blk_start = jnp.arange(n_blocks, dtype=jnp.int32) * blk
    blk_expert = jnp.sum((pend[None, :] <= blk_start[:, None]).astype(jnp.int32), axis=1)
    blk_expert = jnp.minimum(blk_expert, N_EXPERTS - 1)
    blk_valid = jnp.clip((pstart + counts)[blk_expert] - blk_start, 0, blk)
    blk_valid = jnp.where(blk_start < pend[-1], blk_valid, 0).astype(jnp.int32)
    return dest0, dest1, blk_expert, blk_valid, n_blocks


def _sc_mesh():
    return plsc.VectorSubcoreMesh(core_axis_name="c", subcore_axis_name="s")


def _sc_dispatch(h, dest0, dest1, p_rows):
    n, d = h.shape
    per = n // SC_WORKERS
    assert n % (SC_WORKERS * SC_IDX_WIN) == 0

    @pl.kernel(out_type=jax.ShapeDtypeStruct((p_rows, d), h.dtype), mesh=_sc_mesh(),
               scratch_types=[pltpu.VMEM((2, SC_IDX_WIN), jnp.int32), pltpu.VMEM((SC_ROWS, d), h.dtype)])
    def k(h_hbm, d0_hbm, d1_hbm, xs_hbm, idx, buf):
        base = (lax.axis_index("c") * SC_SUBCORES + lax.axis_index("s")) * per

        @pl.loop(0, per // SC_IDX_WIN)
        def _(w):
            off = base + w * SC_IDX_WIN
            pltpu.sync_copy(d0_hbm.at[pl.ds(off, SC_IDX_WIN)], idx.at[0])
            pltpu.sync_copy(d1_hbm.at[pl.ds(off, SC_IDX_WIN)], idx.at[1])
            for r in range(SC_IDX_WIN // SC_ROWS):
                pltpu.sync_copy(h_hbm.at[pl.ds(off + r * SC_ROWS, SC_ROWS)], buf)
                pltpu.sync_copy(buf, xs_hbm.at[idx.at[0, pl.ds(r * SC_ROWS, SC_ROWS)]])
                pltpu.sync_copy(buf, xs_hbm.at[idx.at[1, pl.ds(r * SC_ROWS, SC_ROWS)]])

    return k(h, dest0, dest1)


def _sc_gather2(ys, dest0, dest1):
    n = dest0.shape[0]
    d = ys.shape[1]
    per = n // SC_WORKERS
    out = jax.ShapeDtypeStruct((n, d), ys.dtype)

    @pl.kernel(out_type=[out, out], mesh=_sc_mesh(),
               scratch_types=[pltpu.VMEM((2, SC_IDX_WIN), jnp.int32), pltpu.VMEM((SC_ROWS, d), ys.dtype)])
    def k(ys_hbm, d0_hbm, d1_hbm, y0_hbm, y1_hbm, idx, buf):
        base = (lax.axis_index("c") * SC_SUBCORES + lax.axis_index("s")) * per

        @pl.loop(0, per // SC_IDX_WIN)
        def _(w):
            off = base + w * SC_IDX_WIN
            pltpu.sync_copy(d0_hbm.at[pl.ds(off, SC_IDX_WIN)], idx.at[0])
            pltpu.sync_copy(d1_hbm.at[pl.ds(off, SC_IDX_WIN)], idx.at[1])
            for r in range(SC_IDX_WIN // SC_ROWS):
                for kk, y_hbm in enumerate((y0_hbm, y1_hbm)):
                    pltpu.sync_copy(ys_hbm.at[idx.at[kk, pl.ds(r * SC_ROWS, SC_ROWS)]], buf)
                    pltpu.sync_copy(buf, y_hbm.at[pl.ds(off + r * SC_ROWS, SC_ROWS)])

    return k(ys, dest0, dest1)


def _experts_kernel(be_ref, bv_ref, xs_ref, wg_ref, wu_ref, wd_ref, ys_ref, *bf_refs):
    i = pl.program_id(0)
    valid = bv_ref[i]
    weights = bf_refs if bf_refs else (wg_ref, wu_ref, wd_ref)

    if bf_refs:
        @pl.when(jnp.logical_or(i == 0, be_ref[i] != be_ref[jnp.maximum(i - 1, 0)]))
        def _():
            for dst, src in zip(bf_refs, (wg_ref, wu_ref, wd_ref)):
                dst[0] = src[0].astype(bf16)

    @pl.when(valid > 0)
    def _():
        wg, wu, wd = (w[0] for w in weights)
        row = lax.broadcasted_iota(jnp.int32, xs_ref.shape, 0)
        keep = row < valid
        lo, hi = _unpack_halves(xs_ref[...])
        x = jnp.concatenate([jnp.where(keep, lo, 0.0), jnp.where(keep, hi, 0.0)], axis=1).astype(bf16)
        g = jnp.dot(x, wg, preferred_element_type=f32)
        u = jnp.dot(x, wu, preferred_element_type=f32)
        hmid = (g * _sigmoid(g) * u).astype(bf16)
        ys_ref[...] = _pack_halves(jnp.dot(hmid, wd, preferred_element_type=f32))

    @pl.when(valid == 0)
    def _():
        ys_ref[...] = jnp.zeros_like(ys_ref)


def _experts(xs, blk_expert, blk_valid, wg, wu, wd, n_blocks, blk):
    emit = wg.dtype != bf16
    w_spec = lambda a: pl.BlockSpec((1,) + a.shape[1:], lambda i, be, bv: (be[i], 0, 0))
    ys_spec = pl.BlockSpec((blk, D_PACKED), lambda i, be, bv: (i, 0))
    ys_shape = jax.ShapeDtypeStruct(xs.shape, jnp.uint32)
    grid_spec = pltpu.PrefetchScalarGridSpec(
        num_scalar_prefetch=2,
        grid=(n_blocks,),
        in_specs=[pl.BlockSpec((blk, D_PACKED), lambda i, be, bv: (i, 0)), w_spec(wg), w_spec(wu), w_spec(wd)],
        out_specs=[ys_spec] + ([w_spec(wg), w_spec(wu), w_spec(wd)] if emit else []),
    )
    out = pl.pallas_call(
        _experts_kernel,
        grid_spec=grid_spec,
        out_shape=[ys_shape] + ([jax.ShapeDtypeStruct(w.shape, bf16) for w in (wg, wu, wd)] if emit else []),
        compiler_params=_cparams(1),
        name="experts",
    )(blk_expert, blk_valid, xs, wg, wu, wd)
    return out[0], tuple(out[1:])


def _final_norm_kernel(h_ref, y0_ref, y1_ref, route_ref, g_ref, b_ref, o_ref):
    route_t = route_ref[...].T
    g0, g1 = route_t[:, 2:3], route_t[:, 3:4]
    lo0, hi0 = _unpack_halves(y0_ref[...])
    lo1, hi1 = _unpack_halves(y1_ref[...])
    ffn = jnp.concatenate([lo0 * g0 + lo1 * g1, hi0 * g0 + hi1 * g1], axis=1)
    o_ref[...] = _layer_norm(ALPHA * h_ref[...] + ffn, g_ref[...], b_ref[...])


def _final_norm(h, y0, y1, route, ln_g, ln_b):
    n = h.shape[0]
    ts = NORM_TS
    tile = pl.BlockSpec((ts, D_MODEL), lambda i: (i, 0))
    packed = pl.BlockSpec((ts, D_PACKED), lambda i: (i, 0))
    return pl.pallas_call(
        _final_norm_kernel,
        grid=(n // ts,),
        in_specs=[tile, packed, packed, pl.BlockSpec((ROUTE_ROWS, ts), lambda i: (0, i)),
                  _full(ln_g), _full(ln_b)],
        out_specs=tile,
        out_shape=jax.ShapeDtypeStruct((n, D_MODEL), f32),
        compiler_params=_cparams(1),
        name="final_norm",
    )(h, y0, y1, route, ln_g, ln_b)


def _encode(x, p, expert_w, qkv_meta, rest_meta, slabs):
    bsz, t, _ = x.shape
    n = bsz * t
    assert t % NORM_TS == 0 and t % TAIL_TM == 0 and t % PROJ_TM == 0
    xf = x.reshape(n, D_MODEL)
    qkv, rest = _in_proj(xf, p["w_in"], PROJ_TM)
    att = _attention(qkv, qkv_meta, slabs, bsz, t)
    h1, h1_packed, route, cnt = _mixer_tail(xf, att, rest, rest_meta, p, t)
    blk = MOE_BLK
    emit = expert_w[0].dtype != bf16
    dest0, dest1, blk_expert, blk_valid, n_blocks = _route(route, cnt, n, blk, 1 if emit else 0)
    xs = _sc_dispatch(h1_packed, dest0, dest1, n_blocks * blk)
    ys, w_bf = _experts(xs, blk_expert, blk_valid, *expert_w, n_blocks, blk)
    y0, y1 = _sc_gather2(ys, dest0, dest1)
    y = _final_norm(h1, y0, y1, route, p["ln2_g"], p["ln2_b"])
    return y.reshape(bsz, t, D_MODEL), (w_bf if emit else expert_w)


def kernel(x_prompt, x_sample, meta_tokens, w_in, rpb, conv_w, conv_b, w_att_proj, w_conv_proj, w_out,
           ln1_g, ln1_b, w_router_group, w_router_expert, w_e_gate, w_e_up, w_e_down, ln2_g, ln2_b):
    col_scale = np.ones((1, D_IN_PROJ), np.float32)
    col_scale[:, D_IN_PROJ - 2 * D_MODEL:] = 0.5
    w_in_bf = (w_in[0] * col_scale).astype(bf16)
    w_router = jnp.concatenate([w_router_expert[0], w_router_group[0]], axis=1)
    w_router = jnp.pad(w_router, ((0, 0), (0, LANES - w_router.shape[1])))
    row = lambda v: v[0].reshape(1, -1).astype(f32)
    p = {
        "w_in": w_in_bf,
        "conv_w": conv_w[0].astype(f32), "conv_b": row(conv_b),
        "w_att_proj": w_att_proj[0].astype(bf16), "w_conv_proj": w_conv_proj[0].astype(bf16),
        "w_out": (0.5 * w_out[0]).astype(bf16), "ln1_g": row(ln1_g), "ln1_b": row(ln1_b),
        "w_router": w_router.astype(bf16),
        "ln2_g": row(ln2_g), "ln2_b": row(ln2_b),
    }
    qkv_meta, rest_meta = _in_proj(meta_tokens.astype(f32), p["w_in"], N_META)
    slabs = _bias_slabs(rpb[0])
    expert_w = (w_e_gate[0], w_e_up[0], w_e_down[0])
    y_prompt, expert_w = _encode(x_prompt, p, expert_w, qkv_meta, rest_meta, slabs)
    y_sample, _ = _encode(x_sample, p, expert_w, qkv_meta, rest_meta, slabs)
    return (y_prompt, y_sample)
```

```python
import functools

import numpy as np
import jax
import jax.numpy as jnp
from jax import lax
from jax.experimental import pallas as pl
from jax.experimental.pallas import tpu as pltpu
from jax.experimental.pallas import tpu_sc as plsc

D_MODEL = 1024
N_META = 16
GRID_W = 64
WIN_H = 8
WIN_W = 16
N_HEADS = 8
HEAD_DIM = 64
D_ATT = N_HEADS * HEAD_DIM
D_CONV = D_MODEL
N_GROUPS = 4
EXPERTS_PER_GROUP = 8
N_EXPERTS = N_GROUPS * EXPERTS_PER_GROUP
D_EXPERT = D_MODEL // 2
DEPTH = 1
ALPHA = (2.0 * DEPTH) ** 0.25
LN_EPS = 1e-5
D_QKV = 4 * D_ATT
LOG2E = 1.4426950408889634
D_REST = 3 * D_CONV + 2 * D_MODEL
D_IN_PROJ = 3 * D_ATT + D_REST

LANES = 128
HEADS_PER_VREG = LANES // HEAD_DIM
N_HEAD_PAIRS = N_HEADS // HEADS_PER_VREG
NEG_BIG = -1e30

PROJ_TM = 512
PROJ_TN = 512
ATT_ROWS = 16
ATT_GROUP = 4
SLAB_ROWS = WIN_H + 1
ATT_KEYS = SLAB_ROWS * GRID_W + N_META
TAIL_TM = 512
TAIL_CW = 256
NORM_TS = 1024
MOE_BLK = 512
ROUTE_ROWS = 8
VMEM_LIMIT = 56 * 1024 * 1024

SC_CORES = 2
SC_SUBCORES = 16
SC_WORKERS = SC_CORES * SC_SUBCORES
SC_IDX_WIN = 128
SC_ROWS = 64
D_PACKED = D_MODEL // 2

bf16 = jnp.bfloat16
f32 = jnp.float32


def _cparams(n_axes):
    return pltpu.CompilerParams(dimension_semantics=("arbitrary",) * n_axes,
                                vmem_limit_bytes=VMEM_LIMIT)


def _full(a):
    return pl.BlockSpec(a.shape, lambda *_: (0,) * a.ndim)


def _pack_halves(x):
    half = x.shape[1] // 2
    bits = lambda v: lax.bitcast_convert_type(v.astype(bf16).astype(f32), jnp.uint32)
    return (bits(x[:, :half]) >> 16) | bits(x[:, half:])


def _unpack_halves(p):
    lo = lax.bitcast_convert_type(p << 16, f32)
    hi = lax.bitcast_convert_type(p & jnp.uint32(0xFFFF0000), f32)
    return lo, hi


def _in_proj_kernel(x_ref, w_ref, qkv_ref, rest_ref):
    assert PROJ_TN == D_ATT
    xb = x_ref[...].astype(bf16)
    for c in range(0, D_IN_PROJ, PROJ_TN):
        y = jnp.dot(xb, w_ref[:, c:c + PROJ_TN], preferred_element_type=f32)
        if c == 0:
            y = (y * (HEAD_DIM ** -0.5 * LOG2E)).astype(bf16)
            lane = lax.broadcasted_iota(jnp.int32, y.shape, 1)
            even = (lane & HEAD_DIM) == 0
            zero = jnp.zeros_like(y)
            qkv_ref[:, :D_ATT] = jnp.where(even, y, zero)
            qkv_ref[:, D_ATT:2 * D_ATT] = jnp.where(even, zero, y)
        elif c < 3 * D_ATT:
            qkv_ref[:, c + D_ATT:c + D_ATT + PROJ_TN] = y.astype(bf16)
        else:
            rest_ref[:, c - 3 * D_ATT:c - 3 * D_ATT + PROJ_TN] = y.astype(bf16)


def _in_proj(x, w_bf, tm):
    n = x.shape[0]
    return pl.pallas_call(
        _in_proj_kernel,
        grid=(n // tm,),
        in_specs=[pl.BlockSpec((tm, D_MODEL), lambda i: (i, 0)),
                  pl.BlockSpec((D_MODEL, D_IN_PROJ), lambda i: (0, 0))],
        out_specs=[pl.BlockSpec((tm, D_QKV), lambda i: (i, 0)),
                   pl.BlockSpec((tm, D_REST), lambda i: (i, 0))],
        out_shape=[jax.ShapeDtypeStruct((n, D_QKV), bf16),
                   jax.ShapeDtypeStruct((n, D_REST), bf16)],
        compiler_params=_cparams(1),
        name="in_proj",
    )(x, w_bf)


def _bias_slabs(rpb):
    qc = np.arange(GRID_W)[:, None]
    kc = np.arange(GRID_W)[None, :]
    w_start = np.clip(qc - WIN_W // 2, 0, GRID_W - WIN_W)
    valid = (kc >= w_start) & (kc < w_start + WIN_W)
    n_dx = 2 * WIN_W - 1
    onehot = (kc - qc + WIN_W - 1)[None] == np.arange(n_dx)[:, None, None]
    t = jnp.einsum("hyx,xqk->hyqk", rpb.astype(f32) * LOG2E, jnp.asarray(onehot & valid[None], f32),
                   precision=lax.Precision.HIGHEST)
    t = jnp.where(jnp.asarray(valid)[None, None], t, NEG_BIG)
    masked = jnp.full((N_HEADS, GRID_W, GRID_W), NEG_BIG, f32)
    zeros = jnp.zeros((N_HEADS, GRID_W, N_META), f32)
    slabs = []
    for dy0 in range(WIN_H):
        for off in range(2):
            tiles = [t[:, dy0 + jr - off] if 0 <= jr - off < WIN_H else masked for jr in range(SLAB_ROWS)]
            slabs.append(jnp.concatenate(tiles + [zeros], axis=-1))
    s = jnp.stack(slabs, axis=1).reshape(N_HEAD_PAIRS, HEADS_PER_VREG, WIN_H, 2, GRID_W, ATT_KEYS)
    s = jnp.transpose(s, (0, 2, 3, 1, 4, 5))
    return s.reshape(N_HEAD_PAIRS, WIN_H, 2, HEADS_PER_VREG * GRID_W, ATT_KEYS)


def _attention_kernel(qe_ref, qo_ref, k_ref, v_ref, km_ref, vm_ref, bias_ref, o_ref, *, rows):
    j = pl.program_id(1)
    tq2 = 2 * GRID_W
    lane = lax.broadcasted_iota(jnp.int32, (tq2, LANES), 1)
    low = lane < HEAD_DIM
    nt = (((1,), (1,)), ((), ()))
    slab_keys = SLAB_ROWS * GRID_W

    def scores(ip, pair):
        r0 = j * ATT_ROWS + 2 * ip
        rs = [jnp.clip(r0 + a - WIN_H // 2, 0, rows - WIN_H) for a in range(2)]
        us = jnp.minimum(rs[0], rows - SLAB_ROWS)
        q0 = pl.multiple_of(ip * tq2, tq2)
        k0 = pl.multiple_of(us * GRID_W, GRID_W)
        cs = slice(pair * LANES, (pair + 1) * LANES)
        qq = jnp.concatenate([qe_ref[pl.ds(q0, tq2), cs], qo_ref[pl.ds(q0, tq2), cs]], axis=0)
        k2 = jnp.concatenate([k_ref[pl.ds(k0, slab_keys), cs], km_ref[:, cs]], axis=0)
        s = lax.dot_general(qq, k2, nt, preferred_element_type=f32)
        b = [bias_ref[pair, rs[a] - (r0 + a) + (WIN_H - 1), rs[a] - us] for a in range(2)]
        s = s + jnp.concatenate([b[0][:GRID_W], b[1][:GRID_W], b[0][GRID_W:], b[1][GRID_W:]], axis=0)
        return s, jnp.max(s, axis=-1, keepdims=True), k0, q0, cs

    def weights(state):
        s, m, k0, q0, cs = state
        v2 = jnp.concatenate([v_ref[pl.ds(k0, slab_keys), cs], vm_ref[:, cs]], axis=0)
        e = jnp.exp2(s - m)
        l = jnp.sum(e, axis=-1, keepdims=True)
        return jnp.dot(e.astype(bf16), v2, preferred_element_type=f32), l, q0, cs

    def finish(state):
        o2, l, q0, cs = state
        o2 = o2 / l
        o = jnp.where(low, o2[:tq2], o2[tq2:])
        o_ref[pl.ds(q0, tq2), cs] = o.astype(bf16)

    def group_body(g, carry):
        items = [(g * ATT_GROUP + ip, pair) for ip in range(ATT_GROUP) for pair in range(N_HEAD_PAIRS)]
        a, b = {}, {}
        for step in range(len(items) + 2):
            if step < len(items):
                a[step] = scores(*items[step])
            if 0 <= step - 1 < len(items):
                b[step - 1] = weights(a.pop(step - 1))
            if 0 <= step - 2 < len(items):
                finish(b.pop(step - 2))
        return carry

    lax.fori_loop(0, ATT_ROWS // (2 * ATT_GROUP), group_body, 0)


def _attention(qkv, qkv_meta, slabs, bsz, t):
    rows = t // GRID_W
    assert rows >= 2 * WIN_H and rows % ATT_ROWS == 0 and ATT_ROWS % (2 * ATT_GROUP) == 0
    steps = rows // ATT_ROWS
    tq = ATT_ROWS * GRID_W
    return pl.pallas_call(
        functools.partial(_attention_kernel, rows=rows),
        grid=(bsz, steps),
        in_specs=[pl.BlockSpec((tq, D_ATT), lambda b, j: (b * steps + j, 0)),
                  pl.BlockSpec((tq, D_ATT), lambda b, j: (b * steps + j, 1)),
                  pl.BlockSpec((t, D_ATT), lambda b, j: (b, 2)),
                  pl.BlockSpec((t, D_ATT), lambda b, j: (b, 3)),
                  pl.BlockSpec((N_META, D_ATT), lambda b, j: (0, 2)),
                  pl.BlockSpec((N_META, D_ATT), lambda b, j: (0, 3)),
                  _full(slabs)],
        out_specs=pl.BlockSpec((tq, D_ATT), lambda b, j: (b * steps + j, 0)),
        out_shape=jax.ShapeDtypeStruct((bsz * t, D_ATT), bf16),
        compiler_params=_cparams(2),
        name="attention",
    )(qkv, qkv, qkv, qkv, qkv_meta, qkv_meta, slabs)


def _layer_norm(z, g, b):
    mu = jnp.mean(z, axis=-1, keepdims=True)
    d = z - mu
    var = jnp.mean(d * d, axis=-1, keepdims=True)
    return d * lax.rsqrt(var + LN_EPS) * g + b


def _sigmoid(x):
    return 0.5 * jnp.tanh(0.5 * x) + 0.5


def _route_tile(logits, before_ref):
    tm = logits.shape[0]
    lt = logits.T
    el = lt[:N_EXPERTS]
    gl = lt[N_EXPERTS:N_EXPERTS + N_GROUPS]
    neg = -jnp.inf
    erow = lax.broadcasted_iota(jnp.int32, el.shape, 0).astype(f32)
    grow = lax.broadcasted_iota(jnp.int32, gl.shape, 0).astype(f32)
    first = lambda hit, idx, n: jnp.min(jnp.where(hit, idx, float(n)), axis=0, keepdims=True)

    gmax = jnp.max(gl, axis=0, keepdims=True)
    grp = first(gl == gmax, grow, N_GROUPS)
    pg_sel = 1.0 / jnp.sum(jnp.exp(gl - gmax), axis=0, keepdims=True)

    e_lo = grp * EXPERTS_PER_GROUP
    elm = jnp.where((erow >= e_lo) & (erow < e_lo + EXPERTS_PER_GROUP), el, neg)
    t1 = jnp.max(elm, axis=0, keepdims=True)
    e1 = first(elm == t1, erow, N_EXPERTS)
    el2 = jnp.where(erow == e1, neg, elm)
    t2 = jnp.max(el2, axis=0, keepdims=True)
    e2 = first(el2 == t2, erow, N_EXPERTS)
    r = jnp.exp(t2 - t1)
    g1 = pg_sel / (1.0 + r)
    g2 = pg_sel * r / (1.0 + r)

    hit1 = erow == e1
    hit2 = erow == e2
    onehot = jnp.where(hit1 | hit2, 1.0, 0.0)
    before = jnp.dot(onehot.astype(bf16), before_ref[...], preferred_element_type=f32)
    rank1 = jnp.sum(jnp.where(hit1, before, 0.0), axis=0, keepdims=True)
    rank2 = jnp.sum(jnp.where(hit2, before, 0.0), axis=0, keepdims=True)
    rrow = lax.broadcasted_iota(jnp.int32, (ROUTE_ROWS, tm), 0)
    route = jnp.zeros((ROUTE_ROWS, tm), f32)
    for k, val in enumerate((e1, e2, g1, g2, rank1, rank2)):
        route = jnp.where(rrow == k, val, route)
    return route, jnp.sum(onehot, axis=1, keepdims=True)


def _mixer_tail_kernel(x_ref, att_ref, rest_ref, prevc_ref, prevx_ref, nextc_ref, nextx_ref, meta_ref,
                       convw_ref, convb_ref, wap_ref, wcp_ref, wout_ref, g_ref, b_ref, wr_ref, before_ref,
                       h_ref, hp_ref, route_ref, cnt_ref, merged_scr, *, tiles_per_seq, n_tiles):
    i = pl.program_id(0)
    pos = jnp.minimum(i, n_tiles - 1) % tiles_per_seq
    tm = x_ref.shape[0]
    c2, c0, c1, c3, c4 = (k * D_MODEL for k in range(5))
    cols = lambda j, base=0: slice(base + j * TAIL_CW, base + (j + 1) * TAIL_CW)
    chunks = range(D_MODEL // TAIL_CW)

    @pl.when(i == 0)
    def _():
        merged_scr[...] = jnp.zeros_like(merged_scr)

    def u_of(ref, rws, j):
        return ref[rws, cols(j, c0)].astype(f32) * ref[rws, cols(j, c1)].astype(f32)

    def halo_u(cref, xref, r, j):
        return cref[r:r + 1, cols(j)].astype(f32) * xref[r:r + 1, cols(j)].astype(f32)

    last = prevc_ref.shape[0] - 1
    row = lax.broadcasted_iota(jnp.int32, (tm, TAIL_CW), 0)

    def conv_chunk(j):
        u = u_of(rest_ref, slice(0, tm), j)
        u_prev = jnp.where(pos == 0, u_of(meta_ref, slice(N_META - 1, N_META), j),
                           halo_u(prevc_ref, prevx_ref, last, j))
        u_next = halo_u(nextc_ref, nextx_ref, 0, j)
        u_next = jnp.where(pos == tiles_per_seq - 1, jnp.zeros_like(u_next), u_next)
        u_m1 = jnp.where(row == 0, u_prev, pltpu.roll(u, 1, axis=0))
        u_p1 = jnp.where(row == tm - 1, u_next, pltpu.roll(u, tm - 1, axis=0))
        cw = convw_ref[:, cols(j)]
        s = u_m1 * cw[0:1] + u * cw[1:2] + u_p1 * cw[2:3] + convb_ref[:, cols(j)]
        return (rest_ref[:, cols(j, c2)].astype(f32) * s).astype(bf16)

    mix = [jnp.dot(merged_scr[...], wout_ref[:, cols(j)], preferred_element_type=f32) for j in chunks]

    cv = None
    for j in chunks:
        part = jnp.dot(conv_chunk(j), wcp_ref[cols(j), :], preferred_element_type=f32)
        cv = part if cv is None else cv + part
    ap = jnp.dot(att_ref[...], wap_ref[...], preferred_element_type=f32)

    z = [ALPHA * x_ref[:, cols(j)] + mix[j] for j in chunks]
    mu = sum(jnp.sum(zj, axis=-1, keepdims=True) for zj in z) * (1.0 / D_MODEL)
    d = [zj - mu for zj in z]
    var = sum(jnp.sum(dj * dj, axis=-1, keepdims=True) for dj in d) * (1.0 / D_MODEL)
    rstd = lax.rsqrt(var + LN_EPS)
    logits = None
    hs = []
    for j in chunks:
        hj = d[j] * rstd * g_ref[:, cols(j)] + b_ref[:, cols(j)]
        h_ref[:, cols(j)] = hj
        hs.append(hj)
        part = jnp.dot(hj.astype(bf16), wr_ref[cols(j), :], preferred_element_type=f32)
        logits = part if logits is None else logits + part
    hp_ref[...] = _pack_halves(jnp.concatenate(hs, axis=1))

    for j in chunks:
        gate_a = jnp.tanh(rest_ref[:, cols(j, c3)]) + 1.0
        gate_c = jnp.tanh(rest_ref[:, cols(j, c4)]) + 1.0
        merged_scr[:, cols(j)] = gate_a * ap[:, cols(j)].astype(bf16) + gate_c * cv[:, cols(j)].astype(bf16)

    route, count = _route_tile(logits, before_ref)
    route_ref[...] = route
    cnt_ref[0] = jnp.broadcast_to(count, (N_EXPERTS, LANES))


def _mixer_tail(x, att, rest, rest_meta, p, t):
    n = x.shape[0]
    tm = TAIL_TM
    halo = 16
    hb = tm // halo
    n_halo = n // halo
    n_tiles = n // tm
    tiles_per_seq = t // tm
    before = jnp.asarray(np.triu(np.ones((tm, tm), np.float32), 1), dtype=bf16)
    consts = (rest_meta, p["conv_w"], p["conv_b"], p["w_att_proj"], p["w_conv_proj"], p["w_out"],
              p["ln1_g"], p["ln1_b"], p["w_router"], before)
    front = lambda i: jnp.minimum(i, n_tiles - 1)
    back = lambda i: jnp.maximum(i - 1, 0)
    prev_blk = lambda i: jnp.maximum(front(i) * hb - 1, 0)
    next_blk = lambda i: jnp.minimum((front(i) + 1) * hb, n_halo - 1)
    return pl.pallas_call(
        functools.partial(_mixer_tail_kernel, tiles_per_seq=tiles_per_seq, n_tiles=n_tiles),
        grid=(n_tiles + 1,),
        in_specs=[pl.BlockSpec((tm, D_MODEL), lambda i: (back(i), 0)),
                  pl.BlockSpec((tm, D_ATT), lambda i: (front(i), 0)),
                  pl.BlockSpec((tm, D_REST), lambda i: (front(i), 0)),
                  pl.BlockSpec((halo, D_CONV), lambda i: (prev_blk(i), 1)),
                  pl.BlockSpec((halo, D_CONV), lambda i: (prev_blk(i), 2)),
                  pl.BlockSpec((halo, D_CONV), lambda i: (next_blk(i), 1)),
                  pl.BlockSpec((halo, D_CONV), lambda i: (next_blk(i), 2))]
                 + [_full(c) for c in consts],
        out_specs=[pl.BlockSpec((tm, D_MODEL), lambda i: (back(i), 0)),
                   pl.BlockSpec((tm, D_PACKED), lambda i: (back(i), 0)),
                   pl.BlockSpec((ROUTE_ROWS, tm), lambda i: (0, back(i))),
                   pl.BlockSpec((1, N_EXPERTS, LANES), lambda i: (back(i), 0, 0))],
        out_shape=[jax.ShapeDtypeStruct((n, D_MODEL), f32),
                   jax.ShapeDtypeStruct((n, D_PACKED), jnp.uint32),
                   jax.ShapeDtypeStruct((ROUTE_ROWS, n), f32),
                   jax.ShapeDtypeStruct((n_tiles, N_EXPERTS, LANES), f32)],
        scratch_shapes=[pltpu.VMEM((tm, D_MODEL), bf16)],
        compiler_params=_cparams(1),
        name="mixer_tail",
    )(x, att, rest, rest, rest, rest, rest, *consts)


def _route(route, cnt, n, blk):
    n_tiles = n // TAIL_TM
    tile_cnt = cnt[:, :, 0].astype(jnp.int32)
    tile_off = jnp.cumsum(tile_cnt, axis=0) - tile_cnt
    counts = jnp.sum(tile_cnt, axis=0)
    padded = (counts + blk - 1) // blk * blk
    pend = jnp.cumsum(padded)
    pstart = pend - padded
    base = pstart[None, :] + tile_off
    lanes = jnp.arange(N_EXPERTS, dtype=jnp.int32)

    def rows_of(expert_row, rank_row):
        e = expert_row.astype(jnp.int32).reshape(n_tiles, TAIL_TM, 1)
        sel = jnp.sum(jnp.where(e == lanes, base[:, None, :], 0), axis=-1)
        return sel.reshape(n) + rank_row.astype(jnp.int32)

    dest0 = rows_of(route[0], route[4])
    dest1 = rows_of(route[1], route[5])

    n_blocks = (2 * n + N_EXPERTS * (blk - 1) + blk - 1) // blk
    blk_start = jnp.arange(n_blocks, dtype=jnp.int32) * blk
    blk_expert = jnp.sum((pend[None, :] <= blk_start[:, None]).astype(jnp.int32), axis=1)
    blk_expert = jnp.minimum(blk_expert, N_EXPERTS - 1)
    blk_valid = jnp.clip((pstart + counts)[blk_expert] - blk_start, 0, blk)
    blk_valid = jnp.where(blk_start < pend[-1], blk_valid, 0).astype(jnp.int32)
    return dest0, dest1, blk_expert, blk_valid, n_blocks


def _sc_mesh():
    return plsc.VectorSubcoreMesh(core_axis_name="c", subcore_axis_name="s")


def _sc_dispatch(h, dest0, dest1, p_rows):
    n, d = h.shape
    per = n // SC_WORKERS
    assert n % (SC_WORKERS * SC_IDX_WIN) == 0

    @pl.kernel(out_type=jax.ShapeDtypeStruct((p_rows, d), h.dtype), mesh=_sc_mesh(),
               scratch_types=[pltpu.VMEM((2, SC_IDX_WIN), jnp.int32), pltpu.VMEM((SC_ROWS, d), h.dtype)])
    def k(h_hbm, d0_hbm, d1_hbm, xs_hbm, idx, buf):
        base = (lax.axis_index("c") * SC_SUBCORES + lax.axis_index("s")) * per

        @pl.loop(0, per // SC_IDX_WIN)
        def _(w):
            off = base + w * SC_IDX_WIN
            pltpu.sync_copy(d0_hbm.at[pl.ds(off, SC_IDX_WIN)], idx.at[0])
            pltpu.sync_copy(d1_hbm.at[pl.ds(off, SC_IDX_WIN)], idx.at[1])
            for r in range(SC_IDX_WIN // SC_ROWS):
                pltpu.sync_copy(h_hbm.at[pl.ds(off + r * SC_ROWS, SC_ROWS)], buf)
                pltpu.sync_copy(buf, xs_hbm.at[idx.at[0, pl.ds(r * SC_ROWS, SC_ROWS)]])
                pltpu.sync_copy(buf, xs_hbm.at[idx.at[1, pl.ds(r * SC_ROWS, SC_ROWS)]])

    return k(h, dest0, dest1)


def _sc_gather2(ys, dest0, dest1):
    n = dest0.shape[0]
    d = ys.shape[1]
    per = n // SC_WORKERS
    out = jax.ShapeDtypeStruct((n, d), ys.dtype)

    @pl.kernel(out_type=[out, out], mesh=_sc_mesh(),
               scratch_types=[pltpu.VMEM((2, SC_IDX_WIN), jnp.int32), pltpu.VMEM((SC_ROWS, d), ys.dtype)])
    def k(ys_hbm, d0_hbm, d1_hbm, y0_hbm, y1_hbm, idx, buf):
        base = (lax.axis_index("c") * SC_SUBCORES + lax.axis_index("s")) * per

        @pl.loop(0, per // SC_IDX_WIN)
        def _(w):
            off = base + w * SC_IDX_WIN
            pltpu.sync_copy(d0_hbm.at[pl.ds(off, SC_IDX_WIN)], idx.at[0])
            pltpu.sync_copy(d1_hbm.at[pl.ds(off, SC_IDX_WIN)], idx.at[1])
            for r in range(SC_IDX_WIN // SC_ROWS):
                for kk, y_hbm in enumerate((y0_hbm, y1_hbm)):
                    pltpu.sync_copy(ys_hbm.at[idx.at[kk, pl.ds(r * SC_ROWS, SC_ROWS)]], buf)
                    pltpu.sync_copy(buf, y_hbm.at[pl.ds(off + r * SC_ROWS, SC_ROWS)])

    return k(ys, dest0, dest1)


def _experts_kernel(be_ref, bv_ref, xs_ref, wg_ref, wu_ref, wd_ref, ys_ref, wg_bf, wu_bf, wd_bf):
    i = pl.program_id(0)
    valid = bv_ref[i]

    @pl.when(jnp.logical_or(i == 0, be_ref[i] != be_ref[jnp.maximum(i - 1, 0)]))
    def _():
        wg_bf[...] = wg_ref[0].astype(bf16)
        wu_bf[...] = wu_ref[0].astype(bf16)
        wd_bf[...] = wd_ref[0].astype(bf16)

    @pl.when(valid > 0)
    def _():
        row = lax.broadcasted_iota(jnp.int32, xs_ref.shape, 0)
        keep = row < valid
        lo, hi = _unpack_halves(xs_ref[...])
        x = jnp.concatenate([jnp.where(keep, lo, 0.0), jnp.where(keep, hi, 0.0)], axis=1).astype(bf16)
        g = jnp.dot(x, wg_bf[...], preferred_element_type=f32)
        u = jnp.dot(x, wu_bf[...], preferred_element_type=f32)
        hmid = (g * _sigmoid(g) * u).astype(bf16)
        ys_ref[...] = _pack_halves(jnp.dot(hmid, wd_bf[...], preferred_element_type=f32))

    @pl.when(valid == 0)
    def _():
        ys_ref[...] = jnp.zeros_like(ys_ref)


def _experts(xs, blk_expert, blk_valid, wg, wu, wd, n_blocks, blk):
    grid_spec = pltpu.PrefetchScalarGridSpec(
        num_scalar_prefetch=2,
        grid=(n_blocks,),
        in_specs=[pl.BlockSpec((blk, D_PACKED), lambda i, be, bv: (i, 0)),
                  pl.BlockSpec((1, D_MODEL, D_EXPERT), lambda i, be, bv: (be[i], 0, 0)),
                  pl.BlockSpec((1, D_MODEL, D_EXPERT), lambda i, be, bv: (be[i], 0, 0)),
                  pl.BlockSpec((1, D_EXPERT, D_MODEL), lambda i, be, bv: (be[i], 0, 0))],
        out_specs=pl.BlockSpec((blk, D_PACKED), lambda i, be, bv: (i, 0)),
        scratch_shapes=[pltpu.VMEM((D_MODEL, D_EXPERT), bf16),
                        pltpu.VMEM((D_MODEL, D_EXPERT), bf16),
                        pltpu.VMEM((D_EXPERT, D_MODEL), bf16)],
    )
    return pl.pallas_call(
        _experts_kernel,
        grid_spec=grid_spec,
        out_shape=jax.ShapeDtypeStruct(xs.shape, jnp.uint32),
        compiler_params=_cparams(1),
        name="experts",
    )(blk_expert, blk_valid, xs, wg, wu, wd)


def _final_norm_kernel(h_ref, y0_ref, y1_ref, route_ref, g_ref, b_ref, o_ref):
    route_t = route_ref[...].T
    g0, g1 = route_t[:, 2:3], route_t[:, 3:4]
    lo0, hi0 = _unpack_halves(y0_ref[...])
    lo1, hi1 = _unpack_halves(y1_ref[...])
    ffn = jnp.concatenate([lo0 * g0 + lo1 * g1, hi0 * g0 + hi1 * g1], axis=1)
    o_ref[...] = _layer_norm(ALPHA * h_ref[...] + ffn, g_ref[...], b_ref[...])


def _final_norm(h, y0, y1, route, ln_g, ln_b):
    n = h.shape[0]
    ts = NORM_TS
    tile = pl.BlockSpec((ts, D_MODEL), lambda i: (i, 0))
    packed = pl.BlockSpec((ts, D_PACKED), lambda i: (i, 0))
    return pl.pallas_call(
        _final_norm_kernel,
        grid=(n // ts,),
        in_specs=[tile, packed, packed, pl.BlockSpec((ROUTE_ROWS, ts), lambda i: (0, i)),
                  _full(ln_g), _full(ln_b)],
        out_specs=tile,
        out_shape=jax.ShapeDtypeStruct((n, D_MODEL), f32),
        compiler_params=_cparams(1),
        name="final_norm",
    )(h, y0, y1, route, ln_g, ln_b)


def _encode(x, p, qkv_meta, rest_meta, slabs):
    bsz, t, _ = x.shape
    n = bsz * t
    assert t % NORM_TS == 0 and t % TAIL_TM == 0 and t % PROJ_TM == 0
    xf = x.reshape(n, D_MODEL)
    qkv, rest = _in_proj(xf, p["w_in"], PROJ_TM)
    att = _attention(qkv, qkv_meta, slabs, bsz, t)
    h1, h1_packed, route, cnt = _mixer_tail(xf, att, rest, rest_meta, p, t)
    blk = MOE_BLK
    dest0, dest1, blk_expert, blk_valid, n_blocks = _route(route, cnt, n, blk)
    xs = _sc_dispatch(h1_packed, dest0, dest1, n_blocks * blk)
    ys = _experts(xs, blk_expert, blk_valid, p["w_e_gate"], p["w_e_up"], p["w_e_down"], n_blocks, blk)
    y0, y1 = _sc_gather2(ys, dest0, dest1)
    y = _final_norm(h1, y0, y1, route, p["ln2_g"], p["ln2_b"])
    return y.reshape(bsz, t, D_MODEL)


def kernel(x_prompt, x_sample, meta_tokens, w_in, rpb, conv_w, conv_b, w_att_proj, w_conv_proj, w_out,
           ln1_g, ln1_b, w_router_group, w_router_expert, w_e_gate, w_e_up, w_e_down, ln2_g, ln2_b):
    col_scale = np.ones((1, D_IN_PROJ), np.float32)
    col_scale[:, D_IN_PROJ - 2 * D_MODEL:] = 0.5
    w_in_bf = (w_in[0] * col_scale).astype(bf16)
    w_router = jnp.concatenate([w_router_expert[0], w_router_group[0]], axis=1)
    w_router = jnp.pad(w_router, ((0, 0), (0, LANES - w_router.shape[1])))
    row = lambda v: v[0].reshape(1, -1).astype(f32)
    p = {
        "w_in": w_in_bf,
        "conv_w": conv_w[0].astype(f32), "conv_b": row(conv_b),
        "w_att_proj": w_att_proj[0].astype(bf16), "w_conv_proj": w_conv_proj[0].astype(bf16),
        "w_out": (0.5 * w_out[0]).astype(bf16), "ln1_g": row(ln1_g), "ln1_b": row(ln1_b),
        "w_router": w_router.astype(bf16),
        "w_e_gate": w_e_gate[0], "w_e_up": w_e_up[0], "w_e_down": w_e_down[0],
        "ln2_g": row(ln2_g), "ln2_b": row(ln2_b),
    }
    qkv_meta, rest_meta = _in_proj(meta_tokens.astype(f32), p["w_in"], N_META)
    slabs = _bias_slabs(rpb[0])
    y_prompt = _encode(x_prompt, p, qkv_meta, rest_meta, slabs)
    y_sample = _encode(x_sample, p, qkv_meta, rest_meta, slabs)
    return (y_prompt, y_sample)
```

```python
import functools

import numpy as np
import jax
import jax.numpy as jnp
from jax import lax
from jax.experimental import pallas as pl
from jax.experimental.pallas import tpu as pltpu
from jax.experimental.pallas import tpu_sc as plsc

D_MODEL = 1024
N_META = 16
GRID_W = 64
WIN_H = 8
WIN_W = 16
N_HEADS = 8
HEAD_DIM = 64
D_ATT = N_HEADS * HEAD_DIM
D_CONV = D_MODEL
N_GROUPS = 4
EXPERTS_PER_GROUP = 8
N_EXPERTS = N_GROUPS * EXPERTS_PER_GROUP
D_EXPERT = D_MODEL // 2
DEPTH = 1
ALPHA = (2.0 * DEPTH) ** 0.25
LN_EPS = 1e-5
D_QKV = 4 * D_ATT
LOG2E = 1.4426950408889634
D_REST = 3 * D_CONV + 2 * D_MODEL
D_IN_PROJ = 3 * D_ATT + D_REST

LANES = 128
HEADS_PER_VREG = LANES // HEAD_DIM
N_HEAD_PAIRS = N_HEADS // HEADS_PER_VREG
NEG_BIG = -1e30

PROJ_TM = 512
PROJ_TN = 512
ATT_ROWS = 16
ATT_GROUP = 4
SLAB_ROWS = WIN_H + 1
ATT_KEYS = SLAB_ROWS * GRID_W + N_META
N_DY = 2 * WIN_H - 1
PIECE_MT = N_DY - 1
PIECE_T0 = PIECE_MT + WIN_H
PIECE_M0 = PIECE_T0 + WIN_H
N_BIAS_PIECES = PIECE_M0 + 1
TAIL_TM = 512
TAIL_CW = 256
NORM_TS = 1024
MOE_BLK = 512
ROUTE_ROWS = 8
VMEM_LIMIT = 56 * 1024 * 1024

SC_CORES = 2
SC_SUBCORES = 16
SC_WORKERS = SC_CORES * SC_SUBCORES
SC_IDX_WIN = 128
SC_ROWS = 64
D_PACKED = D_MODEL // 2

bf16 = jnp.bfloat16
f32 = jnp.float32


def _cparams(n_axes):
    return pltpu.CompilerParams(dimension_semantics=("arbitrary",) * n_axes,
                                vmem_limit_bytes=VMEM_LIMIT)


def _full(a):
    return pl.BlockSpec(a.shape, lambda *_: (0,) * a.ndim)


def _pack_halves(x):
    half = x.shape[1] // 2
    bits = lambda v: lax.bitcast_convert_type(v.astype(bf16).astype(f32), jnp.uint32)
    return (bits(x[:, :half]) >> 16) | bits(x[:, half:])


def _unpack_halves(p):
    lo = lax.bitcast_convert_type(p << 16, f32)
    hi = lax.bitcast_convert_type(p & jnp.uint32(0xFFFF0000), f32)
    return lo, hi


def _in_proj_kernel(x_ref, w_ref, qkv_ref, rest_ref):
    assert PROJ_TN == D_ATT
    xb = x_ref[...].astype(bf16)
    for c in range(0, D_IN_PROJ, PROJ_TN):
        y = jnp.dot(xb, w_ref[:, c:c + PROJ_TN], preferred_element_type=f32)
        if c == 0:
            y = (y * (HEAD_DIM ** -0.5 * LOG2E)).astype(bf16)
            lane = lax.broadcasted_iota(jnp.int32, y.shape, 1)
            even = (lane & HEAD_DIM) == 0
            zero = jnp.zeros_like(y)
            qkv_ref[:, :D_ATT] = jnp.where(even, y, zero)
            qkv_ref[:, D_ATT:2 * D_ATT] = jnp.where(even, zero, y)
        elif c < 3 * D_ATT:
            qkv_ref[:, c + D_ATT:c + D_ATT + PROJ_TN] = y.astype(bf16)
        else:
            rest_ref[:, c - 3 * D_ATT:c - 3 * D_ATT + PROJ_TN] = y.astype(bf16)


def _in_proj(x, w_bf, tm):
    n = x.shape[0]
    return pl.pallas_call(
        _in_proj_kernel,
        grid=(n // tm,),
        in_specs=[pl.BlockSpec((tm, D_MODEL), lambda i: (i, 0)),
                  pl.BlockSpec((D_MODEL, D_IN_PROJ), lambda i: (0, 0))],
        out_specs=[pl.BlockSpec((tm, D_QKV), lambda i: (i, 0)),
                   pl.BlockSpec((tm, D_REST), lambda i: (i, 0))],
        out_shape=[jax.ShapeDtypeStruct((n, D_QKV), bf16),
                   jax.ShapeDtypeStruct((n, D_REST), bf16)],
        compiler_params=_cparams(1),
        name="in_proj",
    )(x, w_bf)


def _bias_pieces(rpb):
    qc = np.arange(GRID_W)[:, None]
    kc = np.arange(GRID_W)[None, :]
    w_start = np.clip(qc - WIN_W // 2, 0, GRID_W - WIN_W)
    valid = (kc >= w_start) & (kc < w_start + WIN_W)
    n_dx = 2 * WIN_W - 1
    onehot = (kc - qc + WIN_W - 1)[None] == np.arange(n_dx)[:, None, None]
    t = jnp.einsum("hyx,xqk->hyqk", rpb.astype(f32) * LOG2E, jnp.asarray(onehot & valid[None], f32),
                   precision=lax.Precision.HIGHEST)
    t = jnp.where(jnp.asarray(valid)[None, None], t, NEG_BIG)
    t = t.reshape(N_HEAD_PAIRS, HEADS_PER_VREG, N_DY, GRID_W, GRID_W)
    t = jnp.concatenate([t[:, p] for p in range(HEADS_PER_VREG)], axis=2)
    masked = jnp.full((N_HEAD_PAIRS, WIN_H, HEADS_PER_VREG * GRID_W, GRID_W), NEG_BIG, f32)
    zeros = jnp.zeros_like(masked)
    pieces = jnp.concatenate([
        jnp.concatenate([t[:, :N_DY - 1], t[:, 1:]], axis=-1),
        jnp.concatenate([masked, t[:, :WIN_H]], axis=-1),
        jnp.concatenate([t[:, WIN_H - 1:], zeros], axis=-1),
        jnp.concatenate([masked[:, :1], zeros[:, :1]], axis=-1),
    ], axis=1)
    assert pieces.shape[1] == N_BIAS_PIECES
    return pieces


def _attention_kernel(qe_ref, qo_ref, k_ref, v_ref, km_ref, vm_ref, bias_ref, o_ref, *, rows):
    j = pl.program_id(1)
    tq2 = 2 * GRID_W
    lane = lax.broadcasted_iota(jnp.int32, (tq2, LANES), 1)
    low = lane < HEAD_DIM
    nt = (((1,), (1,)), ((), ()))
    slab_keys = SLAB_ROWS * GRID_W

    def row_bias(pair, dy0, off):
        first = jnp.where(off == 0, dy0, PIECE_MT + dy0)
        mids = [dy0 + 2 * p - off for p in range(1, SLAB_ROWS // 2)]
        last = jnp.where(off == 0, PIECE_M0, PIECE_T0 + dy0)
        tail = ATT_KEYS - (SLAB_ROWS // 2) * LANES
        return jnp.concatenate([bias_ref[pair, first]] + [bias_ref[pair, m] for m in mids]
                               + [bias_ref[pair, last][:, :tail]], axis=1)

    def scores(ip, pair):
        r0 = j * ATT_ROWS + 2 * ip
        rs = [jnp.clip(r0 + a - WIN_H // 2, 0, rows - WIN_H) for a in range(2)]
        us = jnp.minimum(rs[0], rows - SLAB_ROWS)
        q0 = pl.multiple_of(ip * tq2, tq2)
        k0 = pl.multiple_of(us * GRID_W, GRID_W)
        cs = slice(pair * LANES, (pair + 1) * LANES)
        qq = jnp.concatenate([qe_ref[pl.ds(q0, tq2), cs], qo_ref[pl.ds(q0, tq2), cs]], axis=0)
        k2 = jnp.concatenate([k_ref[pl.ds(k0, slab_keys), cs], km_ref[:, cs]], axis=0)
        s = lax.dot_general(qq, k2, nt, preferred_element_type=f32)
        b = [row_bias(pair, rs[a] - (r0 + a) + (WIN_H - 1), rs[a] - us) for a in range(2)]
        s = s + jnp.concatenate([b[0][:GRID_W], b[1][:GRID_W], b[0][GRID_W:], b[1][GRID_W:]], axis=0)
        return s, jnp.max(s, axis=-1, keepdims=True), k0, q0, cs

    def weights(state):
        s, m, k0, q0, cs = state
        v2 = jnp.concatenate([v_ref[pl.ds(k0, slab_keys), cs], vm_ref[:, cs]], axis=0)
        e = jnp.exp2(s - m)
        l = jnp.sum(e, axis=-1, keepdims=True)
        return jnp.dot(e.astype(bf16), v2, preferred_element_type=f32), l, q0, cs

    def finish(state):
        o2, l, q0, cs = state
        o2 = o2 / l
        o = jnp.where(low, o2[:tq2], o2[tq2:])
        o_ref[pl.ds(q0, tq2), cs] = o.astype(bf16)

    def group_body(g, carry):
        items = [(g * ATT_GROUP + ip, pair) for ip in range(ATT_GROUP) for pair in range(N_HEAD_PAIRS)]
        a, b = {}, {}
        for step in range(len(items) + 2):
            if step < len(items):
                a[step] = scores(*items[step])
            if 0 <= step - 1 < len(items):
                b[step - 1] = weights(a.pop(step - 1))
            if 0 <= step - 2 < len(items):
                finish(b.pop(step - 2))
        return carry

    lax.fori_loop(0, ATT_ROWS // (2 * ATT_GROUP), group_body, 0)


def _attention(qkv, qkv_meta, slabs, bsz, t):
    rows = t // GRID_W
    assert rows >= 2 * WIN_H and rows % ATT_ROWS == 0 and ATT_ROWS % (2 * ATT_GROUP) == 0
    steps = rows // ATT_ROWS
    tq = ATT_ROWS * GRID_W
    return pl.pallas_call(
        functools.partial(_attention_kernel, rows=rows),
        grid=(bsz, steps),
        in_specs=[pl.BlockSpec((tq, D_ATT), lambda b, j: (b * steps + j, 0)),
                  pl.BlockSpec((tq, D_ATT), lambda b, j: (b * steps + j, 1)),
                  pl.BlockSpec((t, D_ATT), lambda b, j: (b, 2)),
                  pl.BlockSpec((t, D_ATT), lambda b, j: (b, 3)),
                  pl.BlockSpec((N_META, D_ATT), lambda b, j: (0, 2)),
                  pl.BlockSpec((N_META, D_ATT), lambda b, j: (0, 3)),
                  _full(slabs)],
        out_specs=pl.BlockSpec((tq, D_ATT), lambda b, j: (b * steps + j, 0)),
        out_shape=jax.ShapeDtypeStruct((bsz * t, D_ATT), bf16),
        compiler_params=_cparams(2),
        name="attention",
    )(qkv, qkv, qkv, qkv, qkv_meta, qkv_meta, slabs)


def _layer_norm(z, g, b):
    mu = jnp.mean(z, axis=-1, keepdims=True)
    d = z - mu
    var = jnp.mean(d * d, axis=-1, keepdims=True)
    return d * lax.rsqrt(var + LN_EPS) * g + b


def _sigmoid(x):
    return 0.5 * jnp.tanh(0.5 * x) + 0.5


def _route_tile(logits, before_ref):
    tm = logits.shape[0]
    lt = logits.T
    el = lt[:N_EXPERTS]
    gl = lt[N_EXPERTS:N_EXPERTS + N_GROUPS]
    neg = -jnp.inf
    erow = lax.broadcasted_iota(jnp.int32, el.shape, 0).astype(f32)
    grow = lax.broadcasted_iota(jnp.int32, gl.shape, 0).astype(f32)
    first = lambda hit, idx, n: jnp.min(jnp.where(hit, idx, float(n)), axis=0, keepdims=True)

    gmax = jnp.max(gl, axis=0, keepdims=True)
    grp = first(gl == gmax, grow, N_GROUPS)
    pg_sel = 1.0 / jnp.sum(jnp.exp(gl - gmax), axis=0, keepdims=True)

    e_lo = grp * EXPERTS_PER_GROUP
    elm = jnp.where((erow >= e_lo) & (erow < e_lo + EXPERTS_PER_GROUP), el, neg)
    t1 = jnp.max(elm, axis=0, keepdims=True)
    e1 = first(elm == t1, erow, N_EXPERTS)
    el2 = jnp.where(erow == e1, neg, elm)
    t2 = jnp.max(el2, axis=0, keepdims=True)
    e2 = first(el2 == t2, erow, N_EXPERTS)
    r = jnp.exp(t2 - t1)
    g1 = pg_sel / (1.0 + r)
    g2 = pg_sel * r / (1.0 + r)

    hit1 = erow == e1
    hit2 = erow == e2
    onehot = jnp.where(hit1 | hit2, 1.0, 0.0)
    before = jnp.dot(onehot.astype(bf16), before_ref[...], preferred_element_type=f32)
    rank1 = jnp.sum(jnp.where(hit1, before, 0.0), axis=0, keepdims=True)
    rank2 = jnp.sum(jnp.where(hit2, before, 0.0), axis=0, keepdims=True)
    rrow = lax.broadcasted_iota(jnp.int32, (ROUTE_ROWS, tm), 0)
    route = jnp.zeros((ROUTE_ROWS, tm), f32)
    for k, val in enumerate((e1, e2, g1, g2, rank1, rank2)):
        route = jnp.where(rrow == k, val, route)
    return route, jnp.sum(onehot, axis=1, keepdims=True)


def _mixer_tail_kernel(x_ref, att_ref, rest_ref, prevc_ref, prevx_ref, nextc_ref, nextx_ref, meta_ref,
                       convw_ref, convb_ref, wap_ref, wcp_ref, wout_ref, g_ref, b_ref, wr_ref, before_ref,
                       h_ref, hp_ref, route_ref, cnt_ref, merged_scr, *, tiles_per_seq, n_tiles):
    i = pl.program_id(0)
    pos = jnp.minimum(i, n_tiles - 1) % tiles_per_seq
    tm = x_ref.shape[0]
    c2, c0, c1, c3, c4 = (k * D_MODEL for k in range(5))
    cols = lambda j, base=0: slice(base + j * TAIL_CW, base + (j + 1) * TAIL_CW)
    chunks = range(D_MODEL // TAIL_CW)

    @pl.when(i == 0)
    def _():
        merged_scr[...] = jnp.zeros_like(merged_scr)

    def u_of(ref, rws, j):
        return ref[rws, cols(j, c0)].astype(f32) * ref[rws, cols(j, c1)].astype(f32)

    def halo_u(cref, xref, r, j):
        return cref[r:r + 1, cols(j)].astype(f32) * xref[r:r + 1, cols(j)].astype(f32)

    last = prevc_ref.shape[0] - 1
    row = lax.broadcasted_iota(jnp.int32, (tm, TAIL_CW), 0)

    def conv_chunk(j):
        u = u_of(rest_ref, slice(0, tm), j)
        u_prev = jnp.where(pos == 0, u_of(meta_ref, slice(N_META - 1, N_META), j),
                           halo_u(prevc_ref, prevx_ref, last, j))
        u_next = halo_u(nextc_ref, nextx_ref, 0, j)
        u_next = jnp.where(pos == tiles_per_seq - 1, jnp.zeros_like(u_next), u_next)
        u_m1 = jnp.where(row == 0, u_prev, pltpu.roll(u, 1, axis=0))
        u_p1 = jnp.where(row == tm - 1, u_next, pltpu.roll(u, tm - 1, axis=0))
        cw = convw_ref[:, cols(j)]
        s = u_m1 * cw[0:1] + u * cw[1:2] + u_p1 * cw[2:3] + convb_ref[:, cols(j)]
        return (rest_ref[:, cols(j, c2)].astype(f32) * s).astype(bf16)

    mix = [jnp.dot(merged_scr[...], wout_ref[:, cols(j)], preferred_element_type=f32) for j in chunks]

    cv = None
    for j in chunks:
        part = jnp.dot(conv_chunk(j), wcp_ref[cols(j), :], preferred_element_type=f32)
        cv = part if cv is None else cv + part
    ap = jnp.dot(att_ref[...], wap_ref[...], preferred_element_type=f32)

    z = [ALPHA * x_ref[:, cols(j)] + mix[j] for j in chunks]
    mu = sum(jnp.sum(zj, axis=-1, keepdims=True) for zj in z) * (1.0 / D_MODEL)
    d = [zj - mu for zj in z]
    var = sum(jnp.sum(dj * dj, axis=-1, keepdims=True) for dj in d) * (1.0 / D_MODEL)
    rstd = lax.rsqrt(var + LN_EPS)
    logits = None
    hs = []
    for j in chunks:
        hj = d[j] * rstd * g_ref[:, cols(j)] + b_ref[:, cols(j)]
        h_ref[:, cols(j)] = hj
        hs.append(hj)
        part = jnp.dot(hj.astype(bf16), wr_ref[cols(j), :], preferred_element_type=f32)
        logits = part if logits is None else logits + part
    hp_ref[...] = _pack_halves(jnp.concatenate(hs, axis=1))

    for j in chunks:
        gate_a = jnp.tanh(rest_ref[:, cols(j, c3)]) + 1.0
        gate_c = jnp.tanh(rest_ref[:, cols(j, c4)]) + 1.0
        merged_scr[:, cols(j)] = gate_a * ap[:, cols(j)].astype(bf16) + gate_c * cv[:, cols(j)].astype(bf16)

    route, count = _route_tile(logits, before_ref)
    route_ref[...] = route
    cnt_ref[0] = jnp.broadcast_to(count, (N_EXPERTS, LANES))


def _mixer_tail(x, att, rest, rest_meta, p, t):
    n = x.shape[0]
    tm = TAIL_TM
    halo = 16
    hb = tm // halo
    n_halo = n // halo
    n_tiles = n // tm
    tiles_per_seq = t // tm
    before = jnp.asarray(np.triu(np.ones((tm, tm), np.float32), 1), dtype=bf16)
    consts = (rest_meta, p["conv_w"], p["conv_b"], p["w_att_proj"], p["w_conv_proj"], p["w_out"],
              p["ln1_g"], p["ln1_b"], p["w_router"], before)
    front = lambda i: jnp.minimum(i, n_tiles - 1)
    back = lambda i: jnp.maximum(i - 1, 0)
    prev_blk = lambda i: jnp.maximum(front(i) * hb - 1, 0)
    next_blk = lambda i: jnp.minimum((front(i) + 1) * hb, n_halo - 1)
    return pl.pallas_call(
        functools.partial(_mixer_tail_kernel, tiles_per_seq=tiles_per_seq, n_tiles=n_tiles),
        grid=(n_tiles + 1,),
        in_specs=[pl.BlockSpec((tm, D_MODEL), lambda i: (back(i), 0)),
                  pl.BlockSpec((tm, D_ATT), lambda i: (front(i), 0)),
                  pl.BlockSpec((tm, D_REST), lambda i: (front(i), 0)),
                  pl.BlockSpec((halo, D_CONV), lambda i: (prev_blk(i), 1)),
                  pl.BlockSpec((halo, D_CONV), lambda i: (prev_blk(i), 2)),
                  pl.BlockSpec((halo, D_CONV), lambda i: (next_blk(i), 1)),
                  pl.BlockSpec((halo, D_CONV), lambda i: (next_blk(i), 2))]
                 + [_full(c) for c in consts],
        out_specs=[pl.BlockSpec((tm, D_MODEL), lambda i: (back(i), 0)),
                   pl.BlockSpec((tm, D_PACKED), lambda i: (back(i), 0)),
                   pl.BlockSpec((ROUTE_ROWS, tm), lambda i: (0, back(i))),
                   pl.BlockSpec((1, N_EXPERTS, LANES), lambda i: (back(i), 0, 0))],
        out_shape=[jax.ShapeDtypeStruct((n, D_MODEL), f32),
                   jax.ShapeDtypeStruct((n, D_PACKED), jnp.uint32),
                   jax.ShapeDtypeStruct((ROUTE_ROWS, n), f32),
                   jax.ShapeDtypeStruct((n_tiles, N_EXPERTS, LANES), f32)],
        scratch_shapes=[pltpu.VMEM((tm, D_MODEL), bf16)],
        compiler_params=_cparams(1),
        name="mixer_tail",
    )(x, att, rest, rest, rest, rest, rest, *consts)


def _route(route, cnt, n, blk):
    n_tiles = n // TAIL_TM
    tile_cnt = cnt[:, :, 0].astype(jnp.int32)
    tile_off = jnp.cumsum(tile_cnt, axis=0) - tile_cnt
    counts = jnp.sum(tile_cnt, axis=0)
    padded = (counts + blk - 1) // blk * blk
    pend = jnp.cumsum(padded)
    pstart = pend - padded
    base = pstart[None, :] + tile_off
    lanes = jnp.arange(N_EXPERTS, dtype=jnp.int32)

    def rows_of(expert_row, rank_row):
        e = expert_row.astype(jnp.int32).reshape(n_tiles, TAIL_TM, 1)
        sel = jnp.sum(jnp.where(e == lanes, base[:, None, :], 0), axis=-1)
        return sel.reshape(n) + rank_row.astype(jnp.int32)

    dest0 = rows_of(route[0], route[4])
    dest1 = rows_of(route[1], route[5])

    n_blocks = (2 * n + N_EXPERTS * (blk - 1) + blk - 1) // blk
    blk_start = jnp.arange(n_blocks, dtype=jnp.int32) * blk
    blk_expert = jnp.sum((pend[None, :] <= blk_start[:, None]).astype(jnp.int32), axis=1)
    blk_expert = jnp.minimum(blk_expert, N_EXPERTS - 1)
    blk_valid = jnp.clip((pstart + counts)[blk_expert] - blk_start, 0, blk)
    blk_valid = jnp.where(blk_start < pend[-1], blk_valid, 0).astype(jnp.int32)
    return dest0, dest1, blk_expert, blk_valid, n_blocks


def _sc_mesh():
    return plsc.VectorSubcoreMesh(core_axis_name="c", subcore_axis_name="s")


def _sc_dispatch(h, dest0, dest1, p_rows):
    n, d = h.shape
    per = n // SC_WORKERS
    assert n % (SC_WORKERS * SC_IDX_WIN) == 0

    @pl.kernel(out_type=jax.ShapeDtypeStruct((p_rows, d), h.dtype), mesh=_sc_mesh(),
               scratch_types=[pltpu.VMEM((2, SC_IDX_WIN), jnp.int32), pltpu.VMEM((SC_ROWS, d), h.dtype)])
    def k(h_hbm, d0_hbm, d1_hbm, xs_hbm, idx, buf):
        base = (lax.axis_index("c") * SC_SUBCORES + lax.axis_index("s")) * per

        @pl.loop(0, per // SC_IDX_WIN)
        def _(w):
            off = base + w * SC_IDX_WIN
            pltpu.sync_copy(d0_hbm.at[pl.ds(off, SC_IDX_WIN)], idx.at[0])
            pltpu.sync_copy(d1_hbm.at[pl.ds(off, SC_IDX_WIN)], idx.at[1])
            for r in range(SC_IDX_WIN // SC_ROWS):
                pltpu.sync_copy(h_hbm.at[pl.ds(off + r * SC_ROWS, SC_ROWS)], buf)
                pltpu.sync_copy(buf, xs_hbm.at[idx.at[0, pl.ds(r * SC_ROWS, SC_ROWS)]])
                pltpu.sync_copy(buf, xs_hbm.at[idx.at[1, pl.ds(r * SC_ROWS, SC_ROWS)]])

    return k(h, dest0, dest1)


def _sc_gather2(ys, dest0, dest1):
    n = dest0.shape[0]
    d = ys.shape[1]
    per = n // SC_WORKERS
    out = jax.ShapeDtypeStruct((n, d), ys.dtype)

    @pl.kernel(out_type=[out, out], mesh=_sc_mesh(),
               scratch_types=[pltpu.VMEM((2, SC_IDX_WIN), jnp.int32), pltpu.VMEM((SC_ROWS, d), ys.dtype)])
    def k(ys_hbm, d0_hbm, d1_hbm, y0_hbm, y1_hbm, idx, buf):
        base = (lax.axis_index("c") * SC_SUBCORES + lax.axis_index("s")) * per

        @pl.loop(0, per // SC_IDX_WIN)
        def _(w):
            off = base + w * SC_IDX_WIN
            pltpu.sync_copy(d0_hbm.at[pl.ds(off, SC_IDX_WIN)], idx.at[0])
            pltpu.sync_copy(d1_hbm.at[pl.ds(off, SC_IDX_WIN)], idx.at[1])
            for r in range(SC_IDX_WIN // SC_ROWS):
                for kk, y_hbm in enumerate((y0_hbm, y1_hbm)):
                    pltpu.sync_copy(ys_hbm.at[idx.at[kk, pl.ds(r * SC_ROWS, SC_ROWS)]], buf)
                    pltpu.sync_copy(buf, y_hbm.at[pl.ds(off + r * SC_ROWS, SC_ROWS)])

    return k(ys, dest0, dest1)


def _experts_kernel(be_ref, bv_ref, xs_ref, wg_ref, wu_ref, wd_ref, ys_ref, wg_bf, wu_bf, wd_bf):
    i = pl.program_id(0)
    valid = bv_ref[i]

    @pl.when(jnp.logical_or(i == 0, be_ref[i] != be_ref[jnp.maximum(i - 1, 0)]))
    def _():
        wg_bf[...] = wg_ref[0].astype(bf16)
        wu_bf[...] = wu_ref[0].astype(bf16)
        wd_bf[...] = wd_ref[0].astype(bf16)

    @pl.when(valid > 0)
    def _():
        row = lax.broadcasted_iota(jnp.int32, xs_ref.shape, 0)
        keep = row < valid
        lo, hi = _unpack_halves(xs_ref[...])
        x = jnp.concatenate([jnp.where(keep, lo, 0.0), jnp.where(keep, hi, 0.0)], axis=1).astype(bf16)
        g = jnp.dot(x, wg_bf[...], preferred_element_type=f32)
        u = jnp.dot(x, wu_bf[...], preferred_element_type=f32)
        hmid = (g * _sigmoid(g) * u).astype(bf16)
        ys_ref[...] = _pack_halves(jnp.dot(hmid, wd_bf[...], preferred_element_type=f32))

    @pl.when(valid == 0)
    def _():
        ys_ref[...] = jnp.zeros_like(ys_ref)


def _experts(xs, blk_expert, blk_valid, wg, wu, wd, n_blocks, blk):
    grid_spec = pltpu.PrefetchScalarGridSpec(
        num_scalar_prefetch=2,
        grid=(n_blocks,),
        in_specs=[pl.BlockSpec((blk, D_PACKED), lambda i, be, bv: (i, 0)),
                  pl.BlockSpec((1, D_MODEL, D_EXPERT), lambda i, be, bv: (be[i], 0, 0)),
                  pl.BlockSpec((1, D_MODEL, D_EXPERT), lambda i, be, bv: (be[i], 0, 0)),
                  pl.BlockSpec((1, D_EXPERT, D_MODEL), lambda i, be, bv: (be[i], 0, 0))],
        out_specs=pl.BlockSpec((blk, D_PACKED), lambda i, be, bv: (i, 0)),
        scratch_shapes=[pltpu.VMEM((D_MODEL, D_EXPERT), bf16),
                        pltpu.VMEM((D_MODEL, D_EXPERT), bf16),
                        pltpu.VMEM((D_EXPERT, D_MODEL), bf16)],
    )
    return pl.pallas_call(
        _experts_kernel,
        grid_spec=grid_spec,
        out_shape=jax.ShapeDtypeStruct(xs.shape, jnp.uint32),
        compiler_params=_cparams(1),
        name="experts",
    )(blk_expert, blk_valid, xs, wg, wu, wd)


def _final_norm_kernel(h_ref, y0_ref, y1_ref, route_ref, g_ref, b_ref, o_ref):
    route_t = route_ref[...].T
    g0, g1 = route_t[:, 2:3], route_t[:, 3:4]
    lo0, hi0 = _unpack_halves(y0_ref[...])
    lo1, hi1 = _unpack_halves(y1_ref[...])
    ffn = jnp.concatenate([lo0 * g0 + lo1 * g1, hi0 * g0 + hi1 * g1], axis=1)
    o_ref[...] = _layer_norm(ALPHA * h_ref[...] + ffn, g_ref[...], b_ref[...])


def _final_norm(h, y0, y1, route, ln_g, ln_b):
    n = h.shape[0]
    ts = NORM_TS
    tile = pl.BlockSpec((ts, D_MODEL), lambda i: (i, 0))
    packed = pl.BlockSpec((ts, D_PACKED), lambda i: (i, 0))
    return pl.pallas_call(
        _final_norm_kernel,
        grid=(n // ts,),
        in_specs=[tile, packed, packed, pl.BlockSpec((ROUTE_ROWS, ts), lambda i: (0, i)),
                  _full(ln_g), _full(ln_b)],
        out_specs=tile,
        out_shape=jax.ShapeDtypeStruct((n, D_MODEL), f32),
        compiler_params=_cparams(1),
        name="final_norm",
    )(h, y0, y1, route, ln_g, ln_b)


def _encode(x, p, qkv_meta, rest_meta, slabs):
    bsz, t, _ = x.shape
    n = bsz * t
    assert t % NORM_TS == 0 and t % TAIL_TM == 0 and t % PROJ_TM == 0
    xf = x.reshape(n, D_MODEL)
    qkv, rest = _in_proj(xf, p["w_in"], PROJ_TM)
    att = _attention(qkv, qkv_meta, slabs, bsz, t)
    h1, h1_packed, route, cnt = _mixer_tail(xf, att, rest, rest_meta, p, t)
    blk = MOE_BLK
    dest0, dest1, blk_expert, blk_valid, n_blocks = _route(route, cnt, n, blk)
    xs = _sc_dispatch(h1_packed, dest0, dest1, n_blocks * blk)
    ys = _experts(xs, blk_expert, blk_valid, p["w_e_gate"], p["w_e_up"], p["w_e_down"], n_blocks, blk)
    y0, y1 = _sc_gather2(ys, dest0, dest1)
    y = _final_norm(h1, y0, y1, route, p["ln2_g"], p["ln2_b"])
    return y.reshape(bsz, t, D_MODEL)


def kernel(x_prompt, x_sample, meta_tokens, w_in, rpb, conv_w, conv_b, w_att_proj, w_conv_proj, w_out,
           ln1_g, ln1_b, w_router_group, w_router_expert, w_e_gate, w_e_up, w_e_down, ln2_g, ln2_b):
    col_scale = np.ones((1, D_IN_PROJ), np.float32)
    col_scale[:, D_IN_PROJ - 2 * D_MODEL:] = 0.5
    w_in_bf = (w_in[0] * col_scale).astype(bf16)
    w_router = jnp.concatenate([w_router_expert[0], w_router_group[0]], axis=1)
    w_router = jnp.pad(w_router, ((0, 0), (0, LANES - w_router.shape[1])))
    row = lambda v: v[0].reshape(1, -1).astype(f32)
    p = {
        "w_in": w_in_bf,
        "conv_w": conv_w[0].astype(f32), "conv_b": row(conv_b),
        "w_att_proj": w_att_proj[0].astype(bf16), "w_conv_proj": w_conv_proj[0].astype(bf16),
        "w_out": (0.5 * w_out[0]).astype(bf16), "ln1_g": row(ln1_g), "ln1_b": row(ln1_b),
        "w_router": w_router.astype(bf16),
        "w_e_gate": w_e_gate[0], "w_e_up": w_e_up[0], "w_e_down": w_e_down[0],
        "ln2_g": row(ln2_g), "ln2_b": row(ln2_b),
    }
    qkv_meta, rest_meta = _in_proj(meta_tokens.astype(f32), p["w_in"], N_META)
    slabs = _bias_pieces(rpb[0])
    y_prompt = _encode(x_prompt, p, qkv_meta, rest_meta, slabs)
    y_sample = _encode(x_sample, p, qkv_meta, rest_meta, slabs)
    return (y_prompt, y_sample)
```

```python
import functools

import numpy as np
import jax
import jax.numpy as jnp
from jax import lax
from jax.experimental import pallas as pl
from jax.experimental.pallas import tpu as pltpu
from jax.experimental.pallas import tpu_sc as plsc

D_MODEL = 1024
N_META = 16
GRID_W = 64
WIN_H = 8
WIN_W = 16
N_HEADS = 8
HEAD_DIM = 64
D_ATT = N_HEADS * HEAD_DIM
D_CONV = D_MODEL
N_GROUPS = 4
EXPERTS_PER_GROUP = 8
N_EXPERTS = N_GROUPS * EXPERTS_PER_GROUP
D_EXPERT = D_MODEL // 2
DEPTH = 1
ALPHA = (2.0 * DEPTH) ** 0.25
LN_EPS = 1e-5
D_QKV = 4 * D_ATT
LOG2E = 1.4426950408889634
D_REST = 3 * D_CONV + 2 * D_MODEL
D_IN_PROJ = 3 * D_ATT + D_REST

LANES = 128
HEADS_PER_VREG = LANES // HEAD_DIM
N_HEAD_PAIRS = N_HEADS // HEADS_PER_VREG
NEG_BIG = -1e30

PROJ_TM = 512
PROJ_TN = 512
ATT_ROWS = 16
ATT_GROUP = 4
SLAB_ROWS = WIN_H + 1
ATT_KEYS = SLAB_ROWS * GRID_W + N_META
N_DY = 2 * WIN_H - 1
PIECE_MT = N_DY - 1
PIECE_T0 = PIECE_MT + WIN_H
PIECE_M0 = PIECE_T0 + WIN_H
N_BIAS_PIECES = PIECE_M0 + 1
TAIL_TM = 512
TAIL_CW = 256
NORM_TS = 1024
MOE_BLK = 512
ROUTE_ROWS = 8
VMEM_LIMIT = 56 * 1024 * 1024

SC_CORES = 2
SC_SUBCORES = 16
SC_WORKERS = SC_CORES * SC_SUBCORES
SC_IDX_WIN = 128
SC_ROWS = 64
D_PACKED = D_MODEL // 2

bf16 = jnp.bfloat16
f32 = jnp.float32


def _cparams(n_axes):
    return pltpu.CompilerParams(dimension_semantics=("arbitrary",) * n_axes,
                                vmem_limit_bytes=VMEM_LIMIT)


def _full(a):
    return pl.BlockSpec(a.shape, lambda *_: (0,) * a.ndim)


def _pack_halves(x):
    half = x.shape[1] // 2
    bits = lambda v: lax.bitcast_convert_type(v.astype(bf16).astype(f32), jnp.uint32)
    return (bits(x[:, :half]) >> 16) | bits(x[:, half:])


def _unpack_halves(p):
    lo = lax.bitcast_convert_type(p << 16, f32)
    hi = lax.bitcast_convert_type(p & jnp.uint32(0xFFFF0000), f32)
    return lo, hi


def _in_proj_kernel(x_ref, w_ref, qkv_ref, rest_ref):
    assert PROJ_TN == D_ATT
    xb = x_ref[...].astype(bf16)
    for c in range(0, D_IN_PROJ, PROJ_TN):
        y = jnp.dot(xb, w_ref[:, c:c + PROJ_TN], preferred_element_type=f32)
        if c == 0:
            y = (y * (HEAD_DIM ** -0.5 * LOG2E)).astype(bf16)
            lane = lax.broadcasted_iota(jnp.int32, y.shape, 1)
            even = (lane & HEAD_DIM) == 0
            zero = jnp.zeros_like(y)
            qkv_ref[:, :D_ATT] = jnp.where(even, y, zero)
            qkv_ref[:, D_ATT:2 * D_ATT] = jnp.where(even, zero, y)
        elif c < 3 * D_ATT:
            qkv_ref[:, c + D_ATT:c + D_ATT + PROJ_TN] = y.astype(bf16)
        else:
            rest_ref[:, c - 3 * D_ATT:c - 3 * D_ATT + PROJ_TN] = y.astype(bf16)


def _in_proj(x, w_bf, tm):
    n = x.shape[0]
    return pl.pallas_call(
        _in_proj_kernel,
        grid=(n // tm,),
        in_specs=[pl.BlockSpec((tm, D_MODEL), lambda i: (i, 0)),
                  pl.BlockSpec((D_MODEL, D_IN_PROJ), lambda i: (0, 0))],
        out_specs=[pl.BlockSpec((tm, D_QKV), lambda i: (i, 0)),
                   pl.BlockSpec((tm, D_REST), lambda i: (i, 0))],
        out_shape=[jax.ShapeDtypeStruct((n, D_QKV), bf16),
                   jax.ShapeDtypeStruct((n, D_REST), bf16)],
        compiler_params=_cparams(1),
        name="in_proj",
    )(x, w_bf)


def _bias_pieces(rpb):
    qc = np.arange(GRID_W)[:, None]
    kc = np.arange(GRID_W)[None, :]
    w_start = np.clip(qc - WIN_W // 2, 0, GRID_W - WIN_W)
    valid = (kc >= w_start) & (kc < w_start + WIN_W)
    n_dx = 2 * WIN_W - 1
    onehot = (kc - qc + WIN_W - 1)[None] == np.arange(n_dx)[:, None, None]
    t = jnp.einsum("hyx,xqk->hyqk", rpb.astype(f32) * LOG2E, jnp.asarray(onehot & valid[None], f32),
                   precision=lax.Precision.HIGHEST)
    t = jnp.where(jnp.asarray(valid)[None, None], t, NEG_BIG)
    t = t.reshape(N_HEAD_PAIRS, HEADS_PER_VREG, N_DY, GRID_W, GRID_W)
    t = jnp.concatenate([t[:, p] for p in range(HEADS_PER_VREG)], axis=2)
    masked = jnp.full((N_HEAD_PAIRS, WIN_H, HEADS_PER_VREG * GRID_W, GRID_W), NEG_BIG, f32)
    zeros = jnp.zeros_like(masked)
    pieces = jnp.concatenate([
        jnp.concatenate([t[:, :N_DY - 1], t[:, 1:]], axis=-1),
        jnp.concatenate([masked, t[:, :WIN_H]], axis=-1),
        jnp.concatenate([t[:, WIN_H - 1:], zeros], axis=-1),
        jnp.concatenate([masked[:, :1], zeros[:, :1]], axis=-1),
    ], axis=1)
    assert pieces.shape[1] == N_BIAS_PIECES
    return pieces


def _attention_kernel(qe_ref, qo_ref, k_ref, v_ref, km_ref, vm_ref, bias_ref, o_ref, *, rows):
    j = pl.program_id(1)
    tq2 = 2 * GRID_W
    lane = lax.broadcasted_iota(jnp.int32, (tq2, LANES), 1)
    low = lane < HEAD_DIM
    nt = (((1,), (1,)), ((), ()))
    slab_keys = SLAB_ROWS * GRID_W

    def row_bias(pair, dy0, off):
        first = jnp.where(off == 0, dy0, PIECE_MT + dy0)
        mids = [dy0 + 2 * p - off for p in range(1, SLAB_ROWS // 2)]
        last = jnp.where(off == 0, PIECE_M0, PIECE_T0 + dy0)
        tail = ATT_KEYS - (SLAB_ROWS // 2) * LANES
        return jnp.concatenate([bias_ref[pair, first]] + [bias_ref[pair, m] for m in mids]
                               + [bias_ref[pair, last][:, :tail]], axis=1)

    def scores(ip, pair):
        r0 = j * ATT_ROWS + 2 * ip
        rs = [jnp.clip(r0 + a - WIN_H // 2, 0, rows - WIN_H) for a in range(2)]
        us = jnp.minimum(rs[0], rows - SLAB_ROWS)
        q0 = pl.multiple_of(ip * tq2, tq2)
        k0 = pl.multiple_of(us * GRID_W, GRID_W)
        cs = slice(pair * LANES, (pair + 1) * LANES)
        qq = jnp.concatenate([qe_ref[pl.ds(q0, tq2), cs], qo_ref[pl.ds(q0, tq2), cs]], axis=0)
        k2 = jnp.concatenate([k_ref[pl.ds(k0, slab_keys), cs], km_ref[:, cs]], axis=0)
        s = lax.dot_general(qq, k2, nt, preferred_element_type=f32)
        b = [row_bias(pair, rs[a] - (r0 + a) + (WIN_H - 1), rs[a] - us) for a in range(2)]
        s = s + jnp.concatenate([b[0][:GRID_W], b[1][:GRID_W], b[0][GRID_W:], b[1][GRID_W:]], axis=0)
        return s, jnp.max(s, axis=-1, keepdims=True), k0, q0, cs

    def weights(state):
        s, m, k0, q0, cs = state
        v2 = jnp.concatenate([v_ref[pl.ds(k0, slab_keys), cs], vm_ref[:, cs]], axis=0)
        e = jnp.exp2(s - m)
        l = jnp.sum(e, axis=-1, keepdims=True)
        return jnp.dot(e.astype(bf16), v2, preferred_element_type=f32), l, q0, cs

    def finish(state):
        o2, l, q0, cs = state
        o2 = o2 / l
        o = jnp.where(low, o2[:tq2], o2[tq2:])
        o_ref[pl.ds(q0, tq2), cs] = o.astype(bf16)

    def group_body(g, carry):
        items = [(g * ATT_GROUP + ip, pair) for ip in range(ATT_GROUP) for pair in range(N_HEAD_PAIRS)]
        a, b = {}, {}
        for step in range(len(items) + 2):
            if step < len(items):
                a[step] = scores(*items[step])
            if 0 <= step - 1 < len(items):
                b[step - 1] = weights(a.pop(step - 1))
            if 0 <= step - 2 < len(items):
                finish(b.pop(step - 2))
        return carry

    lax.fori_loop(0, ATT_ROWS // (2 * ATT_GROUP), group_body, 0)


def _attention(qkv, qkv_meta, slabs, bsz, t):
    rows = t // GRID_W
    assert rows >= 2 * WIN_H and rows % ATT_ROWS == 0 and ATT_ROWS % (2 * ATT_GROUP) == 0
    steps = rows // ATT_ROWS
    tq = ATT_ROWS * GRID_W
    return pl.pallas_call(
        functools.partial(_attention_kernel, rows=rows),
        grid=(bsz, steps),
        in_specs=[pl.BlockSpec((tq, D_ATT), lambda b, j: (b * steps + j, 0)),
                  pl.BlockSpec((tq, D_ATT), lambda b, j: (b * steps + j, 1)),
                  pl.BlockSpec((t, D_ATT), lambda b, j: (b, 2)),
                  pl.BlockSpec((t, D_ATT), lambda b, j: (b, 3)),
                  pl.BlockSpec((N_META, D_ATT), lambda b, j: (0, 2)),
                  pl.BlockSpec((N_META, D_ATT), lambda b, j: (0, 3)),
                  _full(slabs)],
        out_specs=pl.BlockSpec((tq, D_ATT), lambda b, j: (b * steps + j, 0)),
        out_shape=jax.ShapeDtypeStruct((bsz * t, D_ATT), bf16),
        compiler_params=_cparams(2),
        name="attention",
    )(qkv, qkv, qkv, qkv, qkv_meta, qkv_meta, slabs)


def _layer_norm(z, g, b):
    mu = jnp.mean(z, axis=-1, keepdims=True)
    d = z - mu
    var = jnp.mean(d * d, axis=-1, keepdims=True)
    return d * lax.rsqrt(var + LN_EPS) * g + b


def _sigmoid(x):
    return 0.5 * jnp.tanh(0.5 * x) + 0.5


def _route_tile(logits, before_ref):
    tm = logits.shape[0]
    lt = logits.T
    el = lt[:N_EXPERTS]
    gl = lt[N_EXPERTS:N_EXPERTS + N_GROUPS]
    neg = -jnp.inf
    erow = lax.broadcasted_iota(jnp.int32, el.shape, 0).astype(f32)
    grow = lax.broadcasted_iota(jnp.int32, gl.shape, 0).astype(f32)
    first = lambda hit, idx, n: jnp.min(jnp.where(hit, idx, float(n)), axis=0, keepdims=True)

    gmax = jnp.max(gl, axis=0, keepdims=True)
    grp = first(gl == gmax, grow, N_GROUPS)
    pg_sel = 1.0 / jnp.sum(jnp.exp(gl - gmax), axis=0, keepdims=True)

    e_lo = grp * EXPERTS_PER_GROUP
    elm = jnp.where((erow >= e_lo) & (erow < e_lo + EXPERTS_PER_GROUP), el, neg)
    t1 = jnp.max(elm, axis=0, keepdims=True)
    e1 = first(elm == t1, erow, N_EXPERTS)
    el2 = jnp.where(erow == e1, neg, elm)
    t2 = jnp.max(el2, axis=0, keepdims=True)
    e2 = first(el2 == t2, erow, N_EXPERTS)
    r = jnp.exp(t2 - t1)
    g1 = pg_sel / (1.0 + r)
    g2 = pg_sel * r / (1.0 + r)

    hit1 = erow == e1
    hit2 = erow == e2
    onehot = jnp.where(hit1 | hit2, 1.0, 0.0)
    before = jnp.dot(onehot.astype(bf16), before_ref[...], preferred_element_type=f32)
    rank1 = jnp.sum(jnp.where(hit1, before, 0.0), axis=0, keepdims=True)
    rank2 = jnp.sum(jnp.where(hit2, before, 0.0), axis=0, keepdims=True)
    rrow = lax.broadcasted_iota(jnp.int32, (ROUTE_ROWS, tm), 0)
    route = jnp.zeros((ROUTE_ROWS, tm), f32)
    for k, val in enumerate((e1, e2, g1, g2, rank1, rank2)):
        route = jnp.where(rrow == k, val, route)
    return route, jnp.sum(onehot, axis=1, keepdims=True)


def _mixer_tail_kernel(x_ref, att_ref, rest_ref, prevc_ref, prevx_ref, nextc_ref, nextx_ref, meta_ref,
                       convw_ref, convb_ref, wap_ref, wcp_ref, wout_ref, g_ref, b_ref, wr_ref, before_ref,
                       h_ref, hp_ref, route_ref, cnt_ref, merged_scr, *, tiles_per_seq, n_tiles):
    i = pl.program_id(0)
    pos = jnp.minimum(i, n_tiles - 1) % tiles_per_seq
    tm = x_ref.shape[0]
    c2, c0, c1, c3, c4 = (k * D_MODEL for k in range(5))
    cols = lambda j, base=0: slice(base + j * TAIL_CW, base + (j + 1) * TAIL_CW)
    chunks = range(D_MODEL // TAIL_CW)

    @pl.when(i == 0)
    def _():
        merged_scr[...] = jnp.zeros_like(merged_scr)

    def u_of(ref, rws, j):
        return ref[rws, cols(j, c0)].astype(f32) * ref[rws, cols(j, c1)].astype(f32)

    def halo_u(cref, xref, r, j):
        return cref[r:r + 1, cols(j)].astype(f32) * xref[r:r + 1, cols(j)].astype(f32)

    last = prevc_ref.shape[0] - 1
    row = lax.broadcasted_iota(jnp.int32, (tm, TAIL_CW), 0)

    def conv_chunk(j):
        u = u_of(rest_ref, slice(0, tm), j)
        u_prev = jnp.where(pos == 0, u_of(meta_ref, slice(N_META - 1, N_META), j),
                           halo_u(prevc_ref, prevx_ref, last, j))
        u_next = halo_u(nextc_ref, nextx_ref, 0, j)
        u_next = jnp.where(pos == tiles_per_seq - 1, jnp.zeros_like(u_next), u_next)
        u_m1 = jnp.where(row == 0, u_prev, pltpu.roll(u, 1, axis=0))
        u_p1 = jnp.where(row == tm - 1, u_next, pltpu.roll(u, tm - 1, axis=0))
        cw = convw_ref[:, cols(j)]
        s = u_m1 * cw[0:1] + u * cw[1:2] + u_p1 * cw[2:3] + convb_ref[:, cols(j)]
        return (rest_ref[:, cols(j, c2)].astype(f32) * s).astype(bf16)

    mix = [jnp.dot(merged_scr[...], wout_ref[:, cols(j)], preferred_element_type=f32) for j in chunks]

    cv = None
    for j in chunks:
        part = jnp.dot(conv_chunk(j), wcp_ref[cols(j), :], preferred_element_type=f32)
        cv = part if cv is None else cv + part
    ap = jnp.dot(att_ref[...], wap_ref[...], preferred_element_type=f32)

    z = [ALPHA * x_ref[:, cols(j)] + mix[j] for j in chunks]
    mu = sum(jnp.sum(zj, axis=-1, keepdims=True) for zj in z) * (1.0 / D_MODEL)
    d = [zj - mu for zj in z]
    var = sum(jnp.sum(dj * dj, axis=-1, keepdims=True) for dj in d) * (1.0 / D_MODEL)
    rstd = lax.rsqrt(var + LN_EPS)
    logits = None
    hs = []
    for j in chunks:
        hj = d[j] * rstd * g_ref[:, cols(j)] + b_ref[:, cols(j)]
        h_ref[:, cols(j)] = hj
        hs.append(hj)
        part = jnp.dot(hj.astype(bf16), wr_ref[cols(j), :], preferred_element_type=f32)
        logits = part if logits is None else logits + part
    hp_ref[...] = _pack_halves(jnp.concatenate(hs, axis=1))

    for j in chunks:
        gate_a = jnp.tanh(rest_ref[:, cols(j, c3)]) + 1.0
        gate_c = jnp.tanh(rest_ref[:, cols(j, c4)]) + 1.0
        merged_scr[:, cols(j)] = gate_a * ap[:, cols(j)].astype(bf16) + gate_c * cv[:, cols(j)].astype(bf16)

    route, count = _route_tile(logits, before_ref)
    route_ref[...] = route
    cnt_ref[0] = jnp.broadcast_to(count, (N_EXPERTS, LANES))


def _mixer_tail(x, att, rest, rest_meta, p, t):
    n = x.shape[0]
    tm = TAIL_TM
    halo = 16
    hb = tm // halo
    n_halo = n // halo
    n_tiles = n // tm
    tiles_per_seq = t // tm
    before = jnp.asarray(np.triu(np.ones((tm, tm), np.float32), 1), dtype=bf16)
    consts = (rest_meta, p["conv_w"], p["conv_b"], p["w_att_proj"], p["w_conv_proj"], p["w_out"],
              p["ln1_g"], p["ln1_b"], p["w_router"], before)
    front = lambda i: jnp.minimum(i, n_tiles - 1)
    back = lambda i: jnp.maximum(i - 1, 0)
    prev_blk = lambda i: jnp.maximum(front(i) * hb - 1, 0)
    next_blk = lambda i: jnp.minimum((front(i) + 1) * hb, n_halo - 1)
    return pl.pallas_call(
        functools.partial(_mixer_tail_kernel, tiles_per_seq=tiles_per_seq, n_tiles=n_tiles),
        grid=(n_tiles + 1,),
        in_specs=[pl.BlockSpec((tm, D_MODEL), lambda i: (back(i), 0)),
                  pl.BlockSpec((tm, D_ATT), lambda i: (front(i), 0)),
                  pl.BlockSpec((tm, D_REST), lambda i: (front(i), 0)),
                  pl.BlockSpec((halo, D_CONV), lambda i: (prev_blk(i), 1)),
                  pl.BlockSpec((halo, D_CONV), lambda i: (prev_blk(i), 2)),
                  pl.BlockSpec((halo, D_CONV), lambda i: (next_blk(i), 1)),
                  pl.BlockSpec((halo, D_CONV), lambda i: (next_blk(i), 2))]
                 + [_full(c) for c in consts],
        out_specs=[pl.BlockSpec((tm, D_MODEL), lambda i: (back(i), 0)),
                   pl.BlockSpec((tm, D_PACKED), lambda i: (back(i), 0)),
                   pl.BlockSpec((ROUTE_ROWS, tm), lambda i: (0, back(i))),
                   pl.BlockSpec((1, N_EXPERTS, LANES), lambda i: (back(i), 0, 0))],
        out_shape=[jax.ShapeDtypeStruct((n, D_MODEL), f32),
                   jax.ShapeDtypeStruct((n, D_PACKED), jnp.uint32),
                   jax.ShapeDtypeStruct((ROUTE_ROWS, n), f32),
                   jax.ShapeDtypeStruct((n_tiles, N_EXPERTS, LANES), f32)],
        scratch_shapes=[pltpu.VMEM((tm, D_MODEL), bf16)],
        compiler_params=_cparams(1),
        name="mixer_tail",
    )(x, att, rest, rest, rest, rest, rest, *consts)


def _route(route, cnt, n, blk):
    n_tiles = n // TAIL_TM
    tile_cnt = cnt[:, :, 0].astype(jnp.int32)
    tile_off = jnp.cumsum(tile_cnt, axis=0) - tile_cnt
    counts = jnp.sum(tile_cnt, axis=0)
    padded = (counts + blk - 1) // blk * blk
    pend = jnp.cumsum(padded)
    pstart = pend - padded
    base = pstart[None, :] + tile_off
    lanes = jnp.arange(N_EXPERTS, dtype=jnp.int32)

    def rows_of(expert_row, rank_row):
        e = expert_row.astype(jnp.int32).reshape(n_tiles, TAIL_TM, 1)
        sel = jnp.sum(jnp.where(e == lanes, base[:, None, :], 0), axis=-1)
        return sel.reshape(n) + rank_row.astype(jnp.int32)

    dest0 = rows_of(route[0], route[4])
    dest1 = rows_of(route[1], route[5])

    n_blocks = (2 * n + N_EXPERTS * (blk - 1) + blk - 1) // blk
    blk_start = jnp.arange(n_blocks, dtype=jnp.int32) * blk
    blk_expert = jnp.sum((pend[None, :] <= blk_start[:, None]).astype(jnp.int32), axis=1)
    blk_expert = jnp.minimum(blk_expert, N_EXPERTS - 1)
    blk_valid = jnp.clip((pstart + counts)[blk_expert] - blk_start, 0, blk)
    blk_valid = jnp.where(blk_start < pend[-1], blk_valid, 0).astype(jnp.int32)
    return dest0, dest1, blk_expert, blk_valid, n_blocks


def _sc_mesh():
    return plsc.VectorSubcoreMesh(core_axis_name="c", subcore_axis_name="s")


def _sc_dispatch(h, dest0, dest1, p_rows):
    n, d = h.shape
    per = n // SC_WORKERS
    assert n % (SC_WORKERS * SC_IDX_WIN) == 0

    @pl.kernel(out_type=jax.ShapeDtypeStruct((p_rows, d), h.dtype), mesh=_sc_mesh(),
               scratch_types=[pltpu.VMEM((2, SC_IDX_WIN), jnp.int32), pltpu.VMEM((SC_ROWS, d), h.dtype)])
    def k(h_hbm, d0_hbm, d1_hbm, xs_hbm, idx, buf):
        base = (lax.axis_index("c") * SC_SUBCORES + lax.axis_index("s")) * per

        @pl.loop(0, per // SC_IDX_WIN)
        def _(w):
            off = base + w * SC_IDX_WIN
            pltpu.sync_copy(d0_hbm.at[pl.ds(off, SC_IDX_WIN)], idx.at[0])
            pltpu.sync_copy(d1_hbm.at[pl.ds(off, SC_IDX_WIN)], idx.at[1])
            for r in range(SC_IDX_WIN // SC_ROWS):
                pltpu.sync_copy(h_hbm.at[pl.ds(off + r * SC_ROWS, SC_ROWS)], buf)
                pltpu.sync_copy(buf, xs_hbm.at[idx.at[0, pl.ds(r * SC_ROWS, SC_ROWS)]])
                pltpu.sync_copy(buf, xs_hbm.at[idx.at[1, pl.ds(r * SC_ROWS, SC_ROWS)]])

    return k(h, dest0, dest1)


def _sc_gather2(ys, dest0, dest1):
    n = dest0.shape[0]
    d = ys.shape[1]
    per = n // SC_WORKERS
    out = jax.ShapeDtypeStruct((n, d), ys.dtype)

    @pl.kernel(out_type=[out, out], mesh=_sc_mesh(),
               scratch_types=[pltpu.VMEM((2, SC_IDX_WIN), jnp.int32), pltpu.VMEM((SC_ROWS, d), ys.dtype)])
    def k(ys_hbm, d0_hbm, d1_hbm, y0_hbm, y1_hbm, idx, buf):
        base = (lax.axis_index("c") * SC_SUBCORES + lax.axis_index("s")) * per

        @pl.loop(0, per // SC_IDX_WIN)
        def _(w):
            off = base + w * SC_IDX_WIN
            pltpu.sync_copy(d0_hbm.at[pl.ds(off, SC_IDX_WIN)], idx.at[0])
            pltpu.sync_copy(d1_hbm.at[pl.ds(off, SC_IDX_WIN)], idx.at[1])
            for r in range(SC_IDX_WIN // SC_ROWS):
                for kk, y_hbm in enumerate((y0_hbm, y1_hbm)):
                    pltpu.sync_copy(ys_hbm.at[idx.at[kk, pl.ds(r * SC_ROWS, SC_ROWS)]], buf)
                    pltpu.sync_copy(buf, y_hbm.at[pl.ds(off + r * SC_ROWS, SC_ROWS)])

    return k(ys, dest0, dest1)


def _experts_kernel(be_ref, bv_ref, slot_ref, nxt_ref, xs_ref, wg_hbm, wu_hbm, wd_hbm, ys_ref,
                    wg_f32, wu_f32, wd_f32, wg_bf, wu_bf, wd_bf, sems):
    i = pl.program_id(0)
    valid = bv_ref[i]
    expert = be_ref[i]
    slot = slot_ref[i]
    run_start = jnp.logical_or(i == 0, expert != be_ref[jnp.maximum(i - 1, 0)])

    def fetch(e, s):
        return [pltpu.make_async_copy(hbm.at[e], buf.at[s], sems.at[s, k])
                for k, (hbm, buf) in enumerate(((wg_hbm, wg_f32), (wu_hbm, wu_f32), (wd_hbm, wd_f32)))]

    @pl.when(i == 0)
    def _():
        for cp in fetch(expert, slot):
            cp.start()

    @pl.when(run_start)
    def _():
        for cp in fetch(expert, slot):
            cp.wait()
        wg_bf[...] = wg_f32[slot].astype(bf16)
        wu_bf[...] = wu_f32[slot].astype(bf16)
        wd_bf[...] = wd_f32[slot].astype(bf16)

        @pl.when(nxt_ref[i] >= 0)
        def _():
            for cp in fetch(nxt_ref[i], 1 - slot):
                cp.start()

    @pl.when(valid > 0)
    def _():
        row = lax.broadcasted_iota(jnp.int32, xs_ref.shape, 0)
        keep = row < valid
        lo, hi = _unpack_halves(xs_ref[...])
        x = jnp.concatenate([jnp.where(keep, lo, 0.0), jnp.where(keep, hi, 0.0)], axis=1).astype(bf16)
        g = jnp.dot(x, wg_bf[...], preferred_element_type=f32)
        u = jnp.dot(x, wu_bf[...], preferred_element_type=f32)
        hmid = (g * _sigmoid(g) * u).astype(bf16)
        ys_ref[...] = _pack_halves(jnp.dot(hmid, wd_bf[...], preferred_element_type=f32))

    @pl.when(valid == 0)
    def _():
        ys_ref[...] = jnp.zeros_like(ys_ref)


def _experts(xs, blk_expert, blk_valid, wg, wu, wd, n_blocks, blk):
    starts = jnp.concatenate([jnp.ones((1,), jnp.int32), (blk_expert[1:] != blk_expert[:-1]).astype(jnp.int32)])
    slot = (jnp.cumsum(starts) - 1) % 2
    idx = jnp.arange(n_blocks, dtype=jnp.int32)
    next_start = lax.cummin(jnp.where(starts > 0, idx, n_blocks)[::-1])[::-1]
    next_start = jnp.concatenate([next_start[1:], jnp.full((1,), n_blocks, jnp.int32)])
    nxt = jnp.where(next_start < n_blocks, blk_expert[jnp.minimum(next_start, n_blocks - 1)], -1)
    tile = pl.BlockSpec((blk, D_PACKED), lambda i, *_: (i, 0))
    any_spec = pl.BlockSpec(memory_space=pl.ANY)
    grid_spec = pltpu.PrefetchScalarGridSpec(
        num_scalar_prefetch=4,
        grid=(n_blocks,),
        in_specs=[tile, any_spec, any_spec, any_spec],
        out_specs=tile,
        scratch_shapes=[pltpu.VMEM((2, D_MODEL, D_EXPERT), f32),
                        pltpu.VMEM((2, D_MODEL, D_EXPERT), f32),
                        pltpu.VMEM((2, D_EXPERT, D_MODEL), f32),
                        pltpu.VMEM((D_MODEL, D_EXPERT), bf16),
                        pltpu.VMEM((D_MODEL, D_EXPERT), bf16),
                        pltpu.VMEM((D_EXPERT, D_MODEL), bf16),
                        pltpu.SemaphoreType.DMA((2, 3))],
    )
    return pl.pallas_call(
        _experts_kernel,
        grid_spec=grid_spec,
        out_shape=jax.ShapeDtypeStruct(xs.shape, jnp.uint32),
        compiler_params=_cparams(1),
        name="experts",
    )(blk_expert, blk_valid, slot.astype(jnp.int32), nxt.astype(jnp.int32), xs, wg, wu, wd)


def _final_norm_kernel(h_ref, y0_ref, y1_ref, route_ref, g_ref, b_ref, o_ref):
    route_t = route_ref[...].T
    g0, g1 = route_t[:, 2:3], route_t[:, 3:4]
    lo0, hi0 = _unpack_halves(y0_ref[...])
    lo1, hi1 = _unpack_halves(y1_ref[...])
    ffn = jnp.concatenate([lo0 * g0 + lo1 * g1, hi0 * g0 + hi1 * g1], axis=1)
    o_ref[...] = _layer_norm(ALPHA * h_ref[...] + ffn, g_ref[...], b_ref[...])


def _final_norm(h, y0, y1, route, ln_g, ln_b):
    n = h.shape[0]
    ts = NORM_TS
    tile = pl.BlockSpec((ts, D_MODEL), lambda i: (i, 0))
    packed = pl.BlockSpec((ts, D_PACKED), lambda i: (i, 0))
    return pl.pallas_call(
        _final_norm_kernel,
        grid=(n // ts,),
        in_specs=[tile, packed, packed, pl.BlockSpec((ROUTE_ROWS, ts), lambda i: (0, i)),
                  _full(ln_g), _full(ln_b)],
        out_specs=tile,
        out_shape=jax.ShapeDtypeStruct((n, D_MODEL), f32),
        compiler_params=_cparams(1),
        name="final_norm",
    )(h, y0, y1, route, ln_g, ln_b)


def _encode(x, p, qkv_meta, rest_meta, slabs):
    bsz, t, _ = x.shape
    n = bsz * t
    assert t % NORM_TS == 0 and t % TAIL_TM == 0 and t % PROJ_TM == 0
    xf = x.reshape(n, D_MODEL)
    qkv, rest = _in_proj(xf, p["w_in"], PROJ_TM)
    att = _attention(qkv, qkv_meta, slabs, bsz, t)
    h1, h1_packed, route, cnt = _mixer_tail(xf, att, rest, rest_meta, p, t)
    blk = MOE_BLK
    dest0, dest1, blk_expert, blk_valid, n_blocks = _route(route, cnt, n, blk)
    xs = _sc_dispatch(h1_packed, dest0, dest1, n_blocks * blk)
    ys = _experts(xs, blk_expert, blk_valid, p["w_e_gate"], p["w_e_up"], p["w_e_down"], n_blocks, blk)
    y0, y1 = _sc_gather2(ys, dest0, dest1)
    y = _final_norm(h1, y0, y1, route, p["ln2_g"], p["ln2_b"])
    return y.reshape(bsz, t, D_MODEL)


def kernel(x_prompt, x_sample, meta_tokens, w_in, rpb, conv_w, conv_b, w_att_proj, w_conv_proj, w_out,
           ln1_g, ln1_b, w_router_group, w_router_expert, w_e_gate, w_e_up, w_e_down, ln2_g, ln2_b):
    col_scale = np.ones((1, D_IN_PROJ), np.float32)
    col_scale[:, D_IN_PROJ - 2 * D_MODEL:] = 0.5
    w_in_bf = (w_in[0] * col_scale).astype(bf16)
    w_router = jnp.concatenate([w_router_expert[0], w_router_group[0]], axis=1)
    w_router = jnp.pad(w_router, ((0, 0), (0, LANES - w_router.shape[1])))
    row = lambda v: v[0].reshape(1, -1).astype(f32)
    p = {
        "w_in": w_in_bf,
        "conv_w": conv_w[0].astype(f32), "conv_b": row(conv_b),
        "w_att_proj": w_att_proj[0].astype(bf16), "w_conv_proj": w_conv_proj[0].astype(bf16),
        "w_out": (0.5 * w_out[0]).astype(bf16), "ln1_g": row(ln1_g), "ln1_b": row(ln1_b),
        "w_router": w_router.astype(bf16),
        "w_e_gate": w_e_gate[0], "w_e_up": w_e_up[0], "w_e_down": w_e_down[0],
        "ln2_g": row(ln2_g), "ln2_b": row(ln2_b),
    }
    qkv_meta, rest_meta = _in_proj(meta_tokens.astype(f32), p["w_in"], N_META)
    slabs = _bias_pieces(rpb[0])
    y_prompt = _encode(x_prompt, p, qkv_meta, rest_meta, slabs)
    y_sample = _encode(x_sample, p, qkv_meta, rest_meta, slabs)
    return (y_prompt, y_sample)
```

```python
import functools

import numpy as np
import jax
import jax.numpy as jnp
from jax import lax
from jax.experimental import pallas as pl
from jax.experimental.pallas import tpu as pltpu
from jax.experimental.pallas import tpu_sc as plsc

D_MODEL = 1024
N_META = 16
GRID_W = 64
WIN_H = 8
WIN_W = 16
N_HEADS = 8
HEAD_DIM = 64
D_ATT = N_HEADS * HEAD_DIM
D_CONV = D_MODEL
N_GROUPS = 4
EXPERTS_PER_GROUP = 8
N_EXPERTS = N_GROUPS * EXPERTS_PER_GROUP
D_EXPERT = D_MODEL // 2
DEPTH = 1
ALPHA = (2.0 * DEPTH) ** 0.25
LN_EPS = 1e-5
D_QKV = 4 * D_ATT
LOG2E = 1.4426950408889634
D_REST = 3 * D_CONV + 2 * D_MODEL
D_IN_PROJ = 3 * D_ATT + D_REST

LANES = 128
HEADS_PER_VREG = LANES // HEAD_DIM
N_HEAD_PAIRS = N_HEADS // HEADS_PER_VREG
NEG_BIG = -1e30

PROJ_TM = 512
PROJ_TN = 512
ATT_ROWS = 16
ATT_GROUP = 4
SLAB_ROWS = WIN_H + 1
ATT_KEYS = SLAB_ROWS * GRID_W + N_META
N_DY = 2 * WIN_H - 1
PIECE_MT = N_DY - 1
PIECE_T0 = PIECE_MT + WIN_H
PIECE_M0 = PIECE_T0 + WIN_H
N_BIAS_PIECES = PIECE_M0 + 1
TAIL_TM = 512
TAIL_CW = 256
NORM_TS = 1024
MOE_BLK = 512
ROUTE_ROWS = 8
VMEM_LIMIT = 56 * 1024 * 1024

SC_CORES = 2
SC_SUBCORES = 16
SC_WORKERS = SC_CORES * SC_SUBCORES
SC_IDX_WIN = 128
SC_ROWS = 64
D_PACKED = D_MODEL // 2

bf16 = jnp.bfloat16
f32 = jnp.float32


def _cparams(n_axes):
    return pltpu.CompilerParams(dimension_semantics=("arbitrary",) * n_axes,
                                vmem_limit_bytes=VMEM_LIMIT)


def _full(a):
    return pl.BlockSpec(a.shape, lambda *_: (0,) * a.ndim)


def _pack_halves(x):
    half = x.shape[1] // 2
    bits = lambda v: lax.bitcast_convert_type(v.astype(bf16).astype(f32), jnp.uint32)
    return (bits(x[:, :half]) >> 16) | bits(x[:, half:])


def _unpack_halves(p):
    lo = lax.bitcast_convert_type(p << 16, f32)
    hi = lax.bitcast_convert_type(p & jnp.uint32(0xFFFF0000), f32)
    return lo, hi


def _in_proj_kernel(x_ref, w_ref, qkv_ref, rest_ref):
    assert PROJ_TN == D_ATT
    xb = x_ref[...].astype(bf16)
    for c in range(0, D_IN_PROJ, PROJ_TN):
        y = jnp.dot(xb, w_ref[:, c:c + PROJ_TN], preferred_element_type=f32)
        if c == 0:
            y = (y * (HEAD_DIM ** -0.5 * LOG2E)).astype(bf16)
            lane = lax.broadcasted_iota(jnp.int32, y.shape, 1)
            even = (lane & HEAD_DIM) == 0
            zero = jnp.zeros_like(y)
            qkv_ref[:, :D_ATT] = jnp.where(even, y, zero)
            qkv_ref[:, D_ATT:2 * D_ATT] = jnp.where(even, zero, y)
        elif c < 3 * D_ATT:
            qkv_ref[:, c + D_ATT:c + D_ATT + PROJ_TN] = y.astype(bf16)
        else:
            rest_ref[:, c - 3 * D_ATT:c - 3 * D_ATT + PROJ_TN] = y.astype(bf16)


def _in_proj(x, w_bf, tm):
    n = x.shape[0]
    return pl.pallas_call(
        _in_proj_kernel,
        grid=(n // tm,),
        in_specs=[pl.BlockSpec((tm, D_MODEL), lambda i: (i, 0)),
                  pl.BlockSpec((D_MODEL, D_IN_PROJ), lambda i: (0, 0))],
        out_specs=[pl.BlockSpec((tm, D_QKV), lambda i: (i, 0)),
                   pl.BlockSpec((tm, D_REST), lambda i: (i, 0))],
        out_shape=[jax.ShapeDtypeStruct((n, D_QKV), bf16),
                   jax.ShapeDtypeStruct((n, D_REST), bf16)],
        compiler_params=_cparams(1),
        name="in_proj",
    )(x, w_bf)


def _bias_pieces(rpb):
    qc = np.arange(GRID_W)[:, None]
    kc = np.arange(GRID_W)[None, :]
    w_start = np.clip(qc - WIN_W // 2, 0, GRID_W - WIN_W)
    valid = (kc >= w_start) & (kc < w_start + WIN_W)
    n_dx = 2 * WIN_W - 1
    onehot = (kc - qc + WIN_W - 1)[None] == np.arange(n_dx)[:, None, None]
    t = jnp.einsum("hyx,xqk->hyqk", rpb.astype(f32) * LOG2E, jnp.asarray(onehot & valid[None], f32),
                   precision=lax.Precision.HIGHEST)
    t = jnp.where(jnp.asarray(valid)[None, None], t, NEG_BIG)
    t = t.reshape(N_HEAD_PAIRS, HEADS_PER_VREG, N_DY, GRID_W, GRID_W)
    t = jnp.concatenate([t[:, p] for p in range(HEADS_PER_VREG)], axis=2)
    masked = jnp.full((N_HEAD_PAIRS, WIN_H, HEADS_PER_VREG * GRID_W, GRID_W), NEG_BIG, f32)
    zeros = jnp.zeros_like(masked)
    pieces = jnp.concatenate([
        jnp.concatenate([t[:, :N_DY - 1], t[:, 1:]], axis=-1),
        jnp.concatenate([masked, t[:, :WIN_H]], axis=-1),
        jnp.concatenate([t[:, WIN_H - 1:], zeros], axis=-1),
        jnp.concatenate([masked[:, :1], zeros[:, :1]], axis=-1),
    ], axis=1)
    assert pieces.shape[1] == N_BIAS_PIECES
    return pieces


def _attention_kernel(qe_ref, qo_ref, k_ref, v_ref, km_ref, vm_ref, bias_ref, o_ref, *, rows):
    j = pl.program_id(1)
    tq2 = 2 * GRID_W
    lane = lax.broadcasted_iota(jnp.int32, (tq2, LANES), 1)
    low = lane < HEAD_DIM
    nt = (((1,), (1,)), ((), ()))
    slab_keys = SLAB_ROWS * GRID_W

    def row_bias(pair, dy0, off):
        first = jnp.where(off == 0, dy0, PIECE_MT + dy0)
        mids = [dy0 + 2 * p - off for p in range(1, SLAB_ROWS // 2)]
        last = jnp.where(off == 0, PIECE_M0, PIECE_T0 + dy0)
        tail = ATT_KEYS - (SLAB_ROWS // 2) * LANES
        return jnp.concatenate([bias_ref[pair, first]] + [bias_ref[pair, m] for m in mids]
                               + [bias_ref[pair, last][:, :tail]], axis=1)

    def scores(ip, pair):
        r0 = j * ATT_ROWS + 2 * ip
        rs = [jnp.clip(r0 + a - WIN_H // 2, 0, rows - WIN_H) for a in range(2)]
        us = jnp.minimum(rs[0], rows - SLAB_ROWS)
        q0 = pl.multiple_of(ip * tq2, tq2)
        k0 = pl.multiple_of(us * GRID_W, GRID_W)
        cs = slice(pair * LANES, (pair + 1) * LANES)
        qq = jnp.concatenate([qe_ref[pl.ds(q0, tq2), cs], qo_ref[pl.ds(q0, tq2), cs]], axis=0)
        k2 = jnp.concatenate([k_ref[pl.ds(k0, slab_keys), cs], km_ref[:, cs]], axis=0)
        s = lax.dot_general(qq, k2, nt, preferred_element_type=f32)
        b = [row_bias(pair, rs[a] - (r0 + a) + (WIN_H - 1), rs[a] - us) for a in range(2)]
        s = s + jnp.concatenate([b[0][:GRID_W], b[1][:GRID_W], b[0][GRID_W:], b[1][GRID_W:]], axis=0)
        return s, jnp.max(s, axis=-1, keepdims=True), k0, q0, cs

    def weights(state):
        s, m, k0, q0, cs = state
        v2 = jnp.concatenate([v_ref[pl.ds(k0, slab_keys), cs], vm_ref[:, cs]], axis=0)
        e = jnp.exp2(s - m)
        l = jnp.sum(e, axis=-1, keepdims=True)
        return jnp.dot(e.astype(bf16), v2, preferred_element_type=f32), l, q0, cs

    def finish(state):
        o2, l, q0, cs = state
        o2 = o2 / l
        o = jnp.where(low, o2[:tq2], o2[tq2:])
        o_ref[pl.ds(q0, tq2), cs] = o.astype(bf16)

    def group_body(g, carry):
        items = [(g * ATT_GROUP + ip, pair) for ip in range(ATT_GROUP) for pair in range(N_HEAD_PAIRS)]
        a, b = {}, {}
        for step in range(len(items) + 2):
            if step < len(items):
                a[step] = scores(*items[step])
            if 0 <= step - 1 < len(items):
                b[step - 1] = weights(a.pop(step - 1))
            if 0 <= step - 2 < len(items):
                finish(b.pop(step - 2))
        return carry

    lax.fori_loop(0, ATT_ROWS // (2 * ATT_GROUP), group_body, 0)


def _attention(qkv, qkv_meta, slabs, bsz, t):
    rows = t // GRID_W
    assert rows >= 2 * WIN_H and rows % ATT_ROWS == 0 and ATT_ROWS % (2 * ATT_GROUP) == 0
    steps = rows // ATT_ROWS
    tq = ATT_ROWS * GRID_W
    return pl.pallas_call(
        functools.partial(_attention_kernel, rows=rows),
        grid=(bsz, steps),
        in_specs=[pl.BlockSpec((tq, D_ATT), lambda b, j: (b * steps + j, 0)),
                  pl.BlockSpec((tq, D_ATT), lambda b, j: (b * steps + j, 1)),
                  pl.BlockSpec((t, D_ATT), lambda b, j: (b, 2)),
                  pl.BlockSpec((t, D_ATT), lambda b, j: (b, 3)),
                  pl.BlockSpec((N_META, D_ATT), lambda b, j: (0, 2)),
                  pl.BlockSpec((N_META, D_ATT), lambda b, j: (0, 3)),
                  _full(slabs)],
        out_specs=pl.BlockSpec((tq, D_ATT), lambda b, j: (b * steps + j, 0)),
        out_shape=jax.ShapeDtypeStruct((bsz * t, D_ATT), bf16),
        compiler_params=_cparams(2),
        name="attention",
    )(qkv, qkv, qkv, qkv, qkv_meta, qkv_meta, slabs)


def _layer_norm(z, g, b):
    mu = jnp.mean(z, axis=-1, keepdims=True)
    d = z - mu
    var = jnp.mean(d * d, axis=-1, keepdims=True)
    return d * lax.rsqrt(var + LN_EPS) * g + b


def _sigmoid(x):
    return 0.5 * jnp.tanh(0.5 * x) + 0.5


def _route_tile(logits, before_ref):
    tm = logits.shape[0]
    lt = logits.T
    el = lt[:N_EXPERTS]
    gl = lt[N_EXPERTS:N_EXPERTS + N_GROUPS]
    neg = -jnp.inf
    erow = lax.broadcasted_iota(jnp.int32, el.shape, 0).astype(f32)
    grow = lax.broadcasted_iota(jnp.int32, gl.shape, 0).astype(f32)
    first = lambda hit, idx, n: jnp.min(jnp.where(hit, idx, float(n)), axis=0, keepdims=True)

    gmax = jnp.max(gl, axis=0, keepdims=True)
    grp = first(gl == gmax, grow, N_GROUPS)
    pg_sel = 1.0 / jnp.sum(jnp.exp(gl - gmax), axis=0, keepdims=True)

    e_lo = grp * EXPERTS_PER_GROUP
    elm = jnp.where((erow >= e_lo) & (erow < e_lo + EXPERTS_PER_GROUP), el, neg)
    t1 = jnp.max(elm, axis=0, keepdims=True)
    e1 = first(elm == t1, erow, N_EXPERTS)
    el2 = jnp.where(erow == e1, neg, elm)
    t2 = jnp.max(el2, axis=0, keepdims=True)
    e2 = first(el2 == t2, erow, N_EXPERTS)
    r = jnp.exp(t2 - t1)
    g1 = pg_sel / (1.0 + r)
    g2 = pg_sel * r / (1.0 + r)

    hit1 = erow == e1
    hit2 = erow == e2
    onehot = jnp.where(hit1 | hit2, 1.0, 0.0)
    before = jnp.dot(onehot.astype(bf16), before_ref[...], preferred_element_type=f32)
    rank1 = jnp.sum(jnp.where(hit1, before, 0.0), axis=0, keepdims=True)
    rank2 = jnp.sum(jnp.where(hit2, before, 0.0), axis=0, keepdims=True)
    rrow = lax.broadcasted_iota(jnp.int32, (ROUTE_ROWS, tm), 0)
    route = jnp.zeros((ROUTE_ROWS, tm), f32)
    for k, val in enumerate((e1, e2, g1, g2, rank1, rank2)):
        route = jnp.where(rrow == k, val, route)
    return route, jnp.sum(onehot, axis=1, keepdims=True)


def _mixer_tail_kernel(x_ref, att_ref, rest_ref, prevc_ref, prevx_ref, nextc_ref, nextx_ref, meta_ref,
                       convw_ref, convb_ref, wap_ref, wcp_ref, wout_ref, g_ref, b_ref, wr_ref, before_ref,
                       h_ref, hp_ref, route_ref, cnt_ref, merged_scr, *, tiles_per_seq, n_tiles):
    i = pl.program_id(0)
    pos = jnp.minimum(i, n_tiles - 1) % tiles_per_seq
    tm = x_ref.shape[0]
    c2, c0, c1, c3, c4 = (k * D_MODEL for k in range(5))
    cols = lambda j, base=0: slice(base + j * TAIL_CW, base + (j + 1) * TAIL_CW)
    chunks = range(D_MODEL // TAIL_CW)

    @pl.when(i == 0)
    def _():
        merged_scr[...] = jnp.zeros_like(merged_scr)

    def u_of(ref, rws, j):
        return ref[rws, cols(j, c0)].astype(f32) * ref[rws, cols(j, c1)].astype(f32)

    def halo_u(cref, xref, r, j):
        return cref[r:r + 1, cols(j)].astype(f32) * xref[r:r + 1, cols(j)].astype(f32)

    last = prevc_ref.shape[0] - 1
    row = lax.broadcasted_iota(jnp.int32, (tm, TAIL_CW), 0)

    def conv_chunk(j):
        u = u_of(rest_ref, slice(0, tm), j)
        u_prev = jnp.where(pos == 0, u_of(meta_ref, slice(N_META - 1, N_META), j),
                           halo_u(prevc_ref, prevx_ref, last, j))
        u_next = halo_u(nextc_ref, nextx_ref, 0, j)
        u_next = jnp.where(pos == tiles_per_seq - 1, jnp.zeros_like(u_next), u_next)
        u_m1 = jnp.where(row == 0, u_prev, pltpu.roll(u, 1, axis=0))
        u_p1 = jnp.where(row == tm - 1, u_next, pltpu.roll(u, tm - 1, axis=0))
        cw = convw_ref[:, cols(j)]
        s = u_m1 * cw[0:1] + u * cw[1:2] + u_p1 * cw[2:3] + convb_ref[:, cols(j)]
        return (rest_ref[:, cols(j, c2)].astype(f32) * s).astype(bf16)

    mix = [jnp.dot(merged_scr[...], wout_ref[:, cols(j)], preferred_element_type=f32) for j in chunks]

    cv = None
    for j in chunks:
        part = jnp.dot(conv_chunk(j), wcp_ref[cols(j), :], preferred_element_type=f32)
        cv = part if cv is None else cv + part
    ap = jnp.dot(att_ref[...], wap_ref[...], preferred_element_type=f32)

    z = [ALPHA * x_ref[:, cols(j)] + mix[j] for j in chunks]
    mu = sum(jnp.sum(zj, axis=-1, keepdims=True) for zj in z) * (1.0 / D_MODEL)
    d = [zj - mu for zj in z]
    var = sum(jnp.sum(dj * dj, axis=-1, keepdims=True) for dj in d) * (1.0 / D_MODEL)
    rstd = lax.rsqrt(var + LN_EPS)
    logits = None
    hs = []
    for j in chunks:
        hj = d[j] * rstd * g_ref[:, cols(j)] + b_ref[:, cols(j)]
        h_ref[:, cols(j)] = hj
        hs.append(hj)
        part = jnp.dot(hj.astype(bf16), wr_ref[cols(j), :], preferred_element_type=f32)
        logits = part if logits is None else logits + part
    hp_ref[...] = _pack_halves(jnp.concatenate(hs, axis=1))

    for j in chunks:
        gate_a = jnp.tanh(rest_ref[:, cols(j, c3)]) + 1.0
        gate_c = jnp.tanh(rest_ref[:, cols(j, c4)]) + 1.0
        merged_scr[:, cols(j)] = gate_a * ap[:, cols(j)].astype(bf16) + gate_c * cv[:, cols(j)].astype(bf16)

    route, count = _route_tile(logits, before_ref)
    route_ref[...] = route
    cnt_ref[0] = jnp.broadcast_to(count, (N_EXPERTS, LANES))


def _mixer_tail(x, att, rest, rest_meta, p, t):
    n = x.shape[0]
    tm = TAIL_TM
    halo = 16
    hb = tm // halo
    n_halo = n // halo
    n_tiles = n // tm
    tiles_per_seq = t // tm
    before = jnp.asarray(np.triu(np.ones((tm, tm), np.float32), 1), dtype=bf16)
    consts = (rest_meta, p["conv_w"], p["conv_b"], p["w_att_proj"], p["w_conv_proj"], p["w_out"],
              p["ln1_g"], p["ln1_b"], p["w_router"], before)
    front = lambda i: jnp.minimum(i, n_tiles - 1)
    back = lambda i: jnp.maximum(i - 1, 0)
    prev_blk = lambda i: jnp.maximum(front(i) * hb - 1, 0)
    next_blk = lambda i: jnp.minimum((front(i) + 1) * hb, n_halo - 1)
    return pl.pallas_call(
        functools.partial(_mixer_tail_kernel, tiles_per_seq=tiles_per_seq, n_tiles=n_tiles),
        grid=(n_tiles + 1,),
        in_specs=[pl.BlockSpec((tm, D_MODEL), lambda i: (back(i), 0)),
                  pl.BlockSpec((tm, D_ATT), lambda i: (front(i), 0)),
                  pl.BlockSpec((tm, D_REST), lambda i: (front(i), 0)),
                  pl.BlockSpec((halo, D_CONV), lambda i: (prev_blk(i), 1)),
                  pl.BlockSpec((halo, D_CONV), lambda i: (prev_blk(i), 2)),
                  pl.BlockSpec((halo, D_CONV), lambda i: (next_blk(i), 1)),
                  pl.BlockSpec((halo, D_CONV), lambda i: (next_blk(i), 2))]
                 + [_full(c) for c in consts],
        out_specs=[pl.BlockSpec((tm, D_MODEL), lambda i: (back(i), 0)),
                   pl.BlockSpec((tm, D_PACKED), lambda i: (back(i), 0)),
                   pl.BlockSpec((ROUTE_ROWS, tm), lambda i: (0, back(i))),
                   pl.BlockSpec((1, N_EXPERTS, LANES), lambda i: (back(i), 0, 0))],
        out_shape=[jax.ShapeDtypeStruct((n, D_MODEL), f32),
                   jax.ShapeDtypeStruct((n, D_PACKED), jnp.uint32),
                   jax.ShapeDtypeStruct((ROUTE_ROWS, n), f32),
                   jax.ShapeDtypeStruct((n_tiles, N_EXPERTS, LANES), f32)],
        scratch_shapes=[pltpu.VMEM((tm, D_MODEL), bf16)],
        compiler_params=_cparams(1),
        name="mixer_tail",
    )(x, att, rest, rest, rest, rest, rest, *consts)


def _route(route, cnt, n, blk, min_blocks):
    n_tiles = n // TAIL_TM
    tile_cnt = cnt[:, :, 0].astype(jnp.int32)
    tile_off = jnp.cumsum(tile_cnt, axis=0) - tile_cnt
    counts = jnp.sum(tile_cnt, axis=0)
    padded = jnp.maximum((counts + blk - 1) // blk, min_blocks) * blk
    pend = jnp.cumsum(padded)
    pstart = pend - padded
    base = pstart[None, :] + tile_off
    lanes = jnp.arange(N_EXPERTS, dtype=jnp.int32)

    def rows_of(expert_row, rank_row):
        e = expert_row.astype(jnp.int32).reshape(n_tiles, TAIL_TM, 1)
        sel = jnp.sum(jnp.where(e == lanes, base[:, None, :], 0), axis=-1)
        return sel.reshape(n) + rank_row.astype(jnp.int32)

    dest0 = rows_of(route[0], route[4])
    dest1 = rows_of(route[1], route[5])

    n_blocks = (2 * n + N_EXPERTS * (blk - 1) + blk - 1) // blk + N_EXPERTS * min_blocks
    blk_start = jnp.arange(n_blocks, dtype=jnp.int32) * blk
    blk_expert = jnp.sum((pend[None, :] <= blk_start[:, None]).astype(jnp.int32), axis=1)
    blk_expert = jnp.minimum(blk_expert, N_EXPERTS - 1)
    blk_valid = jnp.clip((pstart + counts)[blk_expert] - blk_start, 0, blk)
    blk_valid = jnp.where(blk_start < pend[-1], blk_valid, 0).astype(jnp.int32)
    return dest0, dest1, blk_expert, blk_valid, n_blocks


def _sc_mesh():
    return plsc.VectorSubcoreMesh(core_axis_name="c", subcore_axis_name="s")


def _sc_dispatch(h, dest0, dest1, p_rows):
    n, d = h.shape
    per = n // SC_WORKERS
    assert n % (SC_WORKERS * SC_IDX_WIN) == 0

    @pl.kernel(out_type=jax.ShapeDtypeStruct((p_rows, d), h.dtype), mesh=_sc_mesh(),
               scratch_types=[pltpu.VMEM((2, SC_IDX_WIN), jnp.int32), pltpu.VMEM((SC_ROWS, d), h.dtype)])
    def k(h_hbm, d0_hbm, d1_hbm, xs_hbm, idx, buf):
        base = (lax.axis_index("c") * SC_SUBCORES + lax.axis_index("s")) * per

        @pl.loop(0, per // SC_IDX_WIN)
        def _(w):
            off = base + w * SC_IDX_WIN
            pltpu.sync_copy(d0_hbm.at[pl.ds(off, SC_IDX_WIN)], idx.at[0])
            pltpu.sync_copy(d1_hbm.at[pl.ds(off, SC_IDX_WIN)], idx.at[1])
            for r in range(SC_IDX_WIN // SC_ROWS):
                pltpu.sync_copy(h_hbm.at[pl.ds(off + r * SC_ROWS, SC_ROWS)], buf)
                pltpu.sync_copy(buf, xs_hbm.at[idx.at[0, pl.ds(r * SC_ROWS, SC_ROWS)]])
                pltpu.sync_copy(buf, xs_hbm.at[idx.at[1, pl.ds(r * SC_ROWS, SC_ROWS)]])

    return k(h, dest0, dest1)


def _sc_gather2(ys, dest0, dest1):
    n = dest0.shape[0]
    d = ys.shape[1]
    per = n // SC_WORKERS
    out = jax.ShapeDtypeStruct((n, d), ys.dtype)

    @pl.kernel(out_type=[out, out], mesh=_sc_mesh(),
               scratch_types=[pltpu.VMEM((2, SC_IDX_WIN), jnp.int32), pltpu.VMEM((SC_ROWS, d), ys.dtype)])
    def k(ys_hbm, d0_hbm, d1_hbm, y0_hbm, y1_hbm, idx, buf):
        base = (lax.axis_index("c") * SC_SUBCORES + lax.axis_index("s")) * per

        @pl.loop(0, per // SC_IDX_WIN)
        def _(w):
            off = base + w * SC_IDX_WIN
            pltpu.sync_copy(d0_hbm.at[pl.ds(off, SC_IDX_WIN)], idx.at[0])
            pltpu.sync_copy(d1_hbm.at[pl.ds(off, SC_IDX_WIN)], idx.at[1])
            for r in range(SC_IDX_WIN // SC_ROWS):
                for kk, y_hbm in enumerate((y0_hbm, y1_hbm)):
                    pltpu.sync_copy(ys_hbm.at[idx.at[kk, pl.ds(r * SC_ROWS, SC_ROWS)]], buf)
                    pltpu.sync_copy(buf, y_hbm.at[pl.ds(off + r * SC_ROWS, SC_ROWS)])

    return k(ys, dest0, dest1)


def _experts_kernel(be_ref, bv_ref, slot_ref, nxt_ref, xs_ref, wg_hbm, wu_hbm, wd_hbm, ys_ref, *rest, cast):
    if cast:
        wgo_hbm, wuo_hbm, wdo_hbm, wg_st, wu_st, wd_st, wg_bf, wu_bf, wd_bf, sems, out_sems = rest
    else:
        wg_st, wu_st, wd_st, sems = rest
    i = pl.program_id(0)
    valid = bv_ref[i]
    expert = be_ref[i]
    slot = slot_ref[i]
    run_start = jnp.logical_or(i == 0, expert != be_ref[jnp.maximum(i - 1, 0)])
    stages = (wg_st, wu_st, wd_st)

    def fetch(e, s):
        return [pltpu.make_async_copy(hbm.at[e], st.at[s], sems.at[s, k])
                for k, (hbm, st) in enumerate(zip((wg_hbm, wu_hbm, wd_hbm), stages))]

    def write_back(e):
        return [pltpu.make_async_copy(bf, hbm.at[e], out_sems.at[k])
                for k, (bf, hbm) in enumerate(zip((wg_bf, wu_bf, wd_bf), (wgo_hbm, wuo_hbm, wdo_hbm)))]

    @pl.when(i == 0)
    def _():
        for cp in fetch(expert, slot):
            cp.start()

    @pl.when(run_start)
    def _():
        if cast:
            @pl.when(i > 0)
            def _():
                for cp in write_back(expert):
                    cp.wait()
        for cp in fetch(expert, slot):
            cp.wait()
        if cast:
            for bf, st in zip((wg_bf, wu_bf, wd_bf), stages):
                bf[...] = st[slot].astype(bf16)
            for cp in write_back(expert):
                cp.start()

        @pl.when(nxt_ref[i] >= 0)
        def _():
            for cp in fetch(nxt_ref[i], 1 - slot):
                cp.start()

    @pl.when(valid > 0)
    def _():
        wg, wu, wd = (wg_bf[...], wu_bf[...], wd_bf[...]) if cast else (st[slot] for st in stages)
        row = lax.broadcasted_iota(jnp.int32, xs_ref.shape, 0)
        keep = row < valid
        lo, hi = _unpack_halves(xs_ref[...])
        x = jnp.concatenate([jnp.where(keep, lo, 0.0), jnp.where(keep, hi, 0.0)], axis=1).astype(bf16)
        g = jnp.dot(x, wg, preferred_element_type=f32)
        u = jnp.dot(x, wu, preferred_element_type=f32)
        hmid = (g * _sigmoid(g) * u).astype(bf16)
        ys_ref[...] = _pack_halves(jnp.dot(hmid, wd, preferred_element_type=f32))

    @pl.when(valid == 0)
    def _():
        ys_ref[...] = jnp.zeros_like(ys_ref)

    if cast:
        @pl.when(i == pl.num_programs(0) - 1)
        def _():
            for cp in write_back(expert):
                cp.wait()


def _experts(xs, blk_expert, blk_valid, wg, wu, wd, n_blocks, blk):
    cast = wg.dtype != bf16
    starts = jnp.concatenate([jnp.ones((1,), jnp.int32), (blk_expert[1:] != blk_expert[:-1]).astype(jnp.int32)])
    slot = (jnp.cumsum(starts) - 1) % 2
    idx = jnp.arange(n_blocks, dtype=jnp.int32)
    next_start = lax.cummin(jnp.where(starts > 0, idx, n_blocks)[::-1])[::-1]
    next_start = jnp.concatenate([next_start[1:], jnp.full((1,), n_blocks, jnp.int32)])
    nxt = jnp.where(next_start < n_blocks, blk_expert[jnp.minimum(next_start, n_blocks - 1)], -1)
    tile = pl.BlockSpec((blk, D_PACKED), lambda i, *_: (i, 0))
    any_spec = pl.BlockSpec(memory_space=pl.ANY)
    stage = [pltpu.VMEM((2,) + w.shape[1:], w.dtype) for w in (wg, wu, wd)]
    bf_scratch = [pltpu.VMEM(w.shape[1:], bf16) for w in (wg, wu, wd)]
    grid_spec = pltpu.PrefetchScalarGridSpec(
        num_scalar_prefetch=4,
        grid=(n_blocks,),
        in_specs=[tile, any_spec, any_spec, any_spec],
        out_specs=[tile] + ([any_spec] * 3 if cast else []),
        scratch_shapes=stage + (bf_scratch if cast else []) + [pltpu.SemaphoreType.DMA((2, 3))]
                       + ([pltpu.SemaphoreType.DMA((3,))] if cast else []),
    )
    out = pl.pallas_call(
        functools.partial(_experts_kernel, cast=cast),
        grid_spec=grid_spec,
        out_shape=[jax.ShapeDtypeStruct(xs.shape, jnp.uint32)]
                  + ([jax.ShapeDtypeStruct(w.shape, bf16) for w in (wg, wu, wd)] if cast else []),
        compiler_params=_cparams(1),
        name="experts",
    )(blk_expert, blk_valid, slot.astype(jnp.int32), nxt.astype(jnp.int32), xs, wg, wu, wd)
    return out[0], (tuple(out[1:]) if cast else (wg, wu, wd))


def _final_norm_kernel(h_ref, y0_ref, y1_ref, route_ref, g_ref, b_ref, o_ref):
    route_t = route_ref[...].T
    g0, g1 = route_t[:, 2:3], route_t[:, 3:4]
    lo0, hi0 = _unpack_halves(y0_ref[...])
    lo1, hi1 = _unpack_halves(y1_ref[...])
    ffn = jnp.concatenate([lo0 * g0 + lo1 * g1, hi0 * g0 + hi1 * g1], axis=1)
    o_ref[...] = _layer_norm(ALPHA * h_ref[...] + ffn, g_ref[...], b_ref[...])


def _final_norm(h, y0, y1, route, ln_g, ln_b):
    n = h.shape[0]
    ts = NORM_TS
    tile = pl.BlockSpec((ts, D_MODEL), lambda i: (i, 0))
    packed = pl.BlockSpec((ts, D_PACKED), lambda i: (i, 0))
    return pl.pallas_call(
        _final_norm_kernel,
        grid=(n // ts,),
        in_specs=[tile, packed, packed, pl.BlockSpec((ROUTE_ROWS, ts), lambda i: (0, i)),
                  _full(ln_g), _full(ln_b)],
        out_specs=tile,
        out_shape=jax.ShapeDtypeStruct((n, D_MODEL), f32),
        compiler_params=_cparams(1),
        name="final_norm",
    )(h, y0, y1, route, ln_g, ln_b)


def _encode(x, p, expert_w, qkv_meta, rest_meta, slabs):
    bsz, t, _ = x.shape
    n = bsz * t
    assert t % NORM_TS == 0 and t % TAIL_TM == 0 and t % PROJ_TM == 0
    xf = x.reshape(n, D_MODEL)
    qkv, rest = _in_proj(xf, p["w_in"], PROJ_TM)
    att = _attention(qkv, qkv_meta, slabs, bsz, t)
    h1, h1_packed, route, cnt = _mixer_tail(xf, att, rest, rest_meta, p, t)
    blk = MOE_BLK
    emits = expert_w[0].dtype != bf16
    dest0, dest1, blk_expert, blk_valid, n_blocks = _route(route, cnt, n, blk, 1 if emits else 0)
    xs = _sc_dispatch(h1_packed, dest0, dest1, n_blocks * blk)
    ys, expert_w = _experts(xs, blk_expert, blk_valid, *expert_w, n_blocks, blk)
    y0, y1 = _sc_gather2(ys, dest0, dest1)
    y = _final_norm(h1, y0, y1, route, p["ln2_g"], p["ln2_b"])
    return y.reshape(bsz, t, D_MODEL), expert_w


def kernel(x_prompt, x_sample, meta_tokens, w_in, rpb, conv_w, conv_b, w_att_proj, w_conv_proj, w_out,
           ln1_g, ln1_b, w_router_group, w_router_expert, w_e_gate, w_e_up, w_e_down, ln2_g, ln2_b):
    col_scale = np.ones((1, D_IN_PROJ), np.float32)
    col_scale[:, D_IN_PROJ - 2 * D_MODEL:] = 0.5
    w_in_bf = (w_in[0] * col_scale).astype(bf16)
    w_router = jnp.concatenate([w_router_expert[0], w_router_group[0]], axis=1)
    w_router = jnp.pad(w_router, ((0, 0), (0, LANES - w_router.shape[1])))
    row = lambda v: v[0].reshape(1, -1).astype(f32)
    p = {
        "w_in": w_in_bf,
        "conv_w": conv_w[0].astype(f32), "conv_b": row(conv_b),
        "w_att_proj": w_att_proj[0].astype(bf16), "w_conv_proj": w_conv_proj[0].astype(bf16),
        "w_out": (0.5 * w_out[0]).astype(bf16), "ln1_g": row(ln1_g), "ln1_b": row(ln1_b),
        "w_router": w_router.astype(bf16),
        "ln2_g": row(ln2_g), "ln2_b": row(ln2_b),
    }
    qkv_meta, rest_meta = _in_proj(meta_tokens.astype(f32), p["w_in"], N_META)
    slabs = _bias_pieces(rpb[0])
    expert_w = (w_e_gate[0], w_e_up[0], w_e_down[0])
    y_prompt, expert_w = _encode(x_prompt, p, expert_w, qkv_meta, rest_meta, slabs)
    y_sample, _ = _encode(x_sample, p, expert_w, qkv_meta, rest_meta, slabs)
    return (y_prompt, y_sample)
```

```python
import functools

import numpy as np
import jax
import jax.numpy as jnp
from jax import lax
from jax.experimental import pallas as pl
from jax.experimental.pallas import tpu as pltpu
from jax.experimental.pallas import tpu_sc as plsc

D_MODEL = 1024
N_META = 16
GRID_W = 64
WIN_H = 8
WIN_W = 16
N_HEADS = 8
HEAD_DIM = 64
D_ATT = N_HEADS * HEAD_DIM
D_CONV = D_MODEL
N_GROUPS = 4
EXPERTS_PER_GROUP = 8
N_EXPERTS = N_GROUPS * EXPERTS_PER_GROUP
D_EXPERT = D_MODEL // 2
DEPTH = 1
ALPHA = (2.0 * DEPTH) ** 0.25
LN_EPS = 1e-5
D_QKV = 4 * D_ATT
LOG2E = 1.4426950408889634
D_REST = 3 * D_CONV + 2 * D_MODEL
D_IN_PROJ = 3 * D_ATT + D_REST

LANES = 128
HEADS_PER_VREG = LANES // HEAD_DIM
N_HEAD_PAIRS = N_HEADS // HEADS_PER_VREG
NEG_BIG = -1e30

PROJ_TM = 512
PROJ_TN = 512
ATT_ROWS = 16
ATT_GROUP = 4
SLAB_ROWS = WIN_H + 1
ATT_KEYS = SLAB_ROWS * GRID_W + N_META
N_DY = 2 * WIN_H - 1
PIECE_MT = N_DY - 1
PIECE_T0 = PIECE_MT + WIN_H
PIECE_M0 = PIECE_T0 + WIN_H
N_BIAS_PIECES = PIECE_M0 + 1
TAIL_TM = 512
TAIL_CW = 256
NORM_TS = 1024
MOE_BLK = 512
ROUTE_ROWS = 8
VMEM_LIMIT = 56 * 1024 * 1024

SC_CORES = 2
SC_SUBCORES = 16
SC_WORKERS = SC_CORES * SC_SUBCORES
SC_IDX_WIN = 128
SC_ROWS = 64
D_PACKED = D_MODEL // 2

bf16 = jnp.bfloat16
f32 = jnp.float32


def _cparams(n_axes):
    return pltpu.CompilerParams(dimension_semantics=("arbitrary",) * n_axes,
                                vmem_limit_bytes=VMEM_LIMIT)


def _full(a):
    return pl.BlockSpec(a.shape, lambda *_: (0,) * a.ndim)


def _pack_halves(x):
    half = x.shape[1] // 2
    bits = lambda v: lax.bitcast_convert_type(v.astype(bf16).astype(f32), jnp.uint32)
    return (bits(x[:, :half]) >> 16) | bits(x[:, half:])


def _unpack_halves(p):
    lo = lax.bitcast_convert_type(p << 16, f32)
    hi = lax.bitcast_convert_type(p & jnp.uint32(0xFFFF0000), f32)
    return lo, hi


def _in_proj_kernel(x_ref, w_ref, qkv_ref, rest_ref):
    assert PROJ_TN == D_ATT
    xb = x_ref[...].astype(bf16)
    for c in range(0, D_IN_PROJ, PROJ_TN):
        y = jnp.dot(xb, w_ref[:, c:c + PROJ_TN], preferred_element_type=f32)
        if c == 0:
            y = (y * (HEAD_DIM ** -0.5 * LOG2E)).astype(bf16)
            lane = lax.broadcasted_iota(jnp.int32, y.shape, 1)
            even = (lane & HEAD_DIM) == 0
            zero = jnp.zeros_like(y)
            qkv_ref[:, :D_ATT] = jnp.where(even, y, zero)
            qkv_ref[:, D_ATT:2 * D_ATT] = jnp.where(even, zero, y)
        elif c < 3 * D_ATT:
            qkv_ref[:, c + D_ATT:c + D_ATT + PROJ_TN] = y.astype(bf16)
        else:
            rest_ref[:, c - 3 * D_ATT:c - 3 * D_ATT + PROJ_TN] = y.astype(bf16)


def _in_proj(x, w_bf, tm):
    n = x.shape[0]
    return pl.pallas_call(
        _in_proj_kernel,
        grid=(n // tm,),
        in_specs=[pl.BlockSpec((tm, D_MODEL), lambda i: (i, 0)),
                  pl.BlockSpec((D_MODEL, D_IN_PROJ), lambda i: (0, 0))],
        out_specs=[pl.BlockSpec((tm, D_QKV), lambda i: (i, 0)),
                   pl.BlockSpec((tm, D_REST), lambda i: (i, 0))],
        out_shape=[jax.ShapeDtypeStruct((n, D_QKV), bf16),
                   jax.ShapeDtypeStruct((n, D_REST), bf16)],
        compiler_params=_cparams(1),
        name="in_proj",
    )(x, w_bf)


def _bias_pieces(rpb):
    qc = np.arange(GRID_W)[:, None]
    kc = np.arange(GRID_W)[None, :]
    w_start = np.clip(qc - WIN_W // 2, 0, GRID_W - WIN_W)
    valid = (kc >= w_start) & (kc < w_start + WIN_W)
    n_dx = 2 * WIN_W - 1
    onehot = (kc - qc + WIN_W - 1)[None] == np.arange(n_dx)[:, None, None]
    t = jnp.einsum("hyx,xqk->hyqk", rpb.astype(f32) * LOG2E, jnp.asarray(onehot & valid[None], f32),
                   precision=lax.Precision.HIGHEST)
    t = jnp.where(jnp.asarray(valid)[None, None], t, NEG_BIG)
    t = t.reshape(N_HEAD_PAIRS, HEADS_PER_VREG, N_DY, GRID_W, GRID_W)
    t = jnp.concatenate([t[:, p] for p in range(HEADS_PER_VREG)], axis=2)
    masked = jnp.full((N_HEAD_PAIRS, WIN_H, HEADS_PER_VREG * GRID_W, GRID_W), NEG_BIG, f32)
    zeros = jnp.zeros_like(masked)
    pieces = jnp.concatenate([
        jnp.concatenate([t[:, :N_DY - 1], t[:, 1:]], axis=-1),
        jnp.concatenate([masked, t[:, :WIN_H]], axis=-1),
        jnp.concatenate([t[:, WIN_H - 1:], zeros], axis=-1),
        jnp.concatenate([masked[:, :1], zeros[:, :1]], axis=-1),
    ], axis=1)
    assert pieces.shape[1] == N_BIAS_PIECES
    return pieces


def _attention_kernel(qe_ref, qo_ref, k_ref, v_ref, km_ref, vm_ref, bias_ref, o_ref, *, rows):
    j = pl.program_id(1)
    tq2 = 2 * GRID_W
    lane = lax.broadcasted_iota(jnp.int32, (tq2, LANES), 1)
    low = lane < HEAD_DIM
    nt = (((1,), (1,)), ((), ()))
    slab_keys = SLAB_ROWS * GRID_W

    def row_bias(pair, dy0, off):
        first = jnp.where(off == 0, dy0, PIECE_MT + dy0)
        mids = [dy0 + 2 * p - off for p in range(1, SLAB_ROWS // 2)]
        last = jnp.where(off == 0, PIECE_M0, PIECE_T0 + dy0)
        tail = ATT_KEYS - (SLAB_ROWS // 2) * LANES
        return jnp.concatenate([bias_ref[pair, first]] + [bias_ref[pair, m] for m in mids]
                               + [bias_ref[pair, last][:, :tail]], axis=1)

    def scores(ip, pair):
        r0 = j * ATT_ROWS + 2 * ip
        rs = [jnp.clip(r0 + a - WIN_H // 2, 0, rows - WIN_H) for a in range(2)]
        us = jnp.minimum(rs[0], rows - SLAB_ROWS)
        q0 = pl.multiple_of(ip * tq2, tq2)
        k0 = pl.multiple_of(us * GRID_W, GRID_W)
        cs = slice(pair * LANES, (pair + 1) * LANES)
        qq = jnp.concatenate([qe_ref[pl.ds(q0, tq2), cs], qo_ref[pl.ds(q0, tq2), cs]], axis=0)
        k2 = jnp.concatenate([k_ref[pl.ds(k0, slab_keys), cs], km_ref[:, cs]], axis=0)
        s = lax.dot_general(qq, k2, nt, preferred_element_type=f32)
        b = [row_bias(pair, rs[a] - (r0 + a) + (WIN_H - 1), rs[a] - us) for a in range(2)]
        s = s + jnp.concatenate([b[0][:GRID_W], b[1][:GRID_W], b[0][GRID_W:], b[1][GRID_W:]], axis=0)
        return s, jnp.max(s, axis=-1, keepdims=True), k0, q0, cs

    def weights(state):
        s, m, k0, q0, cs = state
        v2 = jnp.concatenate([v_ref[pl.ds(k0, slab_keys), cs], vm_ref[:, cs]], axis=0)
        e = jnp.exp2(s - m)
        l = jnp.sum(e, axis=-1, keepdims=True)
        return jnp.dot(e.astype(bf16), v2, preferred_element_type=f32), l, q0, cs

    def finish(state):
        o2, l, q0, cs = state
        o2 = o2 / l
        o = jnp.where(low, o2[:tq2], o2[tq2:])
        o_ref[pl.ds(q0, tq2), cs] = o.astype(bf16)

    def group_body(g, carry):
        items = [(g * ATT_GROUP + ip, pair) for ip in range(ATT_GROUP) for pair in range(N_HEAD_PAIRS)]
        a, b = {}, {}
        for step in range(len(items) + 2):
            if step < len(items):
                a[step] = scores(*items[step])
            if 0 <= step - 1 < len(items):
                b[step - 1] = weights(a.pop(step - 1))
            if 0 <= step - 2 < len(items):
                finish(b.pop(step - 2))
        return carry

    lax.fori_loop(0, ATT_ROWS // (2 * ATT_GROUP), group_body, 0)


def _attention(qkv, qkv_meta, slabs, bsz, t):
    rows = t // GRID_W
    assert rows >= 2 * WIN_H and rows % ATT_ROWS == 0 and ATT_ROWS % (2 * ATT_GROUP) == 0
    steps = rows // ATT_ROWS
    tq = ATT_ROWS * GRID_W
    return pl.pallas_call(
        functools.partial(_attention_kernel, rows=rows),
        grid=(bsz, steps),
        in_specs=[pl.BlockSpec((tq, D_ATT), lambda b, j: (b * steps + j, 0)),
                  pl.BlockSpec((tq, D_ATT), lambda b, j: (b * steps + j, 1)),
                  pl.BlockSpec((t, D_ATT), lambda b, j: (b, 2)),
                  pl.BlockSpec((t, D_ATT), lambda b, j: (b, 3)),
                  pl.BlockSpec((N_META, D_ATT), lambda b, j: (0, 2)),
                  pl.BlockSpec((N_META, D_ATT), lambda b, j: (0, 3)),
                  _full(slabs)],
        out_specs=pl.BlockSpec((tq, D_ATT), lambda b, j: (b * steps + j, 0)),
        out_shape=jax.ShapeDtypeStruct((bsz * t, D_ATT), bf16),
        compiler_params=_cparams(2),
        name="attention",
    )(qkv, qkv, qkv, qkv, qkv_meta, qkv_meta, slabs)


def _layer_norm(z, g, b):
    mu = jnp.mean(z, axis=-1, keepdims=True)
    d = z - mu
    var = jnp.mean(d * d, axis=-1, keepdims=True)
    return d * lax.rsqrt(var + LN_EPS) * g + b


def _sigmoid(x):
    return 0.5 * jnp.tanh(0.5 * x) + 0.5


def _route_tile(logits, before_ref):
    tm = logits.shape[0]
    lt = logits.T
    el = lt[:N_EXPERTS]
    gl = lt[N_EXPERTS:N_EXPERTS + N_GROUPS]
    neg = -jnp.inf
    erow = lax.broadcasted_iota(jnp.int32, el.shape, 0).astype(f32)
    grow = lax.broadcasted_iota(jnp.int32, gl.shape, 0).astype(f32)
    first = lambda hit, idx, n: jnp.min(jnp.where(hit, idx, float(n)), axis=0, keepdims=True)

    gmax = jnp.max(gl, axis=0, keepdims=True)
    grp = first(gl == gmax, grow, N_GROUPS)
    pg_sel = 1.0 / jnp.sum(jnp.exp(gl - gmax), axis=0, keepdims=True)

    e_lo = grp * EXPERTS_PER_GROUP
    elm = jnp.where((erow >= e_lo) & (erow < e_lo + EXPERTS_PER_GROUP), el, neg)
    t1 = jnp.max(elm, axis=0, keepdims=True)
    e1 = first(elm == t1, erow, N_EXPERTS)
    el2 = jnp.where(erow == e1, neg, elm)
    t2 = jnp.max(el2, axis=0, keepdims=True)
    e2 = first(el2 == t2, erow, N_EXPERTS)
    r = jnp.exp(t2 - t1)
    g1 = pg_sel / (1.0 + r)
    g2 = pg_sel * r / (1.0 + r)

    hit1 = erow == e1
    hit2 = erow == e2
    onehot = jnp.where(hit1 | hit2, 1.0, 0.0)
    before = jnp.dot(onehot.astype(bf16), before_ref[...], preferred_element_type=f32)
    rank1 = jnp.sum(jnp.where(hit1, before, 0.0), axis=0, keepdims=True)
    rank2 = jnp.sum(jnp.where(hit2, before, 0.0), axis=0, keepdims=True)
    rrow = lax.broadcasted_iota(jnp.int32, (ROUTE_ROWS, tm), 0)
    route = jnp.zeros((ROUTE_ROWS, tm), f32)
    for k, val in enumerate((e1, e2, g1, g2, rank1, rank2)):
        route = jnp.where(rrow == k, val, route)
    return route, jnp.sum(onehot, axis=1, keepdims=True)


def _mixer_tail_kernel(x_ref, att_ref, rest_ref, prevc_ref, prevx_ref, nextc_ref, nextx_ref, meta_ref,
                       convw_ref, convb_ref, wap_ref, wcp_ref, wout_ref, g_ref, b_ref, wr_ref, before_ref,
                       h_ref, hp_ref, route_ref, cnt_ref, merged_scr, *, tiles_per_seq, n_tiles):
    i = pl.program_id(0)
    pos = jnp.minimum(i, n_tiles - 1) % tiles_per_seq
    tm = x_ref.shape[0]
    c2, c0, c1, c3, c4 = (k * D_MODEL for k in range(5))
    cols = lambda j, base=0: slice(base + j * TAIL_CW, base + (j + 1) * TAIL_CW)
    chunks = range(D_MODEL // TAIL_CW)

    @pl.when(i == 0)
    def _():
        merged_scr[...] = jnp.zeros_like(merged_scr)

    def u_of(ref, rws, j):
        return ref[rws, cols(j, c0)].astype(f32) * ref[rws, cols(j, c1)].astype(f32)

    def halo_u(cref, xref, r, j):
        return cref[r:r + 1, cols(j)].astype(f32) * xref[r:r + 1, cols(j)].astype(f32)

    last = prevc_ref.shape[0] - 1
    row = lax.broadcasted_iota(jnp.int32, (tm, TAIL_CW), 0)

    def conv_chunk(j):
        u = u_of(rest_ref, slice(0, tm), j)
        u_prev = jnp.where(pos == 0, u_of(meta_ref, slice(N_META - 1, N_META), j),
                           halo_u(prevc_ref, prevx_ref, last, j))
        u_next = halo_u(nextc_ref, nextx_ref, 0, j)
        u_next = jnp.where(pos == tiles_per_seq - 1, jnp.zeros_like(u_next), u_next)
        u_m1 = jnp.where(row == 0, u_prev, pltpu.roll(u, 1, axis=0))
        u_p1 = jnp.where(row == tm - 1, u_next, pltpu.roll(u, tm - 1, axis=0))
        cw = convw_ref[:, cols(j)]
        s = u_m1 * cw[0:1] + u * cw[1:2] + u_p1 * cw[2:3] + convb_ref[:, cols(j)]
        return (rest_ref[:, cols(j, c2)].astype(f32) * s).astype(bf16)

    mix = [jnp.dot(merged_scr[...], wout_ref[:, cols(j)], preferred_element_type=f32) for j in chunks]

    cv = None
    for j in chunks:
        part = jnp.dot(conv_chunk(j), wcp_ref[cols(j), :], preferred_element_type=f32)
        cv = part if cv is None else cv + part
    ap = jnp.dot(att_ref[...], wap_ref[...], preferred_element_type=f32)

    z = [ALPHA * x_ref[:, cols(j)] + mix[j] for j in chunks]
    mu = sum(jnp.sum(zj, axis=-1, keepdims=True) for zj in z) * (1.0 / D_MODEL)
    d = [zj - mu for zj in z]
    var = sum(jnp.sum(dj * dj, axis=-1, keepdims=True) for dj in d) * (1.0 / D_MODEL)
    rstd = lax.rsqrt(var + LN_EPS)
    logits = None
    hs = []
    for j in chunks:
        hj = d[j] * rstd * g_ref[:, cols(j)] + b_ref[:, cols(j)]
        h_ref[:, cols(j)] = hj
        hs.append(hj)
        part = jnp.dot(hj.astype(bf16), wr_ref[cols(j), :], preferred_element_type=f32)
        logits = part if logits is None else logits + part
    hp_ref[...] = _pack_halves(jnp.concatenate(hs, axis=1))

    for j in chunks:
        gate_a = jnp.tanh(rest_ref[:, cols(j, c3)]) + 1.0
        gate_c = jnp.tanh(rest_ref[:, cols(j, c4)]) + 1.0
        merged_scr[:, cols(j)] = gate_a * ap[:, cols(j)].astype(bf16) + gate_c * cv[:, cols(j)].astype(bf16)

    route, count = _route_tile(logits, before_ref)
    route_ref[...] = route
    cnt_ref[0] = jnp.broadcast_to(count, (N_EXPERTS, LANES))


def _mixer_tail(x, att, rest, rest_meta, p, t):
    n = x.shape[0]
    tm = TAIL_TM
    halo = 16
    hb = tm // halo
    n_halo = n // halo
    n_tiles = n // tm
    tiles_per_seq = t // tm
    before = jnp.asarray(np.triu(np.ones((tm, tm), np.float32), 1), dtype=bf16)
    consts = (rest_meta, p["conv_w"], p["conv_b"], p["w_att_proj"], p["w_conv_proj"], p["w_out"],
              p["ln1_g"], p["ln1_b"], p["w_router"], before)
    front = lambda i: jnp.minimum(i, n_tiles - 1)
    back = lambda i: jnp.maximum(i - 1, 0)
    prev_blk = lambda i: jnp.maximum(front(i) * hb - 1, 0)
    next_blk = lambda i: jnp.minimum((front(i) + 1) * hb, n_halo - 1)
    return pl.pallas_call(
        functools.partial(_mixer_tail_kernel, tiles_per_seq=tiles_per_seq, n_tiles=n_tiles),
        grid=(n_tiles + 1,),
        in_specs=[pl.BlockSpec((tm, D_MODEL), lambda i: (back(i), 0)),
                  pl.BlockSpec((tm, D_ATT), lambda i: (front(i), 0)),
                  pl.BlockSpec((tm, D_REST), lambda i: (front(i), 0)),
                  pl.BlockSpec((halo, D_CONV), lambda i: (prev_blk(i), 1)),
                  pl.BlockSpec((halo, D_CONV), lambda i: (prev_blk(i), 2)),
                  pl.BlockSpec((halo, D_CONV), lambda i: (next_blk(i), 1)),
                  pl.BlockSpec((halo, D_CONV), lambda i: (next_blk(i), 2))]
                 + [_full(c) for c in consts],
        out_specs=[pl.BlockSpec((tm, D_MODEL), lambda i: (back(i), 0)),
                   pl.BlockSpec((tm, D_PACKED), lambda i: (back(i), 0)),
                   pl.BlockSpec((ROUTE_ROWS, tm), lambda i: (0, back(i))),
                   pl.BlockSpec((1, N_EXPERTS, LANES), lambda i: (back(i), 0, 0))],
        out_shape=[jax.ShapeDtypeStruct((n, D_MODEL), f32),
                   jax.ShapeDtypeStruct((n, D_PACKED), jnp.uint32),
                   jax.ShapeDtypeStruct((ROUTE_ROWS, n), f32),
                   jax.ShapeDtypeStruct((n_tiles, N_EXPERTS, LANES), f32)],
        scratch_shapes=[pltpu.VMEM((tm, D_MODEL), bf16)],
        compiler_params=_cparams(1),
        name="mixer_tail",
    )(x, att, rest, rest, rest, rest, rest, *consts)


def _route(route, cnt, n, blk):
    n_tiles = n // TAIL_TM
    tile_cnt = cnt[:, :, 0].astype(jnp.int32)
    tile_off = jnp.cumsum(tile_cnt, axis=0) - tile_cnt
    counts = jnp.sum(tile_cnt, axis=0)
    padded = (counts + blk - 1) // blk * blk
    pend = jnp.cumsum(padded)
    pstart = pend - padded
    base = pstart[None, :] + tile_off
    lanes = jnp.arange(N_EXPERTS, dtype=jnp.int32)

    def rows_of(expert_row, rank_row):
        e = expert_row.astype(jnp.int32).reshape(n_tiles, TAIL_TM, 1)
        sel = jnp.sum(jnp.where(e == lanes, base[:, None, :], 0), axis=-1)
        return sel.reshape(n) + rank_row.astype(jnp.int32)

    dest0 = rows_of(route[0], route[4])
    dest1 = rows_of(route[1], route[5])

    n_blocks = (2 * n + N_EXPERTS * (blk - 1) + blk - 1) // blk
    blk_start = jnp.arange(n_blocks, dtype=jnp.int32) * blk
    blk_expert = jnp.sum((pend[None, :] <= blk_start[:, None]).astype(jnp.int32), axis=1)
    blk_expert = jnp.minimum(blk_expert, N_EXPERTS - 1)
    blk_valid = jnp.clip((pstart + counts)[blk_expert] - blk_start, 0, blk)
    blk_valid = jnp.where(blk_start < pend[-1], blk_valid, 0).astype(jnp.int32)
    return dest0, dest1, blk_expert, blk_valid, n_blocks


def _sc_mesh():
    return plsc.VectorSubcoreMesh(core_axis_name="c", subcore_axis_name="s")


def _sc_dispatch(h, dest0, dest1, p_rows):
    n, d = h.shape
    per = n // SC_WORKERS
    assert n % (SC_WORKERS * SC_IDX_WIN) == 0

    @pl.kernel(out_type=jax.ShapeDtypeStruct((p_rows, d), h.dtype), mesh=_sc_mesh(),
               scratch_types=[pltpu.VMEM((2, SC_IDX_WIN), jnp.int32), pltpu.VMEM((SC_ROWS, d), h.dtype)])
    def k(h_hbm, d0_hbm, d1_hbm, xs_hbm, idx, buf):
        base = (lax.axis_index("c") * SC_SUBCORES + lax.axis_index("s")) * per

        @pl.loop(0, per // SC_IDX_WIN)
        def _(w):
            off = base + w * SC_IDX_WIN
            pltpu.sync_copy(d0_hbm.at[pl.ds(off, SC_IDX_WIN)], idx.at[0])
            pltpu.sync_copy(d1_hbm.at[pl.ds(off, SC_IDX_WIN)], idx.at[1])
            for r in range(SC_IDX_WIN // SC_ROWS):
                pltpu.sync_copy(h_hbm.at[pl.ds(off + r * SC_ROWS, SC_ROWS)], buf)
                pltpu.sync_copy(buf, xs_hbm.at[idx.at[0, pl.ds(r * SC_ROWS, SC_ROWS)]])
                pltpu.sync_copy(buf, xs_hbm.at[idx.at[1, pl.ds(r * SC_ROWS, SC_ROWS)]])

    return k(h, dest0, dest1)


def _sc_gather2(ys, dest0, dest1):
    n = dest0.shape[0]
    d = ys.shape[1]
    per = n // SC_WORKERS
    out = jax.ShapeDtypeStruct((n, d), ys.dtype)

    @pl.kernel(out_type=[out, out], mesh=_sc_mesh(),
               scratch_types=[pltpu.VMEM((2, SC_IDX_WIN), jnp.int32), pltpu.VMEM((SC_ROWS, d), ys.dtype)])
    def k(ys_hbm, d0_hbm, d1_hbm, y0_hbm, y1_hbm, idx, buf):
        base = (lax.axis_index("c") * SC_SUBCORES + lax.axis_index("s")) * per

        @pl.loop(0, per // SC_IDX_WIN)
        def _(w):
            off = base + w * SC_IDX_WIN
            pltpu.sync_copy(d0_hbm.at[pl.ds(off, SC_IDX_WIN)], idx.at[0])
            pltpu.sync_copy(d1_hbm.at[pl.ds(off, SC_IDX_WIN)], idx.at[1])
            for r in range(SC_IDX_WIN // SC_ROWS):
                for kk, y_hbm in enumerate((y0_hbm, y1_hbm)):
                    pltpu.sync_copy(ys_hbm.at[idx.at[kk, pl.ds(r * SC_ROWS, SC_ROWS)]], buf)
                    pltpu.sync_copy(buf, y_hbm.at[pl.ds(off + r * SC_ROWS, SC_ROWS)])

    return k(ys, dest0, dest1)


def _experts_kernel(be_ref, bv_ref, slot_ref, nxt_ref, xs_ref, wg_hbm, wu_hbm, wd_hbm, ys_ref,
                    wg_f32, wu_f32, wd_f32, wg_bf, wu_bf, wd_bf, sems):
    i = pl.program_id(0)
    valid = bv_ref[i]
    expert = be_ref[i]
    slot = slot_ref[i]
    run_start = jnp.logical_or(i == 0, expert != be_ref[jnp.maximum(i - 1, 0)])

    def fetch(e, s):
        return [pltpu.make_async_copy(hbm.at[e], buf.at[s], sems.at[s, k])
                for k, (hbm, buf) in enumerate(((wg_hbm, wg_f32), (wu_hbm, wu_f32), (wd_hbm, wd_f32)))]

    @pl.when(i == 0)
    def _():
        for cp in fetch(expert, slot):
            cp.start()

    @pl.when(run_start)
    def _():
        for cp in fetch(expert, slot):
            cp.wait()
        wg_bf[...] = wg_f32[slot].astype(bf16)
        wu_bf[...] = wu_f32[slot].astype(bf16)
        wd_bf[...] = wd_f32[slot].astype(bf16)

        @pl.when(nxt_ref[i] >= 0)
        def _():
            for cp in fetch(nxt_ref[i], 1 - slot):
                cp.start(priority=1)

    @pl.when(valid > 0)
    def _():
        row = lax.broadcasted_iota(jnp.int32, xs_ref.shape, 0)
        keep = row < valid
        lo, hi = _unpack_halves(xs_ref[...])
        x = jnp.concatenate([jnp.where(keep, lo, 0.0), jnp.where(keep, hi, 0.0)], axis=1).astype(bf16)
        g = jnp.dot(x, wg_bf[...], preferred_element_type=f32)
        u = jnp.dot(x, wu_bf[...], preferred_element_type=f32)
        hmid = (g * _sigmoid(g) * u).astype(bf16)
        ys_ref[...] = _pack_halves(jnp.dot(hmid, wd_bf[...], preferred_element_type=f32))

    @pl.when(valid == 0)
    def _():
        ys_ref[...] = jnp.zeros_like(ys_ref)


def _experts(xs, blk_expert, blk_valid, wg, wu, wd, n_blocks, blk):
    starts = jnp.concatenate([jnp.ones((1,), jnp.int32), (blk_expert[1:] != blk_expert[:-1]).astype(jnp.int32)])
    slot = (jnp.cumsum(starts) - 1) % 2
    idx = jnp.arange(n_blocks, dtype=jnp.int32)
    next_start = lax.cummin(jnp.where(starts > 0, idx, n_blocks)[::-1])[::-1]
    next_start = jnp.concatenate([next_start[1:], jnp.full((1,), n_blocks, jnp.int32)])
    nxt = jnp.where(next_start < n_blocks, blk_expert[jnp.minimum(next_start, n_blocks - 1)], -1)
    tile = pl.BlockSpec((blk, D_PACKED), lambda i, *_: (i, 0))
    any_spec = pl.BlockSpec(memory_space=pl.ANY)
    grid_spec = pltpu.PrefetchScalarGridSpec(
        num_scalar_prefetch=4,
        grid=(n_blocks,),
        in_specs=[tile, any_spec, any_spec, any_spec],
        out_specs=tile,
        scratch_shapes=[pltpu.VMEM((2, D_MODEL, D_EXPERT), f32),
                        pltpu.VMEM((2, D_MODEL, D_EXPERT), f32),
                        pltpu.VMEM((2, D_EXPERT, D_MODEL), f32),
                        pltpu.VMEM((D_MODEL, D_EXPERT), bf16),
                        pltpu.VMEM((D_MODEL, D_EXPERT), bf16),
                        pltpu.VMEM((D_EXPERT, D_MODEL), bf16),
                        pltpu.SemaphoreType.DMA((2, 3))],
    )
    return pl.pallas_call(
        _experts_kernel,
        grid_spec=grid_spec,
        out_shape=jax.ShapeDtypeStruct(xs.shape, jnp.uint32),
        compiler_params=_cparams(1),
        name="experts",
    )(blk_expert, blk_valid, slot.astype(jnp.int32), nxt.astype(jnp.int32), xs, wg, wu, wd)


def _final_norm_kernel(h_ref, y0_ref, y1_ref, route_ref, g_ref, b_ref, o_ref):
    route_t = route_ref[...].T
    g0, g1 = route_t[:, 2:3], route_t[:, 3:4]
    lo0, hi0 = _unpack_halves(y0_ref[...])
    lo1, hi1 = _unpack_halves(y1_ref[...])
    ffn = jnp.concatenate([lo0 * g0 + lo1 * g1, hi0 * g0 + hi1 * g1], axis=1)
    o_ref[...] = _layer_norm(ALPHA * h_ref[...] + ffn, g_ref[...], b_ref[...])


def _final_norm(h, y0, y1, route, ln_g, ln_b):
    n = h.shape[0]
    ts = NORM_TS
    tile = pl.BlockSpec((ts, D_MODEL), lambda i: (i, 0))
    packed = pl.BlockSpec((ts, D_PACKED), lambda i: (i, 0))
    return pl.pallas_call(
        _final_norm_kernel,
        grid=(n // ts,),
        in_specs=[tile, packed, packed, pl.BlockSpec((ROUTE_ROWS, ts), lambda i: (0, i)),
                  _full(ln_g), _full(ln_b)],
        out_specs=tile,
        out_shape=jax.ShapeDtypeStruct((n, D_MODEL), f32),
        compiler_params=_cparams(1),
        name="final_norm",
    )(h, y0, y1, route, ln_g, ln_b)


def _encode(x, p, qkv_meta, rest_meta, slabs):
    bsz, t, _ = x.shape
    n = bsz * t
    assert t % NORM_TS == 0 and t % TAIL_TM == 0 and t % PROJ_TM == 0
    xf = x.reshape(n, D_MODEL)
    qkv, rest = _in_proj(xf, p["w_in"], PROJ_TM)
    att = _attention(qkv, qkv_meta, slabs, bsz, t)
    h1, h1_packed, route, cnt = _mixer_tail(xf, att, rest, rest_meta, p, t)
    blk = MOE_BLK
    dest0, dest1, blk_expert, blk_valid, n_blocks = _route(route, cnt, n, blk)
    xs = _sc_dispatch(h1_packed, dest0, dest1, n_blocks * blk)
    ys = _experts(xs, blk_expert, blk_valid, p["w_e_gate"], p["w_e_up"], p["w_e_down"], n_blocks, blk)
    y0, y1 = _sc_gather2(ys, dest0, dest1)
    y = _final_norm(h1, y0, y1, route, p["ln2_g"], p["ln2_b"])
    return y.reshape(bsz, t, D_MODEL)


def kernel(x_prompt, x_sample, meta_tokens, w_in, rpb, conv_w, conv_b, w_att_proj, w_conv_proj, w_out,
           ln1_g, ln1_b, w_router_group, w_router_expert, w_e_gate, w_e_up, w_e_down, ln2_g, ln2_b):
    col_scale = np.ones((1, D_IN_PROJ), np.float32)
    col_scale[:, D_IN_PROJ - 2 * D_MODEL:] = 0.5
    w_in_bf = (w_in[0] * col_scale).astype(bf16)
    w_router = jnp.concatenate([w_router_expert[0], w_router_group[0]], axis=1)
    w_router = jnp.pad(w_router, ((0, 0), (0, LANES - w_router.shape[1])))
    row = lambda v: v[0].reshape(1, -1).astype(f32)
    p = {
        "w_in": w_in_bf,
        "conv_w": conv_w[0].astype(f32), "conv_b": row(conv_b),
        "w_att_proj": w_att_proj[0].astype(bf16), "w_conv_proj": w_conv_proj[0].astype(bf16),
        "w_out": (0.5 * w_out[0]).astype(bf16), "ln1_g": row(ln1_g), "ln1_b": row(ln1_b),
        "w_router": w_router.astype(bf16),
        "w_e_gate": w_e_gate[0], "w_e_up": w_e_up[0], "w_e_down": w_e_down[0],
        "ln2_g": row(ln2_g), "ln2_b": row(ln2_b),
    }
    qkv_meta, rest_meta = _in_proj(meta_tokens.astype(f32), p["w_in"], N_META)
    slabs = _bias_pieces(rpb[0])
    y_prompt = _encode(x_prompt, p, qkv_meta, rest_meta, slabs)
    y_sample = _encode(x_sample, p, qkv_meta, rest_meta, slabs)
    return (y_prompt, y_sample)
```

```python
import functools

import numpy as np
import jax
import jax.numpy as jnp
from jax import lax
from jax.experimental import pallas as pl
from jax.experimental.pallas import tpu as pltpu
from jax.experimental.pallas import tpu_sc as plsc

D_MODEL = 1024
N_META = 16
GRID_W = 64
WIN_H = 8
WIN_W = 16
N_HEADS = 8
HEAD_DIM = 64
D_ATT = N_HEADS * HEAD_DIM
D_CONV = D_MODEL
N_GROUPS = 4
EXPERTS_PER_GROUP = 8
N_EXPERTS = N_GROUPS * EXPERTS_PER_GROUP
D_EXPERT = D_MODEL // 2
DEPTH = 1
ALPHA = (2.0 * DEPTH) ** 0.25
LN_EPS = 1e-5
D_QKV = 4 * D_ATT
LOG2E = 1.4426950408889634
D_REST = 3 * D_CONV + 2 * D_MODEL
D_IN_PROJ = 3 * D_ATT + D_REST

LANES = 128
HEADS_PER_VREG = LANES // HEAD_DIM
N_HEAD_PAIRS = N_HEADS // HEADS_PER_VREG
NEG_BIG = -1e30

PROJ_TM = 1024
PROJ_TN = 512
ATT_ROWS = 16
ATT_GROUP = 4
SLAB_ROWS = WIN_H + 1
ATT_KEYS = SLAB_ROWS * GRID_W + N_META
N_DY = 2 * WIN_H - 1
PIECE_MT = N_DY - 1
PIECE_T0 = PIECE_MT + WIN_H
PIECE_M0 = PIECE_T0 + WIN_H
N_BIAS_PIECES = PIECE_M0 + 1
TAIL_TM = 512
TAIL_CW = 256
NORM_TS = 1024
MOE_BLK = 512
ROUTE_ROWS = 8
VMEM_LIMIT = 56 * 1024 * 1024

SC_CORES = 2
SC_SUBCORES = 16
SC_WORKERS = SC_CORES * SC_SUBCORES
SC_IDX_WIN = 128
SC_ROWS = 64
D_PACKED = D_MODEL // 2

bf16 = jnp.bfloat16
f32 = jnp.float32


def _cparams(n_axes):
    return pltpu.CompilerParams(dimension_semantics=("arbitrary",) * n_axes,
                                vmem_limit_bytes=VMEM_LIMIT)


def _full(a):
    return pl.BlockSpec(a.shape, lambda *_: (0,) * a.ndim)


def _pack_halves(x):
    half = x.shape[1] // 2
    bits = lambda v: lax.bitcast_convert_type(v.astype(bf16).astype(f32), jnp.uint32)
    return (bits(x[:, :half]) >> 16) | bits(x[:, half:])


def _unpack_halves(p):
    lo = lax.bitcast_convert_type(p << 16, f32)
    hi = lax.bitcast_convert_type(p & jnp.uint32(0xFFFF0000), f32)
    return lo, hi


def _in_proj_kernel(x_ref, w_ref, qkv_ref, rest_ref):
    assert PROJ_TN == D_ATT
    xb = x_ref[...].astype(bf16)
    for c in range(0, D_IN_PROJ, PROJ_TN):
        y = jnp.dot(xb, w_ref[:, c:c + PROJ_TN], preferred_element_type=f32)
        if c == 0:
            y = (y * (HEAD_DIM ** -0.5 * LOG2E)).astype(bf16)
            lane = lax.broadcasted_iota(jnp.int32, y.shape, 1)
            even = (lane & HEAD_DIM) == 0
            zero = jnp.zeros_like(y)
            qkv_ref[:, :D_ATT] = jnp.where(even, y, zero)
            qkv_ref[:, D_ATT:2 * D_ATT] = jnp.where(even, zero, y)
        elif c < 3 * D_ATT:
            qkv_ref[:, c + D_ATT:c + D_ATT + PROJ_TN] = y.astype(bf16)
        else:
            rest_ref[:, c - 3 * D_ATT:c - 3 * D_ATT + PROJ_TN] = y.astype(bf16)


def _in_proj(x, w_bf, tm):
    n = x.shape[0]
    return pl.pallas_call(
        _in_proj_kernel,
        grid=(n // tm,),
        in_specs=[pl.BlockSpec((tm, D_MODEL), lambda i: (i, 0)),
                  pl.BlockSpec((D_MODEL, D_IN_PROJ), lambda i: (0, 0))],
        out_specs=[pl.BlockSpec((tm, D_QKV), lambda i: (i, 0)),
                   pl.BlockSpec((tm, D_REST), lambda i: (i, 0))],
        out_shape=[jax.ShapeDtypeStruct((n, D_QKV), bf16),
                   jax.ShapeDtypeStruct((n, D_REST), bf16)],
        compiler_params=_cparams(1),
        name="in_proj",
    )(x, w_bf)


def _bias_pieces(rpb):
    qc = np.arange(GRID_W)[:, None]
    kc = np.arange(GRID_W)[None, :]
    w_start = np.clip(qc - WIN_W // 2, 0, GRID_W - WIN_W)
    valid = (kc >= w_start) & (kc < w_start + WIN_W)
    n_dx = 2 * WIN_W - 1
    onehot = (kc - qc + WIN_W - 1)[None] == np.arange(n_dx)[:, None, None]
    t = jnp.einsum("hyx,xqk->hyqk", rpb.astype(f32) * LOG2E, jnp.asarray(onehot & valid[None], f32),
                   precision=lax.Precision.HIGHEST)
    t = jnp.where(jnp.asarray(valid)[None, None], t, NEG_BIG)
    t = t.reshape(N_HEAD_PAIRS, HEADS_PER_VREG, N_DY, GRID_W, GRID_W)
    t = jnp.concatenate([t[:, p] for p in range(HEADS_PER_VREG)], axis=2)
    masked = jnp.full((N_HEAD_PAIRS, WIN_H, HEADS_PER_VREG * GRID_W, GRID_W), NEG_BIG, f32)
    zeros = jnp.zeros_like(masked)
    pieces = jnp.concatenate([
        jnp.concatenate([t[:, :N_DY - 1], t[:, 1:]], axis=-1),
        jnp.concatenate([masked, t[:, :WIN_H]], axis=-1),
        jnp.concatenate([t[:, WIN_H - 1:], zeros], axis=-1),
        jnp.concatenate([masked[:, :1], zeros[:, :1]], axis=-1),
    ], axis=1)
    assert pieces.shape[1] == N_BIAS_PIECES
    return pieces


def _attention_kernel(qe_ref, qo_ref, k_ref, v_ref, km_ref, vm_ref, bias_ref, o_ref, *, rows):
    j = pl.program_id(1)
    tq2 = 2 * GRID_W
    lane = lax.broadcasted_iota(jnp.int32, (tq2, LANES), 1)
    low = lane < HEAD_DIM
    nt = (((1,), (1,)), ((), ()))
    slab_keys = SLAB_ROWS * GRID_W

    def row_bias(pair, dy0, off):
        first = jnp.where(off == 0, dy0, PIECE_MT + dy0)
        mids = [dy0 + 2 * p - off for p in range(1, SLAB_ROWS // 2)]
        last = jnp.where(off == 0, PIECE_M0, PIECE_T0 + dy0)
        tail = ATT_KEYS - (SLAB_ROWS // 2) * LANES
        return jnp.concatenate([bias_ref[pair, first]] + [bias_ref[pair, m] for m in mids]
                               + [bias_ref[pair, last][:, :tail]], axis=1)

    def scores(ip, pair):
        r0 = j * ATT_ROWS + 2 * ip
        rs = [jnp.clip(r0 + a - WIN_H // 2, 0, rows - WIN_H) for a in range(2)]
        us = jnp.minimum(rs[0], rows - SLAB_ROWS)
        q0 = pl.multiple_of(ip * tq2, tq2)
        k0 = pl.multiple_of(us * GRID_W, GRID_W)
        cs = slice(pair * LANES, (pair + 1) * LANES)
        qq = jnp.concatenate([qe_ref[pl.ds(q0, tq2), cs], qo_ref[pl.ds(q0, tq2), cs]], axis=0)
        k2 = jnp.concatenate([k_ref[pl.ds(k0, slab_keys), cs], km_ref[:, cs]], axis=0)
        s = lax.dot_general(qq, k2, nt, preferred_element_type=f32)
        b = [row_bias(pair, rs[a] - (r0 + a) + (WIN_H - 1), rs[a] - us) for a in range(2)]
        s = s + jnp.concatenate([b[0][:GRID_W], b[1][:GRID_W], b[0][GRID_W:], b[1][GRID_W:]], axis=0)
        return s, jnp.max(s, axis=-1, keepdims=True), k0, q0, cs

    def weights(state):
        s, m, k0, q0, cs = state
        v2 = jnp.concatenate([v_ref[pl.ds(k0, slab_keys), cs], vm_ref[:, cs]], axis=0)
        e = jnp.exp2(s - m)
        l = jnp.sum(e, axis=-1, keepdims=True)
        return jnp.dot(e.astype(bf16), v2, preferred_element_type=f32), l, q0, cs

    def finish(state):
        o2, l, q0, cs = state
        o2 = o2 / l
        o = jnp.where(low, o2[:tq2], o2[tq2:])
        o_ref[pl.ds(q0, tq2), cs] = o.astype(bf16)

    def group_body(g, carry):
        items = [(g * ATT_GROUP + ip, pair) for ip in range(ATT_GROUP) for pair in range(N_HEAD_PAIRS)]
        a, b = {}, {}
        for step in range(len(items) + 2):
            if step < len(items):
                a[step] = scores(*items[step])
            if 0 <= step - 1 < len(items):
                b[step - 1] = weights(a.pop(step - 1))
            if 0 <= step - 2 < len(items):
                finish(b.pop(step - 2))
        return carry

    lax.fori_loop(0, ATT_ROWS // (2 * ATT_GROUP), group_body, 0)


def _attention(qkv, qkv_meta, slabs, bsz, t):
    rows = t // GRID_W
    assert rows >= 2 * WIN_H and rows % ATT_ROWS == 0 and ATT_ROWS % (2 * ATT_GROUP) == 0
    steps = rows // ATT_ROWS
    tq = ATT_ROWS * GRID_W
    return pl.pallas_call(
        functools.partial(_attention_kernel, rows=rows),
        grid=(bsz, steps),
        in_specs=[pl.BlockSpec((tq, D_ATT), lambda b, j: (b * steps + j, 0)),
                  pl.BlockSpec((tq, D_ATT), lambda b, j: (b * steps + j, 1)),
                  pl.BlockSpec((t, D_ATT), lambda b, j: (b, 2)),
                  pl.BlockSpec((t, D_ATT), lambda b, j: (b, 3)),
                  pl.BlockSpec((N_META, D_ATT), lambda b, j: (0, 2)),
                  pl.BlockSpec((N_META, D_ATT), lambda b, j: (0, 3)),
                  _full(slabs)],
        out_specs=pl.BlockSpec((tq, D_ATT), lambda b, j: (b * steps + j, 0)),
        out_shape=jax.ShapeDtypeStruct((bsz * t, D_ATT), bf16),
        compiler_params=_cparams(2),
        name="attention",
    )(qkv, qkv, qkv, qkv, qkv_meta, qkv_meta, slabs)


def _layer_norm(z, g, b):
    mu = jnp.mean(z, axis=-1, keepdims=True)
    d = z - mu
    var = jnp.mean(d * d, axis=-1, keepdims=True)
    return d * lax.rsqrt(var + LN_EPS) * g + b


def _sigmoid(x):
    return 0.5 * jnp.tanh(0.5 * x) + 0.5


def _route_tile(logits, before_ref):
    tm = logits.shape[0]
    lt = logits.T
    el = lt[:N_EXPERTS]
    gl = lt[N_EXPERTS:N_EXPERTS + N_GROUPS]
    neg = -jnp.inf
    erow = lax.broadcasted_iota(jnp.int32, el.shape, 0).astype(f32)
    grow = lax.broadcasted_iota(jnp.int32, gl.shape, 0).astype(f32)
    first = lambda hit, idx, n: jnp.min(jnp.where(hit, idx, float(n)), axis=0, keepdims=True)

    gmax = jnp.max(gl, axis=0, keepdims=True)
    grp = first(gl == gmax, grow, N_GROUPS)
    pg_sel = 1.0 / jnp.sum(jnp.exp(gl - gmax), axis=0, keepdims=True)

    e_lo = grp * EXPERTS_PER_GROUP
    elm = jnp.where((erow >= e_lo) & (erow < e_lo + EXPERTS_PER_GROUP), el, neg)
    t1 = jnp.max(elm, axis=0, keepdims=True)
    e1 = first(elm == t1, erow, N_EXPERTS)
    el2 = jnp.where(erow == e1, neg, elm)
    t2 = jnp.max(el2, axis=0, keepdims=True)
    e2 = first(el2 == t2, erow, N_EXPERTS)
    r = jnp.exp(t2 - t1)
    g1 = pg_sel / (1.0 + r)
    g2 = pg_sel * r / (1.0 + r)

    hit1 = erow == e1
    hit2 = erow == e2
    onehot = jnp.where(hit1 | hit2, 1.0, 0.0)
    before = jnp.dot(onehot.astype(bf16), before_ref[...], preferred_element_type=f32)
    rank1 = jnp.sum(jnp.where(hit1, before, 0.0), axis=0, keepdims=True)
    rank2 = jnp.sum(jnp.where(hit2, before, 0.0), axis=0, keepdims=True)
    rrow = lax.broadcasted_iota(jnp.int32, (ROUTE_ROWS, tm), 0)
    route = jnp.zeros((ROUTE_ROWS, tm), f32)
    for k, val in enumerate((e1, e2, g1, g2, rank1, rank2)):
        route = jnp.where(rrow == k, val, route)
    return route, jnp.sum(onehot, axis=1, keepdims=True)


def _mixer_tail_kernel(x_ref, att_ref, rest_ref, prevc_ref, prevx_ref, nextc_ref, nextx_ref, meta_ref,
                       convw_ref, convb_ref, wap_ref, wcp_ref, wout_ref, g_ref, b_ref, wr_ref, before_ref,
                       h_ref, hp_ref, route_ref, cnt_ref, merged_scr, *, tiles_per_seq, n_tiles):
    i = pl.program_id(0)
    pos = jnp.minimum(i, n_tiles - 1) % tiles_per_seq
    tm = x_ref.shape[0]
    c2, c0, c1, c3, c4 = (k * D_MODEL for k in range(5))
    cols = lambda j, base=0: slice(base + j * TAIL_CW, base + (j + 1) * TAIL_CW)
    chunks = range(D_MODEL // TAIL_CW)

    @pl.when(i == 0)
    def _():
        merged_scr[...] = jnp.zeros_like(merged_scr)

    def u_of(ref, rws, j):
        return ref[rws, cols(j, c0)].astype(f32) * ref[rws, cols(j, c1)].astype(f32)

    def halo_u(cref, xref, r, j):
        return cref[r:r + 1, cols(j)].astype(f32) * xref[r:r + 1, cols(j)].astype(f32)

    last = prevc_ref.shape[0] - 1
    row = lax.broadcasted_iota(jnp.int32, (tm, TAIL_CW), 0)

    def conv_chunk(j):
        u = u_of(rest_ref, slice(0, tm), j)
        u_prev = jnp.where(pos == 0, u_of(meta_ref, slice(N_META - 1, N_META), j),
                           halo_u(prevc_ref, prevx_ref, last, j))
        u_next = halo_u(nextc_ref, nextx_ref, 0, j)
        u_next = jnp.where(pos == tiles_per_seq - 1, jnp.zeros_like(u_next), u_next)
        u_m1 = jnp.where(row == 0, u_prev, pltpu.roll(u, 1, axis=0))
        u_p1 = jnp.where(row == tm - 1, u_next, pltpu.roll(u, tm - 1, axis=0))
        cw = convw_ref[:, cols(j)]
        s = u_m1 * cw[0:1] + u * cw[1:2] + u_p1 * cw[2:3] + convb_ref[:, cols(j)]
        return (rest_ref[:, cols(j, c2)].astype(f32) * s).astype(bf16)

    mix = [jnp.dot(merged_scr[...], wout_ref[:, cols(j)], preferred_element_type=f32) for j in chunks]

    cv = None
    for j in chunks:
        part = jnp.dot(conv_chunk(j), wcp_ref[cols(j), :], preferred_element_type=f32)
        cv = part if cv is None else cv + part
    ap = jnp.dot(att_ref[...], wap_ref[...], preferred_element_type=f32)

    z = [ALPHA * x_ref[:, cols(j)] + mix[j] for j in chunks]
    mu = sum(jnp.sum(zj, axis=-1, keepdims=True) for zj in z) * (1.0 / D_MODEL)
    d = [zj - mu for zj in z]
    var = sum(jnp.sum(dj * dj, axis=-1, keepdims=True) for dj in d) * (1.0 / D_MODEL)
    rstd = lax.rsqrt(var + LN_EPS)
    logits = None
    hs = []
    for j in chunks:
        hj = d[j] * rstd * g_ref[:, cols(j)] + b_ref[:, cols(j)]
        h_ref[:, cols(j)] = hj
        hs.append(hj)
        part = jnp.dot(hj.astype(bf16), wr_ref[cols(j), :], preferred_element_type=f32)
        logits = part if logits is None else logits + part
    hp_ref[...] = _pack_halves(jnp.concatenate(hs, axis=1))

    for j in chunks:
        gate_a = jnp.tanh(rest_ref[:, cols(j, c3)]) + 1.0
        gate_c = jnp.tanh(rest_ref[:, cols(j, c4)]) + 1.0
        merged_scr[:, cols(j)] = gate_a * ap[:, cols(j)].astype(bf16) + gate_c * cv[:, cols(j)].astype(bf16)

    route, count = _route_tile(logits, before_ref)
    route_ref[...] = route
    cnt_ref[0] = jnp.broadcast_to(count, (N_EXPERTS, LANES))


def _mixer_tail(x, att, rest, rest_meta, p, t):
    n = x.shape[0]
    tm = TAIL_TM
    halo = 16
    hb = tm // halo
    n_halo = n // halo
    n_tiles = n // tm
    tiles_per_seq = t // tm
    before = jnp.asarray(np.triu(np.ones((tm, tm), np.float32), 1), dtype=bf16)
    consts = (rest_meta, p["conv_w"], p["conv_b"], p["w_att_proj"], p["w_conv_proj"], p["w_out"],
              p["ln1_g"], p["ln1_b"], p["w_router"], before)
    front = lambda i: jnp.minimum(i, n_tiles - 1)
    back = lambda i: jnp.maximum(i - 1, 0)
    prev_blk = lambda i: jnp.maximum(front(i) * hb - 1, 0)
    next_blk = lambda i: jnp.minimum((front(i) + 1) * hb, n_halo - 1)
    return pl.pallas_call(
        functools.partial(_mixer_tail_kernel, tiles_per_seq=tiles_per_seq, n_tiles=n_tiles),
        grid=(n_tiles + 1,),
        in_specs=[pl.BlockSpec((tm, D_MODEL), lambda i: (back(i), 0)),
                  pl.BlockSpec((tm, D_ATT), lambda i: (front(i), 0)),
                  pl.BlockSpec((tm, D_REST), lambda i: (front(i), 0)),
                  pl.BlockSpec((halo, D_CONV), lambda i: (prev_blk(i), 1)),
                  pl.BlockSpec((halo, D_CONV), lambda i: (prev_blk(i), 2)),
                  pl.BlockSpec((halo, D_CONV), lambda i: (next_blk(i), 1)),
                  pl.BlockSpec((halo, D_CONV), lambda i: (next_blk(i), 2))]
                 + [_full(c) for c in consts],
        out_specs=[pl.BlockSpec((tm, D_MODEL), lambda i: (back(i), 0)),
                   pl.BlockSpec((tm, D_PACKED), lambda i: (back(i), 0)),
                   pl.BlockSpec((ROUTE_ROWS, tm), lambda i: (0, back(i))),
                   pl.BlockSpec((1, N_EXPERTS, LANES), lambda i: (back(i), 0, 0))],
        out_shape=[jax.ShapeDtypeStruct((n, D_MODEL), f32),
                   jax.ShapeDtypeStruct((n, D_PACKED), jnp.uint32),
                   jax.ShapeDtypeStruct((ROUTE_ROWS, n), f32),
                   jax.ShapeDtypeStruct((n_tiles, N_EXPERTS, LANES), f32)],
        scratch_shapes=[pltpu.VMEM((tm, D_MODEL), bf16)],
        compiler_params=_cparams(1),
        name="mixer_tail",
    )(x, att, rest, rest, rest, rest, rest, *consts)


def _route(route, cnt, n, blk):
    n_tiles = n // TAIL_TM
    tile_cnt = cnt[:, :, 0].astype(jnp.int32)
    tile_off = jnp.cumsum(tile_cnt, axis=0) - tile_cnt
    counts = jnp.sum(tile_cnt, axis=0)
    padded = (counts + blk - 1) // blk * blk
    pend = jnp.cumsum(padded)
    pstart = pend - padded
    base = pstart[None, :] + tile_off
    lanes = jnp.arange(N_EXPERTS, dtype=jnp.int32)

    def rows_of(expert_row, rank_row):
        e = expert_row.astype(jnp.int32).reshape(n_tiles, TAIL_TM, 1)
        sel = jnp.sum(jnp.where(e == lanes, base[:, None, :], 0), axis=-1)
        return sel.reshape(n) + rank_row.astype(jnp.int32)

    dest0 = rows_of(route[0], route[4])
    dest1 = rows_of(route[1], route[5])

    n_blocks = (2 * n + N_EXPERTS * (blk - 1) + blk - 1) // blk
    blk_start = jnp.arange(n_blocks, dtype=jnp.int32) * blk
    blk_expert = jnp.sum((pend[None, :] <= blk_start[:, None]).astype(jnp.int32), axis=1)
    blk_expert = jnp.minimum(blk_expert, N_EXPERTS - 1)
    blk_valid = jnp.clip((pstart + counts)[blk_expert] - blk_start, 0, blk)
    blk_valid = jnp.where(blk_start < pend[-1], blk_valid, 0).astype(jnp.int32)
    return dest0, dest1, blk_expert, blk_valid, n_blocks


def _sc_mesh():
    return plsc.VectorSubcoreMesh(core_axis_name="c", subcore_axis_name="s")


def _sc_dispatch(h, dest0, dest1, p_rows):
    n, d = h.shape
    per = n // SC_WORKERS
    assert n % (SC_WORKERS * SC_IDX_WIN) == 0

    @pl.kernel(out_type=jax.ShapeDtypeStruct((p_rows, d), h.dtype), mesh=_sc_mesh(),
               scratch_types=[pltpu.VMEM((2, SC_IDX_WIN), jnp.int32), pltpu.VMEM((SC_ROWS, d), h.dtype)])
    def k(h_hbm, d0_hbm, d1_hbm, xs_hbm, idx, buf):
        base = (lax.axis_index("c") * SC_SUBCORES + lax.axis_index("s")) * per

        @pl.loop(0, per // SC_IDX_WIN)
        def _(w):
            off = base + w * SC_IDX_WIN
            pltpu.sync_copy(d0_hbm.at[pl.ds(off, SC_IDX_WIN)], idx.at[0])
            pltpu.sync_copy(d1_hbm.at[pl.ds(off, SC_IDX_WIN)], idx.at[1])
            for r in range(SC_IDX_WIN // SC_ROWS):
                pltpu.sync_copy(h_hbm.at[pl.ds(off + r * SC_ROWS, SC_ROWS)], buf)
                pltpu.sync_copy(buf, xs_hbm.at[idx.at[0, pl.ds(r * SC_ROWS, SC_ROWS)]])
                pltpu.sync_copy(buf, xs_hbm.at[idx.at[1, pl.ds(r * SC_ROWS, SC_ROWS)]])

    return k(h, dest0, dest1)


def _sc_gather2(ys, dest0, dest1):
    n = dest0.shape[0]
    d = ys.shape[1]
    per = n // SC_WORKERS
    out = jax.ShapeDtypeStruct((n, d), ys.dtype)

    @pl.kernel(out_type=[out, out], mesh=_sc_mesh(),
               scratch_types=[pltpu.VMEM((2, SC_IDX_WIN), jnp.int32), pltpu.VMEM((SC_ROWS, d), ys.dtype)])
    def k(ys_hbm, d0_hbm, d1_hbm, y0_hbm, y1_hbm, idx, buf):
        base = (lax.axis_index("c") * SC_SUBCORES + lax.axis_index("s")) * per

        @pl.loop(0, per // SC_IDX_WIN)
        def _(w):
            off = base + w * SC_IDX_WIN
            pltpu.sync_copy(d0_hbm.at[pl.ds(off, SC_IDX_WIN)], idx.at[0])
            pltpu.sync_copy(d1_hbm.at[pl.ds(off, SC_IDX_WIN)], idx.at[1])
            for r in range(SC_IDX_WIN // SC_ROWS):
                for kk, y_hbm in enumerate((y0_hbm, y1_hbm)):
                    pltpu.sync_copy(ys_hbm.at[idx.at[kk, pl.ds(r * SC_ROWS, SC_ROWS)]], buf)
                    pltpu.sync_copy(buf, y_hbm.at[pl.ds(off + r * SC_ROWS, SC_ROWS)])

    return k(ys, dest0, dest1)


def _experts_kernel(be_ref, bv_ref, slot_ref, nxt_ref, xs_ref, wg_hbm, wu_hbm, wd_hbm, ys_ref,
                    wg_f32, wu_f32, wd_f32, wg_bf, wu_bf, wd_bf, sems):
    i = pl.program_id(0)
    valid = bv_ref[i]
    expert = be_ref[i]
    slot = slot_ref[i]
    run_start = jnp.logical_or(i == 0, expert != be_ref[jnp.maximum(i - 1, 0)])

    def fetch(e, s):
        return [pltpu.make_async_copy(hbm.at[e], buf.at[s], sems.at[s, k])
                for k, (hbm, buf) in enumerate(((wg_hbm, wg_f32), (wu_hbm, wu_f32), (wd_hbm, wd_f32)))]

    @pl.when(i == 0)
    def _():
        for cp in fetch(expert, slot):
            cp.start()

    @pl.when(run_start)
    def _():
        for cp in fetch(expert, slot):
            cp.wait()
        wg_bf[...] = wg_f32[slot].astype(bf16)
        wu_bf[...] = wu_f32[slot].astype(bf16)
        wd_bf[...] = wd_f32[slot].astype(bf16)

        @pl.when(nxt_ref[i] >= 0)
        def _():
            for cp in fetch(nxt_ref[i], 1 - slot):
                cp.start(priority=1)

    @pl.when(valid > 0)
    def _():
        row = lax.broadcasted_iota(jnp.int32, xs_ref.shape, 0)
        keep = row < valid
        lo, hi = _unpack_halves(xs_ref[...])
        x = jnp.concatenate([jnp.where(keep, lo, 0.0), jnp.where(keep, hi, 0.0)], axis=1).astype(bf16)
        g = jnp.dot(x, wg_bf[...], preferred_element_type=f32)
        u = jnp.dot(x, wu_bf[...], preferred_element_type=f32)
        hmid = (g * _sigmoid(g) * u).astype(bf16)
        ys_ref[...] = _pack_halves(jnp.dot(hmid, wd_bf[...], preferred_element_type=f32))

    @pl.when(valid == 0)
    def _():
        ys_ref[...] = jnp.zeros_like(ys_ref)


def _experts(xs, blk_expert, blk_valid, wg, wu, wd, n_blocks, blk):
    starts = jnp.concatenate([jnp.ones((1,), jnp.int32), (blk_expert[1:] != blk_expert[:-1]).astype(jnp.int32)])
    slot = (jnp.cumsum(starts) - 1) % 2
    idx = jnp.arange(n_blocks, dtype=jnp.int32)
    next_start = lax.cummin(jnp.where(starts > 0, idx, n_blocks)[::-1])[::-1]
    next_start = jnp.concatenate([next_start[1:], jnp.full((1,), n_blocks, jnp.int32)])
    nxt = jnp.where(next_start < n_blocks, blk_expert[jnp.minimum(next_start, n_blocks - 1)], -1)
    tile = pl.BlockSpec((blk, D_PACKED), lambda i, *_: (i, 0))
    any_spec = pl.BlockSpec(memory_space=pl.ANY)
    grid_spec = pltpu.PrefetchScalarGridSpec(
        num_scalar_prefetch=4,
        grid=(n_blocks,),
        in_specs=[tile, any_spec, any_spec, any_spec],
        out_specs=tile,
        scratch_shapes=[pltpu.VMEM((2, D_MODEL, D_EXPERT), f32),
                        pltpu.VMEM((2, D_MODEL, D_EXPERT), f32),
                        pltpu.VMEM((2, D_EXPERT, D_MODEL), f32),
                        pltpu.VMEM((D_MODEL, D_EXPERT), bf16),
                        pltpu.VMEM((D_MODEL, D_EXPERT), bf16),
                        pltpu.VMEM((D_EXPERT, D_MODEL), bf16),
                        pltpu.SemaphoreType.DMA((2, 3))],
    )
    return pl.pallas_call(
        _experts_kernel,
        grid_spec=grid_spec,
        out_shape=jax.ShapeDtypeStruct(xs.shape, jnp.uint32),
        compiler_params=_cparams(1),
        name="experts",
    )(blk_expert, blk_valid, slot.astype(jnp.int32), nxt.astype(jnp.int32), xs, wg, wu, wd)


def _final_norm_kernel(h_ref, y0_ref, y1_ref, route_ref, g_ref, b_ref, o_ref):
    route_t = route_ref[...].T
    g0, g1 = route_t[:, 2:3], route_t[:, 3:4]
    lo0, hi0 = _unpack_halves(y0_ref[...])
    lo1, hi1 = _unpack_halves(y1_ref[...])
    ffn = jnp.concatenate([lo0 * g0 + lo1 * g1, hi0 * g0 + hi1 * g1], axis=1)
    o_ref[...] = _layer_norm(ALPHA * h_ref[...] + ffn, g_ref[...], b_ref[...])


def _final_norm(h, y0, y1, route, ln_g, ln_b):
    n = h.shape[0]
    ts = NORM_TS
    tile = pl.BlockSpec((ts, D_MODEL), lambda i: (i, 0))
    packed = pl.BlockSpec((ts, D_PACKED), lambda i: (i, 0))
    return pl.pallas_call(
        _final_norm_kernel,
        grid=(n // ts,),
        in_specs=[tile, packed, packed, pl.BlockSpec((ROUTE_ROWS, ts), lambda i: (0, i)),
                  _full(ln_g), _full(ln_b)],
        out_specs=tile,
        out_shape=jax.ShapeDtypeStruct((n, D_MODEL), f32),
        compiler_params=_cparams(1),
        name="final_norm",
    )(h, y0, y1, route, ln_g, ln_b)


def _encode(x, p, qkv_meta, rest_meta, slabs):
    bsz, t, _ = x.shape
    n = bsz * t
    assert t % NORM_TS == 0 and t % TAIL_TM == 0 and t % PROJ_TM == 0
    xf = x.reshape(n, D_MODEL)
    qkv, rest = _in_proj(xf, p["w_in"], PROJ_TM)
    att = _attention(qkv, qkv_meta, slabs, bsz, t)
    h1, h1_packed, route, cnt = _mixer_tail(xf, att, rest, rest_meta, p, t)
    blk = MOE_BLK
    dest0, dest1, blk_expert, blk_valid, n_blocks = _route(route, cnt, n, blk)
    xs = _sc_dispatch(h1_packed, dest0, dest1, n_blocks * blk)
    ys = _experts(xs, blk_expert, blk_valid, p["w_e_gate"], p["w_e_up"], p["w_e_down"], n_blocks, blk)
    y0, y1 = _sc_gather2(ys, dest0, dest1)
    y = _final_norm(h1, y0, y1, route, p["ln2_g"], p["ln2_b"])
    return y.reshape(bsz, t, D_MODEL)


def kernel(x_prompt, x_sample, meta_tokens, w_in, rpb, conv_w, conv_b, w_att_proj, w_conv_proj, w_out,
           ln1_g, ln1_b, w_router_group, w_router_expert, w_e_gate, w_e_up, w_e_down, ln2_g, ln2_b):
    col_scale = np.ones((1, D_IN_PROJ), np.float32)
    col_scale[:, D_IN_PROJ - 2 * D_MODEL:] = 0.5
    w_in_bf = (w_in[0] * col_scale).astype(bf16)
    w_router = jnp.concatenate([w_router_expert[0], w_router_group[0]], axis=1)
    w_router = jnp.pad(w_router, ((0, 0), (0, LANES - w_router.shape[1])))
    row = lambda v: v[0].reshape(1, -1).astype(f32)
    p = {
        "w_in": w_in_bf,
        "conv_w": conv_w[0].astype(f32), "conv_b": row(conv_b),
        "w_att_proj": w_att_proj[0].astype(bf16), "w_conv_proj": w_conv_proj[0].astype(bf16),
        "w_out": (0.5 * w_out[0]).astype(bf16), "ln1_g": row(ln1_g), "ln1_b": row(ln1_b),
        "w_router": w_router.astype(bf16),
        "w_e_gate": w_e_gate[0], "w_e_up": w_e_up[0], "w_e_down": w_e_down[0],
        "ln2_g": row(ln2_g), "ln2_b": row(ln2_b),
    }
    qkv_meta, rest_meta = _in_proj(meta_tokens.astype(f32), p["w_in"], N_META)
    slabs = _bias_pieces(rpb[0])
    y_prompt = _encode(x_prompt, p, qkv_meta, rest_meta, slabs)
    y_sample = _encode(x_sample, p, qkv_meta, rest_meta, slabs)
    return (y_prompt, y_sample)
```

```python
import functools

import numpy as np
import jax
import jax.numpy as jnp
from jax import lax
from jax.experimental import pallas as pl
from jax.experimental.pallas import tpu as pltpu
from jax.experimental.pallas import tpu_sc as plsc

D_MODEL = 1024
N_META = 16
GRID_W = 64
WIN_H = 8
WIN_W = 16
N_HEADS = 8
HEAD_DIM = 64
D_ATT = N_HEADS * HEAD_DIM
D_CONV = D_MODEL
N_GROUPS = 4
EXPERTS_PER_GROUP = 8
N_EXPERTS = N_GROUPS * EXPERTS_PER_GROUP
D_EXPERT = D_MODEL // 2
DEPTH = 1
ALPHA = (2.0 * DEPTH) ** 0.25
LN_EPS = 1e-5
D_QKV = 4 * D_ATT
LOG2E = 1.4426950408889634
D_REST = 2 * D_CONV + 2 * D_MODEL
D_IN_PROJ = 3 * D_ATT + 3 * D_CONV + 2 * D_MODEL

LANES = 128
HEADS_PER_VREG = LANES // HEAD_DIM
N_HEAD_PAIRS = N_HEADS // HEADS_PER_VREG
NEG_BIG = -1e30

PROJ_TM = 1024
PROJ_TN = 512
ATT_ROWS = 16
ATT_GROUP = 4
SLAB_ROWS = WIN_H + 1
ATT_KEYS = SLAB_ROWS * GRID_W + N_META
N_DY = 2 * WIN_H - 1
PIECE_MT = N_DY - 1
PIECE_T0 = PIECE_MT + WIN_H
PIECE_M0 = PIECE_T0 + WIN_H
N_BIAS_PIECES = PIECE_M0 + 1
TAIL_TM = 512
TAIL_CW = 256
NORM_TS = 1024
MOE_BLK = 512
ROUTE_ROWS = 8
VMEM_LIMIT = 56 * 1024 * 1024

SC_CORES = 2
SC_SUBCORES = 16
SC_WORKERS = SC_CORES * SC_SUBCORES
SC_IDX_WIN = 128
SC_ROWS = 64
D_PACKED = D_MODEL // 2

bf16 = jnp.bfloat16
f32 = jnp.float32


def _cparams(n_axes):
    return pltpu.CompilerParams(dimension_semantics=("arbitrary",) * n_axes,
                                vmem_limit_bytes=VMEM_LIMIT)


def _full(a):
    return pl.BlockSpec(a.shape, lambda *_: (0,) * a.ndim)


def _pack_halves(x):
    half = x.shape[1] // 2
    bits = lambda v: lax.bitcast_convert_type(v.astype(bf16).astype(f32), jnp.uint32)
    return (bits(x[:, :half]) >> 16) | bits(x[:, half:])


def _unpack_halves(p):
    lo = lax.bitcast_convert_type(p << 16, f32)
    hi = lax.bitcast_convert_type(p & jnp.uint32(0xFFFF0000), f32)
    return lo, hi


def _in_proj_kernel(x_ref, w_ref, qkv_ref, rest_ref):
    assert PROJ_TN == D_ATT
    xb = x_ref[...].astype(bf16)
    col = lambda c: jnp.dot(xb, w_ref[:, c:c + PROJ_TN], preferred_element_type=f32)
    proj0 = 3 * D_ATT

    y = (col(0) * (HEAD_DIM ** -0.5 * LOG2E)).astype(bf16)
    lane = lax.broadcasted_iota(jnp.int32, y.shape, 1)
    even = (lane & HEAD_DIM) == 0
    zero = jnp.zeros_like(y)
    qkv_ref[:, :D_ATT] = jnp.where(even, y, zero)
    qkv_ref[:, D_ATT:2 * D_ATT] = jnp.where(even, zero, y)
    for c in range(D_ATT, 3 * D_ATT, PROJ_TN):
        qkv_ref[:, c + D_ATT:c + D_ATT + PROJ_TN] = col(c).astype(bf16)
    for c in range(0, D_CONV, PROJ_TN):
        rest_ref[:, c:c + PROJ_TN] = col(proj0 + c).astype(bf16)
        u = col(proj0 + D_CONV + c) * col(proj0 + 2 * D_CONV + c)
        rest_ref[:, D_CONV + c:D_CONV + c + PROJ_TN] = u.astype(bf16)
    for c in range(0, 2 * D_MODEL, PROJ_TN):
        g = col(proj0 + 3 * D_CONV + c)
        rest_ref[:, 2 * D_CONV + c:2 * D_CONV + c + PROJ_TN] = (jnp.tanh(0.5 * g) + 1.0).astype(bf16)


def _in_proj(x, w_bf, tm):
    n = x.shape[0]
    return pl.pallas_call(
        _in_proj_kernel,
        grid=(n // tm,),
        in_specs=[pl.BlockSpec((tm, D_MODEL), lambda i: (i, 0)),
                  pl.BlockSpec((D_MODEL, D_IN_PROJ), lambda i: (0, 0))],
        out_specs=[pl.BlockSpec((tm, D_QKV), lambda i: (i, 0)),
                   pl.BlockSpec((tm, D_REST), lambda i: (i, 0))],
        out_shape=[jax.ShapeDtypeStruct((n, D_QKV), bf16),
                   jax.ShapeDtypeStruct((n, D_REST), bf16)],
        compiler_params=_cparams(1),
        name="in_proj",
    )(x, w_bf)


def _bias_pieces(rpb):
    qc = np.arange(GRID_W)[:, None]
    kc = np.arange(GRID_W)[None, :]
    w_start = np.clip(qc - WIN_W // 2, 0, GRID_W - WIN_W)
    valid = (kc >= w_start) & (kc < w_start + WIN_W)
    n_dx = 2 * WIN_W - 1
    onehot = (kc - qc + WIN_W - 1)[None] == np.arange(n_dx)[:, None, None]
    t = jnp.einsum("hyx,xqk->hyqk", rpb.astype(f32) * LOG2E, jnp.asarray(onehot & valid[None], f32),
                   precision=lax.Precision.HIGHEST)
    t = jnp.where(jnp.asarray(valid)[None, None], t, NEG_BIG)
    t = t.reshape(N_HEAD_PAIRS, HEADS_PER_VREG, N_DY, GRID_W, GRID_W)
    t = jnp.concatenate([t[:, p] for p in range(HEADS_PER_VREG)], axis=2)
    masked = jnp.full((N_HEAD_PAIRS, WIN_H, HEADS_PER_VREG * GRID_W, GRID_W), NEG_BIG, f32)
    zeros = jnp.zeros_like(masked)
    pieces = jnp.concatenate([
        jnp.concatenate([t[:, :N_DY - 1], t[:, 1:]], axis=-1),
        jnp.concatenate([masked, t[:, :WIN_H]], axis=-1),
        jnp.concatenate([t[:, WIN_H - 1:], zeros], axis=-1),
        jnp.concatenate([masked[:, :1], zeros[:, :1]], axis=-1),
    ], axis=1)
    assert pieces.shape[1] == N_BIAS_PIECES
    return pieces


def _attention_kernel(qe_ref, qo_ref, k_ref, v_ref, km_ref, vm_ref, bias_ref, o_ref, *, rows):
    j = pl.program_id(1)
    tq2 = 2 * GRID_W
    lane = lax.broadcasted_iota(jnp.int32, (tq2, LANES), 1)
    low = lane < HEAD_DIM
    nt = (((1,), (1,)), ((), ()))
    slab_keys = SLAB_ROWS * GRID_W

    def row_bias(pair, dy0, off):
        first = jnp.where(off == 0, dy0, PIECE_MT + dy0)
        mids = [dy0 + 2 * p - off for p in range(1, SLAB_ROWS // 2)]
        last = jnp.where(off == 0, PIECE_M0, PIECE_T0 + dy0)
        tail = ATT_KEYS - (SLAB_ROWS // 2) * LANES
        return jnp.concatenate([bias_ref[pair, first]] + [bias_ref[pair, m] for m in mids]
                               + [bias_ref[pair, last][:, :tail]], axis=1)

    def scores(ip, pair):
        r0 = j * ATT_ROWS + 2 * ip
        rs = [jnp.clip(r0 + a - WIN_H // 2, 0, rows - WIN_H) for a in range(2)]
        us = jnp.minimum(rs[0], rows - SLAB_ROWS)
        q0 = pl.multiple_of(ip * tq2, tq2)
        k0 = pl.multiple_of(us * GRID_W, GRID_W)
        cs = slice(pair * LANES, (pair + 1) * LANES)
        qq = jnp.concatenate([qe_ref[pl.ds(q0, tq2), cs], qo_ref[pl.ds(q0, tq2), cs]], axis=0)
        k2 = jnp.concatenate([k_ref[pl.ds(k0, slab_keys), cs], km_ref[:, cs]], axis=0)
        s = lax.dot_general(qq, k2, nt, preferred_element_type=f32)
        b = [row_bias(pair, rs[a] - (r0 + a) + (WIN_H - 1), rs[a] - us) for a in range(2)]
        s = s + jnp.concatenate([b[0][:GRID_W], b[1][:GRID_W], b[0][GRID_W:], b[1][GRID_W:]], axis=0)
        return s, jnp.max(s, axis=-1, keepdims=True), k0, q0, cs

    def weights(state):
        s, m, k0, q0, cs = state
        v2 = jnp.concatenate([v_ref[pl.ds(k0, slab_keys), cs], vm_ref[:, cs]], axis=0)
        e = jnp.exp2(s - m)
        l = jnp.sum(e, axis=-1, keepdims=True)
        return jnp.dot(e.astype(bf16), v2, preferred_element_type=f32), l, q0, cs

    def finish(state):
        o2, l, q0, cs = state
        o2 = o2 / l
        o = jnp.where(low, o2[:tq2], o2[tq2:])
        o_ref[pl.ds(q0, tq2), cs] = o.astype(bf16)

    def group_body(g, carry):
        items = [(g * ATT_GROUP + ip, pair) for ip in range(ATT_GROUP) for pair in range(N_HEAD_PAIRS)]
        a, b = {}, {}
        for step in range(len(items) + 2):
            if step < len(items):
                a[step] = scores(*items[step])
            if 0 <= step - 1 < len(items):
                b[step - 1] = weights(a.pop(step - 1))
            if 0 <= step - 2 < len(items):
                finish(b.pop(step - 2))
        return carry

    lax.fori_loop(0, ATT_ROWS // (2 * ATT_GROUP), group_body, 0)


def _attention(qkv, qkv_meta, slabs, bsz, t):
    rows = t // GRID_W
    assert rows >= 2 * WIN_H and rows % ATT_ROWS == 0 and ATT_ROWS % (2 * ATT_GROUP) == 0
    steps = rows // ATT_ROWS
    tq = ATT_ROWS * GRID_W
    return pl.pallas_call(
        functools.partial(_attention_kernel, rows=rows),
        grid=(bsz, steps),
        in_specs=[pl.BlockSpec((tq, D_ATT), lambda b, j: (b * steps + j, 0)),
                  pl.BlockSpec((tq, D_ATT), lambda b, j: (b * steps + j, 1)),
                  pl.BlockSpec((t, D_ATT), lambda b, j: (b, 2)),
                  pl.BlockSpec((t, D_ATT), lambda b, j: (b, 3)),
                  pl.BlockSpec((N_META, D_ATT), lambda b, j: (0, 2)),
                  pl.BlockSpec((N_META, D_ATT), lambda b, j: (0, 3)),
                  _full(slabs)],
        out_specs=pl.BlockSpec((tq, D_ATT), lambda b, j: (b * steps + j, 0)),
        out_shape=jax.ShapeDtypeStruct((bsz * t, D_ATT), bf16),
        compiler_params=_cparams(2),
        name="attention",
    )(qkv, qkv, qkv, qkv, qkv_meta, qkv_meta, slabs)


def _layer_norm(z, g, b):
    mu = jnp.mean(z, axis=-1, keepdims=True)
    d = z - mu
    var = jnp.mean(d * d, axis=-1, keepdims=True)
    return d * lax.rsqrt(var + LN_EPS) * g + b


def _sigmoid(x):
    return 0.5 * jnp.tanh(0.5 * x) + 0.5


def _route_tile(logits, before_ref):
    tm = logits.shape[0]
    lt = logits.T
    el = lt[:N_EXPERTS]
    gl = lt[N_EXPERTS:N_EXPERTS + N_GROUPS]
    neg = -jnp.inf
    erow = lax.broadcasted_iota(jnp.int32, el.shape, 0).astype(f32)
    grow = lax.broadcasted_iota(jnp.int32, gl.shape, 0).astype(f32)
    first = lambda hit, idx, n: jnp.min(jnp.where(hit, idx, float(n)), axis=0, keepdims=True)

    gmax = jnp.max(gl, axis=0, keepdims=True)
    grp = first(gl == gmax, grow, N_GROUPS)
    pg_sel = 1.0 / jnp.sum(jnp.exp(gl - gmax), axis=0, keepdims=True)

    e_lo = grp * EXPERTS_PER_GROUP
    elm = jnp.where((erow >= e_lo) & (erow < e_lo + EXPERTS_PER_GROUP), el, neg)
    t1 = jnp.max(elm, axis=0, keepdims=True)
    e1 = first(elm == t1, erow, N_EXPERTS)
    el2 = jnp.where(erow == e1, neg, elm)
    t2 = jnp.max(el2, axis=0, keepdims=True)
    e2 = first(el2 == t2, erow, N_EXPERTS)
    r = jnp.exp(t2 - t1)
    g1 = pg_sel / (1.0 + r)
    g2 = pg_sel * r / (1.0 + r)

    hit1 = erow == e1
    hit2 = erow == e2
    onehot = jnp.where(hit1 | hit2, 1.0, 0.0)
    before = jnp.dot(onehot.astype(bf16), before_ref[...], preferred_element_type=f32)
    rank1 = jnp.sum(jnp.where(hit1, before, 0.0), axis=0, keepdims=True)
    rank2 = jnp.sum(jnp.where(hit2, before, 0.0), axis=0, keepdims=True)
    rrow = lax.broadcasted_iota(jnp.int32, (ROUTE_ROWS, tm), 0)
    route = jnp.zeros((ROUTE_ROWS, tm), f32)
    for k, val in enumerate((e1, e2, g1, g2, rank1, rank2)):
        route = jnp.where(rrow == k, val, route)
    return route, jnp.sum(onehot, axis=1, keepdims=True)


def _mixer_tail_kernel(x_ref, att_ref, rest_ref, prevu_ref, nextu_ref, meta_ref,
                       convw_ref, convb_ref, wap_ref, wcp_ref, wout_ref, g_ref, b_ref, wr_ref, before_ref,
                       h_ref, hp_ref, route_ref, cnt_ref, merged_scr, *, tiles_per_seq, n_tiles):
    i = pl.program_id(0)
    pos = jnp.minimum(i, n_tiles - 1) % tiles_per_seq
    tm = x_ref.shape[0]
    c2, c0, c3, c4 = (k * D_MODEL for k in range(4))
    cols = lambda j, base=0: slice(base + j * TAIL_CW, base + (j + 1) * TAIL_CW)
    chunks = range(D_MODEL // TAIL_CW)

    @pl.when(i == 0)
    def _():
        merged_scr[...] = jnp.zeros_like(merged_scr)

    def u_of(ref, rws, j, base=c0):
        return ref[rws, cols(j, base)].astype(f32)

    last = prevu_ref.shape[0] - 1
    row = lax.broadcasted_iota(jnp.int32, (tm, TAIL_CW), 0)

    def conv_chunk(j):
        u = u_of(rest_ref, slice(0, tm), j)
        u_prev = jnp.where(pos == 0, u_of(meta_ref, slice(N_META - 1, N_META), j),
                           u_of(prevu_ref, slice(last, last + 1), j, 0))
        u_next = u_of(nextu_ref, slice(0, 1), j, 0)
        u_next = jnp.where(pos == tiles_per_seq - 1, jnp.zeros_like(u_next), u_next)
        u_m1 = jnp.where(row == 0, u_prev, pltpu.roll(u, 1, axis=0))
        u_p1 = jnp.where(row == tm - 1, u_next, pltpu.roll(u, tm - 1, axis=0))
        cw = convw_ref[:, cols(j)]
        s = u_m1 * cw[0:1] + u * cw[1:2] + u_p1 * cw[2:3] + convb_ref[:, cols(j)]
        return (rest_ref[:, cols(j, c2)].astype(f32) * s).astype(bf16)

    mix = [jnp.dot(merged_scr[...], wout_ref[:, cols(j)], preferred_element_type=f32) for j in chunks]

    cv = None
    for j in chunks:
        part = jnp.dot(conv_chunk(j), wcp_ref[cols(j), :], preferred_element_type=f32)
        cv = part if cv is None else cv + part
    ap = jnp.dot(att_ref[...], wap_ref[...], preferred_element_type=f32)

    z = [ALPHA * x_ref[:, cols(j)] + mix[j] for j in chunks]
    mu = sum(jnp.sum(zj, axis=-1, keepdims=True) for zj in z) * (1.0 / D_MODEL)
    d = [zj - mu for zj in z]
    var = sum(jnp.sum(dj * dj, axis=-1, keepdims=True) for dj in d) * (1.0 / D_MODEL)
    rstd = lax.rsqrt(var + LN_EPS)
    logits = None
    hs = []
    for j in chunks:
        hj = d[j] * rstd * g_ref[:, cols(j)] + b_ref[:, cols(j)]
        h_ref[:, cols(j)] = hj
        hs.append(hj)
        part = jnp.dot(hj.astype(bf16), wr_ref[cols(j), :], preferred_element_type=f32)
        logits = part if logits is None else logits + part
    hp_ref[...] = _pack_halves(jnp.concatenate(hs, axis=1))

    for j in chunks:
        merged_scr[:, cols(j)] = (rest_ref[:, cols(j, c3)] * ap[:, cols(j)].astype(bf16)
                                  + rest_ref[:, cols(j, c4)] * cv[:, cols(j)].astype(bf16))

    route, count = _route_tile(logits, before_ref)
    route_ref[...] = route
    cnt_ref[0] = jnp.broadcast_to(count, (N_EXPERTS, LANES))


def _mixer_tail(x, att, rest, rest_meta, p, t):
    n = x.shape[0]
    tm = TAIL_TM
    halo = 16
    hb = tm // halo
    n_halo = n // halo
    n_tiles = n // tm
    tiles_per_seq = t // tm
    before = jnp.asarray(np.triu(np.ones((tm, tm), np.float32), 1), dtype=bf16)
    consts = (rest_meta, p["conv_w"], p["conv_b"], p["w_att_proj"], p["w_conv_proj"], p["w_out"],
              p["ln1_g"], p["ln1_b"], p["w_router"], before)
    front = lambda i: jnp.minimum(i, n_tiles - 1)
    back = lambda i: jnp.maximum(i - 1, 0)
    prev_blk = lambda i: jnp.maximum(front(i) * hb - 1, 0)
    next_blk = lambda i: jnp.minimum((front(i) + 1) * hb, n_halo - 1)
    return pl.pallas_call(
        functools.partial(_mixer_tail_kernel, tiles_per_seq=tiles_per_seq, n_tiles=n_tiles),
        grid=(n_tiles + 1,),
        in_specs=[pl.BlockSpec((tm, D_MODEL), lambda i: (back(i), 0)),
                  pl.BlockSpec((tm, D_ATT), lambda i: (front(i), 0)),
                  pl.BlockSpec((tm, D_REST), lambda i: (front(i), 0)),
                  pl.BlockSpec((halo, D_CONV), lambda i: (prev_blk(i), 1)),
                  pl.BlockSpec((halo, D_CONV), lambda i: (next_blk(i), 1))]
                 + [_full(c) for c in consts],
        out_specs=[pl.BlockSpec((tm, D_MODEL), lambda i: (back(i), 0)),
                   pl.BlockSpec((tm, D_PACKED), lambda i: (back(i), 0)),
                   pl.BlockSpec((ROUTE_ROWS, tm), lambda i: (0, back(i))),
                   pl.BlockSpec((1, N_EXPERTS, LANES), lambda i: (back(i), 0, 0))],
        out_shape=[jax.ShapeDtypeStruct((n, D_MODEL), f32),
                   jax.ShapeDtypeStruct((n, D_PACKED), jnp.uint32),
                   jax.ShapeDtypeStruct((ROUTE_ROWS, n), f32),
                   jax.ShapeDtypeStruct((n_tiles, N_EXPERTS, LANES), f32)],
        scratch_shapes=[pltpu.VMEM((tm, D_MODEL), bf16)],
        compiler_params=_cparams(1),
        name="mixer_tail",
    )(x, att, rest, rest, rest, *consts)


def _route(route, cnt, n, blk):
    n_tiles = n // TAIL_TM
    tile_cnt = cnt[:, :, 0].astype(jnp.int32)
    tile_off = jnp.cumsum(tile_cnt, axis=0) - tile_cnt
    counts = jnp.sum(tile_cnt, axis=0)
    padded = (counts + blk - 1) // blk * blk
    pend = jnp.cumsum(padded)
    pstart = pend - padded
    base = pstart[None, :] + tile_off
    lanes = jnp.arange(N_EXPERTS, dtype=jnp.int32)

    def rows_of(expert_row, rank_row):
        e = expert_row.astype(jnp.int32).reshape(n_tiles, TAIL_TM, 1)
        sel = jnp.sum(jnp.where(e == lanes, base[:, None, :], 0), axis=-1)
        return sel.reshape(n) + rank_row.astype(jnp.int32)

    dest0 = rows_of(route[0], route[4])
    dest1 = rows_of(route[1], route[5])

    n_blocks = (2 * n + N_EXPERTS * (blk - 1) + blk - 1) // blk
    blk_start = jnp.arange(n_blocks, dtype=jnp.int32) * blk
    blk_expert = jnp.sum((pend[None, :] <= blk_start[:, None]).astype(jnp.int32), axis=1)
    blk_expert = jnp.minimum(blk_expert, N_EXPERTS - 1)
    blk_valid = jnp.clip((pstart + counts)[blk_expert] - blk_start, 0, blk)
    blk_valid = jnp.where(blk_start < pend[-1], blk_valid, 0).astype(jnp.int32)
    return dest0, dest1, blk_expert, blk_valid, n_blocks


def _sc_mesh():
    return plsc.VectorSubcoreMesh(core_axis_name="c", subcore_axis_name="s")


def _sc_dispatch(h, dest0, dest1, p_rows):
    n, d = h.shape
    per = n // SC_WORKERS
    assert n % (SC_WORKERS * SC_IDX_WIN) == 0

    @pl.kernel(out_type=jax.ShapeDtypeStruct((p_rows, d), h.dtype), mesh=_sc_mesh(),
               scratch_types=[pltpu.VMEM((2, SC_IDX_WIN), jnp.int32), pltpu.VMEM((SC_ROWS, d), h.dtype)])
    def k(h_hbm, d0_hbm, d1_hbm, xs_hbm, idx, buf):
        base = (lax.axis_index("c") * SC_SUBCORES + lax.axis_index("s")) * per

        @pl.loop(0, per // SC_IDX_WIN)
        def _(w):
            off = base + w * SC_IDX_WIN
            pltpu.sync_copy(d0_hbm.at[pl.ds(off, SC_IDX_WIN)], idx.at[0])
            pltpu.sync_copy(d1_hbm.at[pl.ds(off, SC_IDX_WIN)], idx.at[1])
            for r in range(SC_IDX_WIN // SC_ROWS):
                pltpu.sync_copy(h_hbm.at[pl.ds(off + r * SC_ROWS, SC_ROWS)], buf)
                pltpu.sync_copy(buf, xs_hbm.at[idx.at[0, pl.ds(r * SC_ROWS, SC_ROWS)]])
                pltpu.sync_copy(buf, xs_hbm.at[idx.at[1, pl.ds(r * SC_ROWS, SC_ROWS)]])

    return k(h, dest0, dest1)


def _sc_gather2(ys, dest0, dest1):
    n = dest0.shape[0]
    d = ys.shape[1]
    per = n // SC_WORKERS
    out = jax.ShapeDtypeStruct((n, d), ys.dtype)

    @pl.kernel(out_type=[out, out], mesh=_sc_mesh(),
               scratch_types=[pltpu.VMEM((2, SC_IDX_WIN), jnp.int32), pltpu.VMEM((SC_ROWS, d), ys.dtype)])
    def k(ys_hbm, d0_hbm, d1_hbm, y0_hbm, y1_hbm, idx, buf):
        base = (lax.axis_index("c") * SC_SUBCORES + lax.axis_index("s")) * per

        @pl.loop(0, per // SC_IDX_WIN)
        def _(w):
            off = base + w * SC_IDX_WIN
            pltpu.sync_copy(d0_hbm.at[pl.ds(off, SC_IDX_WIN)], idx.at[0])
            pltpu.sync_copy(d1_hbm.at[pl.ds(off, SC_IDX_WIN)], idx.at[1])
            for r in range(SC_IDX_WIN // SC_ROWS):
                for kk, y_hbm in enumerate((y0_hbm, y1_hbm)):
                    pltpu.sync_copy(ys_hbm.at[idx.at[kk, pl.ds(r * SC_ROWS, SC_ROWS)]], buf)
                    pltpu.sync_copy(buf, y_hbm.at[pl.ds(off + r * SC_ROWS, SC_ROWS)])

    return k(ys, dest0, dest1)


def _experts_kernel(be_ref, bv_ref, slot_ref, nxt_ref, xs_ref, wg_hbm, wu_hbm, wd_hbm, ys_ref,
                    wg_f32, wu_f32, wd_f32, wg_bf, wu_bf, wd_bf, sems):
    i = pl.program_id(0)
    valid = bv_ref[i]
    expert = be_ref[i]
    slot = slot_ref[i]
    run_start = jnp.logical_or(i == 0, expert != be_ref[jnp.maximum(i - 1, 0)])

    def fetch(e, s):
        return [pltpu.make_async_copy(hbm.at[e], buf.at[s], sems.at[s, k])
                for k, (hbm, buf) in enumerate(((wg_hbm, wg_f32), (wu_hbm, wu_f32), (wd_hbm, wd_f32)))]

    @pl.when(i == 0)
    def _():
        for cp in fetch(expert, slot):
            cp.start()

    @pl.when(run_start)
    def _():
        for cp in fetch(expert, slot):
            cp.wait()
        wg_bf[...] = wg_f32[slot].astype(bf16)
        wu_bf[...] = wu_f32[slot].astype(bf16)
        wd_bf[...] = wd_f32[slot].astype(bf16)

        @pl.when(nxt_ref[i] >= 0)
        def _():
            for cp in fetch(nxt_ref[i], 1 - slot):
                cp.start(priority=1)

    @pl.when(valid > 0)
    def _():
        row = lax.broadcasted_iota(jnp.int32, xs_ref.shape, 0)
        keep = row < valid
        lo, hi = _unpack_halves(xs_ref[...])
        x = jnp.concatenate([jnp.where(keep, lo, 0.0), jnp.where(keep, hi, 0.0)], axis=1).astype(bf16)
        g = jnp.dot(x, wg_bf[...], preferred_element_type=f32)
        u = jnp.dot(x, wu_bf[...], preferred_element_type=f32)
        hmid = (g * _sigmoid(g) * u).astype(bf16)
        ys_ref[...] = _pack_halves(jnp.dot(hmid, wd_bf[...], preferred_element_type=f32))

    @pl.when(valid == 0)
    def _():
        ys_ref[...] = jnp.zeros_like(ys_ref)


def _experts(xs, blk_expert, blk_valid, wg, wu, wd, n_blocks, blk):
    starts = jnp.concatenate([jnp.ones((1,), jnp.int32), (blk_expert[1:] != blk_expert[:-1]).astype(jnp.int32)])
    slot = (jnp.cumsum(starts) - 1) % 2
    idx = jnp.arange(n_blocks, dtype=jnp.int32)
    next_start = lax.cummin(jnp.where(starts > 0, idx, n_blocks)[::-1])[::-1]
    next_start = jnp.concatenate([next_start[1:], jnp.full((1,), n_blocks, jnp.int32)])
    nxt = jnp.where(next_start < n_blocks, blk_expert[jnp.minimum(next_start, n_blocks - 1)], -1)
    tile = pl.BlockSpec((blk, D_PACKED), lambda i, *_: (i, 0))
    any_spec = pl.BlockSpec(memory_space=pl.ANY)
    grid_spec = pltpu.PrefetchScalarGridSpec(
        num_scalar_prefetch=4,
        grid=(n_blocks,),
        in_specs=[tile, any_spec, any_spec, any_spec],
        out_specs=tile,
        scratch_shapes=[pltpu.VMEM((2, D_MODEL, D_EXPERT), f32),
                        pltpu.VMEM((2, D_MODEL, D_EXPERT), f32),
                        pltpu.VMEM((2, D_EXPERT, D_MODEL), f32),
                        pltpu.VMEM((D_MODEL, D_EXPERT), bf16),
                        pltpu.VMEM((D_MODEL, D_EXPERT), bf16),
                        pltpu.VMEM((D_EXPERT, D_MODEL), bf16),
                        pltpu.SemaphoreType.DMA((2, 3))],
    )
    return pl.pallas_call(
        _experts_kernel,
        grid_spec=grid_spec,
        out_shape=jax.ShapeDtypeStruct(xs.shape, jnp.uint32),
        compiler_params=_cparams(1),
        name="experts",
    )(blk_expert, blk_valid, slot.astype(jnp.int32), nxt.astype(jnp.int32), xs, wg, wu, wd)


def _final_norm_kernel(h_ref, y0_ref, y1_ref, route_ref, g_ref, b_ref, o_ref):
    route_t = route_ref[...].T
    g0, g1 = route_t[:, 2:3], route_t[:, 3:4]
    lo0, hi0 = _unpack_halves(y0_ref[...])
    lo1, hi1 = _unpack_halves(y1_ref[...])
    ffn = jnp.concatenate([lo0 * g0 + lo1 * g1, hi0 * g0 + hi1 * g1], axis=1)
    o_ref[...] = _layer_norm(ALPHA * h_ref[...] + ffn, g_ref[...], b_ref[...])


def _final_norm(h, y0, y1, route, ln_g, ln_b):
    n = h.shape[0]
    ts = NORM_TS
    tile = pl.BlockSpec((ts, D_MODEL), lambda i: (i, 0))
    packed = pl.BlockSpec((ts, D_PACKED), lambda i: (i, 0))
    return pl.pallas_call(
        _final_norm_kernel,
        grid=(n // ts,),
        in_specs=[tile, packed, packed, pl.BlockSpec((ROUTE_ROWS, ts), lambda i: (0, i)),
                  _full(ln_g), _full(ln_b)],
        out_specs=tile,
        out_shape=jax.ShapeDtypeStruct((n, D_MODEL), f32),
        compiler_params=_cparams(1),
        name="final_norm",
    )(h, y0, y1, route, ln_g, ln_b)


def _encode(x, p, qkv_meta, rest_meta, slabs):
    bsz, t, _ = x.shape
    n = bsz * t
    assert t % NORM_TS == 0 and t % TAIL_TM == 0 and t % PROJ_TM == 0
    xf = x.reshape(n, D_MODEL)
    qkv, rest = _in_proj(xf, p["w_in"], PROJ_TM)
    att = _attention(qkv, qkv_meta, slabs, bsz, t)
    h1, h1_packed, route, cnt = _mixer_tail(xf, att, rest, rest_meta, p, t)
    blk = MOE_BLK
    dest0, dest1, blk_expert, blk_valid, n_blocks = _route(route, cnt, n, blk)
    xs = _sc_dispatch(h1_packed, dest0, dest1, n_blocks * blk)
    ys = _experts(xs, blk_expert, blk_valid, p["w_e_gate"], p["w_e_up"], p["w_e_down"], n_blocks, blk)
    y0, y1 = _sc_gather2(ys, dest0, dest1)
    y = _final_norm(h1, y0, y1, route, p["ln2_g"], p["ln2_b"])
    return y.reshape(bsz, t, D_MODEL)


def kernel(x_prompt, x_sample, meta_tokens, w_in, rpb, conv_w, conv_b, w_att_proj, w_conv_proj, w_out,
           ln1_g, ln1_b, w_router_group, w_router_expert, w_e_gate, w_e_up, w_e_down, ln2_g, ln2_b):
    w_router = jnp.concatenate([w_router_expert[0], w_router_group[0]], axis=1)
    w_router = jnp.pad(w_router, ((0, 0), (0, LANES - w_router.shape[1])))
    row = lambda v: v[0].reshape(1, -1).astype(f32)
    p = {
        "w_in": w_in[0].astype(bf16),
        "conv_w": conv_w[0].astype(f32), "conv_b": row(conv_b),
        "w_att_proj": w_att_proj[0].astype(bf16), "w_conv_proj": w_conv_proj[0].astype(bf16),
        "w_out": (0.5 * w_out[0]).astype(bf16), "ln1_g": row(ln1_g), "ln1_b": row(ln1_b),
        "w_router": w_router.astype(bf16),
        "w_e_gate": w_e_gate[0], "w_e_up": w_e_up[0], "w_e_down": w_e_down[0],
        "ln2_g": row(ln2_g), "ln2_b": row(ln2_b),
    }
    qkv_meta, rest_meta = _in_proj(meta_tokens.astype(f32), p["w_in"], N_META)
    slabs = _bias_pieces(rpb[0])
    y_prompt = _encode(x_prompt, p, qkv_meta, rest_meta, slabs)
    y_sample = _encode(x_sample, p, qkv_meta, rest_meta, slabs)
    return (y_prompt, y_sample)
```

```python
import functools

import numpy as np
import jax
import jax.numpy as jnp
from jax import lax
from jax.experimental import pallas as pl
from jax.experimental.pallas import tpu as pltpu
from jax.experimental.pallas import tpu_sc as plsc

D_MODEL = 1024
N_META = 16
GRID_W = 64
WIN_H = 8
WIN_W = 16
N_HEADS = 8
HEAD_DIM = 64
D_ATT = N_HEADS * HEAD_DIM
D_CONV = D_MODEL
N_GROUPS = 4
EXPERTS_PER_GROUP = 8
N_EXPERTS = N_GROUPS * EXPERTS_PER_GROUP
D_EXPERT = D_MODEL // 2
DEPTH = 1
ALPHA = (2.0 * DEPTH) ** 0.25
LN_EPS = 1e-5
D_QKV = 4 * D_ATT
LOG2E = 1.4426950408889634
D_REST = D_CONV + 2 * D_MODEL
D_IN_PROJ = 3 * D_ATT + 3 * D_CONV + 2 * D_MODEL

LANES = 128
HEADS_PER_VREG = LANES // HEAD_DIM
N_HEAD_PAIRS = N_HEADS // HEADS_PER_VREG
NEG_BIG = -1e30

PROJ_TM = 1024
PROJ_TN = 512
X_HALO = 8
ATT_ROWS = 16
ATT_GROUP = 4
SLAB_ROWS = WIN_H + 1
ATT_KEYS = SLAB_ROWS * GRID_W + N_META
N_DY = 2 * WIN_H - 1
PIECE_MT = N_DY - 1
PIECE_T0 = PIECE_MT + WIN_H
PIECE_M0 = PIECE_T0 + WIN_H
N_BIAS_PIECES = PIECE_M0 + 1
TAIL_TM = 512
TAIL_CW = 256
NORM_TS = 1024
MOE_BLK = 512
ROUTE_ROWS = 8
VMEM_LIMIT = 56 * 1024 * 1024

SC_CORES = 2
SC_SUBCORES = 16
SC_WORKERS = SC_CORES * SC_SUBCORES
SC_IDX_WIN = 128
SC_ROWS = 64
D_PACKED = D_MODEL // 2

bf16 = jnp.bfloat16
f32 = jnp.float32


def _cparams(n_axes):
    return pltpu.CompilerParams(dimension_semantics=("arbitrary",) * n_axes,
                                vmem_limit_bytes=VMEM_LIMIT)


def _full(a):
    return pl.BlockSpec(a.shape, lambda *_: (0,) * a.ndim)


def _pack_halves(x):
    half = x.shape[1] // 2
    bits = lambda v: lax.bitcast_convert_type(v.astype(bf16).astype(f32), jnp.uint32)
    return (bits(x[:, :half]) >> 16) | bits(x[:, half:])


def _unpack_halves(p):
    lo = lax.bitcast_convert_type(p << 16, f32)
    hi = lax.bitcast_convert_type(p & jnp.uint32(0xFFFF0000), f32)
    return lo, hi


def _in_proj_kernel(x_ref, prev_ref, next_ref, meta_ref, w_ref, convw_ref, convb_ref, qkv_ref, rest_ref,
                    *, tiles_per_seq):
    assert PROJ_TN == D_ATT
    pos = pl.program_id(0) % tiles_per_seq
    tm = x_ref.shape[0]
    x_prev = jnp.where(pos == 0, meta_ref[N_META - X_HALO:, :], prev_ref[...])
    x_ext = jnp.concatenate([x_prev, x_ref[...], next_ref[...]], axis=0).astype(bf16)
    body = slice(X_HALO, X_HALO + tm)
    col = lambda c: jnp.dot(x_ext, w_ref[:, c:c + PROJ_TN], preferred_element_type=f32)
    proj0 = 3 * D_ATT

    y = (col(0)[body] * (HEAD_DIM ** -0.5 * LOG2E)).astype(bf16)
    lane = lax.broadcasted_iota(jnp.int32, y.shape, 1)
    even = (lane & HEAD_DIM) == 0
    zero = jnp.zeros_like(y)
    qkv_ref[:, :D_ATT] = jnp.where(even, y, zero)
    qkv_ref[:, D_ATT:2 * D_ATT] = jnp.where(even, zero, y)
    for c in range(D_ATT, 3 * D_ATT, PROJ_TN):
        qkv_ref[:, c + D_ATT:c + D_ATT + PROJ_TN] = col(c)[body].astype(bf16)

    row = lax.broadcasted_iota(jnp.int32, (tm, PROJ_TN), 0)
    no_next = jnp.logical_and(pos == tiles_per_seq - 1, row == tm - 1)
    for c in range(0, D_CONV, PROJ_TN):
        u_ext = col(proj0 + D_CONV + c) * col(proj0 + 2 * D_CONV + c)
        u_m1 = pltpu.roll(u_ext, 1, axis=0)[body]
        u_p1 = jnp.where(no_next, 0.0, pltpu.roll(u_ext, tm + 2 * X_HALO - 1, axis=0)[body])
        cw = convw_ref[:, c:c + PROJ_TN]
        conv = u_m1 * cw[0:1] + u_ext[body] * cw[1:2] + u_p1 * cw[2:3] + convb_ref[:, c:c + PROJ_TN]
        rest_ref[:, c:c + PROJ_TN] = (col(proj0 + c)[body] * conv).astype(bf16)
    for c in range(0, 2 * D_MODEL, PROJ_TN):
        g = col(proj0 + 3 * D_CONV + c)[body]
        rest_ref[:, D_CONV + c:D_CONV + c + PROJ_TN] = (jnp.tanh(0.5 * g) + 1.0).astype(bf16)


def _in_proj(x, meta_tokens, w_bf, conv_w, conv_b, t):
    n = x.shape[0]
    tm = PROJ_TM
    hb = tm // X_HALO
    n_halo = n // X_HALO
    halo = lambda index: pl.BlockSpec((X_HALO, D_MODEL), lambda i: (index(i), 0))
    return pl.pallas_call(
        functools.partial(_in_proj_kernel, tiles_per_seq=t // tm),
        grid=(n // tm,),
        in_specs=[pl.BlockSpec((tm, D_MODEL), lambda i: (i, 0)),
                  halo(lambda i: jnp.maximum(i * hb - 1, 0)),
                  halo(lambda i: jnp.minimum((i + 1) * hb, n_halo - 1)),
                  _full(meta_tokens), _full(w_bf), _full(conv_w), _full(conv_b)],
        out_specs=[pl.BlockSpec((tm, D_QKV), lambda i: (i, 0)),
                   pl.BlockSpec((tm, D_REST), lambda i: (i, 0))],
        out_shape=[jax.ShapeDtypeStruct((n, D_QKV), bf16),
                   jax.ShapeDtypeStruct((n, D_REST), bf16)],
        compiler_params=_cparams(1),
        name="in_proj",
    )(x, x, x, meta_tokens, w_bf, conv_w, conv_b)


def _meta_kv_kernel(m_ref, w_ref, kv_ref):
    kv_ref[...] = jnp.dot(m_ref[...].astype(bf16), w_ref[...], preferred_element_type=f32).astype(bf16)


def _meta_kv(meta_tokens, w_kv):
    return pl.pallas_call(
        _meta_kv_kernel,
        out_shape=jax.ShapeDtypeStruct((N_META, 2 * D_ATT), bf16),
        compiler_params=pltpu.CompilerParams(vmem_limit_bytes=VMEM_LIMIT),
        name="meta_kv",
    )(meta_tokens, w_kv)


def _bias_pieces(rpb):
    qc = np.arange(GRID_W)[:, None]
    kc = np.arange(GRID_W)[None, :]
    w_start = np.clip(qc - WIN_W // 2, 0, GRID_W - WIN_W)
    valid = (kc >= w_start) & (kc < w_start + WIN_W)
    n_dx = 2 * WIN_W - 1
    onehot = (kc - qc + WIN_W - 1)[None] == np.arange(n_dx)[:, None, None]
    t = jnp.einsum("hyx,xqk->hyqk", rpb.astype(f32) * LOG2E, jnp.asarray(onehot & valid[None], f32),
                   precision=lax.Precision.HIGHEST)
    t = jnp.where(jnp.asarray(valid)[None, None], t, NEG_BIG)
    t = t.reshape(N_HEAD_PAIRS, HEADS_PER_VREG, N_DY, GRID_W, GRID_W)
    t = jnp.concatenate([t[:, p] for p in range(HEADS_PER_VREG)], axis=2)
    masked = jnp.full((N_HEAD_PAIRS, WIN_H, HEADS_PER_VREG * GRID_W, GRID_W), NEG_BIG, f32)
    zeros = jnp.zeros_like(masked)
    pieces = jnp.concatenate([
        jnp.concatenate([t[:, :N_DY - 1], t[:, 1:]], axis=-1),
        jnp.concatenate([masked, t[:, :WIN_H]], axis=-1),
        jnp.concatenate([t[:, WIN_H - 1:], zeros], axis=-1),
        jnp.concatenate([masked[:, :1], zeros[:, :1]], axis=-1),
    ], axis=1)
    assert pieces.shape[1] == N_BIAS_PIECES
    return pieces


def _attention_kernel(qe_ref, qo_ref, k_ref, v_ref, km_ref, vm_ref, bias_ref, o_ref, *, rows):
    j = pl.program_id(1)
    tq2 = 2 * GRID_W
    lane = lax.broadcasted_iota(jnp.int32, (tq2, LANES), 1)
    low = lane < HEAD_DIM
    nt = (((1,), (1,)), ((), ()))
    slab_keys = SLAB_ROWS * GRID_W

    def row_bias(pair, dy0, off):
        first = jnp.where(off == 0, dy0, PIECE_MT + dy0)
        mids = [dy0 + 2 * p - off for p in range(1, SLAB_ROWS // 2)]
        last = jnp.where(off == 0, PIECE_M0, PIECE_T0 + dy0)
        tail = ATT_KEYS - (SLAB_ROWS // 2) * LANES
        return jnp.concatenate([bias_ref[pair, first]] + [bias_ref[pair, m] for m in mids]
                               + [bias_ref[pair, last][:, :tail]], axis=1)

    def scores(ip, pair):
        r0 = j * ATT_ROWS + 2 * ip
        rs = [jnp.clip(r0 + a - WIN_H // 2, 0, rows - WIN_H) for a in range(2)]
        us = jnp.minimum(rs[0], rows - SLAB_ROWS)
        q0 = pl.multiple_of(ip * tq2, tq2)
        k0 = pl.multiple_of(us * GRID_W, GRID_W)
        cs = slice(pair * LANES, (pair + 1) * LANES)
        qq = jnp.concatenate([qe_ref[pl.ds(q0, tq2), cs], qo_ref[pl.ds(q0, tq2), cs]], axis=0)
        k2 = jnp.concatenate([k_ref[pl.ds(k0, slab_keys), cs], km_ref[:, cs]], axis=0)
        s = lax.dot_general(qq, k2, nt, preferred_element_type=f32)
        b = [row_bias(pair, rs[a] - (r0 + a) + (WIN_H - 1), rs[a] - us) for a in range(2)]
        s = s + jnp.concatenate([b[0][:GRID_W], b[1][:GRID_W], b[0][GRID_W:], b[1][GRID_W:]], axis=0)
        return s, jnp.max(s, axis=-1, keepdims=True), k0, q0, cs

    def weights(state):
        s, m, k0, q0, cs = state
        v2 = jnp.concatenate([v_ref[pl.ds(k0, slab_keys), cs], vm_ref[:, cs]], axis=0)
        e = jnp.exp2(s - m)
        l = jnp.sum(e, axis=-1, keepdims=True)
        return jnp.dot(e.astype(bf16), v2, preferred_element_type=f32), l, q0, cs

    def finish(state):
        o2, l, q0, cs = state
        o2 = o2 / l
        o = jnp.where(low, o2[:tq2], o2[tq2:])
        o_ref[pl.ds(q0, tq2), cs] = o.astype(bf16)

    def group_body(g, carry):
        items = [(g * ATT_GROUP + ip, pair) for ip in range(ATT_GROUP) for pair in range(N_HEAD_PAIRS)]
        a, b = {}, {}
        for step in range(len(items) + 2):
            if step < len(items):
                a[step] = scores(*items[step])
            if 0 <= step - 1 < len(items):
                b[step - 1] = weights(a.pop(step - 1))
            if 0 <= step - 2 < len(items):
                finish(b.pop(step - 2))
        return carry

    lax.fori_loop(0, ATT_ROWS // (2 * ATT_GROUP), group_body, 0)


def _attention(qkv, kv_meta, slabs, bsz, t):
    rows = t // GRID_W
    assert rows >= 2 * WIN_H and rows % ATT_ROWS == 0 and ATT_ROWS % (2 * ATT_GROUP) == 0
    steps = rows // ATT_ROWS
    tq = ATT_ROWS * GRID_W
    return pl.pallas_call(
        functools.partial(_attention_kernel, rows=rows),
        grid=(bsz, steps),
        in_specs=[pl.BlockSpec((tq, D_ATT), lambda b, j: (b * steps + j, 0)),
                  pl.BlockSpec((tq, D_ATT), lambda b, j: (b * steps + j, 1)),
                  pl.BlockSpec((t, D_ATT), lambda b, j: (b, 2)),
                  pl.BlockSpec((t, D_ATT), lambda b, j: (b, 3)),
                  pl.BlockSpec((N_META, D_ATT), lambda b, j: (0, 0)),
                  pl.BlockSpec((N_META, D_ATT), lambda b, j: (0, 1)),
                  _full(slabs)],
        out_specs=pl.BlockSpec((tq, D_ATT), lambda b, j: (b * steps + j, 0)),
        out_shape=jax.ShapeDtypeStruct((bsz * t, D_ATT), bf16),
        compiler_params=_cparams(2),
        name="attention",
    )(qkv, qkv, qkv, qkv, kv_meta, kv_meta, slabs)


def _layer_norm(z, g, b):
    mu = jnp.mean(z, axis=-1, keepdims=True)
    d = z - mu
    var = jnp.mean(d * d, axis=-1, keepdims=True)
    return d * lax.rsqrt(var + LN_EPS) * g + b


def _sigmoid(x):
    return 0.5 * jnp.tanh(0.5 * x) + 0.5


def _route_tile(logits, before_ref):
    tm = logits.shape[0]
    lt = logits.T
    el = lt[:N_EXPERTS]
    gl = lt[N_EXPERTS:N_EXPERTS + N_GROUPS]
    neg = -jnp.inf
    erow = lax.broadcasted_iota(jnp.int32, el.shape, 0).astype(f32)
    grow = lax.broadcasted_iota(jnp.int32, gl.shape, 0).astype(f32)
    first = lambda hit, idx, n: jnp.min(jnp.where(hit, idx, float(n)), axis=0, keepdims=True)

    gmax = jnp.max(gl, axis=0, keepdims=True)
    grp = first(gl == gmax, grow, N_GROUPS)
    pg_sel = 1.0 / jnp.sum(jnp.exp(gl - gmax), axis=0, keepdims=True)

    e_lo = grp * EXPERTS_PER_GROUP
    elm = jnp.where((erow >= e_lo) & (erow < e_lo + EXPERTS_PER_GROUP), el, neg)
    t1 = jnp.max(elm, axis=0, keepdims=True)
    e1 = first(elm == t1, erow, N_EXPERTS)
    el2 = jnp.where(erow == e1, neg, elm)
    t2 = jnp.max(el2, axis=0, keepdims=True)
    e2 = first(el2 == t2, erow, N_EXPERTS)
    r = jnp.exp(t2 - t1)
    g1 = pg_sel / (1.0 + r)
    g2 = pg_sel * r / (1.0 + r)

    hit1 = erow == e1
    hit2 = erow == e2
    onehot = jnp.where(hit1 | hit2, 1.0, 0.0)
    before = jnp.dot(onehot.astype(bf16), before_ref[...], preferred_element_type=f32)
    rank1 = jnp.sum(jnp.where(hit1, before, 0.0), axis=0, keepdims=True)
    rank2 = jnp.sum(jnp.where(hit2, before, 0.0), axis=0, keepdims=True)
    rrow = lax.broadcasted_iota(jnp.int32, (ROUTE_ROWS, tm), 0)
    route = jnp.zeros((ROUTE_ROWS, tm), f32)
    for k, val in enumerate((e1, e2, g1, g2, rank1, rank2)):
        route = jnp.where(rrow == k, val, route)
    return route, jnp.sum(onehot, axis=1, keepdims=True)


def _mixer_tail_kernel(x_ref, att_ref, rest_ref, wap_ref, wcp_ref, wout_ref, g_ref, b_ref, wr_ref, before_ref,
                       h_ref, hp_ref, route_ref, cnt_ref, merged_scr, *, n_tiles):
    i = pl.program_id(0)
    c3, c4 = D_CONV, D_CONV + D_MODEL
    cols = lambda j, base=0: slice(base + j * TAIL_CW, base + (j + 1) * TAIL_CW)
    chunks = range(D_MODEL // TAIL_CW)

    @pl.when(i == 0)
    def _():
        merged_scr[...] = jnp.zeros_like(merged_scr)

    mix = [jnp.dot(merged_scr[...], wout_ref[:, cols(j)], preferred_element_type=f32) for j in chunks]

    cv = jnp.dot(rest_ref[:, :D_CONV], wcp_ref[...], preferred_element_type=f32)
    ap = jnp.dot(att_ref[...], wap_ref[...], preferred_element_type=f32)

    z = [ALPHA * x_ref[:, cols(j)] + mix[j] for j in chunks]
    mu = sum(jnp.sum(zj, axis=-1, keepdims=True) for zj in z) * (1.0 / D_MODEL)
    d = [zj - mu for zj in z]
    var = sum(jnp.sum(dj * dj, axis=-1, keepdims=True) for dj in d) * (1.0 / D_MODEL)
    rstd = lax.rsqrt(var + LN_EPS)
    logits = None
    hs = []
    for j in chunks:
        hj = d[j] * rstd * g_ref[:, cols(j)] + b_ref[:, cols(j)]
        h_ref[:, cols(j)] = hj
        hs.append(hj)
        part = jnp.dot(hj.astype(bf16), wr_ref[cols(j), :], preferred_element_type=f32)
        logits = part if logits is None else logits + part
    hp_ref[...] = _pack_halves(jnp.concatenate(hs, axis=1))

    for j in chunks:
        merged_scr[:, cols(j)] = (rest_ref[:, cols(j, c3)] * ap[:, cols(j)].astype(bf16)
                                  + rest_ref[:, cols(j, c4)] * cv[:, cols(j)].astype(bf16))

    route, count = _route_tile(logits, before_ref)
    route_ref[...] = route
    cnt_ref[0] = jnp.broadcast_to(count, (N_EXPERTS, LANES))


def _mixer_tail(x, att, rest, p):
    n = x.shape[0]
    tm = TAIL_TM
    n_tiles = n // tm
    before = jnp.asarray(np.triu(np.ones((tm, tm), np.float32), 1), dtype=bf16)
    consts = (p["w_att_proj"], p["w_conv_proj"], p["w_out"], p["ln1_g"], p["ln1_b"], p["w_router"], before)
    front = lambda i: jnp.minimum(i, n_tiles - 1)
    back = lambda i: jnp.maximum(i - 1, 0)
    return pl.pallas_call(
        functools.partial(_mixer_tail_kernel, n_tiles=n_tiles),
        grid=(n_tiles + 1,),
        in_specs=[pl.BlockSpec((tm, D_MODEL), lambda i: (back(i), 0)),
                  pl.BlockSpec((tm, D_ATT), lambda i: (front(i), 0)),
                  pl.BlockSpec((tm, D_REST), lambda i: (front(i), 0))]
                 + [_full(c) for c in consts],
        out_specs=[pl.BlockSpec((tm, D_MODEL), lambda i: (back(i), 0)),
                   pl.BlockSpec((tm, D_PACKED), lambda i: (back(i), 0)),
                   pl.BlockSpec((ROUTE_ROWS, tm), lambda i: (0, back(i))),
                   pl.BlockSpec((1, N_EXPERTS, LANES), lambda i: (back(i), 0, 0))],
        out_shape=[jax.ShapeDtypeStruct((n, D_MODEL), f32),
                   jax.ShapeDtypeStruct((n, D_PACKED), jnp.uint32),
                   jax.ShapeDtypeStruct((ROUTE_ROWS, n), f32),
                   jax.ShapeDtypeStruct((n_tiles, N_EXPERTS, LANES), f32)],
        scratch_shapes=[pltpu.VMEM((tm, D_MODEL), bf16)],
        compiler_params=_cparams(1),
        name="mixer_tail",
    )(x, att, rest, *consts)


def _route(route, cnt, n, blk):
    n_tiles = n // TAIL_TM
    tile_cnt = cnt[:, :, 0].astype(jnp.int32)
    tile_off = jnp.cumsum(tile_cnt, axis=0) - tile_cnt
    counts = jnp.sum(tile_cnt, axis=0)
    padded = (counts + blk - 1) // blk * blk
    pend = jnp.cumsum(padded)
    pstart = pend - padded
    base = pstart[None, :] + tile_off
    lanes = jnp.arange(N_EXPERTS, dtype=jnp.int32)

    def rows_of(expert_row, rank_row):
        e = expert_row.astype(jnp.int32).reshape(n_tiles, TAIL_TM, 1)
        sel = jnp.sum(jnp.where(e == lanes, base[:, None, :], 0), axis=-1)
        return sel.reshape(n) + rank_row.astype(jnp.int32)

    dest0 = rows_of(route[0], route[4])
    dest1 = rows_of(route[1], route[5])

    n_blocks = (2 * n + N_EXPERTS * (blk - 1) + blk - 1) // blk
    blk_start = jnp.arange(n_blocks, dtype=jnp.int32) * blk
    blk_expert = jnp.sum((pend[None, :] <= blk_start[:, None]).astype(jnp.int32), axis=1)
    blk_expert = jnp.minimum(blk_expert, N_EXPERTS - 1)
    blk_valid = jnp.clip((pstart + counts)[blk_expert] - blk_start, 0, blk)
    blk_valid = jnp.where(blk_start < pend[-1], blk_valid, 0).astype(jnp.int32)
    return dest0, dest1, blk_expert, blk_valid, n_blocks


def _sc_mesh():
    return plsc.VectorSubcoreMesh(core_axis_name="c", subcore_axis_name="s")


def _sc_dispatch(h, dest0, dest1, p_rows):
    n, d = h.shape
    per = n // SC_WORKERS
    assert n % (SC_WORKERS * SC_IDX_WIN) == 0

    @pl.kernel(out_type=jax.ShapeDtypeStruct((p_rows, d), h.dtype), mesh=_sc_mesh(),
               scratch_types=[pltpu.VMEM((2, SC_IDX_WIN), jnp.int32), pltpu.VMEM((SC_ROWS, d), h.dtype)])
    def k(h_hbm, d0_hbm, d1_hbm, xs_hbm, idx, buf):
        base = (lax.axis_index("c") * SC_SUBCORES + lax.axis_index("s")) * per

        @pl.loop(0, per // SC_IDX_WIN)
        def _(w):
            off = base + w * SC_IDX_WIN
            pltpu.sync_copy(d0_hbm.at[pl.ds(off, SC_IDX_WIN)], idx.at[0])
            pltpu.sync_copy(d1_hbm.at[pl.ds(off, SC_IDX_WIN)], idx.at[1])
            for r in range(SC_IDX_WIN // SC_ROWS):
                pltpu.sync_copy(h_hbm.at[pl.ds(off + r * SC_ROWS, SC_ROWS)], buf)
                pltpu.sync_copy(buf, xs_hbm.at[idx.at[0, pl.ds(r * SC_ROWS, SC_ROWS)]])
                pltpu.sync_copy(buf, xs_hbm.at[idx.at[1, pl.ds(r * SC_ROWS, SC_ROWS)]])

    return k(h, dest0, dest1)


def _sc_gather2(ys, dest0, dest1):
    n = dest0.shape[0]
    d = ys.shape[1]
    per = n // SC_WORKERS
    out = jax.ShapeDtypeStruct((n, d), ys.dtype)

    @pl.kernel(out_type=[out, out], mesh=_sc_mesh(),
               scratch_types=[pltpu.VMEM((2, SC_IDX_WIN), jnp.int32), pltpu.VMEM((SC_ROWS, d), ys.dtype)])
    def k(ys_hbm, d0_hbm, d1_hbm, y0_hbm, y1_hbm, idx, buf):
        base = (lax.axis_index("c") * SC_SUBCORES + lax.axis_index("s")) * per

        @pl.loop(0, per // SC_IDX_WIN)
        def _(w):
            off = base + w * SC_IDX_WIN
            pltpu.sync_copy(d0_hbm.at[pl.ds(off, SC_IDX_WIN)], idx.at[0])
            pltpu.sync_copy(d1_hbm.at[pl.ds(off, SC_IDX_WIN)], idx.at[1])
            for r in range(SC_IDX_WIN // SC_ROWS):
                for kk, y_hbm in enumerate((y0_hbm, y1_hbm)):
                    pltpu.sync_copy(ys_hbm.at[idx.at[kk, pl.ds(r * SC_ROWS, SC_ROWS)]], buf)
                    pltpu.sync_copy(buf, y_hbm.at[pl.ds(off + r * SC_ROWS, SC_ROWS)])

    return k(ys, dest0, dest1)


def _experts_kernel(be_ref, bv_ref, slot_ref, nxt_ref, xs_ref, wg_hbm, wu_hbm, wd_hbm, ys_ref,
                    wg_f32, wu_f32, wd_f32, wg_bf, wu_bf, wd_bf, sems):
    i = pl.program_id(0)
    valid = bv_ref[i]
    expert = be_ref[i]
    slot = slot_ref[i]
    run_start = jnp.logical_or(i == 0, expert != be_ref[jnp.maximum(i - 1, 0)])

    def fetch(e, s):
        return [pltpu.make_async_copy(hbm.at[e], buf.at[s], sems.at[s, k])
                for k, (hbm, buf) in enumerate(((wg_hbm, wg_f32), (wu_hbm, wu_f32), (wd_hbm, wd_f32)))]

    @pl.when(i == 0)
    def _():
        for cp in fetch(expert, slot):
            cp.start()

    @pl.when(run_start)
    def _():
        for cp in fetch(expert, slot):
            cp.wait()
        wg_bf[...] = wg_f32[slot].astype(bf16)
        wu_bf[...] = wu_f32[slot].astype(bf16)
        wd_bf[...] = wd_f32[slot].astype(bf16)

        @pl.when(nxt_ref[i] >= 0)
        def _():
            for cp in fetch(nxt_ref[i], 1 - slot):
                cp.start(priority=1)

    @pl.when(valid > 0)
    def _():
        row = lax.broadcasted_iota(jnp.int32, xs_ref.shape, 0)
        keep = row < valid
        lo, hi = _unpack_halves(xs_ref[...])
        x = jnp.concatenate([jnp.where(keep, lo, 0.0), jnp.where(keep, hi, 0.0)], axis=1).astype(bf16)
        g = jnp.dot(x, wg_bf[...], preferred_element_type=f32)
        u = jnp.dot(x, wu_bf[...], preferred_element_type=f32)
        hmid = (g * _sigmoid(g) * u).astype(bf16)
        ys_ref[...] = _pack_halves(jnp.dot(hmid, wd_bf[...], preferred_element_type=f32))

    @pl.when(valid == 0)
    def _():
        ys_ref[...] = jnp.zeros_like(ys_ref)


def _experts(xs, blk_expert, blk_valid, wg, wu, wd, n_blocks, blk):
    starts = jnp.concatenate([jnp.ones((1,), jnp.int32), (blk_expert[1:] != blk_expert[:-1]).astype(jnp.int32)])
    slot = (jnp.cumsum(starts) - 1) % 2
    idx = jnp.arange(n_blocks, dtype=jnp.int32)
    next_start = lax.cummin(jnp.where(starts > 0, idx, n_blocks)[::-1])[::-1]
    next_start = jnp.concatenate([next_start[1:], jnp.full((1,), n_blocks, jnp.int32)])
    nxt = jnp.where(next_start < n_blocks, blk_expert[jnp.minimum(next_start, n_blocks - 1)], -1)
    tile = pl.BlockSpec((blk, D_PACKED), lambda i, *_: (i, 0))
    any_spec = pl.BlockSpec(memory_space=pl.ANY)
    grid_spec = pltpu.PrefetchScalarGridSpec(
        num_scalar_prefetch=4,
        grid=(n_blocks,),
        in_specs=[tile, any_spec, any_spec, any_spec],
        out_specs=tile,
        scratch_shapes=[pltpu.VMEM((2, D_MODEL, D_EXPERT), f32),
                        pltpu.VMEM((2, D_MODEL, D_EXPERT), f32),
                        pltpu.VMEM((2, D_EXPERT, D_MODEL), f32),
                        pltpu.VMEM((D_MODEL, D_EXPERT), bf16),
                        pltpu.VMEM((D_MODEL, D_EXPERT), bf16),
                        pltpu.VMEM((D_EXPERT, D_MODEL), bf16),
                        pltpu.SemaphoreType.DMA((2, 3))],
    )
    return pl.pallas_call(
        _experts_kernel,
        grid_spec=grid_spec,
        out_shape=jax.ShapeDtypeStruct(xs.shape, jnp.uint32),
        compiler_params=_cparams(1),
        name="experts",
    )(blk_expert, blk_valid, slot.astype(jnp.int32), nxt.astype(jnp.int32), xs, wg, wu, wd)


def _final_norm_kernel(h_ref, y0_ref, y1_ref, route_ref, g_ref, b_ref, o_ref):
    route_t = route_ref[...].T
    g0, g1 = route_t[:, 2:3], route_t[:, 3:4]
    lo0, hi0 = _unpack_halves(y0_ref[...])
    lo1, hi1 = _unpack_halves(y1_ref[...])
    ffn = jnp.concatenate([lo0 * g0 + lo1 * g1, hi0 * g0 + hi1 * g1], axis=1)
    o_ref[...] = _layer_norm(ALPHA * h_ref[...] + ffn, g_ref[...], b_ref[...])


def _final_norm(h, y0, y1, route, ln_g, ln_b):
    n = h.shape[0]
    ts = NORM_TS
    tile = pl.BlockSpec((ts, D_MODEL), lambda i: (i, 0))
    packed = pl.BlockSpec((ts, D_PACKED), lambda i: (i, 0))
    return pl.pallas_call(
        _final_norm_kernel,
        grid=(n // ts,),
        in_specs=[tile, packed, packed, pl.BlockSpec((ROUTE_ROWS, ts), lambda i: (0, i)),
                  _full(ln_g), _full(ln_b)],
        out_specs=tile,
        out_shape=jax.ShapeDtypeStruct((n, D_MODEL), f32),
        compiler_params=_cparams(1),
        name="final_norm",
    )(h, y0, y1, route, ln_g, ln_b)


def _encode(x, p, meta_tokens, kv_meta, slabs):
    bsz, t, _ = x.shape
    n = bsz * t
    assert t % NORM_TS == 0 and t % TAIL_TM == 0 and t % PROJ_TM == 0
    xf = x.reshape(n, D_MODEL)
    qkv, rest = _in_proj(xf, meta_tokens, p["w_in"], p["conv_w"], p["conv_b"], t)
    att = _attention(qkv, kv_meta, slabs, bsz, t)
    h1, h1_packed, route, cnt = _mixer_tail(xf, att, rest, p)
    blk = MOE_BLK
    dest0, dest1, blk_expert, blk_valid, n_blocks = _route(route, cnt, n, blk)
    xs = _sc_dispatch(h1_packed, dest0, dest1, n_blocks * blk)
    ys = _experts(xs, blk_expert, blk_valid, p["w_e_gate"], p["w_e_up"], p["w_e_down"], n_blocks, blk)
    y0, y1 = _sc_gather2(ys, dest0, dest1)
    y = _final_norm(h1, y0, y1, route, p["ln2_g"], p["ln2_b"])
    return y.reshape(bsz, t, D_MODEL)


def kernel(x_prompt, x_sample, meta_tokens, w_in, rpb, conv_w, conv_b, w_att_proj, w_conv_proj, w_out,
           ln1_g, ln1_b, w_router_group, w_router_expert, w_e_gate, w_e_up, w_e_down, ln2_g, ln2_b):
    w_router = jnp.concatenate([w_router_expert[0], w_router_group[0]], axis=1)
    w_router = jnp.pad(w_router, ((0, 0), (0, LANES - w_router.shape[1])))
    row = lambda v: v[0].reshape(1, -1).astype(f32)
    p = {
        "w_in": w_in[0].astype(bf16),
        "conv_w": conv_w[0].astype(f32), "conv_b": row(conv_b),
        "w_att_proj": w_att_proj[0].astype(bf16), "w_conv_proj": w_conv_proj[0].astype(bf16),
        "w_out": (0.5 * w_out[0]).astype(bf16), "ln1_g": row(ln1_g), "ln1_b": row(ln1_b),
        "w_router": w_router.astype(bf16),
        "w_e_gate": w_e_gate[0], "w_e_up": w_e_up[0], "w_e_down": w_e_down[0],
        "ln2_g": row(ln2_g), "ln2_b": row(ln2_b),
    }
    meta_tokens = meta_tokens.astype(f32)
    kv_meta = _meta_kv(meta_tokens, p["w_in"][:, D_ATT:3 * D_ATT])
    slabs = _bias_pieces(rpb[0])
    y_prompt = _encode(x_prompt, p, meta_tokens, kv_meta, slabs)
    y_sample = _encode(x_sample, p, meta_tokens, kv_meta, slabs)
    return (y_prompt, y_sample)
```

```python
import functools

import numpy as np
import jax
import jax.numpy as jnp
from jax import lax
from jax.experimental import pallas as pl
from jax.experimental.pallas import tpu as pltpu
from jax.experimental.pallas import tpu_sc as plsc

D_MODEL = 1024
N_META = 16
GRID_W = 64
WIN_H = 8
WIN_W = 16
N_HEADS = 8
HEAD_DIM = 64
D_ATT = N_HEADS * HEAD_DIM
D_CONV = D_MODEL
N_GROUPS = 4
EXPERTS_PER_GROUP = 8
N_EXPERTS = N_GROUPS * EXPERTS_PER_GROUP
D_EXPERT = D_MODEL // 2
DEPTH = 1
ALPHA = (2.0 * DEPTH) ** 0.25
LN_EPS = 1e-5
D_QKV = 4 * D_ATT
LOG2E = 1.4426950408889634
D_REST = D_CONV + 2 * D_MODEL
D_IN_PROJ = 3 * D_ATT + 3 * D_CONV + 2 * D_MODEL

LANES = 128
HEADS_PER_VREG = LANES // HEAD_DIM
N_HEAD_PAIRS = N_HEADS // HEADS_PER_VREG
NEG_BIG = -1e30

PROJ_TM = 1024
PROJ_TN = 512
X_HALO = 8
ATT_ROWS = 32
ATT_GROUP = 4
SLAB_ROWS = WIN_H + 1
ATT_KEYS = SLAB_ROWS * GRID_W + N_META
N_DY = 2 * WIN_H - 1
PIECE_MT = N_DY - 1
PIECE_T0 = PIECE_MT + WIN_H
PIECE_M0 = PIECE_T0 + WIN_H
N_BIAS_PIECES = PIECE_M0 + 1
TAIL_TM = 512
TAIL_CW = 256
NORM_TS = 1024
MOE_BLK = 512
ROUTE_ROWS = 8
VMEM_LIMIT = 56 * 1024 * 1024

SC_CORES = 2
SC_SUBCORES = 16
SC_WORKERS = SC_CORES * SC_SUBCORES
SC_IDX_WIN = 128
SC_ROWS = 64
D_PACKED = D_MODEL // 2

bf16 = jnp.bfloat16
f32 = jnp.float32


def _cparams(n_axes):
    return pltpu.CompilerParams(dimension_semantics=("arbitrary",) * n_axes,
                                vmem_limit_bytes=VMEM_LIMIT)


def _full(a):
    return pl.BlockSpec(a.shape, lambda *_: (0,) * a.ndim)


def _pack_halves(x):
    half = x.shape[1] // 2
    bits = lambda v: lax.bitcast_convert_type(v.astype(bf16).astype(f32), jnp.uint32)
    return (bits(x[:, :half]) >> 16) | bits(x[:, half:])


def _unpack_halves(p):
    lo = lax.bitcast_convert_type(p << 16, f32)
    hi = lax.bitcast_convert_type(p & jnp.uint32(0xFFFF0000), f32)
    return lo, hi


def _in_proj_kernel(x_ref, prev_ref, next_ref, meta_ref, w_ref, convw_ref, convb_ref, qkv_ref, rest_ref,
                    *, tiles_per_seq):
    assert PROJ_TN == D_ATT
    pos = pl.program_id(0) % tiles_per_seq
    tm = x_ref.shape[0]
    x_prev = jnp.where(pos == 0, meta_ref[N_META - X_HALO:, :], prev_ref[...])
    x_body = x_ref[...].astype(bf16)
    x_ext = jnp.concatenate([x_prev, x_ref[...], next_ref[...]], axis=0).astype(bf16)
    body = slice(X_HALO, X_HALO + tm)
    col = lambda c: jnp.dot(x_body, w_ref[:, c:c + PROJ_TN], preferred_element_type=f32)
    col_ext = lambda c: jnp.dot(x_ext, w_ref[:, c:c + PROJ_TN], preferred_element_type=f32)
    proj0 = 3 * D_ATT

    y = (col(0) * (HEAD_DIM ** -0.5 * LOG2E)).astype(bf16)
    lane = lax.broadcasted_iota(jnp.int32, y.shape, 1)
    even = (lane & HEAD_DIM) == 0
    zero = jnp.zeros_like(y)
    qkv_ref[:, :D_ATT] = jnp.where(even, y, zero)
    qkv_ref[:, D_ATT:2 * D_ATT] = jnp.where(even, zero, y)
    for c in range(D_ATT, 3 * D_ATT, PROJ_TN):
        qkv_ref[:, c + D_ATT:c + D_ATT + PROJ_TN] = col(c).astype(bf16)

    row = lax.broadcasted_iota(jnp.int32, (tm, PROJ_TN), 0)
    no_next = jnp.logical_and(pos == tiles_per_seq - 1, row == tm - 1)
    for c in range(0, D_CONV, PROJ_TN):
        u_ext = col_ext(proj0 + D_CONV + c) * col_ext(proj0 + 2 * D_CONV + c)
        u_m1 = pltpu.roll(u_ext, 1, axis=0)[body]
        u_p1 = jnp.where(no_next, 0.0, pltpu.roll(u_ext, tm + 2 * X_HALO - 1, axis=0)[body])
        cw = convw_ref[:, c:c + PROJ_TN]
        conv = u_m1 * cw[0:1] + u_ext[body] * cw[1:2] + u_p1 * cw[2:3] + convb_ref[:, c:c + PROJ_TN]
        rest_ref[:, c:c + PROJ_TN] = (col(proj0 + c) * conv).astype(bf16)
    for c in range(0, 2 * D_MODEL, PROJ_TN):
        g = col(proj0 + 3 * D_CONV + c)
        rest_ref[:, D_CONV + c:D_CONV + c + PROJ_TN] = (jnp.tanh(0.5 * g) + 1.0).astype(bf16)


def _in_proj(x, meta_tokens, w_bf, conv_w, conv_b, t):
    n = x.shape[0]
    tm = PROJ_TM
    hb = tm // X_HALO
    n_halo = n // X_HALO
    halo = lambda index: pl.BlockSpec((X_HALO, D_MODEL), lambda i: (index(i), 0))
    return pl.pallas_call(
        functools.partial(_in_proj_kernel, tiles_per_seq=t // tm),
        grid=(n // tm,),
        in_specs=[pl.BlockSpec((tm, D_MODEL), lambda i: (i, 0)),
                  halo(lambda i: jnp.maximum(i * hb - 1, 0)),
                  halo(lambda i: jnp.minimum((i + 1) * hb, n_halo - 1)),
                  _full(meta_tokens), _full(w_bf), _full(conv_w), _full(conv_b)],
        out_specs=[pl.BlockSpec((tm, D_QKV), lambda i: (i, 0)),
                   pl.BlockSpec((tm, D_REST), lambda i: (i, 0))],
        out_shape=[jax.ShapeDtypeStruct((n, D_QKV), bf16),
                   jax.ShapeDtypeStruct((n, D_REST), bf16)],
        compiler_params=_cparams(1),
        name="in_proj",
    )(x, x, x, meta_tokens, w_bf, conv_w, conv_b)


def _meta_kv_kernel(m_ref, w_ref, kv_ref):
    kv_ref[...] = jnp.dot(m_ref[...].astype(bf16), w_ref[...], preferred_element_type=f32).astype(bf16)


def _meta_kv(meta_tokens, w_kv):
    return pl.pallas_call(
        _meta_kv_kernel,
        out_shape=jax.ShapeDtypeStruct((N_META, 2 * D_ATT), bf16),
        compiler_params=pltpu.CompilerParams(vmem_limit_bytes=VMEM_LIMIT),
        name="meta_kv",
    )(meta_tokens, w_kv)


def _bias_pieces(rpb):
    qc = np.arange(GRID_W)[:, None]
    kc = np.arange(GRID_W)[None, :]
    w_start = np.clip(qc - WIN_W // 2, 0, GRID_W - WIN_W)
    valid = (kc >= w_start) & (kc < w_start + WIN_W)
    n_dx = 2 * WIN_W - 1
    onehot = (kc - qc + WIN_W - 1)[None] == np.arange(n_dx)[:, None, None]
    t = jnp.einsum("hyx,xqk->hyqk", rpb.astype(f32) * LOG2E, jnp.asarray(onehot & valid[None], f32),
                   precision=lax.Precision.HIGHEST)
    t = jnp.where(jnp.asarray(valid)[None, None], t, NEG_BIG)
    t = t.reshape(N_HEAD_PAIRS, HEADS_PER_VREG, N_DY, GRID_W, GRID_W)
    t = jnp.concatenate([t[:, p] for p in range(HEADS_PER_VREG)], axis=2)
    masked = jnp.full((N_HEAD_PAIRS, WIN_H, HEADS_PER_VREG * GRID_W, GRID_W), NEG_BIG, f32)
    zeros = jnp.zeros_like(masked)
    pieces = jnp.concatenate([
        jnp.concatenate([t[:, :N_DY - 1], t[:, 1:]], axis=-1),
        jnp.concatenate([masked, t[:, :WIN_H]], axis=-1),
        jnp.concatenate([t[:, WIN_H - 1:], zeros], axis=-1),
        jnp.concatenate([masked[:, :1], zeros[:, :1]], axis=-1),
    ], axis=1)
    assert pieces.shape[1] == N_BIAS_PIECES
    return pieces


def _attention_kernel(qe_ref, qo_ref, k_ref, v_ref, km_ref, vm_ref, bias_ref, o_ref, *, rows):
    j = pl.program_id(1)
    tq2 = 2 * GRID_W
    lane = lax.broadcasted_iota(jnp.int32, (tq2, LANES), 1)
    low = lane < HEAD_DIM
    nt = (((1,), (1,)), ((), ()))
    slab_keys = SLAB_ROWS * GRID_W

    def row_bias(pair, dy0, off):
        first = jnp.where(off == 0, dy0, PIECE_MT + dy0)
        mids = [dy0 + 2 * p - off for p in range(1, SLAB_ROWS // 2)]
        last = jnp.where(off == 0, PIECE_M0, PIECE_T0 + dy0)
        tail = ATT_KEYS - (SLAB_ROWS // 2) * LANES
        return jnp.concatenate([bias_ref[pair, first]] + [bias_ref[pair, m] for m in mids]
                               + [bias_ref[pair, last][:, :tail]], axis=1)

    def scores(ip, pair):
        r0 = j * ATT_ROWS + 2 * ip
        rs = [jnp.clip(r0 + a - WIN_H // 2, 0, rows - WIN_H) for a in range(2)]
        us = jnp.minimum(rs[0], rows - SLAB_ROWS)
        q0 = pl.multiple_of(ip * tq2, tq2)
        k0 = pl.multiple_of(us * GRID_W, GRID_W)
        cs = slice(pair * LANES, (pair + 1) * LANES)
        qq = jnp.concatenate([qe_ref[pl.ds(q0, tq2), cs], qo_ref[pl.ds(q0, tq2), cs]], axis=0)
        k2 = jnp.concatenate([k_ref[pl.ds(k0, slab_keys), cs], km_ref[:, cs]], axis=0)
        s = lax.dot_general(qq, k2, nt, preferred_element_type=f32)
        b = [row_bias(pair, rs[a] - (r0 + a) + (WIN_H - 1), rs[a] - us) for a in range(2)]
        s = s + jnp.concatenate([b[0][:GRID_W], b[1][:GRID_W], b[0][GRID_W:], b[1][GRID_W:]], axis=0)
        return s, jnp.max(s, axis=-1, keepdims=True), k0, q0, cs

    def weights(state):
        s, m, k0, q0, cs = state
        v2 = jnp.concatenate([v_ref[pl.ds(k0, slab_keys), cs], vm_ref[:, cs]], axis=0)
        e = jnp.exp2(s - m)
        l = jnp.sum(e, axis=-1, keepdims=True)
        return jnp.dot(e.astype(bf16), v2, preferred_element_type=f32), l, q0, cs

    def finish(state):
        o2, l, q0, cs = state
        o2 = o2 / l
        o = jnp.where(low, o2[:tq2], o2[tq2:])
        o_ref[pl.ds(q0, tq2), cs] = o.astype(bf16)

    def group_body(g, carry):
        items = [(g * ATT_GROUP + ip, pair) for ip in range(ATT_GROUP) for pair in range(N_HEAD_PAIRS)]
        a, b = {}, {}
        for step in range(len(items) + 2):
            if step < len(items):
                a[step] = scores(*items[step])
            if 0 <= step - 1 < len(items):
                b[step - 1] = weights(a.pop(step - 1))
            if 0 <= step - 2 < len(items):
                finish(b.pop(step - 2))
        return carry

    lax.fori_loop(0, ATT_ROWS // (2 * ATT_GROUP), group_body, 0)


def _attention(qkv, kv_meta, slabs, bsz, t):
    rows = t // GRID_W
    assert rows >= 2 * WIN_H and rows % ATT_ROWS == 0 and ATT_ROWS % (2 * ATT_GROUP) == 0
    steps = rows // ATT_ROWS
    tq = ATT_ROWS * GRID_W
    return pl.pallas_call(
        functools.partial(_attention_kernel, rows=rows),
        grid=(bsz, steps),
        in_specs=[pl.BlockSpec((tq, D_ATT), lambda b, j: (b * steps + j, 0)),
                  pl.BlockSpec((tq, D_ATT), lambda b, j: (b * steps + j, 1)),
                  pl.BlockSpec((t, D_ATT), lambda b, j: (b, 2)),
                  pl.BlockSpec((t, D_ATT), lambda b, j: (b, 3)),
                  pl.BlockSpec((N_META, D_ATT), lambda b, j: (0, 0)),
                  pl.BlockSpec((N_META, D_ATT), lambda b, j: (0, 1)),
                  _full(slabs)],
        out_specs=pl.BlockSpec((tq, D_ATT), lambda b, j: (b * steps + j, 0)),
        out_shape=jax.ShapeDtypeStruct((bsz * t, D_ATT), bf16),
        compiler_params=_cparams(2),
        name="attention",
    )(qkv, qkv, qkv, qkv, kv_meta, kv_meta, slabs)


def _layer_norm(z, g, b):
    mu = jnp.mean(z, axis=-1, keepdims=True)
    d = z - mu
    var = jnp.mean(d * d, axis=-1, keepdims=True)
    return d * lax.rsqrt(var + LN_EPS) * g + b


def _sigmoid(x):
    return 0.5 * jnp.tanh(0.5 * x) + 0.5


def _route_tile(logits, before_ref):
    tm = logits.shape[0]
    lt = logits.T
    el = lt[:N_EXPERTS]
    gl = lt[N_EXPERTS:N_EXPERTS + N_GROUPS]
    neg = -jnp.inf
    erow = lax.broadcasted_iota(jnp.int32, el.shape, 0).astype(f32)
    grow = lax.broadcasted_iota(jnp.int32, gl.shape, 0).astype(f32)
    first = lambda hit, idx, n: jnp.min(jnp.where(hit, idx, float(n)), axis=0, keepdims=True)

    gmax = jnp.max(gl, axis=0, keepdims=True)
    grp = first(gl == gmax, grow, N_GROUPS)
    pg_sel = 1.0 / jnp.sum(jnp.exp(gl - gmax), axis=0, keepdims=True)

    e_lo = grp * EXPERTS_PER_GROUP
    elm = jnp.where((erow >= e_lo) & (erow < e_lo + EXPERTS_PER_GROUP), el, neg)
    t1 = jnp.max(elm, axis=0, keepdims=True)
    e1 = first(elm == t1, erow, N_EXPERTS)
    el2 = jnp.where(erow == e1, neg, elm)
    t2 = jnp.max(el2, axis=0, keepdims=True)
    e2 = first(el2 == t2, erow, N_EXPERTS)
    r = jnp.exp(t2 - t1)
    g1 = pg_sel / (1.0 + r)
    g2 = pg_sel * r / (1.0 + r)

    hit1 = erow == e1
    hit2 = erow == e2
    onehot = jnp.where(hit1 | hit2, 1.0, 0.0)
    before = jnp.dot(onehot.astype(bf16), before_ref[...], preferred_element_type=f32)
    rank1 = jnp.sum(jnp.where(hit1, before, 0.0), axis=0, keepdims=True)
    rank2 = jnp.sum(jnp.where(hit2, before, 0.0), axis=0, keepdims=True)
    rrow = lax.broadcasted_iota(jnp.int32, (ROUTE_ROWS, tm), 0)
    route = jnp.zeros((ROUTE_ROWS, tm), f32)
    for k, val in enumerate((e1, e2, g1, g2, rank1, rank2)):
        route = jnp.where(rrow == k, val, route)
    return route, jnp.sum(onehot, axis=1, keepdims=True)


def _mixer_tail_kernel(x_ref, att_ref, rest_ref, wap_ref, wcp_ref, wout_ref, g_ref, b_ref, wr_ref, before_ref,
                       h_ref, hp_ref, route_ref, cnt_ref, merged_scr, *, n_tiles):
    i = pl.program_id(0)
    c3, c4 = D_CONV, D_CONV + D_MODEL
    cols = lambda j, base=0: slice(base + j * TAIL_CW, base + (j + 1) * TAIL_CW)
    chunks = range(D_MODEL // TAIL_CW)

    @pl.when(i == 0)
    def _():
        merged_scr[...] = jnp.zeros_like(merged_scr)

    mix = [jnp.dot(merged_scr[...], wout_ref[:, cols(j)], preferred_element_type=f32) for j in chunks]

    cv = jnp.dot(rest_ref[:, :D_CONV], wcp_ref[...], preferred_element_type=f32)
    ap = jnp.dot(att_ref[...], wap_ref[...], preferred_element_type=f32)

    z = [ALPHA * x_ref[:, cols(j)] + mix[j] for j in chunks]
    mu = sum(jnp.sum(zj, axis=-1, keepdims=True) for zj in z) * (1.0 / D_MODEL)
    d = [zj - mu for zj in z]
    var = sum(jnp.sum(dj * dj, axis=-1, keepdims=True) for dj in d) * (1.0 / D_MODEL)
    rstd = lax.rsqrt(var + LN_EPS)
    logits = None
    hs = []
    for j in chunks:
        hj = d[j] * rstd * g_ref[:, cols(j)] + b_ref[:, cols(j)]
        h_ref[:, cols(j)] = hj
        hs.append(hj)
        part = jnp.dot(hj.astype(bf16), wr_ref[cols(j), :], preferred_element_type=f32)
        logits = part if logits is None else logits + part
    hp_ref[...] = _pack_halves(jnp.concatenate(hs, axis=1))

    for j in chunks:
        merged_scr[:, cols(j)] = (rest_ref[:, cols(j, c3)] * ap[:, cols(j)].astype(bf16)
                                  + rest_ref[:, cols(j, c4)] * cv[:, cols(j)].astype(bf16))

    route, count = _route_tile(logits, before_ref)
    route_ref[...] = route
    cnt_ref[0] = jnp.broadcast_to(count, (N_EXPERTS, LANES))


def _mixer_tail(x, att, rest, p):
    n = x.shape[0]
    tm = TAIL_TM
    n_tiles = n // tm
    before = jnp.asarray(np.triu(np.ones((tm, tm), np.float32), 1), dtype=bf16)
    consts = (p["w_att_proj"], p["w_conv_proj"], p["w_out"], p["ln1_g"], p["ln1_b"], p["w_router"], before)
    front = lambda i: jnp.minimum(i, n_tiles - 1)
    back = lambda i: jnp.maximum(i - 1, 0)
    return pl.pallas_call(
        functools.partial(_mixer_tail_kernel, n_tiles=n_tiles),
        grid=(n_tiles + 1,),
        in_specs=[pl.BlockSpec((tm, D_MODEL), lambda i: (back(i), 0)),
                  pl.BlockSpec((tm, D_ATT), lambda i: (front(i), 0)),
                  pl.BlockSpec((tm, D_REST), lambda i: (front(i), 0))]
                 + [_full(c) for c in consts],
        out_specs=[pl.BlockSpec((tm, D_MODEL), lambda i: (back(i), 0)),
                   pl.BlockSpec((tm, D_PACKED), lambda i: (back(i), 0)),
                   pl.BlockSpec((ROUTE_ROWS, tm), lambda i: (0, back(i))),
                   pl.BlockSpec((1, N_EXPERTS, LANES), lambda i: (back(i), 0, 0))],
        out_shape=[jax.ShapeDtypeStruct((n, D_MODEL), f32),
                   jax.ShapeDtypeStruct((n, D_PACKED), jnp.uint32),
                   jax.ShapeDtypeStruct((ROUTE_ROWS, n), f32),
                   jax.ShapeDtypeStruct((n_tiles, N_EXPERTS, LANES), f32)],
        scratch_shapes=[pltpu.VMEM((tm, D_MODEL), bf16)],
        compiler_params=_cparams(1),
        name="mixer_tail",
    )(x, att, rest, *consts)


def _route(route, cnt, n, blk):
    n_tiles = n // TAIL_TM
    tile_cnt = cnt[:, :, 0].astype(jnp.int32)
    tile_off = jnp.cumsum(tile_cnt, axis=0) - tile_cnt
    counts = jnp.sum(tile_cnt, axis=0)
    padded = (counts + blk - 1) // blk * blk
    pend = jnp.cumsum(padded)
    pstart = pend - padded
    base = pstart[None, :] + tile_off
    lanes = jnp.arange(N_EXPERTS, dtype=jnp.int32)

    def rows_of(expert_row, rank_row):
        e = expert_row.astype(jnp.int32).reshape(n_tiles, TAIL_TM, 1)
        sel = jnp.sum(jnp.where(e == lanes, base[:, None, :], 0), axis=-1)
        return sel.reshape(n) + rank_row.astype(jnp.int32)

    dest0 = rows_of(route[0], route[4])
    dest1 = rows_of(route[1], route[5])

    n_blocks = (2 * n + N_EXPERTS * (blk - 1) + blk - 1) // blk
    blk_start = jnp.arange(n_blocks, dtype=jnp.int32) * blk
    blk_expert = jnp.sum((pend[None, :] <= blk_start[:, None]).astype(jnp.int32), axis=1)
    blk_expert = jnp.minimum(blk_expert, N_EXPERTS - 1)
    blk_valid = jnp.clip((pstart + counts)[blk_expert] - blk_start, 0, blk)
    blk_valid = jnp.where(blk_start < pend[-1], blk_valid, 0).astype(jnp.int32)
    return dest0, dest1, blk_expert, blk_valid, n_blocks


def _sc_mesh():
    return plsc.VectorSubcoreMesh(core_axis_name="c", subcore_axis_name="s")


def _sc_dispatch(h, dest0, dest1, p_rows):
    n, d = h.shape
    per = n // SC_WORKERS
    assert n % (SC_WORKERS * SC_IDX_WIN) == 0

    @pl.kernel(out_type=jax.ShapeDtypeStruct((p_rows, d), h.dtype), mesh=_sc_mesh(),
               scratch_types=[pltpu.VMEM((2, SC_IDX_WIN), jnp.int32), pltpu.VMEM((SC_ROWS, d), h.dtype)])
    def k(h_hbm, d0_hbm, d1_hbm, xs_hbm, idx, buf):
        base = (lax.axis_index("c") * SC_SUBCORES + lax.axis_index("s")) * per

        @pl.loop(0, per // SC_IDX_WIN)
        def _(w):
            off = base + w * SC_IDX_WIN
            pltpu.sync_copy(d0_hbm.at[pl.ds(off, SC_IDX_WIN)], idx.at[0])
            pltpu.sync_copy(d1_hbm.at[pl.ds(off, SC_IDX_WIN)], idx.at[1])
            for r in range(SC_IDX_WIN // SC_ROWS):
                pltpu.sync_copy(h_hbm.at[pl.ds(off + r * SC_ROWS, SC_ROWS)], buf)
                pltpu.sync_copy(buf, xs_hbm.at[idx.at[0, pl.ds(r * SC_ROWS, SC_ROWS)]])
                pltpu.sync_copy(buf, xs_hbm.at[idx.at[1, pl.ds(r * SC_ROWS, SC_ROWS)]])

    return k(h, dest0, dest1)


def _sc_gather2(ys, dest0, dest1):
    n = dest0.shape[0]
    d = ys.shape[1]
    per = n // SC_WORKERS
    out = jax.ShapeDtypeStruct((n, d), ys.dtype)

    @pl.kernel(out_type=[out, out], mesh=_sc_mesh(),
               scratch_types=[pltpu.VMEM((2, SC_IDX_WIN), jnp.int32), pltpu.VMEM((SC_ROWS, d), ys.dtype)])
    def k(ys_hbm, d0_hbm, d1_hbm, y0_hbm, y1_hbm, idx, buf):
        base = (lax.axis_index("c") * SC_SUBCORES + lax.axis_index("s")) * per

        @pl.loop(0, per // SC_IDX_WIN)
        def _(w):
            off = base + w * SC_IDX_WIN
            pltpu.sync_copy(d0_hbm.at[pl.ds(off, SC_IDX_WIN)], idx.at[0])
            pltpu.sync_copy(d1_hbm.at[pl.ds(off, SC_IDX_WIN)], idx.at[1])
            for r in range(SC_IDX_WIN // SC_ROWS):
                for kk, y_hbm in enumerate((y0_hbm, y1_hbm)):
                    pltpu.sync_copy(ys_hbm.at[idx.at[kk, pl.ds(r * SC_ROWS, SC_ROWS)]], buf)
                    pltpu.sync_copy(buf, y_hbm.at[pl.ds(off + r * SC_ROWS, SC_ROWS)])

    return k(ys, dest0, dest1)


def _experts_kernel(be_ref, bv_ref, slot_ref, nxt_ref, xs_ref, wg_hbm, wu_hbm, wd_hbm, ys_ref,
                    wg_f32, wu_f32, wd_f32, wg_bf, wu_bf, wd_bf, sems):
    i = pl.program_id(0)
    valid = bv_ref[i]
    expert = be_ref[i]
    slot = slot_ref[i]
    run_start = jnp.logical_or(i == 0, expert != be_ref[jnp.maximum(i - 1, 0)])

    def fetch(e, s):
        return [pltpu.make_async_copy(hbm.at[e], buf.at[s], sems.at[s, k])
                for k, (hbm, buf) in enumerate(((wg_hbm, wg_f32), (wu_hbm, wu_f32), (wd_hbm, wd_f32)))]

    @pl.when(i == 0)
    def _():
        for cp in fetch(expert, slot):
            cp.start()

    @pl.when(run_start)
    def _():
        for cp in fetch(expert, slot):
            cp.wait()
        wg_bf[...] = wg_f32[slot].astype(bf16)
        wu_bf[...] = wu_f32[slot].astype(bf16)
        wd_bf[...] = wd_f32[slot].astype(bf16)

        @pl.when(nxt_ref[i] >= 0)
        def _():
            for cp in fetch(nxt_ref[i], 1 - slot):
                cp.start(priority=1)

    @pl.when(valid > 0)
    def _():
        row = lax.broadcasted_iota(jnp.int32, xs_ref.shape, 0)
        keep = row < valid
        lo, hi = _unpack_halves(xs_ref[...])
        x = jnp.concatenate([jnp.where(keep, lo, 0.0), jnp.where(keep, hi, 0.0)], axis=1).astype(bf16)
        g = jnp.dot(x, wg_bf[...], preferred_element_type=f32)
        u = jnp.dot(x, wu_bf[...], preferred_element_type=f32)
        hmid = (g * _sigmoid(g) * u).astype(bf16)
        ys_ref[...] = _pack_halves(jnp.dot(hmid, wd_bf[...], preferred_element_type=f32))

    @pl.when(valid == 0)
    def _():
        ys_ref[...] = jnp.zeros_like(ys_ref)


def _experts(xs, blk_expert, blk_valid, wg, wu, wd, n_blocks, blk):
    starts = jnp.concatenate([jnp.ones((1,), jnp.int32), (blk_expert[1:] != blk_expert[:-1]).astype(jnp.int32)])
    slot = (jnp.cumsum(starts) - 1) % 2
    idx = jnp.arange(n_blocks, dtype=jnp.int32)
    next_start = lax.cummin(jnp.where(starts > 0, idx, n_blocks)[::-1])[::-1]
    next_start = jnp.concatenate([next_start[1:], jnp.full((1,), n_blocks, jnp.int32)])
    nxt = jnp.where(next_start < n_blocks, blk_expert[jnp.minimum(next_start, n_blocks - 1)], -1)
    tile = pl.BlockSpec((blk, D_PACKED), lambda i, *_: (i, 0))
    any_spec = pl.BlockSpec(memory_space=pl.ANY)
    grid_spec = pltpu.PrefetchScalarGridSpec(
        num_scalar_prefetch=4,
        grid=(n_blocks,),
        in_specs=[tile, any_spec, any_spec, any_spec],
        out_specs=tile,
        scratch_shapes=[pltpu.VMEM((2, D_MODEL, D_EXPERT), f32),
                        pltpu.VMEM((2, D_MODEL, D_EXPERT), f32),
                        pltpu.VMEM((2, D_EXPERT, D_MODEL), f32),
                        pltpu.VMEM((D_MODEL, D_EXPERT), bf16),
                        pltpu.VMEM((D_MODEL, D_EXPERT), bf16),
                        pltpu.VMEM((D_EXPERT, D_MODEL), bf16),
                        pltpu.SemaphoreType.DMA((2, 3))],
    )
    return pl.pallas_call(
        _experts_kernel,
        grid_spec=grid_spec,
        out_shape=jax.ShapeDtypeStruct(xs.shape, jnp.uint32),
        compiler_params=_cparams(1),
        name="experts",
    )(blk_expert, blk_valid, slot.astype(jnp.int32), nxt.astype(jnp.int32), xs, wg, wu, wd)


def _final_norm_kernel(h_ref, y0_ref, y1_ref, route_ref, g_ref, b_ref, o_ref):
    route_t = route_ref[...].T
    g0, g1 = route_t[:, 2:3], route_t[:, 3:4]
    lo0, hi0 = _unpack_halves(y0_ref[...])
    lo1, hi1 = _unpack_halves(y1_ref[...])
    ffn = jnp.concatenate([lo0 * g0 + lo1 * g1, hi0 * g0 + hi1 * g1], axis=1)
    o_ref[...] = _layer_norm(ALPHA * h_ref[...] + ffn, g_ref[...], b_ref[...])


def _final_norm(h, y0, y1, route, ln_g, ln_b):
    n = h.shape[0]
    ts = NORM_TS
    tile = pl.BlockSpec((ts, D_MODEL), lambda i: (i, 0))
    packed = pl.BlockSpec((ts, D_PACKED), lambda i: (i, 0))
    return pl.pallas_call(
        _final_norm_kernel,
        grid=(n // ts,),
        in_specs=[tile, packed, packed, pl.BlockSpec((ROUTE_ROWS, ts), lambda i: (0, i)),
                  _full(ln_g), _full(ln_b)],
        out_specs=tile,
        out_shape=jax.ShapeDtypeStruct((n, D_MODEL), f32),
        compiler_params=_cparams(1),
        name="final_norm",
    )(h, y0, y1, route, ln_g, ln_b)


def _encode(x, p, meta_tokens, kv_meta, slabs):
    bsz, t, _ = x.shape
    n = bsz * t
    assert t % NORM_TS == 0 and t % TAIL_TM == 0 and t % PROJ_TM == 0
    xf = x.reshape(n, D_MODEL)
    qkv, rest = _in_proj(xf, meta_tokens, p["w_in"], p["conv_w"], p["conv_b"], t)
    att = _attention(qkv, kv_meta, slabs, bsz, t)
    h1, h1_packed, route, cnt = _mixer_tail(xf, att, rest, p)
    blk = MOE_BLK
    dest0, dest1, blk_expert, blk_valid, n_blocks = _route(route, cnt, n, blk)
    xs = _sc_dispatch(h1_packed, dest0, dest1, n_blocks * blk)
    ys = _experts(xs, blk_expert, blk_valid, p["w_e_gate"], p["w_e_up"], p["w_e_down"], n_blocks, blk)
    y0, y1 = _sc_gather2(ys, dest0, dest1)
    y = _final_norm(h1, y0, y1, route, p["ln2_g"], p["ln2_b"])
    return y.reshape(bsz, t, D_MODEL)


def kernel(x_prompt, x_sample, meta_tokens, w_in, rpb, conv_w, conv_b, w_att_proj, w_conv_proj, w_out,
           ln1_g, ln1_b, w_router_group, w_router_expert, w_e_gate, w_e_up, w_e_down, ln2_g, ln2_b):
    w_router = jnp.concatenate([w_router_expert[0], w_router_group[0]], axis=1)
    w_router = jnp.pad(w_router, ((0, 0), (0, LANES - w_router.shape[1])))
    row = lambda v: v[0].reshape(1, -1).astype(f32)
    p = {
        "w_in": w_in[0].astype(bf16),
        "conv_w": conv_w[0].astype(f32), "conv_b": row(conv_b),
        "w_att_proj": w_att_proj[0].astype(bf16), "w_conv_proj": w_conv_proj[0].astype(bf16),
        "w_out": (0.5 * w_out[0]).astype(bf16), "ln1_g": row(ln1_g), "ln1_b": row(ln1_b),
        "w_router": w_router.astype(bf16),
        "w_e_gate": w_e_gate[0], "w_e_up": w_e_up[0], "w_e_down": w_e_down[0],
        "ln2_g": row(ln2_g), "ln2_b": row(ln2_b),
    }
    meta_tokens = meta_tokens.astype(f32)
    kv_meta = _meta_kv(meta_tokens, p["w_in"][:, D_ATT:3 * D_ATT])
    slabs = _bias_pieces(rpb[0])
    y_prompt = _encode(x_prompt, p, meta_tokens, kv_meta, slabs)
    y_sample = _encode(x_sample, p, meta_tokens, kv_meta, slabs)
    return (y_prompt, y_sample)
```

```python
import functools

import numpy as np
import jax
import jax.numpy as jnp
from jax import lax
from jax.experimental import pallas as pl
from jax.experimental.pallas import tpu as pltpu
from jax.experimental.pallas import tpu_sc as plsc

D_MODEL = 1024
N_META = 16
GRID_W = 64
WIN_H = 8
WIN_W = 16
N_HEADS = 8
HEAD_DIM = 64
D_ATT = N_HEADS * HEAD_DIM
D_CONV = D_MODEL
N_GROUPS = 4
EXPERTS_PER_GROUP = 8
N_EXPERTS = N_GROUPS * EXPERTS_PER_GROUP
D_EXPERT = D_MODEL // 2
DEPTH = 1
ALPHA = (2.0 * DEPTH) ** 0.25
LN_EPS = 1e-5
D_QKV = 4 * D_ATT
LOG2E = 1.4426950408889634
D_REST = D_CONV + 2 * D_MODEL
D_IN_PROJ = 3 * D_ATT + 3 * D_CONV + 2 * D_MODEL

LANES = 128
HEADS_PER_VREG = LANES // HEAD_DIM
N_HEAD_PAIRS = N_HEADS // HEADS_PER_VREG
NEG_BIG = -1e30

PROJ_TM = 1024
PROJ_TN = 512
X_HALO = 8
ATT_ROWS = 32
ATT_GROUP = 4
SLAB_ROWS = WIN_H + 1
ATT_KEYS = SLAB_ROWS * GRID_W + N_META
N_DY = 2 * WIN_H - 1
PIECE_MT = N_DY - 1
PIECE_T0 = PIECE_MT + WIN_H
PIECE_M0 = PIECE_T0 + WIN_H
N_BIAS_PIECES = PIECE_M0 + 1
TAIL_TM = 512
TAIL_CW = 256
NORM_TS = 1024
MOE_BLK = 512
ROUTE_ROWS = 8
VMEM_LIMIT = 56 * 1024 * 1024

SC_CORES = 2
SC_SUBCORES = 16
SC_WORKERS = SC_CORES * SC_SUBCORES
SC_IDX_WIN = 128
SC_ROWS = 128
D_PACKED = D_MODEL // 2

bf16 = jnp.bfloat16
f32 = jnp.float32


def _cparams(n_axes):
    return pltpu.CompilerParams(dimension_semantics=("arbitrary",) * n_axes,
                                vmem_limit_bytes=VMEM_LIMIT)


def _full(a):
    return pl.BlockSpec(a.shape, lambda *_: (0,) * a.ndim)


def _pack_halves(x):
    half = x.shape[1] // 2
    bits = lambda v: lax.bitcast_convert_type(v.astype(bf16).astype(f32), jnp.uint32)
    return (bits(x[:, :half]) >> 16) | bits(x[:, half:])


def _unpack_halves(p):
    lo = lax.bitcast_convert_type(p << 16, f32)
    hi = lax.bitcast_convert_type(p & jnp.uint32(0xFFFF0000), f32)
    return lo, hi


def _in_proj_kernel(x_ref, prev_ref, next_ref, meta_ref, w_ref, convw_ref, convb_ref, qkv_ref, rest_ref,
                    *, tiles_per_seq):
    assert PROJ_TN == D_ATT
    pos = pl.program_id(0) % tiles_per_seq
    tm = x_ref.shape[0]
    x_prev = jnp.where(pos == 0, meta_ref[N_META - X_HALO:, :], prev_ref[...])
    x_body = x_ref[...].astype(bf16)
    x_ext = jnp.concatenate([x_prev, x_ref[...], next_ref[...]], axis=0).astype(bf16)
    body = slice(X_HALO, X_HALO + tm)
    col = lambda c: jnp.dot(x_body, w_ref[:, c:c + PROJ_TN], preferred_element_type=f32)
    col_ext = lambda c: jnp.dot(x_ext, w_ref[:, c:c + PROJ_TN], preferred_element_type=f32)
    proj0 = 3 * D_ATT

    y = (col(0) * (HEAD_DIM ** -0.5 * LOG2E)).astype(bf16)
    lane = lax.broadcasted_iota(jnp.int32, y.shape, 1)
    even = (lane & HEAD_DIM) == 0
    zero = jnp.zeros_like(y)
    qkv_ref[:, :D_ATT] = jnp.where(even, y, zero)
    qkv_ref[:, D_ATT:2 * D_ATT] = jnp.where(even, zero, y)
    for c in range(D_ATT, 3 * D_ATT, PROJ_TN):
        qkv_ref[:, c + D_ATT:c + D_ATT + PROJ_TN] = col(c).astype(bf16)

    row = lax.broadcasted_iota(jnp.int32, (tm, PROJ_TN), 0)
    no_next = jnp.logical_and(pos == tiles_per_seq - 1, row == tm - 1)
    for c in range(0, D_CONV, PROJ_TN):
        u_ext = col_ext(proj0 + D_CONV + c) * col_ext(proj0 + 2 * D_CONV + c)
        u_m1 = pltpu.roll(u_ext, 1, axis=0)[body]
        u_p1 = jnp.where(no_next, 0.0, pltpu.roll(u_ext, tm + 2 * X_HALO - 1, axis=0)[body])
        cw = convw_ref[:, c:c + PROJ_TN]
        conv = u_m1 * cw[0:1] + u_ext[body] * cw[1:2] + u_p1 * cw[2:3] + convb_ref[:, c:c + PROJ_TN]
        rest_ref[:, c:c + PROJ_TN] = (col(proj0 + c) * conv).astype(bf16)
    for c in range(0, 2 * D_MODEL, PROJ_TN):
        g = col(proj0 + 3 * D_CONV + c)
        rest_ref[:, D_CONV + c:D_CONV + c + PROJ_TN] = (jnp.tanh(0.5 * g) + 1.0).astype(bf16)


def _in_proj(x, meta_tokens, w_bf, conv_w, conv_b, t):
    n = x.shape[0]
    tm = PROJ_TM
    hb = tm // X_HALO
    n_halo = n // X_HALO
    halo = lambda index: pl.BlockSpec((X_HALO, D_MODEL), lambda i: (index(i), 0))
    return pl.pallas_call(
        functools.partial(_in_proj_kernel, tiles_per_seq=t // tm),
        grid=(n // tm,),
        in_specs=[pl.BlockSpec((tm, D_MODEL), lambda i: (i, 0)),
                  halo(lambda i: jnp.maximum(i * hb - 1, 0)),
                  halo(lambda i: jnp.minimum((i + 1) * hb, n_halo - 1)),
                  _full(meta_tokens), _full(w_bf), _full(conv_w), _full(conv_b)],
        out_specs=[pl.BlockSpec((tm, D_QKV), lambda i: (i, 0)),
                   pl.BlockSpec((tm, D_REST), lambda i: (i, 0))],
        out_shape=[jax.ShapeDtypeStruct((n, D_QKV), bf16),
                   jax.ShapeDtypeStruct((n, D_REST), bf16)],
        compiler_params=_cparams(1),
        name="in_proj",
    )(x, x, x, meta_tokens, w_bf, conv_w, conv_b)


def _meta_kv_kernel(m_ref, w_ref, kv_ref):
    kv_ref[...] = jnp.dot(m_ref[...].astype(bf16), w_ref[...], preferred_element_type=f32).astype(bf16)


def _meta_kv(meta_tokens, w_kv):
    return pl.pallas_call(
        _meta_kv_kernel,
        out_shape=jax.ShapeDtypeStruct((N_META, 2 * D_ATT), bf16),
        compiler_params=pltpu.CompilerParams(vmem_limit_bytes=VMEM_LIMIT),
        name="meta_kv",
    )(meta_tokens, w_kv)


def _bias_pieces(rpb):
    qc = np.arange(GRID_W)[:, None]
    kc = np.arange(GRID_W)[None, :]
    w_start = np.clip(qc - WIN_W // 2, 0, GRID_W - WIN_W)
    valid = (kc >= w_start) & (kc < w_start + WIN_W)
    n_dx = 2 * WIN_W - 1
    onehot = (kc - qc + WIN_W - 1)[None] == np.arange(n_dx)[:, None, None]
    t = jnp.einsum("hyx,xqk->hyqk", rpb.astype(f32) * LOG2E, jnp.asarray(onehot & valid[None], f32),
                   precision=lax.Precision.HIGHEST)
    t = jnp.where(jnp.asarray(valid)[None, None], t, NEG_BIG)
    t = t.reshape(N_HEAD_PAIRS, HEADS_PER_VREG, N_DY, GRID_W, GRID_W)
    t = jnp.concatenate([t[:, p] for p in range(HEADS_PER_VREG)], axis=2)
    masked = jnp.full((N_HEAD_PAIRS, WIN_H, HEADS_PER_VREG * GRID_W, GRID_W), NEG_BIG, f32)
    zeros = jnp.zeros_like(masked)
    pieces = jnp.concatenate([
        jnp.concatenate([t[:, :N_DY - 1], t[:, 1:]], axis=-1),
        jnp.concatenate([masked, t[:, :WIN_H]], axis=-1),
        jnp.concatenate([t[:, WIN_H - 1:], zeros], axis=-1),
        jnp.concatenate([masked[:, :1], zeros[:, :1]], axis=-1),
    ], axis=1)
    assert pieces.shape[1] == N_BIAS_PIECES
    return pieces


def _attention_kernel(qe_ref, qo_ref, k_ref, v_ref, km_ref, vm_ref, bias_ref, o_ref, *, rows):
    j = pl.program_id(1)
    tq2 = 2 * GRID_W
    lane = lax.broadcasted_iota(jnp.int32, (tq2, LANES), 1)
    low = lane < HEAD_DIM
    nt = (((1,), (1,)), ((), ()))
    slab_keys = SLAB_ROWS * GRID_W

    def row_bias(pair, dy0, off):
        first = jnp.where(off == 0, dy0, PIECE_MT + dy0)
        mids = [dy0 + 2 * p - off for p in range(1, SLAB_ROWS // 2)]
        last = jnp.where(off == 0, PIECE_M0, PIECE_T0 + dy0)
        tail = ATT_KEYS - (SLAB_ROWS // 2) * LANES
        return jnp.concatenate([bias_ref[pair, first]] + [bias_ref[pair, m] for m in mids]
                               + [bias_ref[pair, last][:, :tail]], axis=1)

    def scores(ip, pair):
        r0 = j * ATT_ROWS + 2 * ip
        rs = [jnp.clip(r0 + a - WIN_H // 2, 0, rows - WIN_H) for a in range(2)]
        us = jnp.minimum(rs[0], rows - SLAB_ROWS)
        q0 = pl.multiple_of(ip * tq2, tq2)
        k0 = pl.multiple_of(us * GRID_W, GRID_W)
        cs = slice(pair * LANES, (pair + 1) * LANES)
        qq = jnp.concatenate([qe_ref[pl.ds(q0, tq2), cs], qo_ref[pl.ds(q0, tq2), cs]], axis=0)
        k2 = jnp.concatenate([k_ref[pl.ds(k0, slab_keys), cs], km_ref[:, cs]], axis=0)
        s = lax.dot_general(qq, k2, nt, preferred_element_type=f32)
        b = [row_bias(pair, rs[a] - (r0 + a) + (WIN_H - 1), rs[a] - us) for a in range(2)]
        s = s + jnp.concatenate([b[0][:GRID_W], b[1][:GRID_W], b[0][GRID_W:], b[1][GRID_W:]], axis=0)
        return s, jnp.max(s, axis=-1, keepdims=True), k0, q0, cs

    def weights(state):
        s, m, k0, q0, cs = state
        v2 = jnp.concatenate([v_ref[pl.ds(k0, slab_keys), cs], vm_ref[:, cs]], axis=0)
        e = jnp.exp2(s - m)
        l = jnp.sum(e, axis=-1, keepdims=True)
        return jnp.dot(e.astype(bf16), v2, preferred_element_type=f32), l, q0, cs

    def finish(state):
        o2, l, q0, cs = state
        o2 = o2 / l
        o = jnp.where(low, o2[:tq2], o2[tq2:])
        o_ref[pl.ds(q0, tq2), cs] = o.astype(bf16)

    def group_body(g, carry):
        items = [(g * ATT_GROUP + ip, pair) for ip in range(ATT_GROUP) for pair in range(N_HEAD_PAIRS)]
        a, b = {}, {}
        for step in range(len(items) + 2):
            if step < len(items):
                a[step] = scores(*items[step])
            if 0 <= step - 1 < len(items):
                b[step - 1] = weights(a.pop(step - 1))
            if 0 <= step - 2 < len(items):
                finish(b.pop(step - 2))
        return carry

    lax.fori_loop(0, ATT_ROWS // (2 * ATT_GROUP), group_body, 0)


def _attention(qkv, kv_meta, slabs, bsz, t):
    rows = t // GRID_W
    assert rows >= 2 * WIN_H and rows % ATT_ROWS == 0 and ATT_ROWS % (2 * ATT_GROUP) == 0
    steps = rows // ATT_ROWS
    tq = ATT_ROWS * GRID_W
    return pl.pallas_call(
        functools.partial(_attention_kernel, rows=rows),
        grid=(bsz, steps),
        in_specs=[pl.BlockSpec((tq, D_ATT), lambda b, j: (b * steps + j, 0)),
                  pl.BlockSpec((tq, D_ATT), lambda b, j: (b * steps + j, 1)),
                  pl.BlockSpec((t, D_ATT), lambda b, j: (b, 2)),
                  pl.BlockSpec((t, D_ATT), lambda b, j: (b, 3)),
                  pl.BlockSpec((N_META, D_ATT), lambda b, j: (0, 0)),
                  pl.BlockSpec((N_META, D_ATT), lambda b, j: (0, 1)),
                  _full(slabs)],
        out_specs=pl.BlockSpec((tq, D_ATT), lambda b, j: (b * steps + j, 0)),
        out_shape=jax.ShapeDtypeStruct((bsz * t, D_ATT), bf16),
        compiler_params=_cparams(2),
        name="attention",
    )(qkv, qkv, qkv, qkv, kv_meta, kv_meta, slabs)


def _layer_norm(z, g, b):
    mu = jnp.mean(z, axis=-1, keepdims=True)
    d = z - mu
    var = jnp.mean(d * d, axis=-1, keepdims=True)
    return d * lax.rsqrt(var + LN_EPS) * g + b


def _sigmoid(x):
    return 0.5 * jnp.tanh(0.5 * x) + 0.5


def _route_tile(logits, before_ref):
    tm = logits.shape[0]
    lt = logits.T
    el = lt[:N_EXPERTS]
    gl = lt[N_EXPERTS:N_EXPERTS + N_GROUPS]
    neg = -jnp.inf
    erow = lax.broadcasted_iota(jnp.int32, el.shape, 0).astype(f32)
    grow = lax.broadcasted_iota(jnp.int32, gl.shape, 0).astype(f32)
    first = lambda hit, idx, n: jnp.min(jnp.where(hit, idx, float(n)), axis=0, keepdims=True)

    gmax = jnp.max(gl, axis=0, keepdims=True)
    grp = first(gl == gmax, grow, N_GROUPS)
    pg_sel = 1.0 / jnp.sum(jnp.exp(gl - gmax), axis=0, keepdims=True)

    e_lo = grp * EXPERTS_PER_GROUP
    elm = jnp.where((erow >= e_lo) & (erow < e_lo + EXPERTS_PER_GROUP), el, neg)
    t1 = jnp.max(elm, axis=0, keepdims=True)
    e1 = first(elm == t1, erow, N_EXPERTS)
    el2 = jnp.where(erow == e1, neg, elm)
    t2 = jnp.max(el2, axis=0, keepdims=True)
    e2 = first(el2 == t2, erow, N_EXPERTS)
    r = jnp.exp(t2 - t1)
    g1 = pg_sel / (1.0 + r)
    g2 = pg_sel * r / (1.0 + r)

    hit1 = erow == e1
    hit2 = erow == e2
    onehot = jnp.where(hit1 | hit2, 1.0, 0.0)
    before = jnp.dot(onehot.astype(bf16), before_ref[...], preferred_element_type=f32)
    rank1 = jnp.sum(jnp.where(hit1, before, 0.0), axis=0, keepdims=True)
    rank2 = jnp.sum(jnp.where(hit2, before, 0.0), axis=0, keepdims=True)
    rrow = lax.broadcasted_iota(jnp.int32, (ROUTE_ROWS, tm), 0)
    route = jnp.zeros((ROUTE_ROWS, tm), f32)
    for k, val in enumerate((e1, e2, g1, g2, rank1, rank2)):
        route = jnp.where(rrow == k, val, route)
    return route, jnp.sum(onehot, axis=1, keepdims=True)


def _mixer_tail_kernel(x_ref, att_ref, rest_ref, wap_ref, wcp_ref, wout_ref, g_ref, b_ref, wr_ref, before_ref,
                       h_ref, hp_ref, route_ref, cnt_ref, merged_scr, *, n_tiles):
    i = pl.program_id(0)
    c3, c4 = D_CONV, D_CONV + D_MODEL
    cols = lambda j, base=0: slice(base + j * TAIL_CW, base + (j + 1) * TAIL_CW)
    chunks = range(D_MODEL // TAIL_CW)

    @pl.when(i == 0)
    def _():
        merged_scr[...] = jnp.zeros_like(merged_scr)

    mix = [jnp.dot(merged_scr[...], wout_ref[:, cols(j)], preferred_element_type=f32) for j in chunks]

    cv = jnp.dot(rest_ref[:, :D_CONV], wcp_ref[...], preferred_element_type=f32)
    ap = jnp.dot(att_ref[...], wap_ref[...], preferred_element_type=f32)

    z = [ALPHA * x_ref[:, cols(j)] + mix[j] for j in chunks]
    mu = sum(jnp.sum(zj, axis=-1, keepdims=True) for zj in z) * (1.0 / D_MODEL)
    d = [zj - mu for zj in z]
    var = sum(jnp.sum(dj * dj, axis=-1, keepdims=True) for dj in d) * (1.0 / D_MODEL)
    rstd = lax.rsqrt(var + LN_EPS)
    logits = None
    hs = []
    for j in chunks:
        hj = d[j] * rstd * g_ref[:, cols(j)] + b_ref[:, cols(j)]
        h_ref[:, cols(j)] = hj
        hs.append(hj)
        part = jnp.dot(hj.astype(bf16), wr_ref[cols(j), :], preferred_element_type=f32)
        logits = part if logits is None else logits + part
    hp_ref[...] = _pack_halves(jnp.concatenate(hs, axis=1))

    for j in chunks:
        merged_scr[:, cols(j)] = (rest_ref[:, cols(j, c3)] * ap[:, cols(j)].astype(bf16)
                                  + rest_ref[:, cols(j, c4)] * cv[:, cols(j)].astype(bf16))

    route, count = _route_tile(logits, before_ref)
    route_ref[...] = route
    cnt_ref[0] = jnp.broadcast_to(count, (N_EXPERTS, LANES))


def _mixer_tail(x, att, rest, p):
    n = x.shape[0]
    tm = TAIL_TM
    n_tiles = n // tm
    before = jnp.asarray(np.triu(np.ones((tm, tm), np.float32), 1), dtype=bf16)
    consts = (p["w_att_proj"], p["w_conv_proj"], p["w_out"], p["ln1_g"], p["ln1_b"], p["w_router"], before)
    front = lambda i: jnp.minimum(i, n_tiles - 1)
    back = lambda i: jnp.maximum(i - 1, 0)
    return pl.pallas_call(
        functools.partial(_mixer_tail_kernel, n_tiles=n_tiles),
        grid=(n_tiles + 1,),
        in_specs=[pl.BlockSpec((tm, D_MODEL), lambda i: (back(i), 0)),
                  pl.BlockSpec((tm, D_ATT), lambda i: (front(i), 0)),
                  pl.BlockSpec((tm, D_REST), lambda i: (front(i), 0))]
                 + [_full(c) for c in consts],
        out_specs=[pl.BlockSpec((tm, D_MODEL), lambda i: (back(i), 0)),
                   pl.BlockSpec((tm, D_PACKED), lambda i: (back(i), 0)),
                   pl.BlockSpec((ROUTE_ROWS, tm), lambda i: (0, back(i))),
                   pl.BlockSpec((1, N_EXPERTS, LANES), lambda i: (back(i), 0, 0))],
        out_shape=[jax.ShapeDtypeStruct((n, D_MODEL), f32),
                   jax.ShapeDtypeStruct((n, D_PACKED), jnp.uint32),
                   jax.ShapeDtypeStruct((ROUTE_ROWS, n), f32),
                   jax.ShapeDtypeStruct((n_tiles, N_EXPERTS, LANES), f32)],
        scratch_shapes=[pltpu.VMEM((tm, D_MODEL), bf16)],
        compiler_params=_cparams(1),
        name="mixer_tail",
    )(x, att, rest, *consts)


def _route(route, cnt, n, blk):
    n_tiles = n // TAIL_TM
    tile_cnt = cnt[:, :, 0].astype(jnp.int32)
    tile_off = jnp.cumsum(tile_cnt, axis=0) - tile_cnt
    counts = jnp.sum(tile_cnt, axis=0)
    padded = (counts + blk - 1) // blk * blk
    pend = jnp.cumsum(padded)
    pstart = pend - padded
    base = pstart[None, :] + tile_off
    lanes = jnp.arange(N_EXPERTS, dtype=jnp.int32)

    def rows_of(expert_row, rank_row):
        e = expert_row.astype(jnp.int32).reshape(n_tiles, TAIL_TM, 1)
        sel = jnp.sum(jnp.where(e == lanes, base[:, None, :], 0), axis=-1)
        return sel.reshape(n) + rank_row.astype(jnp.int32)

    dest0 = rows_of(route[0], route[4])
    dest1 = rows_of(route[1], route[5])

    n_blocks = (2 * n + N_EXPERTS * (blk - 1) + blk - 1) // blk
    blk_start = jnp.arange(n_blocks, dtype=jnp.int32) * blk
    blk_expert = jnp.sum((pend[None, :] <= blk_start[:, None]).astype(jnp.int32), axis=1)
    blk_expert = jnp.minimum(blk_expert, N_EXPERTS - 1)
    blk_valid = jnp.clip((pstart + counts)[blk_expert] - blk_start, 0, blk)
    blk_valid = jnp.where(blk_start < pend[-1], blk_valid, 0).astype(jnp.int32)
    return dest0, dest1, blk_expert, blk_valid, n_blocks


def _sc_mesh():
    return plsc.VectorSubcoreMesh(core_axis_name="c", subcore_axis_name="s")


def _sc_dispatch(h, dest0, dest1, p_rows):
    n, d = h.shape
    per = n // SC_WORKERS
    assert n % (SC_WORKERS * SC_IDX_WIN) == 0

    @pl.kernel(out_type=jax.ShapeDtypeStruct((p_rows, d), h.dtype), mesh=_sc_mesh(),
               scratch_types=[pltpu.VMEM((2, SC_IDX_WIN), jnp.int32), pltpu.VMEM((SC_ROWS, d), h.dtype)])
    def k(h_hbm, d0_hbm, d1_hbm, xs_hbm, idx, buf):
        base = (lax.axis_index("c") * SC_SUBCORES + lax.axis_index("s")) * per

        @pl.loop(0, per // SC_IDX_WIN)
        def _(w):
            off = base + w * SC_IDX_WIN
            pltpu.sync_copy(d0_hbm.at[pl.ds(off, SC_IDX_WIN)], idx.at[0])
            pltpu.sync_copy(d1_hbm.at[pl.ds(off, SC_IDX_WIN)], idx.at[1])
            for r in range(SC_IDX_WIN // SC_ROWS):
                pltpu.sync_copy(h_hbm.at[pl.ds(off + r * SC_ROWS, SC_ROWS)], buf)
                pltpu.sync_copy(buf, xs_hbm.at[idx.at[0, pl.ds(r * SC_ROWS, SC_ROWS)]])
                pltpu.sync_copy(buf, xs_hbm.at[idx.at[1, pl.ds(r * SC_ROWS, SC_ROWS)]])

    return k(h, dest0, dest1)


def _sc_gather2(ys, dest0, dest1):
    n = dest0.shape[0]
    d = ys.shape[1]
    per = n // SC_WORKERS
    out = jax.ShapeDtypeStruct((n, d), ys.dtype)

    @pl.kernel(out_type=[out, out], mesh=_sc_mesh(),
               scratch_types=[pltpu.VMEM((2, SC_IDX_WIN), jnp.int32), pltpu.VMEM((SC_ROWS, d), ys.dtype)])
    def k(ys_hbm, d0_hbm, d1_hbm, y0_hbm, y1_hbm, idx, buf):
        base = (lax.axis_index("c") * SC_SUBCORES + lax.axis_index("s")) * per

        @pl.loop(0, per // SC_IDX_WIN)
        def _(w):
            off = base + w * SC_IDX_WIN
            pltpu.sync_copy(d0_hbm.at[pl.ds(off, SC_IDX_WIN)], idx.at[0])
            pltpu.sync_copy(d1_hbm.at[pl.ds(off, SC_IDX_WIN)], idx.at[1])
            for r in range(SC_IDX_WIN // SC_ROWS):
                for kk, y_hbm in enumerate((y0_hbm, y1_hbm)):
                    pltpu.sync_copy(ys_hbm.at[idx.at[kk, pl.ds(r * SC_ROWS, SC_ROWS)]], buf)
                    pltpu.sync_copy(buf, y_hbm.at[pl.ds(off + r * SC_ROWS, SC_ROWS)])

    return k(ys, dest0, dest1)


def _experts_kernel(be_ref, bv_ref, slot_ref, nxt_ref, xs_ref, wg_hbm, wu_hbm, wd_hbm, ys_ref,
                    wg_f32, wu_f32, wd_f32, wg_bf, wu_bf, wd_bf, sems):
    i = pl.program_id(0)
    valid = bv_ref[i]
    expert = be_ref[i]
    slot = slot_ref[i]
    run_start = jnp.logical_or(i == 0, expert != be_ref[jnp.maximum(i - 1, 0)])

    def fetch(e, s):
        return [pltpu.make_async_copy(hbm.at[e], buf.at[s], sems.at[s, k])
                for k, (hbm, buf) in enumerate(((wg_hbm, wg_f32), (wu_hbm, wu_f32), (wd_hbm, wd_f32)))]

    @pl.when(i == 0)
    def _():
        for cp in fetch(expert, slot):
            cp.start()

    @pl.when(run_start)
    def _():
        for cp in fetch(expert, slot):
            cp.wait()
        wg_bf[...] = wg_f32[slot].astype(bf16)
        wu_bf[...] = wu_f32[slot].astype(bf16)
        wd_bf[...] = wd_f32[slot].astype(bf16)

        @pl.when(nxt_ref[i] >= 0)
        def _():
            for cp in fetch(nxt_ref[i], 1 - slot):
                cp.start(priority=1)

    @pl.when(valid > 0)
    def _():
        row = lax.broadcasted_iota(jnp.int32, xs_ref.shape, 0)
        keep = row < valid
        lo, hi = _unpack_halves(xs_ref[...])
        x = jnp.concatenate([jnp.where(keep, lo, 0.0), jnp.where(keep, hi, 0.0)], axis=1).astype(bf16)
        g = jnp.dot(x, wg_bf[...], preferred_element_type=f32)
        u = jnp.dot(x, wu_bf[...], preferred_element_type=f32)
        hmid = (g * _sigmoid(g) * u).astype(bf16)
        ys_ref[...] = _pack_halves(jnp.dot(hmid, wd_bf[...], preferred_element_type=f32))

    @pl.when(valid == 0)
    def _():
        ys_ref[...] = jnp.zeros_like(ys_ref)


def _experts(xs, blk_expert, blk_valid, wg, wu, wd, n_blocks, blk):
    starts = jnp.concatenate([jnp.ones((1,), jnp.int32), (blk_expert[1:] != blk_expert[:-1]).astype(jnp.int32)])
    slot = (jnp.cumsum(starts) - 1) % 2
    idx = jnp.arange(n_blocks, dtype=jnp.int32)
    next_start = lax.cummin(jnp.where(starts > 0, idx, n_blocks)[::-1])[::-1]
    next_start = jnp.concatenate([next_start[1:], jnp.full((1,), n_blocks, jnp.int32)])
    nxt = jnp.where(next_start < n_blocks, blk_expert[jnp.minimum(next_start, n_blocks - 1)], -1)
    tile = pl.BlockSpec((blk, D_PACKED), lambda i, *_: (i, 0))
    any_spec = pl.BlockSpec(memory_space=pl.ANY)
    grid_spec = pltpu.PrefetchScalarGridSpec(
        num_scalar_prefetch=4,
        grid=(n_blocks,),
        in_specs=[tile, any_spec, any_spec, any_spec],
        out_specs=tile,
        scratch_shapes=[pltpu.VMEM((2, D_MODEL, D_EXPERT), f32),
                        pltpu.VMEM((2, D_MODEL, D_EXPERT), f32),
                        pltpu.VMEM((2, D_EXPERT, D_MODEL), f32),
                        pltpu.VMEM((D_MODEL, D_EXPERT), bf16),
                        pltpu.VMEM((D_MODEL, D_EXPERT), bf16),
                        pltpu.VMEM((D_EXPERT, D_MODEL), bf16),
                        pltpu.SemaphoreType.DMA((2, 3))],
    )
    return pl.pallas_call(
        _experts_kernel,
        grid_spec=grid_spec,
        out_shape=jax.ShapeDtypeStruct(xs.shape, jnp.uint32),
        compiler_params=_cparams(1),
        name="experts",
    )(blk_expert, blk_valid, slot.astype(jnp.int32), nxt.astype(jnp.int32), xs, wg, wu, wd)


def _final_norm_kernel(h_ref, y0_ref, y1_ref, route_ref, g_ref, b_ref, o_ref):
    route_t = route_ref[...].T
    g0, g1 = route_t[:, 2:3], route_t[:, 3:4]
    lo0, hi0 = _unpack_halves(y0_ref[...])
    lo1, hi1 = _unpack_halves(y1_ref[...])
    ffn = jnp.concatenate([lo0 * g0 + lo1 * g1, hi0 * g0 + hi1 * g1], axis=1)
    o_ref[...] = _layer_norm(ALPHA * h_ref[...] + ffn, g_ref[...], b_ref[...])


def _final_norm(h, y0, y1, route, ln_g, ln_b):
    n = h.shape[0]
    ts = NORM_TS
    tile = pl.BlockSpec((ts, D_MODEL), lambda i: (i, 0))
    packed = pl.BlockSpec((ts, D_PACKED), lambda i: (i, 0))
    return pl.pallas_call(
        _final_norm_kernel,
        grid=(n // ts,),
        in_specs=[tile, packed, packed, pl.BlockSpec((ROUTE_ROWS, ts), lambda i: (0, i)),
                  _full(ln_g), _full(ln_b)],
        out_specs=tile,
        out_shape=jax.ShapeDtypeStruct((n, D_MODEL), f32),
        compiler_params=_cparams(1),
        name="final_norm",
    )(h, y0, y1, route, ln_g, ln_b)


def _encode(x, p, meta_tokens, kv_meta, slabs):
    bsz, t, _ = x.shape
    n = bsz * t
    assert t % NORM_TS == 0 and t % TAIL_TM == 0 and t % PROJ_TM == 0
    xf = x.reshape(n, D_MODEL)
    qkv, rest = _in_proj(xf, meta_tokens, p["w_in"], p["conv_w"], p["conv_b"], t)
    att = _attention(qkv, kv_meta, slabs, bsz, t)
    h1, h1_packed, route, cnt = _mixer_tail(xf, att, rest, p)
    blk = MOE_BLK
    dest0, dest1, blk_expert, blk_valid, n_blocks = _route(route, cnt, n, blk)
    xs = _sc_dispatch(h1_packed, dest0, dest1, n_blocks * blk)
    ys = _experts(xs, blk_expert, blk_valid, p["w_e_gate"], p["w_e_up"], p["w_e_down"], n_blocks, blk)
    y0, y1 = _sc_gather2(ys, dest0, dest1)
    y = _final_norm(h1, y0, y1, route, p["ln2_g"], p["ln2_b"])
    return y.reshape(bsz, t, D_MODEL)


def kernel(x_prompt, x_sample, meta_tokens, w_in, rpb, conv_w, conv_b, w_att_proj, w_conv_proj, w_out,
           ln1_g, ln1_b, w_router_group, w_router_expert, w_e_gate, w_e_up, w_e_down, ln2_g, ln2_b):
    w_router = jnp.concatenate([w_router_expert[0], w_router_group[0]], axis=1)
    w_router = jnp.pad(w_router, ((0, 0), (0, LANES - w_router.shape[1])))
    row = lambda v: v[0].reshape(1, -1).astype(f32)
    p = {
        "w_in": w_in[0].astype(bf16),
        "conv_w": conv_w[0].astype(f32), "conv_b": row(conv_b),
        "w_att_proj": w_att_proj[0].astype(bf16), "w_conv_proj": w_conv_proj[0].astype(bf16),
        "w_out": (0.5 * w_out[0]).astype(bf16), "ln1_g": row(ln1_g), "ln1_b": row(ln1_b),
        "w_router": w_router.astype(bf16),
        "w_e_gate": w_e_gate[0], "w_e_up": w_e_up[0], "w_e_down": w_e_down[0],
        "ln2_g": row(ln2_g), "ln2_b": row(ln2_b),
    }
    meta_tokens = meta_tokens.astype(f32)
    kv_meta = _meta_kv(meta_tokens, p["w_in"][:, D_ATT:3 * D_ATT])
    slabs = _bias_pieces(rpb[0])
    y_prompt = _encode(x_prompt, p, meta_tokens, kv_meta, slabs)
    y_sample = _encode(x_sample, p, meta_tokens, kv_meta, slabs)
    return (y_prompt, y_sample)
```

```python
import functools

import numpy as np
import jax
import jax.numpy as jnp
from jax import lax
from jax.experimental import pallas as pl
from jax.experimental.pallas import tpu as pltpu
from jax.experimental.pallas import tpu_sc as plsc

D_MODEL = 1024
N_META = 16
GRID_W = 64
WIN_H = 8
WIN_W = 16
N_HEADS = 8
HEAD_DIM = 64
D_ATT = N_HEADS * HEAD_DIM
D_CONV = D_MODEL
N_GROUPS = 4
EXPERTS_PER_GROUP = 8
N_EXPERTS = N_GROUPS * EXPERTS_PER_GROUP
D_EXPERT = D_MODEL // 2
DEPTH = 1
ALPHA = (2.0 * DEPTH) ** 0.25
LN_EPS = 1e-5
D_QKV = 4 * D_ATT
LOG2E = 1.4426950408889634
D_REST = D_CONV + 2 * D_MODEL
D_IN_PROJ = 3 * D_ATT + 3 * D_CONV + 2 * D_MODEL

LANES = 128
HEADS_PER_VREG = LANES // HEAD_DIM
N_HEAD_PAIRS = N_HEADS // HEADS_PER_VREG
NEG_BIG = -1e30

PROJ_TM = 1024
PROJ_TN = 512
X_HALO = 8
ATT_ROWS = 32
ATT_GROUP = 8
SLAB_ROWS = WIN_H + 1
ATT_KEYS = SLAB_ROWS * GRID_W + N_META
N_DY = 2 * WIN_H - 1
PIECE_MT = N_DY - 1
PIECE_T0 = PIECE_MT + WIN_H
PIECE_M0 = PIECE_T0 + WIN_H
N_BIAS_PIECES = PIECE_M0 + 1
TAIL_TM = 512
TAIL_CW = 256
NORM_TS = 1024
MOE_BLK = 512
ROUTE_ROWS = 8
VMEM_LIMIT = 56 * 1024 * 1024

SC_CORES = 2
SC_SUBCORES = 16
SC_WORKERS = SC_CORES * SC_SUBCORES
SC_IDX_WIN = 128
SC_ROWS = 128
D_PACKED = D_MODEL // 2

bf16 = jnp.bfloat16
f32 = jnp.float32


def _cparams(n_axes):
    return pltpu.CompilerParams(dimension_semantics=("arbitrary",) * n_axes,
                                vmem_limit_bytes=VMEM_LIMIT)


def _full(a):
    return pl.BlockSpec(a.shape, lambda *_: (0,) * a.ndim)


def _pack_halves(x):
    half = x.shape[1] // 2
    bits = lambda v: lax.bitcast_convert_type(v.astype(bf16).astype(f32), jnp.uint32)
    return (bits(x[:, :half]) >> 16) | bits(x[:, half:])


def _unpack_halves(p):
    lo = lax.bitcast_convert_type(p << 16, f32)
    hi = lax.bitcast_convert_type(p & jnp.uint32(0xFFFF0000), f32)
    return lo, hi


def _in_proj_kernel(x_ref, prev_ref, next_ref, meta_ref, w_ref, convw_ref, convb_ref, qkv_ref, rest_ref,
                    *, tiles_per_seq):
    assert PROJ_TN == D_ATT
    pos = pl.program_id(0) % tiles_per_seq
    tm = x_ref.shape[0]
    x_prev = jnp.where(pos == 0, meta_ref[N_META - X_HALO:, :], prev_ref[...])
    x_body = x_ref[...].astype(bf16)
    x_ext = jnp.concatenate([x_prev, x_ref[...], next_ref[...]], axis=0).astype(bf16)
    body = slice(X_HALO, X_HALO + tm)
    col = lambda c: jnp.dot(x_body, w_ref[:, c:c + PROJ_TN], preferred_element_type=f32)
    col_ext = lambda c: jnp.dot(x_ext, w_ref[:, c:c + PROJ_TN], preferred_element_type=f32)
    proj0 = 3 * D_ATT

    y = (col(0) * (HEAD_DIM ** -0.5 * LOG2E)).astype(bf16)
    lane = lax.broadcasted_iota(jnp.int32, y.shape, 1)
    even = (lane & HEAD_DIM) == 0
    zero = jnp.zeros_like(y)
    qkv_ref[:, :D_ATT] = jnp.where(even, y, zero)
    qkv_ref[:, D_ATT:2 * D_ATT] = jnp.where(even, zero, y)
    for c in range(D_ATT, 3 * D_ATT, PROJ_TN):
        qkv_ref[:, c + D_ATT:c + D_ATT + PROJ_TN] = col(c).astype(bf16)

    row = lax.broadcasted_iota(jnp.int32, (tm, PROJ_TN), 0)
    no_next = jnp.logical_and(pos == tiles_per_seq - 1, row == tm - 1)
    for c in range(0, D_CONV, PROJ_TN):
        u_ext = col_ext(proj0 + D_CONV + c) * col_ext(proj0 + 2 * D_CONV + c)
        u_m1 = pltpu.roll(u_ext, 1, axis=0)[body]
        u_p1 = jnp.where(no_next, 0.0, pltpu.roll(u_ext, tm + 2 * X_HALO - 1, axis=0)[body])
        cw = convw_ref[:, c:c + PROJ_TN]
        conv = u_m1 * cw[0:1] + u_ext[body] * cw[1:2] + u_p1 * cw[2:3] + convb_ref[:, c:c + PROJ_TN]
        rest_ref[:, c:c + PROJ_TN] = (col(proj0 + c) * conv).astype(bf16)
    for c in range(0, 2 * D_MODEL, PROJ_TN):
        g = col(proj0 + 3 * D_CONV + c)
        rest_ref[:, D_CONV + c:D_CONV + c + PROJ_TN] = (jnp.tanh(0.5 * g) + 1.0).astype(bf16)


def _in_proj(x, meta_tokens, w_bf, conv_w, conv_b, t):
    assert w_bf.shape == (D_MODEL, D_IN_PROJ)
    n = x.shape[0]
    tm = PROJ_TM
    hb = tm // X_HALO
    n_halo = n // X_HALO
    halo = lambda index: pl.BlockSpec((X_HALO, D_MODEL), lambda i: (index(i), 0))
    return pl.pallas_call(
        functools.partial(_in_proj_kernel, tiles_per_seq=t // tm),
        grid=(n // tm,),
        in_specs=[pl.BlockSpec((tm, D_MODEL), lambda i: (i, 0)),
                  halo(lambda i: jnp.maximum(i * hb - 1, 0)),
                  halo(lambda i: jnp.minimum((i + 1) * hb, n_halo - 1)),
                  _full(meta_tokens), _full(w_bf), _full(conv_w), _full(conv_b)],
        out_specs=[pl.BlockSpec((tm, D_QKV), lambda i: (i, 0)),
                   pl.BlockSpec((tm, D_REST), lambda i: (i, 0))],
        out_shape=[jax.ShapeDtypeStruct((n, D_QKV), bf16),
                   jax.ShapeDtypeStruct((n, D_REST), bf16)],
        compiler_params=_cparams(1),
        name="in_proj",
    )(x, x, x, meta_tokens, w_bf, conv_w, conv_b)


def _meta_kv_kernel(m_ref, w_ref, kv_ref):
    kv_ref[...] = jnp.dot(m_ref[...].astype(bf16), w_ref[...], preferred_element_type=f32).astype(bf16)


def _meta_kv(meta_tokens, w_kv):
    return pl.pallas_call(
        _meta_kv_kernel,
        out_shape=jax.ShapeDtypeStruct((N_META, 2 * D_ATT), bf16),
        compiler_params=pltpu.CompilerParams(vmem_limit_bytes=VMEM_LIMIT),
        name="meta_kv",
    )(meta_tokens, w_kv)


def _bias_pieces(rpb):
    qc = np.arange(GRID_W)[:, None]
    kc = np.arange(GRID_W)[None, :]
    w_start = np.clip(qc - WIN_W // 2, 0, GRID_W - WIN_W)
    valid = (kc >= w_start) & (kc < w_start + WIN_W)
    n_dx = 2 * WIN_W - 1
    onehot = (kc - qc + WIN_W - 1)[None] == np.arange(n_dx)[:, None, None]
    t = jnp.einsum("hyx,xqk->hyqk", rpb.astype(f32) * LOG2E, jnp.asarray(onehot & valid[None], f32),
                   precision=lax.Precision.HIGHEST)
    t = jnp.where(jnp.asarray(valid)[None, None], t, NEG_BIG)
    t = t.reshape(N_HEAD_PAIRS, HEADS_PER_VREG, N_DY, GRID_W, GRID_W)
    t = jnp.concatenate([t[:, p] for p in range(HEADS_PER_VREG)], axis=2)
    masked = jnp.full((N_HEAD_PAIRS, WIN_H, HEADS_PER_VREG * GRID_W, GRID_W), NEG_BIG, f32)
    zeros = jnp.zeros_like(masked)
    pieces = jnp.concatenate([
        jnp.concatenate([t[:, :N_DY - 1], t[:, 1:]], axis=-1),
        jnp.concatenate([masked, t[:, :WIN_H]], axis=-1),
        jnp.concatenate([t[:, WIN_H - 1:], zeros], axis=-1),
        jnp.concatenate([masked[:, :1], zeros[:, :1]], axis=-1),
    ], axis=1)
    assert pieces.shape[1] == N_BIAS_PIECES
    return pieces


def _attention_kernel(qe_ref, qo_ref, k_ref, v_ref, km_ref, vm_ref, bias_ref, o_ref, *, rows):
    j = pl.program_id(1)
    tq2 = 2 * GRID_W
    lane = lax.broadcasted_iota(jnp.int32, (tq2, LANES), 1)
    low = lane < HEAD_DIM
    nt = (((1,), (1,)), ((), ()))
    slab_keys = SLAB_ROWS * GRID_W

    def row_bias(pair, dy0, off):
        first = jnp.where(off == 0, dy0, PIECE_MT + dy0)
        mids = [dy0 + 2 * p - off for p in range(1, SLAB_ROWS // 2)]
        last = jnp.where(off == 0, PIECE_M0, PIECE_T0 + dy0)
        tail = ATT_KEYS - (SLAB_ROWS // 2) * LANES
        return jnp.concatenate([bias_ref[pair, first]] + [bias_ref[pair, m] for m in mids]
                               + [bias_ref[pair, last][:, :tail]], axis=1)

    def scores(ip, pair):
        r0 = j * ATT_ROWS + 2 * ip
        rs = [jnp.clip(r0 + a - WIN_H // 2, 0, rows - WIN_H) for a in range(2)]
        us = jnp.minimum(rs[0], rows - SLAB_ROWS)
        q0 = pl.multiple_of(ip * tq2, tq2)
        k0 = pl.multiple_of(us * GRID_W, GRID_W)
        cs = slice(pair * LANES, (pair + 1) * LANES)
        qq = jnp.concatenate([qe_ref[pl.ds(q0, tq2), cs], qo_ref[pl.ds(q0, tq2), cs]], axis=0)
        k2 = jnp.concatenate([k_ref[pl.ds(k0, slab_keys), cs], km_ref[:, cs]], axis=0)
        s = lax.dot_general(qq, k2, nt, preferred_element_type=f32)
        b = [row_bias(pair, rs[a] - (r0 + a) + (WIN_H - 1), rs[a] - us) for a in range(2)]
        s = s + jnp.concatenate([b[0][:GRID_W], b[1][:GRID_W], b[0][GRID_W:], b[1][GRID_W:]], axis=0)
        return s, jnp.max(s, axis=-1, keepdims=True), k0, q0, cs

    def weights(state):
        s, m, k0, q0, cs = state
        v2 = jnp.concatenate([v_ref[pl.ds(k0, slab_keys), cs], vm_ref[:, cs]], axis=0)
        e = jnp.exp2(s - m)
        l = jnp.sum(e, axis=-1, keepdims=True)
        return jnp.dot(e.astype(bf16), v2, preferred_element_type=f32), l, q0, cs

    def finish(state):
        o2, l, q0, cs = state
        o2 = o2 / l
        o = jnp.where(low, o2[:tq2], o2[tq2:])
        o_ref[pl.ds(q0, tq2), cs] = o.astype(bf16)

    def group_body(g, carry):
        items = [(g * ATT_GROUP + ip, pair) for ip in range(ATT_GROUP) for pair in range(N_HEAD_PAIRS)]
        a, b = {}, {}
        for step in range(len(items) + 2):
            if step < len(items):
                a[step] = scores(*items[step])
            if 0 <= step - 1 < len(items):
                b[step - 1] = weights(a.pop(step - 1))
            if 0 <= step - 2 < len(items):
                finish(b.pop(step - 2))
        return carry

    lax.fori_loop(0, ATT_ROWS // (2 * ATT_GROUP), group_body, 0)


def _attention(qkv, kv_meta, slabs, bsz, t):
    rows = t // GRID_W
    assert rows >= 2 * WIN_H and rows % ATT_ROWS == 0 and ATT_ROWS % (2 * ATT_GROUP) == 0
    steps = rows // ATT_ROWS
    tq = ATT_ROWS * GRID_W
    return pl.pallas_call(
        functools.partial(_attention_kernel, rows=rows),
        grid=(bsz, steps),
        in_specs=[pl.BlockSpec((tq, D_ATT), lambda b, j: (b * steps + j, 0)),
                  pl.BlockSpec((tq, D_ATT), lambda b, j: (b * steps + j, 1)),
                  pl.BlockSpec((t, D_ATT), lambda b, j: (b, 2)),
                  pl.BlockSpec((t, D_ATT), lambda b, j: (b, 3)),
                  pl.BlockSpec((N_META, D_ATT), lambda b, j: (0, 0)),
                  pl.BlockSpec((N_META, D_ATT), lambda b, j: (0, 1)),
                  _full(slabs)],
        out_specs=pl.BlockSpec((tq, D_ATT), lambda b, j: (b * steps + j, 0)),
        out_shape=jax.ShapeDtypeStruct((bsz * t, D_ATT), bf16),
        compiler_params=_cparams(2),
        name="attention",
    )(qkv, qkv, qkv, qkv, kv_meta, kv_meta, slabs)


def _layer_norm(z, g, b):
    mu = jnp.mean(z, axis=-1, keepdims=True)
    d = z - mu
    var = jnp.mean(d * d, axis=-1, keepdims=True)
    return d * lax.rsqrt(var + LN_EPS) * g + b


def _sigmoid(x):
    return 0.5 * jnp.tanh(0.5 * x) + 0.5


def _route_tile(logits, before_ref):
    tm = logits.shape[0]
    lt = logits.T
    el = lt[:N_EXPERTS]
    gl = lt[N_EXPERTS:N_EXPERTS + N_GROUPS]
    neg = -jnp.inf
    erow = lax.broadcasted_iota(jnp.int32, el.shape, 0).astype(f32)
    grow = lax.broadcasted_iota(jnp.int32, gl.shape, 0).astype(f32)
    first = lambda hit, idx, n: jnp.min(jnp.where(hit, idx, float(n)), axis=0, keepdims=True)

    gmax = jnp.max(gl, axis=0, keepdims=True)
    grp = first(gl == gmax, grow, N_GROUPS)
    pg_sel = 1.0 / jnp.sum(jnp.exp(gl - gmax), axis=0, keepdims=True)

    e_lo = grp * EXPERTS_PER_GROUP
    elm = jnp.where((erow >= e_lo) & (erow < e_lo + EXPERTS_PER_GROUP), el, neg)
    t1 = jnp.max(elm, axis=0, keepdims=True)
    e1 = first(elm == t1, erow, N_EXPERTS)
    el2 = jnp.where(erow == e1, neg, elm)
    t2 = jnp.max(el2, axis=0, keepdims=True)
    e2 = first(el2 == t2, erow, N_EXPERTS)
    r = jnp.exp(t2 - t1)
    g1 = pg_sel / (1.0 + r)
    g2 = pg_sel * r / (1.0 + r)

    hit1 = erow == e1
    hit2 = erow == e2
    onehot = jnp.where(hit1 | hit2, 1.0, 0.0)
    before = jnp.dot(onehot.astype(bf16), before_ref[...], preferred_element_type=f32)
    rank1 = jnp.sum(jnp.where(hit1, before, 0.0), axis=0, keepdims=True)
    rank2 = jnp.sum(jnp.where(hit2, before, 0.0), axis=0, keepdims=True)
    rrow = lax.broadcasted_iota(jnp.int32, (ROUTE_ROWS, tm), 0)
    route = jnp.zeros((ROUTE_ROWS, tm), f32)
    for k, val in enumerate((e1, e2, g1, g2, rank1, rank2)):
        route = jnp.where(rrow == k, val, route)
    return route, jnp.sum(onehot, axis=1, keepdims=True)


def _mixer_tail_kernel(x_ref, att_ref, rest_ref, wap_ref, wcp_ref, wout_ref, g_ref, b_ref, wr_ref, before_ref,
                       h_ref, hp_ref, route_ref, cnt_ref, merged_scr, *, n_tiles):
    i = pl.program_id(0)
    c3, c4 = D_CONV, D_CONV + D_MODEL
    cols = lambda j, base=0: slice(base + j * TAIL_CW, base + (j + 1) * TAIL_CW)
    chunks = range(D_MODEL // TAIL_CW)

    @pl.when(i == 0)
    def _():
        merged_scr[...] = jnp.zeros_like(merged_scr)

    mix = [jnp.dot(merged_scr[...], wout_ref[:, cols(j)], preferred_element_type=f32) for j in chunks]

    cv = jnp.dot(rest_ref[:, :D_CONV], wcp_ref[...], preferred_element_type=f32)
    ap = jnp.dot(att_ref[...], wap_ref[...], preferred_element_type=f32)

    z = [ALPHA * x_ref[:, cols(j)] + mix[j] for j in chunks]
    mu = sum(jnp.sum(zj, axis=-1, keepdims=True) for zj in z) * (1.0 / D_MODEL)
    d = [zj - mu for zj in z]
    var = sum(jnp.sum(dj * dj, axis=-1, keepdims=True) for dj in d) * (1.0 / D_MODEL)
    rstd = lax.rsqrt(var + LN_EPS)
    logits = None
    hs = []
    for j in chunks:
        hj = d[j] * rstd * g_ref[:, cols(j)] + b_ref[:, cols(j)]
        h_ref[:, cols(j)] = hj
        hs.append(hj)
        part = jnp.dot(hj.astype(bf16), wr_ref[cols(j), :], preferred_element_type=f32)
        logits = part if logits is None else logits + part
    hp_ref[...] = _pack_halves(jnp.concatenate(hs, axis=1))

    for j in chunks:
        merged_scr[:, cols(j)] = (rest_ref[:, cols(j, c3)] * ap[:, cols(j)].astype(bf16)
                                  + rest_ref[:, cols(j, c4)] * cv[:, cols(j)].astype(bf16))

    route, count = _route_tile(logits, before_ref)
    route_ref[...] = route
    cnt_ref[0] = jnp.broadcast_to(count, (N_EXPERTS, LANES))


def _mixer_tail(x, att, rest, p):
    n = x.shape[0]
    tm = TAIL_TM
    n_tiles = n // tm
    before = jnp.asarray(np.triu(np.ones((tm, tm), np.float32), 1), dtype=bf16)
    consts = (p["w_att_proj"], p["w_conv_proj"], p["w_out"], p["ln1_g"], p["ln1_b"], p["w_router"], before)
    front = lambda i: jnp.minimum(i, n_tiles - 1)
    back = lambda i: jnp.maximum(i - 1, 0)
    return pl.pallas_call(
        functools.partial(_mixer_tail_kernel, n_tiles=n_tiles),
        grid=(n_tiles + 1,),
        in_specs=[pl.BlockSpec((tm, D_MODEL), lambda i: (back(i), 0)),
                  pl.BlockSpec((tm, D_ATT), lambda i: (front(i), 0)),
                  pl.BlockSpec((tm, D_REST), lambda i: (front(i), 0))]
                 + [_full(c) for c in consts],
        out_specs=[pl.BlockSpec((tm, D_MODEL), lambda i: (back(i), 0)),
                   pl.BlockSpec((tm, D_PACKED), lambda i: (back(i), 0)),
                   pl.BlockSpec((ROUTE_ROWS, tm), lambda i: (0, back(i))),
                   pl.BlockSpec((1, N_EXPERTS, LANES), lambda i: (back(i), 0, 0))],
        out_shape=[jax.ShapeDtypeStruct((n, D_MODEL), f32),
                   jax.ShapeDtypeStruct((n, D_PACKED), jnp.uint32),
                   jax.ShapeDtypeStruct((ROUTE_ROWS, n), f32),
                   jax.ShapeDtypeStruct((n_tiles, N_EXPERTS, LANES), f32)],
        scratch_shapes=[pltpu.VMEM((tm, D_MODEL), bf16)],
        compiler_params=_cparams(1),
        name="mixer_tail",
    )(x, att, rest, *consts)


def _route(route, cnt, n, blk):
    n_tiles = n // TAIL_TM
    tile_cnt = cnt[:, :, 0].astype(jnp.int32)
    tile_off = jnp.cumsum(tile_cnt, axis=0) - tile_cnt
    counts = jnp.sum(tile_cnt, axis=0)
    padded = (counts + blk - 1) // blk * blk
    pend = jnp.cumsum(padded)
    pstart = pend - padded
    base = pstart[None, :] + tile_off
    lanes = jnp.arange(N_EXPERTS, dtype=jnp.int32)

    def rows_of(expert_row, rank_row):
        e = expert_row.astype(jnp.int32).reshape(n_tiles, TAIL_TM, 1)
        sel = jnp.sum(jnp.where(e == lanes, base[:, None, :], 0), axis=-1)
        return sel.reshape(n) + rank_row.astype(jnp.int32)

    dest0 = rows_of(route[0], route[4])
    dest1 = rows_of(route[1], route[5])

    n_blocks = (2 * n + N_EXPERTS * (blk - 1) + blk - 1) // blk
    blk_start = jnp.arange(n_blocks, dtype=jnp.int32) * blk
    blk_expert = jnp.sum((pend[None, :] <= blk_start[:, None]).astype(jnp.int32), axis=1)
    blk_expert = jnp.minimum(blk_expert, N_EXPERTS - 1)
    blk_valid = jnp.clip((pstart + counts)[blk_expert] - blk_start, 0, blk)
    blk_valid = jnp.where(blk_start < pend[-1], blk_valid, 0).astype(jnp.int32)
    return dest0, dest1, blk_expert, blk_valid, n_blocks


def _sc_mesh():
    return plsc.VectorSubcoreMesh(core_axis_name="c", subcore_axis_name="s")


def _sc_dispatch(h, dest0, dest1, p_rows):
    n, d = h.shape
    per = n // SC_WORKERS
    assert n % (SC_WORKERS * SC_IDX_WIN) == 0

    @pl.kernel(out_type=jax.ShapeDtypeStruct((p_rows, d), h.dtype), mesh=_sc_mesh(),
               scratch_types=[pltpu.VMEM((2, SC_IDX_WIN), jnp.int32), pltpu.VMEM((SC_ROWS, d), h.dtype)])
    def k(h_hbm, d0_hbm, d1_hbm, xs_hbm, idx, buf):
        base = (lax.axis_index("c") * SC_SUBCORES + lax.axis_index("s")) * per

        @pl.loop(0, per // SC_IDX_WIN)
        def _(w):
            off = base + w * SC_IDX_WIN
            pltpu.sync_copy(d0_hbm.at[pl.ds(off, SC_IDX_WIN)], idx.at[0])
            pltpu.sync_copy(d1_hbm.at[pl.ds(off, SC_IDX_WIN)], idx.at[1])
            for r in range(SC_IDX_WIN // SC_ROWS):
                pltpu.sync_copy(h_hbm.at[pl.ds(off + r * SC_ROWS, SC_ROWS)], buf)
                pltpu.sync_copy(buf, xs_hbm.at[idx.at[0, pl.ds(r * SC_ROWS, SC_ROWS)]])
                pltpu.sync_copy(buf, xs_hbm.at[idx.at[1, pl.ds(r * SC_ROWS, SC_ROWS)]])

    return k(h, dest0, dest1)


def _sc_gather2(ys, dest0, dest1):
    n = dest0.shape[0]
    d = ys.shape[1]
    per = n // SC_WORKERS
    out = jax.ShapeDtypeStruct((n, d), ys.dtype)

    @pl.kernel(out_type=[out, out], mesh=_sc_mesh(),
               scratch_types=[pltpu.VMEM((2, SC_IDX_WIN), jnp.int32), pltpu.VMEM((SC_ROWS, d), ys.dtype)])
    def k(ys_hbm, d0_hbm, d1_hbm, y0_hbm, y1_hbm, idx, buf):
        base = (lax.axis_index("c") * SC_SUBCORES + lax.axis_index("s")) * per

        @pl.loop(0, per // SC_IDX_WIN)
        def _(w):
            off = base + w * SC_IDX_WIN
            pltpu.sync_copy(d0_hbm.at[pl.ds(off, SC_IDX_WIN)], idx.at[0])
            pltpu.sync_copy(d1_hbm.at[pl.ds(off, SC_IDX_WIN)], idx.at[1])
            for r in range(SC_IDX_WIN // SC_ROWS):
                for kk, y_hbm in enumerate((y0_hbm, y1_hbm)):
                    pltpu.sync_copy(ys_hbm.at[idx.at[kk, pl.ds(r * SC_ROWS, SC_ROWS)]], buf)
                    pltpu.sync_copy(buf, y_hbm.at[pl.ds(off + r * SC_ROWS, SC_ROWS)])

    return k(ys, dest0, dest1)


def _experts_kernel(be_ref, bv_ref, slot_ref, nxt_ref, xs_ref, wg_hbm, wu_hbm, wd_hbm, ys_ref,
                    wg_f32, wu_f32, wd_f32, wg_bf, wu_bf, wd_bf, sems):
    i = pl.program_id(0)
    valid = bv_ref[i]
    expert = be_ref[i]
    slot = slot_ref[i]
    run_start = jnp.logical_or(i == 0, expert != be_ref[jnp.maximum(i - 1, 0)])

    def fetch(e, s):
        return [pltpu.make_async_copy(hbm.at[e], buf.at[s], sems.at[s, k])
                for k, (hbm, buf) in enumerate(((wg_hbm, wg_f32), (wu_hbm, wu_f32), (wd_hbm, wd_f32)))]

    @pl.when(i == 0)
    def _():
        for cp in fetch(expert, slot):
            cp.start()

    @pl.when(run_start)
    def _():
        for cp in fetch(expert, slot):
            cp.wait()
        wg_bf[...] = wg_f32[slot].astype(bf16)
        wu_bf[...] = wu_f32[slot].astype(bf16)
        wd_bf[...] = wd_f32[slot].astype(bf16)

        @pl.when(nxt_ref[i] >= 0)
        def _():
            for cp in fetch(nxt_ref[i], 1 - slot):
                cp.start(priority=1)

    @pl.when(valid > 0)
    def _():
        row = lax.broadcasted_iota(jnp.int32, xs_ref.shape, 0)
        keep = row < valid
        lo, hi = _unpack_halves(xs_ref[...])
        x = jnp.concatenate([jnp.where(keep, lo, 0.0), jnp.where(keep, hi, 0.0)], axis=1).astype(bf16)
        g = jnp.dot(x, wg_bf[...], preferred_element_type=f32)
        u = jnp.dot(x, wu_bf[...], preferred_element_type=f32)
        hmid = (g * _sigmoid(g) * u).astype(bf16)
        ys_ref[...] = _pack_halves(jnp.dot(hmid, wd_bf[...], preferred_element_type=f32))

    @pl.when(valid == 0)
    def _():
        ys_ref[...] = jnp.zeros_like(ys_ref)


def _experts(xs, blk_expert, blk_valid, wg, wu, wd, n_blocks, blk):
    starts = jnp.concatenate([jnp.ones((1,), jnp.int32), (blk_expert[1:] != blk_expert[:-1]).astype(jnp.int32)])
    slot = (jnp.cumsum(starts) - 1) % 2
    idx = jnp.arange(n_blocks, dtype=jnp.int32)
    next_start = lax.cummin(jnp.where(starts > 0, idx, n_blocks)[::-1])[::-1]
    next_start = jnp.concatenate([next_start[1:], jnp.full((1,), n_blocks, jnp.int32)])
    nxt = jnp.where(next_start < n_blocks, blk_expert[jnp.minimum(next_start, n_blocks - 1)], -1)
    tile = pl.BlockSpec((blk, D_PACKED), lambda i, *_: (i, 0))
    any_spec = pl.BlockSpec(memory_space=pl.ANY)
    grid_spec = pltpu.PrefetchScalarGridSpec(
        num_scalar_prefetch=4,
        grid=(n_blocks,),
        in_specs=[tile, any_spec, any_spec, any_spec],
        out_specs=tile,
        scratch_shapes=[pltpu.VMEM((2, D_MODEL, D_EXPERT), f32),
                        pltpu.VMEM((2, D_MODEL, D_EXPERT), f32),
                        pltpu.VMEM((2, D_EXPERT, D_MODEL), f32),
                        pltpu.VMEM((D_MODEL, D_EXPERT), bf16),
                        pltpu.VMEM((D_MODEL, D_EXPERT), bf16),
                        pltpu.VMEM((D_EXPERT, D_MODEL), bf16),
                        pltpu.SemaphoreType.DMA((2, 3))],
    )
    return pl.pallas_call(
        _experts_kernel,
        grid_spec=grid_spec,
        out_shape=jax.ShapeDtypeStruct(xs.shape, jnp.uint32),
        compiler_params=_cparams(1),
        name="experts",
    )(blk_expert, blk_valid, slot.astype(jnp.int32), nxt.astype(jnp.int32), xs, wg, wu, wd)


def _final_norm_kernel(h_ref, y0_ref, y1_ref, route_ref, g_ref, b_ref, o_ref):
    route_t = route_ref[...].T
    g0, g1 = route_t[:, 2:3], route_t[:, 3:4]
    lo0, hi0 = _unpack_halves(y0_ref[...])
    lo1, hi1 = _unpack_halves(y1_ref[...])
    ffn = jnp.concatenate([lo0 * g0 + lo1 * g1, hi0 * g0 + hi1 * g1], axis=1)
    o_ref[...] = _layer_norm(ALPHA * h_ref[...] + ffn, g_ref[...], b_ref[...])


def _final_norm(h, y0, y1, route, ln_g, ln_b):
    n = h.shape[0]
    ts = NORM_TS
    tile = pl.BlockSpec((ts, D_MODEL), lambda i: (i, 0))
    packed = pl.BlockSpec((ts, D_PACKED), lambda i: (i, 0))
    return pl.pallas_call(
        _final_norm_kernel,
        grid=(n // ts,),
        in_specs=[tile, packed, packed, pl.BlockSpec((ROUTE_ROWS, ts), lambda i: (0, i)),
                  _full(ln_g), _full(ln_b)],
        out_specs=tile,
        out_shape=jax.ShapeDtypeStruct((n, D_MODEL), f32),
        compiler_params=_cparams(1),
        name="final_norm",
    )(h, y0, y1, route, ln_g, ln_b)


def _encode(x, p, meta_tokens, kv_meta, slabs):
    bsz, t, _ = x.shape
    n = bsz * t
    assert t % NORM_TS == 0 and t % TAIL_TM == 0 and t % PROJ_TM == 0
    xf = x.reshape(n, D_MODEL)
    qkv, rest = _in_proj(xf, meta_tokens, p["w_in"], p["conv_w"], p["conv_b"], t)
    att = _attention(qkv, kv_meta, slabs, bsz, t)
    h1, h1_packed, route, cnt = _mixer_tail(xf, att, rest, p)
    blk = MOE_BLK
    dest0, dest1, blk_expert, blk_valid, n_blocks = _route(route, cnt, n, blk)
    xs = _sc_dispatch(h1_packed, dest0, dest1, n_blocks * blk)
    ys = _experts(xs, blk_expert, blk_valid, p["w_e_gate"], p["w_e_up"], p["w_e_down"], n_blocks, blk)
    y0, y1 = _sc_gather2(ys, dest0, dest1)
    y = _final_norm(h1, y0, y1, route, p["ln2_g"], p["ln2_b"])
    return y.reshape(bsz, t, D_MODEL)


def kernel(x_prompt, x_sample, meta_tokens, w_in, rpb, conv_w, conv_b, w_att_proj, w_conv_proj, w_out,
           ln1_g, ln1_b, w_router_group, w_router_expert, w_e_gate, w_e_up, w_e_down, ln2_g, ln2_b):
    w_router = jnp.concatenate([w_router_expert[0], w_router_group[0]], axis=1)
    w_router = jnp.pad(w_router, ((0, 0), (0, LANES - w_router.shape[1])))
    row = lambda v: v[0].reshape(1, -1).astype(f32)
    p = {
        "w_in": w_in[0].astype(bf16),
        "conv_w": conv_w[0].astype(f32), "conv_b": row(conv_b),
        "w_att_proj": w_att_proj[0].astype(bf16), "w_conv_proj": w_conv_proj[0].astype(bf16),
        "w_out": (0.5 * w_out[0]).astype(bf16), "ln1_g": row(ln1_g), "ln1_b": row(ln1_b),
        "w_router": w_router.astype(bf16),
        "w_e_gate": w_e_gate[0], "w_e_up": w_e_up[0], "w_e_down": w_e_down[0],
        "ln2_g": row(ln2_g), "ln2_b": row(ln2_b),
    }
    meta_tokens = meta_tokens.astype(f32)
    kv_meta = _meta_kv(meta_tokens, p["w_in"][:, D_ATT:3 * D_ATT])
    slabs = _bias_pieces(rpb[0])
    y_prompt = _encode(x_prompt, p, meta_tokens, kv_meta, slabs)
    y_sample = _encode(x_sample, p, meta_tokens, kv_meta, slabs)
    return (y_prompt, y_sample)
```

```python
import functools

import numpy as np
import jax
import jax.numpy as jnp
from jax import lax
from jax.experimental import pallas as pl
from jax.experimental.pallas import tpu as pltpu
from jax.experimental.pallas import tpu_sc as plsc

D_MODEL = 1024
N_META = 16
GRID_W = 64
WIN_H = 8
WIN_W = 16
N_HEADS = 8
HEAD_DIM = 64
D_ATT = N_HEADS * HEAD_DIM
D_CONV = D_MODEL
N_GROUPS = 4
EXPERTS_PER_GROUP = 8
N_EXPERTS = N_GROUPS * EXPERTS_PER_GROUP
D_EXPERT = D_MODEL // 2
DEPTH = 1
ALPHA = (2.0 * DEPTH) ** 0.25
LN_EPS = 1e-5
D_QKV = 4 * D_ATT
LOG2E = 1.4426950408889634
D_REST = D_CONV + 2 * D_MODEL
D_IN_PROJ = 3 * D_ATT + 3 * D_CONV + 2 * D_MODEL

LANES = 128
HEADS_PER_VREG = LANES // HEAD_DIM
N_HEAD_PAIRS = N_HEADS // HEADS_PER_VREG
NEG_BIG = -1e30

PROJ_TM = 1024
PROJ_TN = 512
X_HALO = 8
ATT_ROWS = 32
ATT_GROUP = 8
SLAB_ROWS = WIN_H + 1
ATT_KEYS = SLAB_ROWS * GRID_W + N_META
N_DY = 2 * WIN_H - 1
PIECE_MT = N_DY - 1
PIECE_T0 = PIECE_MT + WIN_H
PIECE_M0 = PIECE_T0 + WIN_H
N_BIAS_PIECES = PIECE_M0 + 1
TAIL_TM = 512
TAIL_CW = 256
NORM_TS = 1024
NORM_RING = 3
MOE_BLK = 512
ROUTE_ROWS = 8
VMEM_LIMIT = 56 * 1024 * 1024

SC_CORES = 2
SC_SUBCORES = 16
SC_WORKERS = SC_CORES * SC_SUBCORES
SC_IDX_WIN = 128
SC_ROWS = 128
D_PACKED = D_MODEL // 2

bf16 = jnp.bfloat16
f32 = jnp.float32


def _cparams(n_axes):
    return pltpu.CompilerParams(dimension_semantics=("arbitrary",) * n_axes,
                                vmem_limit_bytes=VMEM_LIMIT)


def _full(a):
    return pl.BlockSpec(a.shape, lambda *_: (0,) * a.ndim)


def _pack_halves(x):
    half = x.shape[1] // 2
    bits = lambda v: lax.bitcast_convert_type(v.astype(bf16).astype(f32), jnp.uint32)
    return (bits(x[:, :half]) >> 16) | bits(x[:, half:])


def _unpack_halves(p):
    lo = lax.bitcast_convert_type(p << 16, f32)
    hi = lax.bitcast_convert_type(p & jnp.uint32(0xFFFF0000), f32)
    return lo, hi


def _in_proj_kernel(x_ref, prev_ref, next_ref, meta_ref, w_ref, convw_ref, convb_ref, qkv_ref, rest_ref,
                    *, tiles_per_seq):
    assert PROJ_TN == D_ATT
    pos = pl.program_id(0) % tiles_per_seq
    tm = x_ref.shape[0]
    x_prev = jnp.where(pos == 0, meta_ref[N_META - X_HALO:, :], prev_ref[...])
    x_body = x_ref[...].astype(bf16)
    x_ext = jnp.concatenate([x_prev, x_ref[...], next_ref[...]], axis=0).astype(bf16)
    body = slice(X_HALO, X_HALO + tm)
    col = lambda c: jnp.dot(x_body, w_ref[:, c:c + PROJ_TN], preferred_element_type=f32)
    col_ext = lambda c: jnp.dot(x_ext, w_ref[:, c:c + PROJ_TN], preferred_element_type=f32)
    proj0 = 3 * D_ATT

    y = (col(0) * (HEAD_DIM ** -0.5 * LOG2E)).astype(bf16)
    lane = lax.broadcasted_iota(jnp.int32, y.shape, 1)
    even = (lane & HEAD_DIM) == 0
    zero = jnp.zeros_like(y)
    qkv_ref[:, :D_ATT] = jnp.where(even, y, zero)
    qkv_ref[:, D_ATT:2 * D_ATT] = jnp.where(even, zero, y)
    for c in range(D_ATT, 3 * D_ATT, PROJ_TN):
        qkv_ref[:, c + D_ATT:c + D_ATT + PROJ_TN] = col(c).astype(bf16)

    row = lax.broadcasted_iota(jnp.int32, (tm, PROJ_TN), 0)
    no_next = jnp.logical_and(pos == tiles_per_seq - 1, row == tm - 1)
    for c in range(0, D_CONV, PROJ_TN):
        u_ext = col_ext(proj0 + D_CONV + c) * col_ext(proj0 + 2 * D_CONV + c)
        u_m1 = pltpu.roll(u_ext, 1, axis=0)[body]
        u_p1 = jnp.where(no_next, 0.0, pltpu.roll(u_ext, tm + 2 * X_HALO - 1, axis=0)[body])
        cw = convw_ref[:, c:c + PROJ_TN]
        conv = u_m1 * cw[0:1] + u_ext[body] * cw[1:2] + u_p1 * cw[2:3] + convb_ref[:, c:c + PROJ_TN]
        rest_ref[:, c:c + PROJ_TN] = (col(proj0 + c) * conv).astype(bf16)
    for c in range(0, 2 * D_MODEL, PROJ_TN):
        g = col(proj0 + 3 * D_CONV + c)
        rest_ref[:, D_CONV + c:D_CONV + c + PROJ_TN] = (jnp.tanh(0.5 * g) + 1.0).astype(bf16)


def _in_proj(x, meta_tokens, w_bf, conv_w, conv_b, t):
    assert w_bf.shape == (D_MODEL, D_IN_PROJ)
    n = x.shape[0]
    tm = PROJ_TM
    hb = tm // X_HALO
    n_halo = n // X_HALO
    halo = lambda index: pl.BlockSpec((X_HALO, D_MODEL), lambda i: (index(i), 0))
    return pl.pallas_call(
        functools.partial(_in_proj_kernel, tiles_per_seq=t // tm),
        grid=(n // tm,),
        in_specs=[pl.BlockSpec((tm, D_MODEL), lambda i: (i, 0)),
                  halo(lambda i: jnp.maximum(i * hb - 1, 0)),
                  halo(lambda i: jnp.minimum((i + 1) * hb, n_halo - 1)),
                  _full(meta_tokens), _full(w_bf), _full(conv_w), _full(conv_b)],
        out_specs=[pl.BlockSpec((tm, D_QKV), lambda i: (i, 0)),
                   pl.BlockSpec((tm, D_REST), lambda i: (i, 0))],
        out_shape=[jax.ShapeDtypeStruct((n, D_QKV), bf16),
                   jax.ShapeDtypeStruct((n, D_REST), bf16)],
        compiler_params=_cparams(1),
        name="in_proj",
    )(x, x, x, meta_tokens, w_bf, conv_w, conv_b)


def _meta_kv_kernel(m_ref, w_ref, kv_ref):
    kv_ref[...] = jnp.dot(m_ref[...].astype(bf16), w_ref[...], preferred_element_type=f32).astype(bf16)


def _meta_kv(meta_tokens, w_kv):
    return pl.pallas_call(
        _meta_kv_kernel,
        out_shape=jax.ShapeDtypeStruct((N_META, 2 * D_ATT), bf16),
        compiler_params=pltpu.CompilerParams(vmem_limit_bytes=VMEM_LIMIT),
        name="meta_kv",
    )(meta_tokens, w_kv)


def _bias_pieces(rpb):
    qc = np.arange(GRID_W)[:, None]
    kc = np.arange(GRID_W)[None, :]
    w_start = np.clip(qc - WIN_W // 2, 0, GRID_W - WIN_W)
    valid = (kc >= w_start) & (kc < w_start + WIN_W)
    n_dx = 2 * WIN_W - 1
    onehot = (kc - qc + WIN_W - 1)[None] == np.arange(n_dx)[:, None, None]
    t = jnp.einsum("hyx,xqk->hyqk", rpb.astype(f32) * LOG2E, jnp.asarray(onehot & valid[None], f32),
                   precision=lax.Precision.HIGHEST)
    t = jnp.where(jnp.asarray(valid)[None, None], t, NEG_BIG)
    t = t.reshape(N_HEAD_PAIRS, HEADS_PER_VREG, N_DY, GRID_W, GRID_W)
    t = jnp.concatenate([t[:, p] for p in range(HEADS_PER_VREG)], axis=2)
    masked = jnp.full((N_HEAD_PAIRS, WIN_H, HEADS_PER_VREG * GRID_W, GRID_W), NEG_BIG, f32)
    zeros = jnp.zeros_like(masked)
    pieces = jnp.concatenate([
        jnp.concatenate([t[:, :N_DY - 1], t[:, 1:]], axis=-1),
        jnp.concatenate([masked, t[:, :WIN_H]], axis=-1),
        jnp.concatenate([t[:, WIN_H - 1:], zeros], axis=-1),
        jnp.concatenate([masked[:, :1], zeros[:, :1]], axis=-1),
    ], axis=1)
    assert pieces.shape[1] == N_BIAS_PIECES
    return pieces


def _attention_kernel(qe_ref, qo_ref, k_ref, v_ref, km_ref, vm_ref, bias_ref, o_ref, *, rows):
    j = pl.program_id(1)
    tq2 = 2 * GRID_W
    lane = lax.broadcasted_iota(jnp.int32, (tq2, LANES), 1)
    low = lane < HEAD_DIM
    nt = (((1,), (1,)), ((), ()))
    slab_keys = SLAB_ROWS * GRID_W

    def row_bias(pair, dy0, off):
        first = jnp.where(off == 0, dy0, PIECE_MT + dy0)
        mids = [dy0 + 2 * p - off for p in range(1, SLAB_ROWS // 2)]
        last = jnp.where(off == 0, PIECE_M0, PIECE_T0 + dy0)
        tail = ATT_KEYS - (SLAB_ROWS // 2) * LANES
        return jnp.concatenate([bias_ref[pair, first]] + [bias_ref[pair, m] for m in mids]
                               + [bias_ref[pair, last][:, :tail]], axis=1)

    def scores(ip, pair):
        r0 = j * ATT_ROWS + 2 * ip
        rs = [jnp.clip(r0 + a - WIN_H // 2, 0, rows - WIN_H) for a in range(2)]
        us = jnp.minimum(rs[0], rows - SLAB_ROWS)
        q0 = pl.multiple_of(ip * tq2, tq2)
        k0 = pl.multiple_of(us * GRID_W, GRID_W)
        cs = slice(pair * LANES, (pair + 1) * LANES)
        qq = jnp.concatenate([qe_ref[pl.ds(q0, tq2), cs], qo_ref[pl.ds(q0, tq2), cs]], axis=0)
        k2 = jnp.concatenate([k_ref[pl.ds(k0, slab_keys), cs], km_ref[:, cs]], axis=0)
        s = lax.dot_general(qq, k2, nt, preferred_element_type=f32)
        b = [row_bias(pair, rs[a] - (r0 + a) + (WIN_H - 1), rs[a] - us) for a in range(2)]
        s = s + jnp.concatenate([b[0][:GRID_W], b[1][:GRID_W], b[0][GRID_W:], b[1][GRID_W:]], axis=0)
        return s, jnp.max(s, axis=-1, keepdims=True), k0, q0, cs

    def weights(state):
        s, m, k0, q0, cs = state
        v2 = jnp.concatenate([v_ref[pl.ds(k0, slab_keys), cs], vm_ref[:, cs]], axis=0)
        e = jnp.exp2(s - m)
        l = jnp.sum(e, axis=-1, keepdims=True)
        return jnp.dot(e.astype(bf16), v2, preferred_element_type=f32), l, q0, cs

    def finish(state):
        o2, l, q0, cs = state
        o2 = o2 / l
        o = jnp.where(low, o2[:tq2], o2[tq2:])
        o_ref[pl.ds(q0, tq2), cs] = o.astype(bf16)

    def group_body(g, carry):
        items = [(g * ATT_GROUP + ip, pair) for ip in range(ATT_GROUP) for pair in range(N_HEAD_PAIRS)]
        a, b = {}, {}
        for step in range(len(items) + 2):
            if step < len(items):
                a[step] = scores(*items[step])
            if 0 <= step - 1 < len(items):
                b[step - 1] = weights(a.pop(step - 1))
            if 0 <= step - 2 < len(items):
                finish(b.pop(step - 2))
        return carry

    lax.fori_loop(0, ATT_ROWS // (2 * ATT_GROUP), group_body, 0)


def _attention(qkv, kv_meta, slabs, bsz, t):
    rows = t // GRID_W
    assert rows >= 2 * WIN_H and rows % ATT_ROWS == 0 and ATT_ROWS % (2 * ATT_GROUP) == 0
    steps = rows // ATT_ROWS
    tq = ATT_ROWS * GRID_W
    return pl.pallas_call(
        functools.partial(_attention_kernel, rows=rows),
        grid=(bsz, steps),
        in_specs=[pl.BlockSpec((tq, D_ATT), lambda b, j: (b * steps + j, 0)),
                  pl.BlockSpec((tq, D_ATT), lambda b, j: (b * steps + j, 1)),
                  pl.BlockSpec((t, D_ATT), lambda b, j: (b, 2)),
                  pl.BlockSpec((t, D_ATT), lambda b, j: (b, 3)),
                  pl.BlockSpec((N_META, D_ATT), lambda b, j: (0, 0)),
                  pl.BlockSpec((N_META, D_ATT), lambda b, j: (0, 1)),
                  _full(slabs)],
        out_specs=pl.BlockSpec((tq, D_ATT), lambda b, j: (b * steps + j, 0)),
        out_shape=jax.ShapeDtypeStruct((bsz * t, D_ATT), bf16),
        compiler_params=_cparams(2),
        name="attention",
    )(qkv, qkv, qkv, qkv, kv_meta, kv_meta, slabs)


def _layer_norm(z, g, b):
    mu = jnp.mean(z, axis=-1, keepdims=True)
    d = z - mu
    var = jnp.mean(d * d, axis=-1, keepdims=True)
    return d * lax.rsqrt(var + LN_EPS) * g + b


def _sigmoid(x):
    return 0.5 * jnp.tanh(0.5 * x) + 0.5


def _route_tile(logits, before_ref):
    tm = logits.shape[0]
    lt = logits.T
    el = lt[:N_EXPERTS]
    gl = lt[N_EXPERTS:N_EXPERTS + N_GROUPS]
    neg = -jnp.inf
    erow = lax.broadcasted_iota(jnp.int32, el.shape, 0).astype(f32)
    grow = lax.broadcasted_iota(jnp.int32, gl.shape, 0).astype(f32)
    first = lambda hit, idx, n: jnp.min(jnp.where(hit, idx, float(n)), axis=0, keepdims=True)

    gmax = jnp.max(gl, axis=0, keepdims=True)
    grp = first(gl == gmax, grow, N_GROUPS)
    pg_sel = 1.0 / jnp.sum(jnp.exp(gl - gmax), axis=0, keepdims=True)

    e_lo = grp * EXPERTS_PER_GROUP
    elm = jnp.where((erow >= e_lo) & (erow < e_lo + EXPERTS_PER_GROUP), el, neg)
    t1 = jnp.max(elm, axis=0, keepdims=True)
    e1 = first(elm == t1, erow, N_EXPERTS)
    el2 = jnp.where(erow == e1, neg, elm)
    t2 = jnp.max(el2, axis=0, keepdims=True)
    e2 = first(el2 == t2, erow, N_EXPERTS)
    r = jnp.exp(t2 - t1)
    g1 = pg_sel / (1.0 + r)
    g2 = pg_sel * r / (1.0 + r)

    hit1 = erow == e1
    hit2 = erow == e2
    onehot = jnp.where(hit1 | hit2, 1.0, 0.0)
    before = jnp.dot(onehot.astype(bf16), before_ref[...], preferred_element_type=f32)
    rank1 = jnp.sum(jnp.where(hit1, before, 0.0), axis=0, keepdims=True)
    rank2 = jnp.sum(jnp.where(hit2, before, 0.0), axis=0, keepdims=True)
    rrow = lax.broadcasted_iota(jnp.int32, (ROUTE_ROWS, tm), 0)
    route = jnp.zeros((ROUTE_ROWS, tm), f32)
    for k, val in enumerate((e1, e2, g1, g2, rank1, rank2)):
        route = jnp.where(rrow == k, val, route)
    return route, jnp.sum(onehot, axis=1, keepdims=True)


def _mixer_tail_kernel(x_ref, att_ref, rest_ref, wap_ref, wcp_ref, wout_ref, g_ref, b_ref, wr_ref, before_ref,
                       h_ref, hp_ref, route_ref, cnt_ref, merged_scr, *, n_tiles):
    i = pl.program_id(0)
    c3, c4 = D_CONV, D_CONV + D_MODEL
    cols = lambda j, base=0: slice(base + j * TAIL_CW, base + (j + 1) * TAIL_CW)
    chunks = range(D_MODEL // TAIL_CW)

    @pl.when(i == 0)
    def _():
        merged_scr[...] = jnp.zeros_like(merged_scr)

    mix = [jnp.dot(merged_scr[...], wout_ref[:, cols(j)], preferred_element_type=f32) for j in chunks]

    cv = jnp.dot(rest_ref[:, :D_CONV], wcp_ref[...], preferred_element_type=f32)
    ap = jnp.dot(att_ref[...], wap_ref[...], preferred_element_type=f32)

    z = [ALPHA * x_ref[:, cols(j)] + mix[j] for j in chunks]
    mu = sum(jnp.sum(zj, axis=-1, keepdims=True) for zj in z) * (1.0 / D_MODEL)
    d = [zj - mu for zj in z]
    var = sum(jnp.sum(dj * dj, axis=-1, keepdims=True) for dj in d) * (1.0 / D_MODEL)
    rstd = lax.rsqrt(var + LN_EPS)
    logits = None
    hs = []
    for j in chunks:
        hj = d[j] * rstd * g_ref[:, cols(j)] + b_ref[:, cols(j)]
        h_ref[:, cols(j)] = hj
        hs.append(hj)
        part = jnp.dot(hj.astype(bf16), wr_ref[cols(j), :], preferred_element_type=f32)
        logits = part if logits is None else logits + part
    hp_ref[...] = _pack_halves(jnp.concatenate(hs, axis=1))

    for j in chunks:
        merged_scr[:, cols(j)] = (rest_ref[:, cols(j, c3)] * ap[:, cols(j)].astype(bf16)
                                  + rest_ref[:, cols(j, c4)] * cv[:, cols(j)].astype(bf16))

    route, count = _route_tile(logits, before_ref)
    route_ref[...] = route
    cnt_ref[0] = jnp.broadcast_to(count, (N_EXPERTS, LANES))


def _mixer_tail(x, att, rest, p):
    n = x.shape[0]
    tm = TAIL_TM
    n_tiles = n // tm
    before = jnp.asarray(np.triu(np.ones((tm, tm), np.float32), 1), dtype=bf16)
    consts = (p["w_att_proj"], p["w_conv_proj"], p["w_out"], p["ln1_g"], p["ln1_b"], p["w_router"], before)
    front = lambda i: jnp.minimum(i, n_tiles - 1)
    back = lambda i: jnp.maximum(i - 1, 0)
    return pl.pallas_call(
        functools.partial(_mixer_tail_kernel, n_tiles=n_tiles),
        grid=(n_tiles + 1,),
        in_specs=[pl.BlockSpec((tm, D_MODEL), lambda i: (back(i), 0)),
                  pl.BlockSpec((tm, D_ATT), lambda i: (front(i), 0)),
                  pl.BlockSpec((tm, D_REST), lambda i: (front(i), 0))]
                 + [_full(c) for c in consts],
        out_specs=[pl.BlockSpec((tm, D_MODEL), lambda i: (back(i), 0)),
                   pl.BlockSpec((tm, D_PACKED), lambda i: (back(i), 0)),
                   pl.BlockSpec((ROUTE_ROWS, tm), lambda i: (0, back(i))),
                   pl.BlockSpec((1, N_EXPERTS, LANES), lambda i: (back(i), 0, 0))],
        out_shape=[jax.ShapeDtypeStruct((n, D_MODEL), f32),
                   jax.ShapeDtypeStruct((n, D_PACKED), jnp.uint32),
                   jax.ShapeDtypeStruct((ROUTE_ROWS, n), f32),
                   jax.ShapeDtypeStruct((n_tiles, N_EXPERTS, LANES), f32)],
        scratch_shapes=[pltpu.VMEM((tm, D_MODEL), bf16)],
        compiler_params=_cparams(1),
        name="mixer_tail",
    )(x, att, rest, *consts)


def _route(route, cnt, n, blk):
    n_tiles = n // TAIL_TM
    tile_cnt = cnt[:, :, 0].astype(jnp.int32)
    tile_off = jnp.cumsum(tile_cnt, axis=0) - tile_cnt
    counts = jnp.sum(tile_cnt, axis=0)
    padded = (counts + blk - 1) // blk * blk
    pend = jnp.cumsum(padded)
    pstart = pend - padded
    base = pstart[None, :] + tile_off
    lanes = jnp.arange(N_EXPERTS, dtype=jnp.int32)

    def rows_of(expert_row, rank_row):
        e = expert_row.astype(jnp.int32).reshape(n_tiles, TAIL_TM, 1)
        sel = jnp.sum(jnp.where(e == lanes, base[:, None, :], 0), axis=-1)
        return sel.reshape(n) + rank_row.astype(jnp.int32)

    dest0 = rows_of(route[0], route[4])
    dest1 = rows_of(route[1], route[5])

    n_blocks = (2 * n + N_EXPERTS * (blk - 1) + blk - 1) // blk
    blk_start = jnp.arange(n_blocks, dtype=jnp.int32) * blk
    blk_expert = jnp.sum((pend[None, :] <= blk_start[:, None]).astype(jnp.int32), axis=1)
    blk_expert = jnp.minimum(blk_expert, N_EXPERTS - 1)
    blk_valid = jnp.clip((pstart + counts)[blk_expert] - blk_start, 0, blk)
    blk_valid = jnp.where(blk_start < pend[-1], blk_valid, 0).astype(jnp.int32)
    return dest0, dest1, blk_expert, blk_valid, n_blocks


def _sc_mesh():
    return plsc.VectorSubcoreMesh(core_axis_name="c", subcore_axis_name="s")


def _sc_dispatch(h, dest0, dest1, p_rows):
    n, d = h.shape
    per = n // SC_WORKERS
    assert n % (SC_WORKERS * SC_IDX_WIN) == 0

    @pl.kernel(out_type=jax.ShapeDtypeStruct((p_rows, d), h.dtype), mesh=_sc_mesh(),
               scratch_types=[pltpu.VMEM((2, SC_IDX_WIN), jnp.int32), pltpu.VMEM((SC_ROWS, d), h.dtype)])
    def k(h_hbm, d0_hbm, d1_hbm, xs_hbm, idx, buf):
        base = (lax.axis_index("c") * SC_SUBCORES + lax.axis_index("s")) * per

        @pl.loop(0, per // SC_IDX_WIN)
        def _(w):
            off = base + w * SC_IDX_WIN
            pltpu.sync_copy(d0_hbm.at[pl.ds(off, SC_IDX_WIN)], idx.at[0])
            pltpu.sync_copy(d1_hbm.at[pl.ds(off, SC_IDX_WIN)], idx.at[1])
            for r in range(SC_IDX_WIN // SC_ROWS):
                pltpu.sync_copy(h_hbm.at[pl.ds(off + r * SC_ROWS, SC_ROWS)], buf)
                pltpu.sync_copy(buf, xs_hbm.at[idx.at[0, pl.ds(r * SC_ROWS, SC_ROWS)]])
                pltpu.sync_copy(buf, xs_hbm.at[idx.at[1, pl.ds(r * SC_ROWS, SC_ROWS)]])

    return k(h, dest0, dest1)


def _sc_gather2(ys, dest0, dest1):
    n = dest0.shape[0]
    d = ys.shape[1]
    per = n // SC_WORKERS
    out = jax.ShapeDtypeStruct((n, d), ys.dtype)

    @pl.kernel(out_type=[out, out], mesh=_sc_mesh(),
               scratch_types=[pltpu.VMEM((2, SC_IDX_WIN), jnp.int32), pltpu.VMEM((SC_ROWS, d), ys.dtype)])
    def k(ys_hbm, d0_hbm, d1_hbm, y0_hbm, y1_hbm, idx, buf):
        base = (lax.axis_index("c") * SC_SUBCORES + lax.axis_index("s")) * per

        @pl.loop(0, per // SC_IDX_WIN)
        def _(w):
            off = base + w * SC_IDX_WIN
            pltpu.sync_copy(d0_hbm.at[pl.ds(off, SC_IDX_WIN)], idx.at[0])
            pltpu.sync_copy(d1_hbm.at[pl.ds(off, SC_IDX_WIN)], idx.at[1])
            for r in range(SC_IDX_WIN // SC_ROWS):
                for kk, y_hbm in enumerate((y0_hbm, y1_hbm)):
                    pltpu.sync_copy(ys_hbm.at[idx.at[kk, pl.ds(r * SC_ROWS, SC_ROWS)]], buf)
                    pltpu.sync_copy(buf, y_hbm.at[pl.ds(off + r * SC_ROWS, SC_ROWS)])

    return k(ys, dest0, dest1)


def _experts_kernel(be_ref, bv_ref, slot_ref, nxt_ref, xs_ref, wg_hbm, wu_hbm, wd_hbm, ys_ref,
                    wg_f32, wu_f32, wd_f32, wg_bf, wu_bf, wd_bf, sems):
    i = pl.program_id(0)
    valid = bv_ref[i]
    expert = be_ref[i]
    slot = slot_ref[i]
    run_start = jnp.logical_or(i == 0, expert != be_ref[jnp.maximum(i - 1, 0)])

    def fetch(e, s):
        return [pltpu.make_async_copy(hbm.at[e], buf.at[s], sems.at[s, k])
                for k, (hbm, buf) in enumerate(((wg_hbm, wg_f32), (wu_hbm, wu_f32), (wd_hbm, wd_f32)))]

    @pl.when(i == 0)
    def _():
        for cp in fetch(expert, slot):
            cp.start()

    @pl.when(run_start)
    def _():
        for cp in fetch(expert, slot):
            cp.wait()
        wg_bf[...] = wg_f32[slot].astype(bf16)
        wu_bf[...] = wu_f32[slot].astype(bf16)
        wd_bf[...] = wd_f32[slot].astype(bf16)

        @pl.when(nxt_ref[i] >= 0)
        def _():
            for cp in fetch(nxt_ref[i], 1 - slot):
                cp.start(priority=1)

    @pl.when(valid > 0)
    def _():
        row = lax.broadcasted_iota(jnp.int32, xs_ref.shape, 0)
        keep = row < valid
        lo, hi = _unpack_halves(xs_ref[...])
        x = jnp.concatenate([jnp.where(keep, lo, 0.0), jnp.where(keep, hi, 0.0)], axis=1).astype(bf16)
        g = jnp.dot(x, wg_bf[...], preferred_element_type=f32)
        u = jnp.dot(x, wu_bf[...], preferred_element_type=f32)
        hmid = (g * _sigmoid(g) * u).astype(bf16)
        ys_ref[...] = _pack_halves(jnp.dot(hmid, wd_bf[...], preferred_element_type=f32))

    @pl.when(valid == 0)
    def _():
        ys_ref[...] = jnp.zeros_like(ys_ref)


def _experts(xs, blk_expert, blk_valid, wg, wu, wd, n_blocks, blk):
    starts = jnp.concatenate([jnp.ones((1,), jnp.int32), (blk_expert[1:] != blk_expert[:-1]).astype(jnp.int32)])
    slot = (jnp.cumsum(starts) - 1) % 2
    idx = jnp.arange(n_blocks, dtype=jnp.int32)
    next_start = lax.cummin(jnp.where(starts > 0, idx, n_blocks)[::-1])[::-1]
    next_start = jnp.concatenate([next_start[1:], jnp.full((1,), n_blocks, jnp.int32)])
    nxt = jnp.where(next_start < n_blocks, blk_expert[jnp.minimum(next_start, n_blocks - 1)], -1)
    tile = pl.BlockSpec((blk, D_PACKED), lambda i, *_: (i, 0))
    any_spec = pl.BlockSpec(memory_space=pl.ANY)
    grid_spec = pltpu.PrefetchScalarGridSpec(
        num_scalar_prefetch=4,
        grid=(n_blocks,),
        in_specs=[tile, any_spec, any_spec, any_spec],
        out_specs=tile,
        scratch_shapes=[pltpu.VMEM((2, D_MODEL, D_EXPERT), f32),
                        pltpu.VMEM((2, D_MODEL, D_EXPERT), f32),
                        pltpu.VMEM((2, D_EXPERT, D_MODEL), f32),
                        pltpu.VMEM((D_MODEL, D_EXPERT), bf16),
                        pltpu.VMEM((D_MODEL, D_EXPERT), bf16),
                        pltpu.VMEM((D_EXPERT, D_MODEL), bf16),
                        pltpu.SemaphoreType.DMA((2, 3))],
    )
    return pl.pallas_call(
        _experts_kernel,
        grid_spec=grid_spec,
        out_shape=jax.ShapeDtypeStruct(xs.shape, jnp.uint32),
        compiler_params=_cparams(1),
        name="experts",
    )(blk_expert, blk_valid, slot.astype(jnp.int32), nxt.astype(jnp.int32), xs, wg, wu, wd)


def _final_norm_kernel(h_hbm, y0_ref, y1_ref, route_ref, g_ref, b_ref, o_ref, h_buf, h_sems):
    s = pl.program_id(0)
    n_steps = pl.num_programs(0)
    ts = o_ref.shape[0]

    def fetch(step):
        slot = step % NORM_RING
        return pltpu.make_async_copy(h_hbm.at[pl.ds(pl.multiple_of(step * ts, ts), ts)], h_buf.at[slot],
                                     h_sems.at[slot])

    @pl.when(s == 0)
    def _():
        for k in range(NORM_RING - 1):
            pl.when(k < n_steps)(lambda k=k: fetch(k).start())

    @pl.when(s + NORM_RING - 1 < n_steps)
    def _():
        fetch(s + NORM_RING - 1).start()

    fetch(s).wait()
    h = h_buf[s % NORM_RING]
    route_t = route_ref[...].T
    g0, g1 = route_t[:, 2:3], route_t[:, 3:4]
    lo0, hi0 = _unpack_halves(y0_ref[...])
    lo1, hi1 = _unpack_halves(y1_ref[...])
    ffn = jnp.concatenate([lo0 * g0 + lo1 * g1, hi0 * g0 + hi1 * g1], axis=1)
    o_ref[...] = _layer_norm(ALPHA * h + ffn, g_ref[...], b_ref[...])


def _final_norm(h, y0, y1, route, ln_g, ln_b):
    n = h.shape[0]
    ts = NORM_TS
    tile = pl.BlockSpec((ts, D_MODEL), lambda i: (i, 0))
    packed = pl.BlockSpec((ts, D_PACKED), lambda i: (i, 0))
    return pl.pallas_call(
        _final_norm_kernel,
        grid=(n // ts,),
        in_specs=[pl.BlockSpec(memory_space=pl.ANY), packed, packed,
                  pl.BlockSpec((ROUTE_ROWS, ts), lambda i: (0, i)), _full(ln_g), _full(ln_b)],
        out_specs=tile,
        out_shape=jax.ShapeDtypeStruct((n, D_MODEL), f32),
        scratch_shapes=[pltpu.VMEM((NORM_RING, ts, D_MODEL), f32), pltpu.SemaphoreType.DMA((NORM_RING,))],
        compiler_params=_cparams(1),
        name="final_norm",
    )(h, y0, y1, route, ln_g, ln_b)


def _encode(x, p, meta_tokens, kv_meta, slabs):
    bsz, t, _ = x.shape
    n = bsz * t
    assert t % NORM_TS == 0 and t % TAIL_TM == 0 and t % PROJ_TM == 0
    xf = x.reshape(n, D_MODEL)
    qkv, rest = _in_proj(xf, meta_tokens, p["w_in"], p["conv_w"], p["conv_b"], t)
    att = _attention(qkv, kv_meta, slabs, bsz, t)
    h1, h1_packed, route, cnt = _mixer_tail(xf, att, rest, p)
    blk = MOE_BLK
    dest0, dest1, blk_expert, blk_valid, n_blocks = _route(route, cnt, n, blk)
    xs = _sc_dispatch(h1_packed, dest0, dest1, n_blocks * blk)
    ys = _experts(xs, blk_expert, blk_valid, p["w_e_gate"], p["w_e_up"], p["w_e_down"], n_blocks, blk)
    y0, y1 = _sc_gather2(ys, dest0, dest1)
    y = _final_norm(h1, y0, y1, route, p["ln2_g"], p["ln2_b"])
    return y.reshape(bsz, t, D_MODEL)


def kernel(x_prompt, x_sample, meta_tokens, w_in, rpb, conv_w, conv_b, w_att_proj, w_conv_proj, w_out,
           ln1_g, ln1_b, w_router_group, w_router_expert, w_e_gate, w_e_up, w_e_down, ln2_g, ln2_b):
    w_router = jnp.concatenate([w_router_expert[0], w_router_group[0]], axis=1)
    w_router = jnp.pad(w_router, ((0, 0), (0, LANES - w_router.shape[1])))
    row = lambda v: v[0].reshape(1, -1).astype(f32)
    p = {
        "w_in": w_in[0].astype(bf16),
        "conv_w": conv_w[0].astype(f32), "conv_b": row(conv_b),
        "w_att_proj": w_att_proj[0].astype(bf16), "w_conv_proj": w_conv_proj[0].astype(bf16),
        "w_out": (0.5 * w_out[0]).astype(bf16), "ln1_g": row(ln1_g), "ln1_b": row(ln1_b),
        "w_router": w_router.astype(bf16),
        "w_e_gate": w_e_gate[0], "w_e_up": w_e_up[0], "w_e_down": w_e_down[0],
        "ln2_g": row(ln2_g), "ln2_b": row(ln2_b),
    }
    meta_tokens = meta_tokens.astype(f32)
    kv_meta = _meta_kv(meta_tokens, p["w_in"][:, D_ATT:3 * D_ATT])
    slabs = _bias_pieces(rpb[0])
    y_prompt = _encode(x_prompt, p, meta_tokens, kv_meta, slabs)
    y_sample = _encode(x_sample, p, meta_tokens, kv_meta, slabs)
    return (y_prompt, y_sample)
```

```python
import functools

import numpy as np
import jax
import jax.numpy as jnp
from jax import lax
from jax.experimental import pallas as pl
from jax.experimental.pallas import tpu as pltpu
from jax.experimental.pallas import tpu_sc as plsc

D_MODEL = 1024
N_META = 16
GRID_W = 64
WIN_H = 8
WIN_W = 16
N_HEADS = 8
HEAD_DIM = 64
D_ATT = N_HEADS * HEAD_DIM
D_CONV = D_MODEL
N_GROUPS = 4
EXPERTS_PER_GROUP = 8
N_EXPERTS = N_GROUPS * EXPERTS_PER_GROUP
D_EXPERT = D_MODEL // 2
DEPTH = 1
ALPHA = (2.0 * DEPTH) ** 0.25
LN_EPS = 1e-5
D_QKV = 4 * D_ATT
LOG2E = 1.4426950408889634
D_REST = D_CONV + 2 * D_MODEL
D_IN_PROJ = 3 * D_ATT + 3 * D_CONV + 2 * D_MODEL

LANES = 128
HEADS_PER_VREG = LANES // HEAD_DIM
N_HEAD_PAIRS = N_HEADS // HEADS_PER_VREG
NEG_BIG = -1e30

PROJ_TM = 1024
PROJ_TN = 512
X_HALO = 8
ATT_ROWS = 32
ATT_GROUP = 8
SLAB_ROWS = WIN_H + 1
ATT_KEYS = SLAB_ROWS * GRID_W + N_META
N_DY = 2 * WIN_H - 1
PIECE_MT = N_DY - 1
PIECE_T0 = PIECE_MT + WIN_H
PIECE_M0 = PIECE_T0 + WIN_H
N_BIAS_PIECES = PIECE_M0 + 1
TAIL_TM = 512
TAIL_CW = 256
NORM_TS = 1024
NORM_RING = 3
MOE_BLK = 512
ROUTE_ROWS = 8
VMEM_LIMIT = 56 * 1024 * 1024

SC_CORES = 2
SC_SUBCORES = 16
SC_WORKERS = SC_CORES * SC_SUBCORES
SC_IDX_WIN = 128
SC_ROWS = 128
D_PACKED = D_MODEL // 2

bf16 = jnp.bfloat16
f32 = jnp.float32


def _cparams(n_axes):
    return pltpu.CompilerParams(dimension_semantics=("arbitrary",) * n_axes,
                                vmem_limit_bytes=VMEM_LIMIT)


def _full(a):
    return pl.BlockSpec(a.shape, lambda *_: (0,) * a.ndim)


def _pack_halves(x):
    half = x.shape[1] // 2
    bits = lambda v: lax.bitcast_convert_type(v.astype(bf16).astype(f32), jnp.uint32)
    return (bits(x[:, :half]) >> 16) | bits(x[:, half:])


def _unpack_halves(p):
    lo = lax.bitcast_convert_type(p << 16, f32)
    hi = lax.bitcast_convert_type(p & jnp.uint32(0xFFFF0000), f32)
    return lo, hi


def _in_proj_kernel(x_ref, prev_ref, next_ref, meta_ref, w_ref, convw_ref, convb_ref, qkv_ref, rest_ref,
                    *, tiles_per_seq):
    assert PROJ_TN == D_ATT
    pos = pl.program_id(0) % tiles_per_seq
    tm = x_ref.shape[0]
    x_prev = jnp.where(pos == 0, meta_ref[N_META - X_HALO:, :], prev_ref[...])
    x_body = x_ref[...].astype(bf16)
    x_ext = jnp.concatenate([x_prev, x_ref[...], next_ref[...]], axis=0).astype(bf16)
    body = slice(X_HALO, X_HALO + tm)
    col = lambda c: jnp.dot(x_body, w_ref[:, c:c + PROJ_TN], preferred_element_type=f32)
    col_ext = lambda c: jnp.dot(x_ext, w_ref[:, c:c + PROJ_TN], preferred_element_type=f32)
    proj0 = 3 * D_ATT

    y = (col(0) * (HEAD_DIM ** -0.5 * LOG2E)).astype(bf16)
    lane = lax.broadcasted_iota(jnp.int32, y.shape, 1)
    even = (lane & HEAD_DIM) == 0
    zero = jnp.zeros_like(y)
    qkv_ref[:, :D_ATT] = jnp.where(even, y, zero)
    qkv_ref[:, D_ATT:2 * D_ATT] = jnp.where(even, zero, y)
    for c in range(D_ATT, 3 * D_ATT, PROJ_TN):
        qkv_ref[:, c + D_ATT:c + D_ATT + PROJ_TN] = col(c).astype(bf16)

    row = lax.broadcasted_iota(jnp.int32, (tm, PROJ_TN), 0)
    no_next = jnp.logical_and(pos == tiles_per_seq - 1, row == tm - 1)
    for c in range(0, D_CONV, PROJ_TN):
        u_ext = col_ext(proj0 + D_CONV + c) * col_ext(proj0 + 2 * D_CONV + c)
        u_m1 = pltpu.roll(u_ext, 1, axis=0)[body]
        u_p1 = jnp.where(no_next, 0.0, pltpu.roll(u_ext, tm + 2 * X_HALO - 1, axis=0)[body])
        cw = convw_ref[:, c:c + PROJ_TN]
        conv = u_m1 * cw[0:1] + u_ext[body] * cw[1:2] + u_p1 * cw[2:3] + convb_ref[:, c:c + PROJ_TN]
        rest_ref[:, c:c + PROJ_TN] = (col(proj0 + c) * conv).astype(bf16)
    for c in range(0, 2 * D_MODEL, PROJ_TN):
        g = col(proj0 + 3 * D_CONV + c)
        rest_ref[:, D_CONV + c:D_CONV + c + PROJ_TN] = (jnp.tanh(0.5 * g) + 1.0).astype(bf16)


def _in_proj(x, meta_tokens, w_bf, conv_w, conv_b, t):
    assert w_bf.shape == (D_MODEL, D_IN_PROJ)
    n = x.shape[0]
    tm = PROJ_TM
    hb = tm // X_HALO
    n_halo = n // X_HALO
    halo = lambda index: pl.BlockSpec((X_HALO, D_MODEL), lambda i: (index(i), 0))
    return pl.pallas_call(
        functools.partial(_in_proj_kernel, tiles_per_seq=t // tm),
        grid=(n // tm,),
        in_specs=[pl.BlockSpec((tm, D_MODEL), lambda i: (i, 0)),
                  halo(lambda i: jnp.maximum(i * hb - 1, 0)),
                  halo(lambda i: jnp.minimum((i + 1) * hb, n_halo - 1)),
                  _full(meta_tokens), _full(w_bf), _full(conv_w), _full(conv_b)],
        out_specs=[pl.BlockSpec((tm, D_QKV), lambda i: (i, 0)),
                   pl.BlockSpec((tm, D_REST), lambda i: (i, 0))],
        out_shape=[jax.ShapeDtypeStruct((n, D_QKV), bf16),
                   jax.ShapeDtypeStruct((n, D_REST), bf16)],
        compiler_params=_cparams(1),
        name="in_proj",
    )(x, x, x, meta_tokens, w_bf, conv_w, conv_b)


def _meta_kv_kernel(m_ref, w_ref, kv_ref):
    kv_ref[...] = jnp.dot(m_ref[...].astype(bf16), w_ref[...], preferred_element_type=f32).astype(bf16)


def _meta_kv(meta_tokens, w_kv):
    return pl.pallas_call(
        _meta_kv_kernel,
        out_shape=jax.ShapeDtypeStruct((N_META, 2 * D_ATT), bf16),
        compiler_params=pltpu.CompilerParams(vmem_limit_bytes=VMEM_LIMIT),
        name="meta_kv",
    )(meta_tokens, w_kv)


def _bias_pieces(rpb):
    qc = np.arange(GRID_W)[:, None]
    kc = np.arange(GRID_W)[None, :]
    w_start = np.clip(qc - WIN_W // 2, 0, GRID_W - WIN_W)
    valid = (kc >= w_start) & (kc < w_start + WIN_W)
    n_dx = 2 * WIN_W - 1
    onehot = (kc - qc + WIN_W - 1)[None] == np.arange(n_dx)[:, None, None]
    t = jnp.einsum("hyx,xqk->hyqk", rpb.astype(f32) * LOG2E, jnp.asarray(onehot & valid[None], f32),
                   precision=lax.Precision.HIGHEST)
    t = jnp.where(jnp.asarray(valid)[None, None], t, NEG_BIG)
    t = t.reshape(N_HEAD_PAIRS, HEADS_PER_VREG, N_DY, GRID_W, GRID_W)
    t = jnp.concatenate([t[:, p] for p in range(HEADS_PER_VREG)], axis=2)
    masked = jnp.full((N_HEAD_PAIRS, WIN_H, HEADS_PER_VREG * GRID_W, GRID_W), NEG_BIG, f32)
    zeros = jnp.zeros_like(masked)
    pieces = jnp.concatenate([
        jnp.concatenate([t[:, :N_DY - 1], t[:, 1:]], axis=-1),
        jnp.concatenate([masked, t[:, :WIN_H]], axis=-1),
        jnp.concatenate([t[:, WIN_H - 1:], zeros], axis=-1),
        jnp.concatenate([masked[:, :1], zeros[:, :1]], axis=-1),
    ], axis=1)
    assert pieces.shape[1] == N_BIAS_PIECES
    return pieces


def _attention_kernel(qe_ref, qo_ref, k_ref, v_ref, km_ref, vm_ref, bias_ref, o_ref, *, rows):
    j = pl.program_id(1)
    tq2 = 2 * GRID_W
    lane = lax.broadcasted_iota(jnp.int32, (tq2, LANES), 1)
    low = lane < HEAD_DIM
    nt = (((1,), (1,)), ((), ()))
    slab_keys = SLAB_ROWS * GRID_W

    def row_bias(pair, dy0, off):
        first = jnp.where(off == 0, dy0, PIECE_MT + dy0)
        mids = [dy0 + 2 * p - off for p in range(1, SLAB_ROWS // 2)]
        last = jnp.where(off == 0, PIECE_M0, PIECE_T0 + dy0)
        tail = ATT_KEYS - (SLAB_ROWS // 2) * LANES
        return jnp.concatenate([bias_ref[pair, first]] + [bias_ref[pair, m] for m in mids]
                               + [bias_ref[pair, last][:, :tail]], axis=1)

    def scores(ip, pair):
        r0 = j * ATT_ROWS + 2 * ip
        rs = [jnp.clip(r0 + a - WIN_H // 2, 0, rows - WIN_H) for a in range(2)]
        us = jnp.minimum(rs[0], rows - SLAB_ROWS)
        q0 = pl.multiple_of(ip * tq2, tq2)
        k0 = pl.multiple_of(us * GRID_W, GRID_W)
        cs = slice(pair * LANES, (pair + 1) * LANES)
        qq = jnp.concatenate([qe_ref[pl.ds(q0, tq2), cs], qo_ref[pl.ds(q0, tq2), cs]], axis=0)
        k2 = jnp.concatenate([k_ref[pl.ds(k0, slab_keys), cs], km_ref[:, cs]], axis=0)
        s = lax.dot_general(qq, k2, nt, preferred_element_type=f32)
        b = [row_bias(pair, rs[a] - (r0 + a) + (WIN_H - 1), rs[a] - us) for a in range(2)]
        s = s + jnp.concatenate([b[0][:GRID_W], b[1][:GRID_W], b[0][GRID_W:], b[1][GRID_W:]], axis=0)
        return s, jnp.max(s, axis=-1, keepdims=True), k0, q0, cs

    def weights(state):
        s, m, k0, q0, cs = state
        v2 = jnp.concatenate([v_ref[pl.ds(k0, slab_keys), cs], vm_ref[:, cs]], axis=0)
        e = jnp.exp2(s - m)
        l = jnp.sum(e, axis=-1, keepdims=True)
        return jnp.dot(e.astype(bf16), v2, preferred_element_type=f32), l, q0, cs

    def finish(state):
        o2, l, q0, cs = state
        o2 = o2 / l
        o = jnp.where(low, o2[:tq2], o2[tq2:])
        o_ref[pl.ds(q0, tq2), cs] = o.astype(bf16)

    def group_body(g, carry):
        items = [(g * ATT_GROUP + ip, pair) for ip in range(ATT_GROUP) for pair in range(N_HEAD_PAIRS)]
        a, b = {}, {}
        for step in range(len(items) + 2):
            if step < len(items):
                a[step] = scores(*items[step])
            if 0 <= step - 1 < len(items):
                b[step - 1] = weights(a.pop(step - 1))
            if 0 <= step - 2 < len(items):
                finish(b.pop(step - 2))
        return carry

    lax.fori_loop(0, ATT_ROWS // (2 * ATT_GROUP), group_body, 0)


def _attention(qkv, kv_meta, slabs, bsz, t):
    rows = t // GRID_W
    assert rows >= 2 * WIN_H and rows % ATT_ROWS == 0 and ATT_ROWS % (2 * ATT_GROUP) == 0
    steps = rows // ATT_ROWS
    tq = ATT_ROWS * GRID_W
    return pl.pallas_call(
        functools.partial(_attention_kernel, rows=rows),
        grid=(bsz, steps),
        in_specs=[pl.BlockSpec((tq, D_ATT), lambda b, j: (b * steps + j, 0)),
                  pl.BlockSpec((tq, D_ATT), lambda b, j: (b * steps + j, 1)),
                  pl.BlockSpec((t, D_ATT), lambda b, j: (b, 2)),
                  pl.BlockSpec((t, D_ATT), lambda b, j: (b, 3)),
                  pl.BlockSpec((N_META, D_ATT), lambda b, j: (0, 0)),
                  pl.BlockSpec((N_META, D_ATT), lambda b, j: (0, 1)),
                  _full(slabs)],
        out_specs=pl.BlockSpec((tq, D_ATT), lambda b, j: (b * steps + j, 0)),
        out_shape=jax.ShapeDtypeStruct((bsz * t, D_ATT), bf16),
        compiler_params=_cparams(2),
        name="attention",
    )(qkv, qkv, qkv, qkv, kv_meta, kv_meta, slabs)


def _layer_norm(z, g, b):
    mu = jnp.mean(z, axis=-1, keepdims=True)
    d = z - mu
    var = jnp.mean(d * d, axis=-1, keepdims=True)
    return d * lax.rsqrt(var + LN_EPS) * g + b


def _sigmoid(x):
    return 0.5 * jnp.tanh(0.5 * x) + 0.5


def _route_tile(logits, before_ref):
    tm = logits.shape[0]
    lt = logits.T
    el = lt[:N_EXPERTS]
    gl = lt[N_EXPERTS:N_EXPERTS + N_GROUPS]
    neg = -jnp.inf
    erow = lax.broadcasted_iota(jnp.int32, el.shape, 0).astype(f32)
    grow = lax.broadcasted_iota(jnp.int32, gl.shape, 0).astype(f32)
    first = lambda hit, idx, n: jnp.min(jnp.where(hit, idx, float(n)), axis=0, keepdims=True)

    gmax = jnp.max(gl, axis=0, keepdims=True)
    grp = first(gl == gmax, grow, N_GROUPS)
    pg_sel = 1.0 / jnp.sum(jnp.exp(gl - gmax), axis=0, keepdims=True)

    e_lo = grp * EXPERTS_PER_GROUP
    elm = jnp.where((erow >= e_lo) & (erow < e_lo + EXPERTS_PER_GROUP), el, neg)
    t1 = jnp.max(elm, axis=0, keepdims=True)
    e1 = first(elm == t1, erow, N_EXPERTS)
    el2 = jnp.where(erow == e1, neg, elm)
    t2 = jnp.max(el2, axis=0, keepdims=True)
    e2 = first(el2 == t2, erow, N_EXPERTS)
    r = jnp.exp(t2 - t1)
    g1 = pg_sel / (1.0 + r)
    g2 = pg_sel * r / (1.0 + r)

    hit1 = erow == e1
    hit2 = erow == e2
    onehot = jnp.where(hit1 | hit2, 1.0, 0.0)
    before = jnp.dot(onehot.astype(bf16), before_ref[...], preferred_element_type=f32)
    rank1 = jnp.sum(jnp.where(hit1, before, 0.0), axis=0, keepdims=True)
    rank2 = jnp.sum(jnp.where(hit2, before, 0.0), axis=0, keepdims=True)
    rrow = lax.broadcasted_iota(jnp.int32, (ROUTE_ROWS, tm), 0)
    route = jnp.zeros((ROUTE_ROWS, tm), f32)
    for k, val in enumerate((e1, e2, g1, g2, rank1, rank2)):
        route = jnp.where(rrow == k, val, route)
    return route, jnp.sum(onehot, axis=1, keepdims=True)


def _mixer_tail_kernel(x_ref, att_ref, rest_ref, wap_ref, wcp_ref, wout_ref, g_ref, b_ref, wr_ref, before_ref,
                       h_ref, hp_ref, route_ref, cnt_ref, merged_scr, *, n_tiles):
    i = pl.program_id(0)
    c3, c4 = D_CONV, D_CONV + D_MODEL
    cols = lambda j, base=0: slice(base + j * TAIL_CW, base + (j + 1) * TAIL_CW)
    chunks = range(D_MODEL // TAIL_CW)

    @pl.when(i == 0)
    def _():
        merged_scr[...] = jnp.zeros_like(merged_scr)

    mix = [jnp.dot(merged_scr[...], wout_ref[:, cols(j)], preferred_element_type=f32) for j in chunks]

    cv = jnp.dot(rest_ref[:, :D_CONV], wcp_ref[...], preferred_element_type=f32)
    ap = jnp.dot(att_ref[...], wap_ref[...], preferred_element_type=f32)

    z = [ALPHA * x_ref[:, cols(j)] + mix[j] for j in chunks]
    mu = sum(jnp.sum(zj, axis=-1, keepdims=True) for zj in z) * (1.0 / D_MODEL)
    d = [zj - mu for zj in z]
    var = sum(jnp.sum(dj * dj, axis=-1, keepdims=True) for dj in d) * (1.0 / D_MODEL)
    rstd = lax.rsqrt(var + LN_EPS)
    logits = None
    hs = []
    for j in chunks:
        hj = d[j] * rstd * g_ref[:, cols(j)] + b_ref[:, cols(j)]
        h_ref[:, cols(j)] = hj
        hs.append(hj)
        part = jnp.dot(hj.astype(bf16), wr_ref[cols(j), :], preferred_element_type=f32)
        logits = part if logits is None else logits + part
    hp_ref[...] = _pack_halves(jnp.concatenate(hs, axis=1))

    for j in chunks:
        merged_scr[:, cols(j)] = (rest_ref[:, cols(j, c3)] * ap[:, cols(j)].astype(bf16)
                                  + rest_ref[:, cols(j, c4)] * cv[:, cols(j)].astype(bf16))

    route, count = _route_tile(logits, before_ref)
    route_ref[...] = route
    cnt_ref[0] = jnp.broadcast_to(count, (N_EXPERTS, LANES))


def _mixer_tail(x, att, rest, p):
    n = x.shape[0]
    tm = TAIL_TM
    n_tiles = n // tm
    before = jnp.asarray(np.triu(np.ones((tm, tm), np.float32), 1), dtype=bf16)
    consts = (p["w_att_proj"], p["w_conv_proj"], p["w_out"], p["ln1_g"], p["ln1_b"], p["w_router"], before)
    front = lambda i: jnp.minimum(i, n_tiles - 1)
    back = lambda i: jnp.maximum(i - 1, 0)
    return pl.pallas_call(
        functools.partial(_mixer_tail_kernel, n_tiles=n_tiles),
        grid=(n_tiles + 1,),
        in_specs=[pl.BlockSpec((tm, D_MODEL), lambda i: (back(i), 0)),
                  pl.BlockSpec((tm, D_ATT), lambda i: (front(i), 0)),
                  pl.BlockSpec((tm, D_REST), lambda i: (front(i), 0))]
                 + [_full(c) for c in consts],
        out_specs=[pl.BlockSpec((tm, D_MODEL), lambda i: (back(i), 0)),
                   pl.BlockSpec((tm, D_PACKED), lambda i: (back(i), 0)),
                   pl.BlockSpec((ROUTE_ROWS, tm), lambda i: (0, back(i))),
                   pl.BlockSpec((1, N_EXPERTS, LANES), lambda i: (back(i), 0, 0))],
        out_shape=[jax.ShapeDtypeStruct((n, D_MODEL), f32),
                   jax.ShapeDtypeStruct((n, D_PACKED), jnp.uint32),
                   jax.ShapeDtypeStruct((ROUTE_ROWS, n), f32),
                   jax.ShapeDtypeStruct((n_tiles, N_EXPERTS, LANES), f32)],
        scratch_shapes=[pltpu.VMEM((tm, D_MODEL), bf16)],
        compiler_params=_cparams(1),
        name="mixer_tail",
    )(x, att, rest, *consts)


def _route(route, cnt, n, blk):
    n_tiles = n // TAIL_TM
    tile_cnt = cnt[:, :, 0].astype(jnp.int32)
    tile_off = jnp.cumsum(tile_cnt, axis=0) - tile_cnt
    counts = jnp.sum(tile_cnt, axis=0)
    padded = (counts + blk - 1) // blk * blk
    pend = jnp.cumsum(padded)
    pstart = pend - padded
    base = pstart[None, :] + tile_off
    lanes = jnp.arange(N_EXPERTS, dtype=jnp.int32)

    def rows_of(expert_row, rank_row):
        e = expert_row.astype(jnp.int32).reshape(n_tiles, TAIL_TM, 1)
        sel = jnp.sum(jnp.where(e == lanes, base[:, None, :], 0), axis=-1)
        return sel.reshape(n) + rank_row.astype(jnp.int32)

    dest0 = rows_of(route[0], route[4])
    dest1 = rows_of(route[1], route[5])

    n_blocks = (2 * n + N_EXPERTS * (blk - 1) + blk - 1) // blk
    blk_start = jnp.arange(n_blocks, dtype=jnp.int32) * blk
    blk_expert = jnp.sum((pend[None, :] <= blk_start[:, None]).astype(jnp.int32), axis=1)
    blk_expert = jnp.minimum(blk_expert, N_EXPERTS - 1)
    blk_valid = jnp.clip((pstart + counts)[blk_expert] - blk_start, 0, blk)
    blk_valid = jnp.where(blk_start < pend[-1], blk_valid, 0).astype(jnp.int32)
    return dest0, dest1, blk_expert, blk_valid, n_blocks


def _sc_mesh():
    return plsc.VectorSubcoreMesh(core_axis_name="c", subcore_axis_name="s")


def _sc_dispatch(h, dest0, dest1, p_rows):
    n, d = h.shape
    per = n // SC_WORKERS
    assert n % (SC_WORKERS * SC_IDX_WIN) == 0

    @pl.kernel(out_type=jax.ShapeDtypeStruct((p_rows, d), h.dtype), mesh=_sc_mesh(),
               scratch_types=[pltpu.VMEM((2, SC_IDX_WIN), jnp.int32), pltpu.VMEM((SC_ROWS, d), h.dtype)])
    def k(h_hbm, d0_hbm, d1_hbm, xs_hbm, idx, buf):
        base = (lax.axis_index("c") * SC_SUBCORES + lax.axis_index("s")) * per

        @pl.loop(0, per // SC_IDX_WIN)
        def _(w):
            off = base + w * SC_IDX_WIN
            pltpu.sync_copy(d0_hbm.at[pl.ds(off, SC_IDX_WIN)], idx.at[0])
            pltpu.sync_copy(d1_hbm.at[pl.ds(off, SC_IDX_WIN)], idx.at[1])
            for r in range(SC_IDX_WIN // SC_ROWS):
                pltpu.sync_copy(h_hbm.at[pl.ds(off + r * SC_ROWS, SC_ROWS)], buf)
                pltpu.sync_copy(buf, xs_hbm.at[idx.at[0, pl.ds(r * SC_ROWS, SC_ROWS)]])
                pltpu.sync_copy(buf, xs_hbm.at[idx.at[1, pl.ds(r * SC_ROWS, SC_ROWS)]])

    return k(h, dest0, dest1)


def _sc_gather2(ys, dest0, dest1):
    n = dest0.shape[0]
    d = ys.shape[1]
    per = n // SC_WORKERS
    out = jax.ShapeDtypeStruct((n, d), ys.dtype)

    @pl.kernel(out_type=[out, out], mesh=_sc_mesh(),
               scratch_types=[pltpu.VMEM((2, SC_IDX_WIN), jnp.int32), pltpu.VMEM((SC_ROWS, d), ys.dtype)])
    def k(ys_hbm, d0_hbm, d1_hbm, y0_hbm, y1_hbm, idx, buf):
        base = (lax.axis_index("c") * SC_SUBCORES + lax.axis_index("s")) * per

        @pl.loop(0, per // SC_IDX_WIN)
        def _(w):
            off = base + w * SC_IDX_WIN
            pltpu.sync_copy(d0_hbm.at[pl.ds(off, SC_IDX_WIN)], idx.at[0])
            pltpu.sync_copy(d1_hbm.at[pl.ds(off, SC_IDX_WIN)], idx.at[1])
            for r in range(SC_IDX_WIN // SC_ROWS):
                for kk, y_hbm in enumerate((y0_hbm, y1_hbm)):
                    pltpu.sync_copy(ys_hbm.at[idx.at[kk, pl.ds(r * SC_ROWS, SC_ROWS)]], buf)
                    pltpu.sync_copy(buf, y_hbm.at[pl.ds(off + r * SC_ROWS, SC_ROWS)])

    return k(ys, dest0, dest1)


def _experts_kernel(be_ref, bv_ref, slot_ref, nxt_ref, xs_ref, wg_hbm, wu_hbm, wd_hbm, ys_ref,
                    wg_f32, wu_f32, wd_f32, wg_bf, wu_bf, wd_bf, sems):
    i = pl.program_id(0)
    valid = bv_ref[i]
    expert = be_ref[i]
    slot = slot_ref[i]
    run_start = jnp.logical_or(i == 0, expert != be_ref[jnp.maximum(i - 1, 0)])

    def fetch(e, s):
        return [pltpu.make_async_copy(hbm.at[e], buf.at[s], sems.at[s, k])
                for k, (hbm, buf) in enumerate(((wg_hbm, wg_f32), (wu_hbm, wu_f32), (wd_hbm, wd_f32)))]

    @pl.when(i == 0)
    def _():
        for cp in fetch(expert, slot):
            cp.start()

    @pl.when(run_start)
    def _():
        for cp in fetch(expert, slot):
            cp.wait()
        wg_bf[...] = wg_f32[slot].astype(bf16)
        wu_bf[...] = wu_f32[slot].astype(bf16)
        wd_bf[...] = wd_f32[slot].astype(bf16)

        @pl.when(nxt_ref[i] >= 0)
        def _():
            for cp in fetch(nxt_ref[i], 1 - slot):
                cp.start(priority=1)

    @pl.when(valid > 0)
    def _():
        row = lax.broadcasted_iota(jnp.int32, xs_ref.shape, 0)
        keep = row < valid
        lo, hi = _unpack_halves(xs_ref[...])
        x = jnp.concatenate([jnp.where(keep, lo, 0.0), jnp.where(keep, hi, 0.0)], axis=1).astype(bf16)
        g = jnp.dot(x, wg_bf[...], preferred_element_type=f32)
        u = jnp.dot(x, wu_bf[...], preferred_element_type=f32)
        hmid = (g * _sigmoid(g) * u).astype(bf16)
        ys_ref[...] = _pack_halves(jnp.dot(hmid, wd_bf[...], preferred_element_type=f32))

    @pl.when(valid == 0)
    def _():
        ys_ref[...] = jnp.zeros_like(ys_ref)


def _experts(xs, blk_expert, blk_valid, wg, wu, wd, n_blocks, blk):
    starts = jnp.concatenate([jnp.ones((1,), jnp.int32), (blk_expert[1:] != blk_expert[:-1]).astype(jnp.int32)])
    slot = (jnp.cumsum(starts) - 1) % 2
    idx = jnp.arange(n_blocks, dtype=jnp.int32)
    next_start = lax.cummin(jnp.where(starts > 0, idx, n_blocks)[::-1])[::-1]
    next_start = jnp.concatenate([next_start[1:], jnp.full((1,), n_blocks, jnp.int32)])
    nxt = jnp.where(next_start < n_blocks, blk_expert[jnp.minimum(next_start, n_blocks - 1)], -1)
    tile = pl.BlockSpec((blk, D_PACKED), lambda i, *_: (i, 0))
    any_spec = pl.BlockSpec(memory_space=pl.ANY)
    grid_spec = pltpu.PrefetchScalarGridSpec(
        num_scalar_prefetch=4,
        grid=(n_blocks,),
        in_specs=[tile, any_spec, any_spec, any_spec],
        out_specs=tile,
        scratch_shapes=[pltpu.VMEM((2, D_MODEL, D_EXPERT), f32),
                        pltpu.VMEM((2, D_MODEL, D_EXPERT), f32),
                        pltpu.VMEM((2, D_EXPERT, D_MODEL), f32),
                        pltpu.VMEM((D_MODEL, D_EXPERT), bf16),
                        pltpu.VMEM((D_MODEL, D_EXPERT), bf16),
                        pltpu.VMEM((D_EXPERT, D_MODEL), bf16),
                        pltpu.SemaphoreType.DMA((2, 3))],
    )
    return pl.pallas_call(
        _experts_kernel,
        grid_spec=grid_spec,
        out_shape=jax.ShapeDtypeStruct(xs.shape, jnp.uint32),
        compiler_params=_cparams(1),
        name="experts",
    )(blk_expert, blk_valid, slot.astype(jnp.int32), nxt.astype(jnp.int32), xs, wg, wu, wd)


def _final_norm_kernel(h_hbm, y0_hbm, y1_hbm, route_ref, g_ref, b_ref, o_ref, h_buf, y0_buf, y1_buf, sems):
    s = pl.program_id(0)
    n_steps = pl.num_programs(0)
    ts = o_ref.shape[0]
    streams = ((h_hbm, h_buf), (y0_hbm, y0_buf), (y1_hbm, y1_buf))

    def fetch(step):
        slot = step % NORM_RING
        rows = pl.ds(pl.multiple_of(step * ts, ts), ts)
        return [pltpu.make_async_copy(hbm.at[rows], buf.at[slot], sems.at[k, slot])
                for k, (hbm, buf) in enumerate(streams)]

    @pl.when(s == 0)
    def _():
        for k in range(NORM_RING - 1):
            pl.when(k < n_steps)(lambda k=k: [cp.start() for cp in fetch(k)])

    @pl.when(s + NORM_RING - 1 < n_steps)
    def _():
        for cp in fetch(s + NORM_RING - 1):
            cp.start()

    for cp in fetch(s):
        cp.wait()
    slot = s % NORM_RING
    h = h_buf[slot]
    route_t = route_ref[...].T
    g0, g1 = route_t[:, 2:3], route_t[:, 3:4]
    lo0, hi0 = _unpack_halves(y0_buf[slot])
    lo1, hi1 = _unpack_halves(y1_buf[slot])
    ffn = jnp.concatenate([lo0 * g0 + lo1 * g1, hi0 * g0 + hi1 * g1], axis=1)
    o_ref[...] = _layer_norm(ALPHA * h + ffn, g_ref[...], b_ref[...])


def _final_norm(h, y0, y1, route, ln_g, ln_b):
    n = h.shape[0]
    ts = NORM_TS
    any_spec = pl.BlockSpec(memory_space=pl.ANY)
    return pl.pallas_call(
        _final_norm_kernel,
        grid=(n // ts,),
        in_specs=[any_spec, any_spec, any_spec,
                  pl.BlockSpec((ROUTE_ROWS, ts), lambda i: (0, i)), _full(ln_g), _full(ln_b)],
        out_specs=pl.BlockSpec((ts, D_MODEL), lambda i: (i, 0)),
        out_shape=jax.ShapeDtypeStruct((n, D_MODEL), f32),
        scratch_shapes=[pltpu.VMEM((NORM_RING, ts, D_MODEL), f32),
                        pltpu.VMEM((NORM_RING, ts, D_PACKED), jnp.uint32),
                        pltpu.VMEM((NORM_RING, ts, D_PACKED), jnp.uint32),
                        pltpu.SemaphoreType.DMA((3, NORM_RING))],
        compiler_params=_cparams(1),
        name="final_norm",
    )(h, y0, y1, route, ln_g, ln_b)


def _encode(x, p, meta_tokens, kv_meta, slabs):
    bsz, t, _ = x.shape
    n = bsz * t
    assert t % NORM_TS == 0 and t % TAIL_TM == 0 and t % PROJ_TM == 0
    xf = x.reshape(n, D_MODEL)
    qkv, rest = _in_proj(xf, meta_tokens, p["w_in"], p["conv_w"], p["conv_b"], t)
    att = _attention(qkv, kv_meta, slabs, bsz, t)
    h1, h1_packed, route, cnt = _mixer_tail(xf, att, rest, p)
    blk = MOE_BLK
    dest0, dest1, blk_expert, blk_valid, n_blocks = _route(route, cnt, n, blk)
    xs = _sc_dispatch(h1_packed, dest0, dest1, n_blocks * blk)
    ys = _experts(xs, blk_expert, blk_valid, p["w_e_gate"], p["w_e_up"], p["w_e_down"], n_blocks, blk)
    y0, y1 = _sc_gather2(ys, dest0, dest1)
    y = _final_norm(h1, y0, y1, route, p["ln2_g"], p["ln2_b"])
    return y.reshape(bsz, t, D_MODEL)


def kernel(x_prompt, x_sample, meta_tokens, w_in, rpb, conv_w, conv_b, w_att_proj, w_conv_proj, w_out,
           ln1_g, ln1_b, w_router_group, w_router_expert, w_e_gate, w_e_up, w_e_down, ln2_g, ln2_b):
    w_router = jnp.concatenate([w_router_expert[0], w_router_group[0]], axis=1)
    w_router = jnp.pad(w_router, ((0, 0), (0, LANES - w_router.shape[1])))
    row = lambda v: v[0].reshape(1, -1).astype(f32)
    p = {
        "w_in": w_in[0].astype(bf16),
        "conv_w": conv_w[0].astype(f32), "conv_b": row(conv_b),
        "w_att_proj": w_att_proj[0].astype(bf16), "w_conv_proj": w_conv_proj[0].astype(bf16),
        "w_out": (0.5 * w_out[0]).astype(bf16), "ln1_g": row(ln1_g), "ln1_b": row(ln1_b),
        "w_router": w_router.astype(bf16),
        "w_e_gate": w_e_gate[0], "w_e_up": w_e_up[0], "w_e_down": w_e_down[0],
        "ln2_g": row(ln2_g), "ln2_b": row(ln2_b),
    }
    meta_tokens = meta_tokens.astype(f32)
    kv_meta = _meta_kv(meta_tokens, p["w_in"][:, D_ATT:3 * D_ATT])
    slabs = _bias_pieces(rpb[0])
    y_prompt = _encode(x_prompt, p, meta_tokens, kv_meta, slabs)
    y_sample = _encode(x_sample, p, meta_tokens, kv_meta, slabs)
    return (y_prompt, y_sample)
```
